```python
import math
import jax, jax.numpy as jnp
from jax import lax
import numpy as np

D_MODEL = 1024
BATCH = 4
SEQ = 8192
DEPTH = 1

N_Q_HEADS = 8
N_KV_HEADS = 2
HEAD_DIM = 64
WINDOW = 128
ATTN_BLOCK = WINDOW
ROPE_THETA = 10000.0
MLSTM_HEADS = 4
MLSTM_HEAD_DIM = 128
MLSTM_CHUNK = 64
CONV_WIDTH = 4
D_FF = -(-8 * D_MODEL // (3 * 256)) * 256
NORM_EPS = 1e-6

ATTN_Q_WIDTH = N_Q_HEADS * HEAD_DIM
ATTN_KV_WIDTH = N_KV_HEADS * HEAD_DIM
MLSTM_WIDTH = MLSTM_HEADS * MLSTM_HEAD_DIM
IN_SPLIT_SIZES = (ATTN_Q_WIDTH, ATTN_KV_WIDTH, ATTN_KV_WIDTH,
                  MLSTM_WIDTH, MLSTM_WIDTH, MLSTM_WIDTH, MLSTM_WIDTH,
                  MLSTM_HEADS, MLSTM_HEADS, D_MODEL, D_MODEL)
IN_WIDTH = sum(IN_SPLIT_SIZES)

kernel_name = 'hybrid_swa_sink_mlstm_gated_block'


def rms_norm(x, g):
    xf = x.astype(jnp.float32)
    y = xf * lax.rsqrt(jnp.mean(xf * xf, axis=-1, keepdims=True) + NORM_EPS)
    return (y * g.astype(jnp.float32)).astype(x.dtype)


def modulate(h, shift, scale):
    return h * (1 + scale[:, None, :]) + shift[:, None, :]


def rope(t, positions):
    half = HEAD_DIM // 2
    inv_freq = ROPE_THETA ** (-2.0 * jnp.arange(half, dtype=jnp.float32) / HEAD_DIM)
    ang = positions.astype(jnp.float32)[..., None] * inv_freq
    cos = jnp.cos(ang)[:, :, None, :]
    sin = jnp.sin(ang)[:, :, None, :]
    tf = t.astype(jnp.float32)
    t1, t2 = tf[..., :half], tf[..., half:]
    return jnp.concatenate([t1 * cos - t2 * sin, t2 * cos + t1 * sin], axis=-1).astype(t.dtype)


def causal_depthwise_conv(u, w, b):
    S = u.shape[1]
    up = jnp.pad(u, ((0, 0), (CONV_WIDTH - 1, 0), (0, 0)))
    out = b
    for j in range(CONV_WIDTH):
        out = out + up[:, j:j + S] * w[j]
    return out


def sliding_window_attention(q, k, v, sinks):
    B, S = q.shape[0], q.shape[1]
    nb = S // ATTN_BLOCK
    g = N_Q_HEADS // N_KV_HEADS
    qb = q.astype(jnp.float32).reshape(B, nb, ATTN_BLOCK, N_KV_HEADS, g, HEAD_DIM) * (HEAD_DIM ** -0.5)

    def band(t):
        t = t.astype(jnp.float32).reshape(B, nb, ATTN_BLOCK, N_KV_HEADS, HEAD_DIM)
        prev = jnp.pad(t, ((0, 0), (1, 0), (0, 0), (0, 0), (0, 0)))[:, :-1]
        return jnp.concatenate([prev, t], axis=2)

    kb, vb = band(k), band(v)
    s = jnp.einsum('bnqhgd,bnkhd->bnhgqk', qb, kb)
    qi = jnp.arange(ATTN_BLOCK)[:, None]
    kj = jnp.arange(2 * ATTN_BLOCK)[None, :]
    rel = kj - ATTN_BLOCK
    in_band = (rel <= qi) & (qi - rel < WINDOW)
    blk = jnp.arange(nb)[:, None, None]
    mask = in_band[None] & ((blk > 0) | (kj[None] >= ATTN_BLOCK))
    s = jnp.where(mask[None, :, None, None], s, -jnp.inf)
    sink = sinks.astype(jnp.float32).reshape(N_KV_HEADS, g)[None, None, :, :, None, None]
    m = jnp.maximum(jnp.max(s, axis=-1, keepdims=True), sink)
    p = jnp.exp(s - m)
    p = p / (jnp.sum(p, axis=-1, keepdims=True) + jnp.exp(sink - m))
    o = jnp.einsum('bnhgqk,bnkhd->bnqhgd', p, vb)
    return o.reshape(B, S, N_Q_HEADS * HEAD_DIM)


def mlstm_chunkwise(q, k, v, i_pre, f_pre):
    B, S, H, D = q.shape
    L = MLSTM_CHUNK
    nc = S // L

    def chunks(t):
        return t.astype(jnp.float32).reshape(B, nc, L, H, D).transpose(0, 3, 1, 2, 4)

    qc = chunks(q)
    kc = chunks(k) * (D ** -0.5)
    vc = chunks(v)
    ig = i_pre.astype(jnp.float32).reshape(B, nc, L, H).transpose(0, 3, 1, 2)
    logf = jax.nn.log_sigmoid(f_pre.astype(jnp.float32)).reshape(B, nc, L, H).transpose(0, 3, 1, 2)
    b = jnp.cumsum(logf, axis=-1)
    b_last = b[..., -1]
    a = b_last[..., None] - b + ig

    def step(carry, inp):
        C, n, m = carry
        k_c, v_c, a_c, bl_c = inp
        m_new = jnp.maximum(bl_c + m, jnp.max(a_c, axis=-1))
        decay = jnp.exp(bl_c + m - m_new)
        kw = k_c * jnp.exp(a_c - m_new[..., None])[..., None]
        C_new = decay[..., None, None] * C + jnp.einsum('bhld,bhle->bhde', kw, v_c)
        n_new = decay[..., None] * n + jnp.sum(kw, axis=-2)
        return (C_new, n_new, m_new), (C, n, m)

    init = (jnp.zeros((B, H, D, D), jnp.float32), jnp.zeros((B, H, D), jnp.float32),
            jnp.zeros((B, H), jnp.float32))
    xs = (kc.transpose(2, 0, 1, 3, 4), vc.transpose(2, 0, 1, 3, 4),
          a.transpose(2, 0, 1, 3), b_last.transpose(2, 0, 1))
    _, (C_prev, n_prev, m_prev) = lax.scan(step, init, xs)
    C_prev = C_prev.transpose(1, 2, 0, 3, 4)
    n_prev = n_prev.transpose(1, 2, 0, 3)
    m_prev = m_prev.transpose(1, 2, 0)

    causal = jnp.tril(jnp.ones((L, L), dtype=bool))
    d_mat = jnp.where(causal, b[..., :, None] - b[..., None, :] + ig[..., None, :], -jnp.inf)
    inter = b + m_prev[..., None]
    m_t = jnp.maximum(inter, jnp.max(d_mat, axis=-1))
    w_intra = jnp.exp(d_mat - m_t[..., None])
    w_inter = jnp.exp(inter - m_t)
    scores = jnp.einsum('bhctd,bhcsd->bhcts', qc, kc) * w_intra
    num = jnp.einsum('bhcts,bhcsd->bhctd', scores, vc) + \
        w_inter[..., None] * jnp.einsum('bhctd,bhcde->bhcte', qc, C_prev)
    den = jnp.sum(scores, axis=-1) + w_inter * jnp.einsum('bhctd,bhcd->bhct', qc, n_prev)
    h = num / jnp.maximum(jnp.abs(den), jnp.exp(-m_t))[..., None]
    return h.transpose(0, 2, 3, 1, 4).reshape(B, S, H, D)


def head_layer_norm(h, w):
    mu = jnp.mean(h, axis=-1, keepdims=True)
    var = jnp.mean(jnp.square(h - mu), axis=-1, keepdims=True)
    y = (h - mu) * lax.rsqrt(var + NORM_EPS)
    return y.reshape(h.shape[0], h.shape[1], -1) * w.astype(jnp.float32)


def setup_inputs(seed: int = 0) -> dict:
    key = jax.random.key(seed)
    ks = jax.random.split(key, 24)
    f32 = jnp.float32

    def dense(k, shape, fan_in, scale=1.0):
        return jax.random.normal(k, shape, f32) * (scale * fan_in ** -0.5)

    def gain(k, width):
        return 1.0 + 0.05 * jax.random.normal(k, (DEPTH, width), f32)

    x = jax.random.normal(ks[0], (BATCH, SEQ, D_MODEL), f32)
    c = jax.random.normal(ks[1], (BATCH, D_MODEL), f32)
    offsets = jax.random.randint(ks[2], (BATCH, 1), 0, 4096, dtype=jnp.int32)
    positions = (offsets + jnp.arange(SEQ, dtype=jnp.int32)[None, :]).astype(jnp.int32)
    w_ada = dense(ks[3], (DEPTH, D_MODEL, 6 * D_MODEL), D_MODEL, 0.5)
    b_ada = 0.02 * jax.random.normal(ks[4], (DEPTH, 6 * D_MODEL), f32)
    g_pre_mix = gain(ks[5], D_MODEL)
    g_post_mix = gain(ks[6], D_MODEL)
    w_in = dense(ks[7], (DEPTH, D_MODEL, IN_WIDTH), D_MODEL)
    b_i = 0.1 * jax.random.normal(ks[8], (DEPTH, MLSTM_HEADS), f32)
    b_f = jnp.linspace(3.0, 6.0, MLSTM_HEADS, dtype=f32)[None, :] + \
        0.1 * jax.random.normal(ks[9], (DEPTH, MLSTM_HEADS), f32)
    b_if = jnp.concatenate([b_i, b_f], axis=-1)
    conv_w = dense(ks[10], (DEPTH, CONV_WIDTH, 2 * MLSTM_WIDTH), CONV_WIDTH)
    conv_b = 0.02 * jax.random.normal(ks[11], (DEPTH, 2 * MLSTM_WIDTH), f32)
    attn_sinks = 0.5 * jax.random.normal(ks[12], (DEPTH, N_Q_HEADS), f32)
    mlstm_norm_w = gain(ks[13], MLSTM_WIDTH)
    w_branch_attn = dense(ks[14], (DEPTH, ATTN_Q_WIDTH, D_MODEL), ATTN_Q_WIDTH)
    w_branch_mlstm = dense(ks[15], (DEPTH, MLSTM_WIDTH, D_MODEL), MLSTM_WIDTH)
    w_out = dense(ks[16], (DEPTH, D_MODEL, D_MODEL), D_MODEL)
    g_pre_ffn = gain(ks[17], D_MODEL)
    g_post_ffn = gain(ks[18], D_MODEL)
    w_ffn_gate = dense(ks[19], (DEPTH, D_MODEL, D_FF), D_MODEL)
    w_ffn_up = dense(ks[20], (DEPTH, D_MODEL, D_FF), D_MODEL)
    w_ffn_down = dense(ks[21], (DEPTH, D_FF, D_MODEL), D_FF)
    return {'x': x, 'c': c, 'positions': positions, 'w_ada': w_ada, 'b_ada': b_ada,
            'g_pre_mix': g_pre_mix, 'g_post_mix': g_post_mix, 'w_in': w_in, 'b_if': b_if,
            'conv_w': conv_w, 'conv_b': conv_b, 'attn_sinks': attn_sinks,
            'mlstm_norm_w': mlstm_norm_w, 'w_branch_attn': w_branch_attn,
            'w_branch_mlstm': w_branch_mlstm, 'w_out': w_out, 'g_pre_ffn': g_pre_ffn,
            'g_post_ffn': g_post_ffn, 'w_ffn_gate': w_ffn_gate, 'w_ffn_up': w_ffn_up,
            'w_ffn_down': w_ffn_down}


def reference(x, c, positions, w_ada, b_ada, g_pre_mix, g_post_mix, w_in, b_if, conv_w, conv_b,
              attn_sinks, mlstm_norm_w, w_branch_attn, w_branch_mlstm, w_out, g_pre_ffn,
              g_post_ffn, w_ffn_gate, w_ffn_up, w_ffn_down):
    B, S, _ = x.shape
    split_points = np.cumsum(IN_SPLIT_SIZES)[:-1].tolist()
    for l in range(DEPTH):
        mod = c @ w_ada[l] + b_ada[l]
        shift_m, scale_m, gate_m, shift_f, scale_f, gate_f = jnp.split(mod, 6, axis=-1)

        h = modulate(rms_norm(x, g_pre_mix[l]), shift_m, scale_m)
        proj = h @ w_in[l]
        q_a, k_a, v_a, q_m, k_m, v_m, o_m, i_m, f_m, g_a, g_m = jnp.split(proj, split_points, axis=-1)

        q_a = rope(q_a.reshape(B, S, N_Q_HEADS, HEAD_DIM), positions)
        k_a = rope(k_a.reshape(B, S, N_KV_HEADS, HEAD_DIM), positions)
        v_a = v_a.reshape(B, S, N_KV_HEADS, HEAD_DIM)
        y_a = sliding_window_attention(q_a, k_a, v_a, attn_sinks[l]).astype(x.dtype)

        qk_m = jax.nn.silu(causal_depthwise_conv(jnp.concatenate([q_m, k_m], axis=-1), conv_w[l], conv_b[l]))
        q_m, k_m = jnp.split(qk_m, 2, axis=-1)
        i_pre = i_m + b_if[l][:MLSTM_HEADS]
        f_pre = f_m + b_if[l][MLSTM_HEADS:]
        h_m = mlstm_chunkwise(q_m.reshape(B, S, MLSTM_HEADS, MLSTM_HEAD_DIM),
                              k_m.reshape(B, S, MLSTM_HEADS, MLSTM_HEAD_DIM),
                              v_m.reshape(B, S, MLSTM_HEADS, MLSTM_HEAD_DIM), i_pre, f_pre)
        y_m = (jax.nn.sigmoid(o_m.astype(jnp.float32)) * head_layer_norm(h_m, mlstm_norm_w[l])).astype(x.dtype)

        merged = jax.nn.sigmoid(g_a) * (y_a @ w_branch_attn[l]) + jax.nn.sigmoid(g_m) * (y_m @ w_branch_mlstm[l])
        mix = merged @ w_out[l]
        x = x + gate_m[:, None, :] * rms_norm(mix, g_post_mix[l])

        h2 = modulate(rms_norm(x, g_pre_ffn[l]), shift_f, scale_f)
        ff = (jax.nn.silu(h2 @ w_ffn_gate[l]) * (h2 @ w_ffn_up[l])) @ w_ffn_down[l]
        x = x + gate_f[:, None, :] * rms_norm(ff, g_post_ffn[l])
    return x
```

```python
import functools

import numpy as np
import jax
import jax.numpy as jnp
from jax import lax
from jax.experimental import pallas as pl
from jax.experimental.pallas import tpu as pltpu

F32 = jnp.float32
BF16 = jnp.bfloat16

D_MODEL = 1024
N_Q_HEADS = 8
N_KV_HEADS = 2
HEAD_DIM = 64
ROPE_THETA = 10000.0
MLSTM_HEADS = 4
MLSTM_HEAD_DIM = 128
CONV_WIDTH = 4
D_FF = 2816
NORM_EPS = 1e-6
ATTN_Q_WIDTH = N_Q_HEADS * HEAD_DIM
ATTN_KV_WIDTH = N_KV_HEADS * HEAD_DIM
MLSTM_WIDTH = MLSTM_HEADS * MLSTM_HEAD_DIM
ATTN_BLOCK = 128

LANES = 128
SUBLANES = 8
MXU_WIDTH = 256
V7X_SCOPED_VMEM_BYTES = 60000 * 1024

TOKEN_TILE = 512
MLSTM_CHUNK = 128
FF_CHUNK = MXU_WIDTH
ROPE_TILE = 2048

C_QA = 0
C_KV = C_QA + ATTN_Q_WIDTH
C_QK = C_KV + 2 * ATTN_KV_WIDTH
C_VM = C_QK + 2 * MLSTM_WIDTH
C_OM = C_VM + MLSTM_WIDTH
C_IF = C_OM + MLSTM_WIDTH
C_GA = C_IF + LANES
C_GM = C_GA + D_MODEL
IN_COLS = C_GM + D_MODEL


def _vmem_limit(estimate_bytes):
    return int(min(V7X_SCOPED_VMEM_BYTES, max(estimate_bytes, 16 * 1024 * 1024)))


def _rms(x, g):
    return (x * lax.rsqrt(jnp.mean(x * x, axis=-1, keepdims=True) + NORM_EPS)) * g


def _ada_kernel(c_ref, w_ref, b_ref, o_ref):
    o_ref[...] = jnp.dot(c_ref[...], w_ref[...], preferred_element_type=F32,
                         precision=lax.Precision.HIGHEST) + b_ref[...]


def _ada(c_pad, w_ada, b_ada):
    rows = c_pad.shape[0]
    n = w_ada.shape[1]
    return pl.pallas_call(
        _ada_kernel,
        grid=(n // D_MODEL,),
        in_specs=[pl.BlockSpec((rows, D_MODEL), lambda j: (0, 0)),
                  pl.BlockSpec((D_MODEL, D_MODEL), lambda j: (0, j)),
                  pl.BlockSpec((1, D_MODEL), lambda j: (0, j))],
        out_specs=pl.BlockSpec((rows, D_MODEL), lambda j: (0, j)),
        out_shape=jax.ShapeDtypeStruct((rows, n), F32),
        name="ada",
    )(c_pad, w_ada, b_ada)


def _rope_kernel(pos_ref, freq_ref, cos_ref, sin_ref):
    ang = freq_ref[...] * pos_ref[...].astype(F32)
    c = jnp.cos(ang)
    s = jnp.sin(ang)
    cos_ref[...] = jnp.concatenate([c, c, c, c], axis=0).T
    sin_ref[...] = jnp.concatenate([-s, s, -s, s], axis=0).T


def _rope_tables(pos_row, inv_freq):
    t = pos_row.shape[1]
    half = inv_freq.shape[0]
    return pl.pallas_call(
        _rope_kernel,
        grid=(t // ROPE_TILE,),
        in_specs=[pl.BlockSpec((1, ROPE_TILE), lambda i: (0, i)),
                  pl.BlockSpec((half, 1), lambda i: (0, 0))],
        out_specs=[pl.BlockSpec((ROPE_TILE, LANES), lambda i: (i, 0)),
                   pl.BlockSpec((ROPE_TILE, LANES), lambda i: (i, 0))],
        out_shape=[jax.ShapeDtypeStruct((t, LANES), F32),
                   jax.ShapeDtypeStruct((t, LANES), F32)],
        name="rope",
    )(pos_row, inv_freq)


def _inproj_kernel(x_ref, mod_ref, g_ref, cos_ref, sin_ref, w_ref,
                   qa_ref, kv_ref, qk_ref, vm_ref, om_ref, if_ref, ga_ref, gm_ref):
    x = x_ref[...]
    shift = mod_ref[0, 0:1, :]
    scale = mod_ref[0, 1:2, :]
    h = (_rms(x, g_ref[...]) * (1.0 + scale) + shift).astype(BF16)

    def proj(lo, width):
        return jnp.dot(h, w_ref[:, lo:lo + width], preferred_element_type=F32)

    cos = cos_ref[...]
    sin = sin_ref[...]
    lane = lax.broadcasted_iota(jnp.int32, (1, LANES), 1)
    first_half = (lane % HEAD_DIM) < (HEAD_DIM // 2)

    def rope(t):
        swapped = jnp.where(first_half, pltpu.roll(t, LANES - HEAD_DIM // 2, axis=1),
                            pltpu.roll(t, HEAD_DIM // 2, axis=1))
        return t * cos + swapped * sin

    q = proj(C_QA, ATTN_Q_WIDTH)
    for p in range(ATTN_Q_WIDTH // LANES):
        sl = slice(p * LANES, (p + 1) * LANES)
        qa_ref[:, sl] = (rope(q[:, sl]) * (HEAD_DIM ** -0.5)).astype(BF16)
    kv = proj(C_KV, 2 * ATTN_KV_WIDTH)
    kv_ref[:, :LANES] = rope(kv[:, :LANES]).astype(BF16)
    kv_ref[:, LANES:] = kv[:, LANES:].astype(BF16)
    qk_ref[...] = proj(C_QK, 2 * MLSTM_WIDTH).astype(BF16)
    vm_ref[...] = proj(C_VM, MLSTM_WIDTH).astype(BF16)
    om_ref[...] = proj(C_OM, MLSTM_WIDTH).astype(BF16)
    if_ref[...] = proj(C_IF, LANES)
    ga_ref[...] = proj(C_GA, D_MODEL).astype(BF16)
    gm_ref[...] = proj(C_GM, D_MODEL).astype(BF16)


def _inproj(x2, mod3, g_pre, cos_t, sin_t, w_all, tiles_per_batch):
    t = x2.shape[0]
    tm = TOKEN_TILE
    row = lambda i: (i, 0)
    const = lambda i: (0, 0)
    widths = [ATTN_Q_WIDTH, 2 * ATTN_KV_WIDTH, 2 * MLSTM_WIDTH, MLSTM_WIDTH, MLSTM_WIDTH,
              LANES, D_MODEL, D_MODEL]
    dtypes = [BF16, BF16, BF16, BF16, BF16, F32, BF16, BF16]
    out_bytes = sum(w * np.dtype(d).itemsize for w, d in zip(widths, dtypes)) * tm
    est = (2 * w_all.size * 2 + 2 * tm * D_MODEL * 4 + 2 * out_bytes
           + 4 * tm * LANES * 4 + 6 * tm * D_MODEL * 4)
    return pl.pallas_call(
        _inproj_kernel,
        grid=(t // tm,),
        in_specs=[pl.BlockSpec((tm, D_MODEL), row),
                  pl.BlockSpec((1, 6, D_MODEL), lambda i: (i // tiles_per_batch, 0, 0)),
                  pl.BlockSpec((1, D_MODEL), const),
                  pl.BlockSpec((tm, LANES), row),
                  pl.BlockSpec((tm, LANES), row),
                  pl.BlockSpec((D_MODEL, IN_COLS), const)],
        out_specs=[pl.BlockSpec((tm, w), row) for w in widths],
        out_shape=[jax.ShapeDtypeStruct((t, w), d) for w, d in zip(widths, dtypes)],
        compiler_params=pltpu.CompilerParams(dimension_semantics=("arbitrary",),
                                             vmem_limit_bytes=_vmem_limit(est)),
        name="inproj",
    )(x2, mod3, g_pre, cos_t, sin_t, w_all)


def _attn_kernel(sink_ref, q_ref, kv_ref, kvp_ref, o_ref):
    j = pl.program_id(1)
    blk = ATTN_BLOCK
    nblk = TOKEN_TILE // blk
    lane = lax.broadcasted_iota(jnp.int32, (1, LANES), 1)
    low = lane < HEAD_DIM
    qi = lax.broadcasted_iota(jnp.int32, (blk, 2 * blk), 0)
    kj = lax.broadcasted_iota(jnp.int32, (blk, 2 * blk), 1)
    in_cur = (kj >= blk) & (kj - blk <= qi)
    in_prev = (kj < blk) & (kj > qi)
    in_prev_first = (kj < blk) & (kj > qi + jnp.where(j > 0, 0, blk))
    ones = jnp.ones((2 * blk, LANES), F32)
    zeros = jnp.zeros((2 * blk, LANES), F32)
    one_lo = jnp.where(low, ones, zeros).astype(BF16)
    one_hi = jnp.where(low, zeros, ones).astype(BF16)

    for i in range(nblk):
        cur = kv_ref[i * blk:(i + 1) * blk, :]
        prev = kvp_ref[...] if i == 0 else kv_ref[(i - 1) * blk:i * blk, :]
        band = jnp.concatenate([prev, cur], axis=0).astype(F32)
        k = band[:, :LANES]
        v = band[:, LANES:]
        k_cat = jnp.concatenate([jnp.where(low, k, 0.0), jnp.where(low, 0.0, k)], axis=0).astype(BF16)
        v_cat = jnp.concatenate(
            [jnp.concatenate([jnp.where(low, v, 0.0).astype(BF16), one_lo], axis=1),
             jnp.concatenate([jnp.where(low, 0.0, v).astype(BF16), one_hi], axis=1)], axis=0)
        mask = in_cur | (in_prev_first if i == 0 else in_prev)
        for p in range(ATTN_Q_WIDTH // LANES):
            q = q_ref[i * blk:(i + 1) * blk, p * LANES:(p + 1) * LANES]
            s = lax.dot_general(q, k_cat, (((1,), (1,)), ((), ())), preferred_element_type=F32)
            s0 = jnp.where(mask, s[:, :2 * blk], -jnp.inf)
            s1 = jnp.where(mask, s[:, 2 * blk:], -jnp.inf)
            sink0 = sink_ref[p]
            sink1 = sink_ref[p + N_Q_HEADS // N_KV_HEADS]
            m0 = jnp.maximum(jnp.max(s0, axis=-1, keepdims=True), sink0)
            m1 = jnp.maximum(jnp.max(s1, axis=-1, keepdims=True), sink1)
            pr = jnp.concatenate([jnp.exp(s0 - m0), jnp.exp(s1 - m1)], axis=1).astype(BF16)
            r = jnp.dot(pr, v_cat, preferred_element_type=F32)
            den = r[:, LANES:] + jnp.where(low, jnp.exp(sink0 - m0), jnp.exp(sink1 - m1))
            o_ref[i * blk:(i + 1) * blk, p * LANES:(p + 1) * LANES] = (r[:, :LANES] / den).astype(BF16)


def _attention(sinks, qa, kv, batch, seq):
    t = qa.shape[0]
    tq = TOKEN_TILE
    tiles = seq // tq
    per_tile = tq // ATTN_BLOCK
    row = lambda b, j: (b * tiles + j, 0)
    prev = lambda b, j: (jnp.maximum((b * tiles + j) * per_tile - 1, b * tiles * per_tile), 0)
    return pl.pallas_call(
        _attn_kernel,
        grid=(batch, tiles),
        in_specs=[pl.BlockSpec(memory_space=pltpu.SMEM),
                  pl.BlockSpec((tq, ATTN_Q_WIDTH), row),
                  pl.BlockSpec((tq, 2 * ATTN_KV_WIDTH), row),
                  pl.BlockSpec((ATTN_BLOCK, 2 * ATTN_KV_WIDTH), prev)],
        out_specs=pl.BlockSpec((tq, ATTN_Q_WIDTH), row),
        out_shape=jax.ShapeDtypeStruct((t, ATTN_Q_WIDTH), BF16),
        compiler_params=pltpu.CompilerParams(dimension_semantics=("arbitrary", "arbitrary")),
        name="attn",
    )(sinks, qa, kv, kv)


def _mlstm_kernel(qk_ref, v_ref, o_ref, if_ref, bif_ref, cw_ref, cb_ref, nw_ref, y_ref,
                  cn_ref, mrow_ref, mcol_ref, halo_ref, q_s, kt_s):
    ts = TOKEN_TILE
    L = MLSTM_CHUNK
    D = MLSTM_HEAD_DIM
    j = pl.program_id(1)

    @pl.when(j == 0)
    def _():
        cn_ref[...] = jnp.zeros_like(cn_ref)
        mrow_ref[...] = jnp.zeros_like(mrow_ref)
        mcol_ref[...] = jnp.zeros_like(mcol_ref)
        halo_ref[...] = jnp.zeros_like(halo_ref)

    row8 = lax.broadcasted_iota(jnp.int32, (SUBLANES, LANES), 0)
    for cb in range(2 * MLSTM_HEADS):
        cs = slice(cb * LANES, (cb + 1) * LANES)
        u = qk_ref[:, cs].astype(F32)
        hl = halo_ref[:, cs]
        acc = cb_ref[:, cs] + cw_ref[CONV_WIDTH - 1:CONV_WIDTH, cs] * u
        for sh in range(1, CONV_WIDTH):
            us = pltpu.roll(u, sh, axis=0)
            top = jnp.where(row8 < sh, pltpu.roll(hl, sh, axis=0), us[:SUBLANES])
            us = jnp.concatenate([top, us[SUBLANES:]], axis=0)
            acc = acc + cw_ref[CONV_WIDTH - 1 - sh:CONV_WIDTH - sh, cs] * us
        halo_ref[:, cs] = u[ts - SUBLANES:]
        a = acc * jax.nn.sigmoid(acc)
        if cb < MLSTM_HEADS:
            q_s[:, cs] = a.astype(BF16)
        else:
            a = a * (D ** -0.5)
            hs = slice((cb - MLSTM_HEADS) * D, (cb - MLSTM_HEADS + 1) * D)
            for c in range(ts // L):
                kt_s[hs, c * L:(c + 1) * L] = a[c * L:(c + 1) * L].T

    gates = if_ref[...] + bif_ref[...]
    logf_all = jax.nn.log_sigmoid(pltpu.roll(gates, LANES - MLSTM_HEADS, axis=1))

    ti = lax.broadcasted_iota(jnp.int32, (L, L), 0)
    si = lax.broadcasted_iota(jnp.int32, (L, L), 1)
    causal = si <= ti
    tri = causal.astype(BF16)
    ones_v = jnp.ones((L, LANES), BF16)

    for c in range(ts // L):
        rows = slice(c * L, (c + 1) * L)
        lf = logf_all[rows]
        lf_hi = lf.astype(BF16)
        rem = lf - lf_hi.astype(F32)
        lf_mid = rem.astype(BF16)
        lf_lo = (rem - lf_mid.astype(F32)).astype(BF16)
        b = (jnp.dot(tri, lf_hi, preferred_element_type=F32)
             + jnp.dot(tri, lf_mid, preferred_element_type=F32)
             + jnp.dot(tri, lf_lo, preferred_element_type=F32))
        r = b - gates[rows]
        pm = -r
        sh = 1
        while sh < L:
            pm = jnp.maximum(pm, jnp.where(ti >= sh, pltpu.roll(pm, sh, axis=0), -jnp.inf))
            sh *= 2
        m_row = mrow_ref[0:1, :]
        inter = b + m_row
        mt = jnp.maximum(inter, b + pm)
        e_col = b - mt
        w_inter = jnp.exp(inter - mt)
        e_mt = jnp.exp(-mt)
        b_last = b[L - 1:L, :]
        m_new_row = jnp.maximum(b_last + m_row, jnp.max(b_last - r, axis=0, keepdims=True))
        decay_row = jnp.exp(b_last + m_row - m_new_row)
        mrow_ref[...] = jnp.broadcast_to(m_new_row, mrow_ref.shape)
        b_t = b.T[:SUBLANES]
        r_t = r.T[:SUBLANES]
        m_col = mcol_ref[:, 0:1]
        bl_col = b_t[:, L - 1:L]
        a_t = bl_col - r_t
        m_new_col = jnp.maximum(bl_col + m_col, jnp.max(a_t, axis=1, keepdims=True))
        wk_t = jnp.exp(a_t - m_new_col)
        mcol_ref[...] = jnp.broadcast_to(m_new_col, mcol_ref.shape)

        for h in range(MLSTM_HEADS):
            hs = slice(h * D, (h + 1) * D)
            qh = q_s[rows, hs]
            kt = kt_s[hs, rows]
            s = jnp.dot(qh, kt.astype(BF16), preferred_element_type=F32)
            w_intra = jnp.where(causal, jnp.exp(e_col[:, h:h + 1] - r_t[h:h + 1, :]), 0.0)
            pr = (s * w_intra).astype(BF16)
            v_ext = jnp.concatenate([v_ref[rows, hs], ones_v], axis=1)
            cn = cn_ref[h]
            nd = (jnp.dot(pr, v_ext, preferred_element_type=F32)
                  + w_inter[:, h:h + 1] * jnp.dot(qh, cn.astype(BF16), preferred_element_type=F32))
            hh = nd[:, :D] / jnp.maximum(jnp.abs(nd[:, D:]), e_mt[:, h:h + 1])
            kw = (kt * wk_t[h:h + 1, :]).astype(BF16)
            cn_ref[h] = decay_row[:, h:h + 1] * cn + jnp.dot(kw, v_ext, preferred_element_type=F32)
            mu = jnp.mean(hh, axis=-1, keepdims=True)
            xc = hh - mu
            var = jnp.mean(xc * xc, axis=-1, keepdims=True)
            yn = (xc * lax.rsqrt(var + NORM_EPS)) * nw_ref[:, hs]
            y_ref[rows, hs] = (jax.nn.sigmoid(o_ref[rows, hs].astype(F32)) * yn).astype(BF16)


def _mlstm(qk, vm, om, gates, bif, conv_w, conv_b, norm_w, batch, seq):
    t = qk.shape[0]
    ts = TOKEN_TILE
    tiles = seq // ts
    row = lambda b, j: (b * tiles + j, 0)
    const = lambda b, j: (0, 0)
    return pl.pallas_call(
        _mlstm_kernel,
        grid=(batch, tiles),
        in_specs=[pl.BlockSpec((ts, 2 * MLSTM_WIDTH), row),
                  pl.BlockSpec((ts, MLSTM_WIDTH), row),
                  pl.BlockSpec((ts, MLSTM_WIDTH), row),
                  pl.BlockSpec((ts, LANES), row),
                  pl.BlockSpec((1, LANES), const),
                  pl.BlockSpec((CONV_WIDTH, 2 * MLSTM_WIDTH), const),
                  pl.BlockSpec((1, 2 * MLSTM_WIDTH), const),
                  pl.BlockSpec((1, MLSTM_WIDTH), const)],
        out_specs=pl.BlockSpec((ts, MLSTM_WIDTH), row),
        out_shape=jax.ShapeDtypeStruct((t, MLSTM_WIDTH), BF16),
        scratch_shapes=[pltpu.VMEM((MLSTM_HEADS, MLSTM_HEAD_DIM, 2 * MLSTM_HEAD_DIM), F32),
                        pltpu.VMEM((SUBLANES, LANES), F32),
                        pltpu.VMEM((SUBLANES, LANES), F32),
                        pltpu.VMEM((SUBLANES, 2 * MLSTM_WIDTH), F32),
                        pltpu.VMEM((ts, MLSTM_WIDTH), BF16),
                        pltpu.VMEM((MLSTM_WIDTH, ts), F32)],
        compiler_params=pltpu.CompilerParams(dimension_semantics=("arbitrary", "arbitrary")),
        name="mlstm",
    )(qk, vm, om, gates, bif, conv_w, conv_b, norm_w)


def _out_kernel(x_ref, ya_ref, ym_ref, ga_ref, gm_ref, mod_ref, gpm_ref, gqf_ref, gpf_ref,
                wa_ref, wb_ref, wo_ref, wg_ref, wu_ref, wd_ref, o_ref, h2_s, acc_s):
    gate_m = mod_ref[0, 2:3, :]
    shift_f = mod_ref[0, 3:4, :]
    scale_f = mod_ref[0, 4:5, :]
    gate_f = mod_ref[0, 5:6, :]
    ba = jnp.dot(ya_ref[...], wa_ref[...], preferred_element_type=F32)
    bb = jnp.dot(ym_ref[...], wb_ref[...], preferred_element_type=F32)
    merged = (jax.nn.sigmoid(ga_ref[...].astype(F32)) * ba
              + jax.nn.sigmoid(gm_ref[...].astype(F32)) * bb)
    mix = jnp.dot(merged.astype(BF16), wo_ref[...], preferred_element_type=F32)
    x1 = x_ref[...] + gate_m * _rms(mix, gpm_ref[...])
    h2_s[...] = (_rms(x1, gqf_ref[...]) * (1.0 + scale_f) + shift_f).astype(BF16)
    acc_s[...] = jnp.zeros_like(acc_s)

    def ff_step(c, carry):
        h2 = h2_s[...]
        g = jnp.dot(h2, wg_ref[c], preferred_element_type=F32)
        u = jnp.dot(h2, wu_ref[c], preferred_element_type=F32)
        act = ((g * jax.nn.sigmoid(g)) * u).astype(BF16)
        acc_s[...] += jnp.dot(act, wd_ref[c], preferred_element_type=F32)
        return carry

    lax.fori_loop(0, D_FF // FF_CHUNK, ff_step, 0)
    o_ref[...] = x1 + gate_f * _rms(acc_s[...], gpf_ref[...])


def _out(x2, ya, ym, ga, gm, mod3, g_post_mix, g_pre_ffn, g_post_ffn, wa, wb, wo, wg, wu, wd,
         tiles_per_batch):
    t = x2.shape[0]
    tm = TOKEN_TILE
    row = lambda i: (i, 0)
    const = lambda i: (0, 0)
    const3 = lambda i: (0, 0, 0)
    once = pl.Buffered(1)
    weight_bytes = 2 * (wa.size + wb.size + wo.size + wg.size + wu.size + wd.size)
    est = (weight_bytes + 2 * 2 * tm * D_MODEL * 4 + 2 * tm * (2 * MLSTM_WIDTH + 2 * D_MODEL) * 2
           + tm * D_MODEL * 6 + 8 * tm * D_MODEL * 4)
    return pl.pallas_call(
        _out_kernel,
        grid=(t // tm,),
        in_specs=[pl.BlockSpec((tm, D_MODEL), row),
                  pl.BlockSpec((tm, ATTN_Q_WIDTH), row),
                  pl.BlockSpec((tm, MLSTM_WIDTH), row),
                  pl.BlockSpec((tm, D_MODEL), row),
                  pl.BlockSpec((tm, D_MODEL), row),
                  pl.BlockSpec((1, 6, D_MODEL), lambda i: (i // tiles_per_batch, 0, 0)),
                  pl.BlockSpec((1, D_MODEL), const),
                  pl.BlockSpec((1, D_MODEL), const),
                  pl.BlockSpec((1, D_MODEL), const),
                  pl.BlockSpec(wa.shape, const, pipeline_mode=once),
                  pl.BlockSpec(wb.shape, const, pipeline_mode=once),
                  pl.BlockSpec(wo.shape, const, pipeline_mode=once),
                  pl.BlockSpec(wg.shape, const3, pipeline_mode=once),
                  pl.BlockSpec(wu.shape, const3, pipeline_mode=once),
                  pl.BlockSpec(wd.shape, const3, pipeline_mode=once)],
        out_specs=pl.BlockSpec((tm, D_MODEL), row),
        out_shape=jax.ShapeDtypeStruct((t, D_MODEL), F32),
        scratch_shapes=[pltpu.VMEM((tm, D_MODEL), BF16), pltpu.VMEM((tm, D_MODEL), F32)],
        compiler_params=pltpu.CompilerParams(dimension_semantics=("arbitrary",),
                                             vmem_limit_bytes=_vmem_limit(est)),
        name="out",
    )(x2, ya, ym, ga, gm, mod3, g_post_mix, g_pre_ffn, g_post_ffn, wa, wb, wo, wg, wu, wd)


def _q_head_order():
    per_kv = N_Q_HEADS // N_KV_HEADS
    return [h for p in range(per_kv) for h in (p, p + per_kv)]


def _layout_w_in(w_in):
    splits = np.cumsum([ATTN_Q_WIDTH, ATTN_KV_WIDTH, ATTN_KV_WIDTH, MLSTM_WIDTH, MLSTM_WIDTH,
                        MLSTM_WIDTH, MLSTM_WIDTH, MLSTM_HEADS, MLSTM_HEADS, D_MODEL]).tolist()
    q_a, k_a, v_a, q_m, k_m, v_m, o_m, i_m, f_m, g_a, g_m = jnp.split(w_in, splits, axis=1)
    q_a = q_a.reshape(D_MODEL, N_Q_HEADS, HEAD_DIM)[:, np.array(_q_head_order())].reshape(D_MODEL, -1)
    pad = jnp.zeros((D_MODEL, LANES - 2 * MLSTM_HEADS), w_in.dtype)
    return jnp.concatenate([q_a, k_a, v_a, q_m, k_m, v_m, o_m, i_m, f_m, pad, g_a, g_m],
                           axis=1).astype(BF16)


def kernel(x, c, positions, w_ada, b_ada, g_pre_mix, g_post_mix, w_in, b_if, conv_w, conv_b,
           attn_sinks, mlstm_norm_w, w_branch_attn, w_branch_mlstm, w_out, g_pre_ffn, g_post_ffn,
           w_ffn_gate, w_ffn_up, w_ffn_down):
    batch, seq, d = x.shape
    depth = w_in.shape[0]
    assert d == D_MODEL and seq % TOKEN_TILE == 0 and (batch * seq) % ROPE_TILE == 0
    assert D_FF % FF_CHUNK == 0
    t = batch * seq
    tiles_per_batch = seq // TOKEN_TILE
    x2 = x.reshape(t, d)

    inv_freq = (ROPE_THETA ** (-2.0 * jnp.arange(HEAD_DIM // 2, dtype=F32) / HEAD_DIM)).reshape(-1, 1)
    cos_t, sin_t = _rope_tables(positions.reshape(1, t), inv_freq)
    c_pad = jnp.pad(c, ((0, SUBLANES - batch % SUBLANES), (0, 0))) if batch % SUBLANES else c
    head_order = np.array(_q_head_order())

    for l in range(depth):
        mod = _ada(c_pad, w_ada[l], b_ada[l].reshape(1, -1))[:batch]
        mod3 = mod.reshape(batch, 6, d)
        qa, kv, qk, vm, om, gates, ga, gm = _inproj(
            x2, mod3, g_pre_mix[l].reshape(1, d), cos_t, sin_t, _layout_w_in(w_in[l]), tiles_per_batch)
        ya = _attention(attn_sinks[l], qa, kv, batch, seq)
        bif = jnp.pad(b_if[l], (0, LANES - 2 * MLSTM_HEADS)).reshape(1, LANES)
        ym = _mlstm(qk, vm, om, gates, bif, conv_w[l], conv_b[l].reshape(1, -1),
                    mlstm_norm_w[l].reshape(1, -1), batch, seq)
        wa = w_branch_attn[l].reshape(N_Q_HEADS, HEAD_DIM, d)[head_order].reshape(ATTN_Q_WIDTH, d)
        n_ff = D_FF // FF_CHUNK
        wg = w_ffn_gate[l].reshape(d, n_ff, FF_CHUNK).transpose(1, 0, 2).astype(BF16)
        wu = w_ffn_up[l].reshape(d, n_ff, FF_CHUNK).transpose(1, 0, 2).astype(BF16)
        wd = w_ffn_down[l].reshape(n_ff, FF_CHUNK, d).astype(BF16)
        x2 = _out(x2, ya, ym, ga, gm, mod3, g_post_mix[l].reshape(1, d), g_pre_ffn[l].reshape(1, d),
                  g_post_ffn[l].reshape(1, d), wa.astype(BF16), w_branch_mlstm[l].astype(BF16),
                  w_out[l].astype(BF16), wg, wu, wd, tiles_per_batch)
    return x2.reshape(batch, seq, d)
```

```python
import functools

import numpy as np
import jax
import jax.numpy as jnp
from jax import lax
from jax.experimental import pallas as pl
from jax.experimental.pallas import tpu as pltpu

F32 = jnp.float32
BF16 = jnp.bfloat16

D_MODEL = 1024
N_Q_HEADS = 8
N_KV_HEADS = 2
HEAD_DIM = 64
ROPE_THETA = 10000.0
MLSTM_HEADS = 4
MLSTM_HEAD_DIM = 128
CONV_WIDTH = 4
D_FF = 2816
NORM_EPS = 1e-6
ATTN_Q_WIDTH = N_Q_HEADS * HEAD_DIM
ATTN_KV_WIDTH = N_KV_HEADS * HEAD_DIM
MLSTM_WIDTH = MLSTM_HEADS * MLSTM_HEAD_DIM
ATTN_BLOCK = 128

LANES = 128
SUBLANES = 8
MXU_WIDTH = 256
V7X_SCOPED_VMEM_BYTES = 60000 * 1024

TOKEN_TILE = 512
MLSTM_CHUNK = 128
FF_CHUNK = MXU_WIDTH
ROPE_TILE = 2048

C_QA = 0
C_KV = C_QA + ATTN_Q_WIDTH
C_QK = C_KV + 2 * ATTN_KV_WIDTH
C_VM = C_QK + 2 * MLSTM_WIDTH
C_OM = C_VM + MLSTM_WIDTH
C_IF = C_OM + MLSTM_WIDTH
C_GA = C_IF + LANES
C_GM = C_GA + D_MODEL
IN_COLS = C_GM + D_MODEL


def _vmem_limit(estimate_bytes):
    return int(min(V7X_SCOPED_VMEM_BYTES, max(estimate_bytes, 16 * 1024 * 1024)))


def _rms(x, g):
    return (x * lax.rsqrt(jnp.mean(x * x, axis=-1, keepdims=True) + NORM_EPS)) * g


def _ada_kernel(c_ref, w_ref, b_ref, o_ref):
    o_ref[...] = jnp.dot(c_ref[...], w_ref[...], preferred_element_type=F32,
                         precision=lax.Precision.HIGHEST) + b_ref[...]


def _ada(c_pad, w_ada, b_ada):
    rows = c_pad.shape[0]
    n = w_ada.shape[1]
    return pl.pallas_call(
        _ada_kernel,
        grid=(n // D_MODEL,),
        in_specs=[pl.BlockSpec((rows, D_MODEL), lambda j: (0, 0)),
                  pl.BlockSpec((D_MODEL, D_MODEL), lambda j: (0, j)),
                  pl.BlockSpec((1, D_MODEL), lambda j: (0, j))],
        out_specs=pl.BlockSpec((rows, D_MODEL), lambda j: (0, j)),
        out_shape=jax.ShapeDtypeStruct((rows, n), F32),
        name="ada",
    )(c_pad, w_ada, b_ada)


def _rope_kernel(pos_ref, freq_ref, cos_ref, sin_ref):
    ang = freq_ref[...] * pos_ref[...].astype(F32)
    c = jnp.cos(ang)
    s = jnp.sin(ang)
    cos_ref[...] = jnp.concatenate([c, c, c, c], axis=0).T
    sin_ref[...] = jnp.concatenate([-s, s, -s, s], axis=0).T


def _rope_tables(pos_row, inv_freq):
    t = pos_row.shape[1]
    half = inv_freq.shape[0]
    return pl.pallas_call(
        _rope_kernel,
        grid=(t // ROPE_TILE,),
        in_specs=[pl.BlockSpec((1, ROPE_TILE), lambda i: (0, i)),
                  pl.BlockSpec((half, 1), lambda i: (0, 0))],
        out_specs=[pl.BlockSpec((ROPE_TILE, LANES), lambda i: (i, 0)),
                   pl.BlockSpec((ROPE_TILE, LANES), lambda i: (i, 0))],
        out_shape=[jax.ShapeDtypeStruct((t, LANES), F32),
                   jax.ShapeDtypeStruct((t, LANES), F32)],
        name="rope",
    )(pos_row, inv_freq)


def _inproj_kernel(tiles_per_batch, x_ref, mod_ref, g_ref, cos_ref, sin_ref, w_ref, cw_ref, cb_ref,
                   qa_ref, kv_ref, qm_ref, kt_ref, vm_ref, om_ref, if_ref, ga_ref, gm_ref, raw_s, h_s):
    tm = TOKEN_TILE
    x = x_ref[...]
    shift = mod_ref[0, 0:1, :]
    scale = mod_ref[0, 1:2, :]
    h_s[...] = (_rms(x, g_ref[...]) * (1.0 + scale) + shift).astype(BF16)

    def proj(lo, width):
        return jnp.dot(h_s[...], w_ref[:, lo:lo + width], preferred_element_type=F32)

    cos = cos_ref[...]
    sin = sin_ref[...]
    lane = lax.broadcasted_iota(jnp.int32, (1, LANES), 1)
    first_half = (lane % HEAD_DIM) < (HEAD_DIM // 2)

    def rope(t):
        swapped = jnp.where(first_half, pltpu.roll(t, LANES - HEAD_DIM // 2, axis=1),
                            pltpu.roll(t, HEAD_DIM // 2, axis=1))
        return t * cos + swapped * sin

    q = proj(C_QA, ATTN_Q_WIDTH)
    for p in range(ATTN_Q_WIDTH // LANES):
        sl = slice(p * LANES, (p + 1) * LANES)
        qa_ref[:, sl] = (rope(q[:, sl]) * (HEAD_DIM ** -0.5)).astype(BF16)
    kv = proj(C_KV, 2 * ATTN_KV_WIDTH)
    kv_ref[:, :LANES] = rope(kv[:, :LANES]).astype(BF16)
    kv_ref[:, LANES:] = kv[:, LANES:].astype(BF16)
    @pl.when(pl.program_id(0) % tiles_per_batch == 0)
    def _():
        raw_s[0:SUBLANES, :] = jnp.zeros((SUBLANES, raw_s.shape[1]), F32)

    blk = MXU_WIDTH
    for cbk in range(2 * MLSTM_WIDTH // blk):
        cs = slice(cbk * blk, (cbk + 1) * blk)
        raw = proj(C_QK + cbk * blk, blk)
        raw_s[SUBLANES:, cs] = raw
        acc = cb_ref[:, cs] + cw_ref[CONV_WIDTH - 1:CONV_WIDTH, cs] * raw
        for sh in range(1, CONV_WIDTH):
            acc = acc + (cw_ref[CONV_WIDTH - 1 - sh:CONV_WIDTH - sh, cs]
                         * raw_s[SUBLANES - sh:SUBLANES - sh + tm, cs])
        raw_s[0:SUBLANES, cs] = raw[tm - SUBLANES:]
        a = acc * jax.nn.sigmoid(acc)
        if cbk * blk < MLSTM_WIDTH:
            qm_ref[:, cs] = a.astype(BF16)
        else:
            a = a * (MLSTM_HEAD_DIM ** -0.5)
            for d0 in range(0, blk, LANES):
                r0 = cbk * blk - MLSTM_WIDTH + d0
                for t0 in range(0, tm, LANES):
                    kt_ref[r0:r0 + LANES, t0:t0 + LANES] = a[t0:t0 + LANES, d0:d0 + LANES].T.astype(BF16)
    vm_ref[...] = proj(C_VM, MLSTM_WIDTH).astype(BF16)
    om_ref[...] = jax.nn.sigmoid(proj(C_OM, MLSTM_WIDTH)).astype(BF16)
    if_ref[...] = proj(C_IF, LANES)
    ga_ref[...] = jax.nn.sigmoid(proj(C_GA, D_MODEL)).astype(BF16)
    gm_ref[...] = jax.nn.sigmoid(proj(C_GM, D_MODEL)).astype(BF16)


def _inproj(x2, mod3, g_pre, cos_t, sin_t, w_all, conv_w, conv_b, tiles_per_batch):
    t = x2.shape[0]
    tm = TOKEN_TILE
    row = lambda i: (i, 0)
    const = lambda i: (0, 0)
    widths = [ATTN_Q_WIDTH, 2 * ATTN_KV_WIDTH, MLSTM_WIDTH, None, MLSTM_WIDTH, MLSTM_WIDTH,
              LANES, D_MODEL, D_MODEL]
    dtypes = [BF16, BF16, BF16, BF16, BF16, BF16, F32, BF16, BF16]
    out_specs = [pl.BlockSpec((MLSTM_WIDTH, tm), lambda i: (0, i)) if w is None
                 else pl.BlockSpec((tm, w), row) for w in widths]
    out_shape = [jax.ShapeDtypeStruct((MLSTM_WIDTH, t) if w is None else (t, w), d)
                 for w, d in zip(widths, dtypes)]
    out_bytes = sum((w or MLSTM_WIDTH) * np.dtype(d).itemsize for w, d in zip(widths, dtypes)) * tm
    est = (2 * w_all.size * 2 + 2 * tm * D_MODEL * 4 + 2 * out_bytes
           + 4 * tm * LANES * 4 + 8 * tm * D_MODEL * 4)
    return pl.pallas_call(
        functools.partial(_inproj_kernel, tiles_per_batch),
        grid=(t // tm,),
        in_specs=[pl.BlockSpec((tm, D_MODEL), row),
                  pl.BlockSpec((1, 6, D_MODEL), lambda i: (i // tiles_per_batch, 0, 0)),
                  pl.BlockSpec((1, D_MODEL), const),
                  pl.BlockSpec((tm, LANES), row),
                  pl.BlockSpec((tm, LANES), row),
                  pl.BlockSpec((D_MODEL, IN_COLS), const),
                  pl.BlockSpec((CONV_WIDTH, 2 * MLSTM_WIDTH), const),
                  pl.BlockSpec((1, 2 * MLSTM_WIDTH), const)],
        out_specs=out_specs,
        out_shape=out_shape,
        scratch_shapes=[pltpu.VMEM((tm + SUBLANES, 2 * MLSTM_WIDTH), F32),
                        pltpu.VMEM((tm, D_MODEL), BF16)],
        compiler_params=pltpu.CompilerParams(dimension_semantics=("arbitrary",),
                                             vmem_limit_bytes=_vmem_limit(est)),
        name="inproj",
    )(x2, mod3, g_pre, cos_t, sin_t, w_all, conv_w, conv_b)


def _attn_kernel(sink_ref, q_ref, kv_ref, kvp_ref, o_ref):
    j = pl.program_id(1)
    blk = ATTN_BLOCK
    nblk = TOKEN_TILE // blk
    lane = lax.broadcasted_iota(jnp.int32, (1, LANES), 1)
    low = lane < HEAD_DIM
    qi = lax.broadcasted_iota(jnp.int32, (blk, 2 * blk), 0)
    kj = lax.broadcasted_iota(jnp.int32, (blk, 2 * blk), 1)
    in_cur = (kj >= blk) & (kj - blk <= qi)
    in_prev = (kj < blk) & (kj > qi)
    in_prev_first = (kj < blk) & (kj > qi + jnp.where(j > 0, 0, blk))
    ones = jnp.ones((2 * blk, LANES), F32)
    zeros = jnp.zeros((2 * blk, LANES), F32)
    one_lo = jnp.where(low, ones, zeros).astype(BF16)
    one_hi = jnp.where(low, zeros, ones).astype(BF16)

    for i in range(nblk):
        cur = kv_ref[i * blk:(i + 1) * blk, :]
        prev = kvp_ref[...] if i == 0 else kv_ref[(i - 1) * blk:i * blk, :]
        band = jnp.concatenate([prev, cur], axis=0).astype(F32)
        k = band[:, :LANES]
        v = band[:, LANES:]
        k_cat = jnp.concatenate([jnp.where(low, k, 0.0), jnp.where(low, 0.0, k)], axis=0).astype(BF16)
        v_cat = jnp.concatenate(
            [jnp.concatenate([jnp.where(low, v, 0.0).astype(BF16), one_lo], axis=1),
             jnp.concatenate([jnp.where(low, 0.0, v).astype(BF16), one_hi], axis=1)], axis=0)
        mask = in_cur | (in_prev_first if i == 0 else in_prev)
        for p in range(ATTN_Q_WIDTH // LANES):
            q = q_ref[i * blk:(i + 1) * blk, p * LANES:(p + 1) * LANES]
            s = lax.dot_general(q, k_cat, (((1,), (1,)), ((), ())), preferred_element_type=F32)
            s0 = jnp.where(mask, s[:, :2 * blk], -jnp.inf)
            s1 = jnp.where(mask, s[:, 2 * blk:], -jnp.inf)
            sink0 = sink_ref[p]
            sink1 = sink_ref[p + N_Q_HEADS // N_KV_HEADS]
            m0 = jnp.maximum(jnp.max(s0, axis=-1, keepdims=True), sink0)
            m1 = jnp.maximum(jnp.max(s1, axis=-1, keepdims=True), sink1)
            pr = jnp.concatenate([jnp.exp(s0 - m0), jnp.exp(s1 - m1)], axis=1).astype(BF16)
            r = jnp.dot(pr, v_cat, preferred_element_type=F32)
            den = r[:, LANES:] + jnp.where(low, jnp.exp(sink0 - m0), jnp.exp(sink1 - m1))
            o_ref[i * blk:(i + 1) * blk, p * LANES:(p + 1) * LANES] = (r[:, :LANES] / den).astype(BF16)


def _attention(sinks, qa, kv, batch, seq):
    t = qa.shape[0]
    tq = TOKEN_TILE
    tiles = seq // tq
    per_tile = tq // ATTN_BLOCK
    row = lambda b, j: (b * tiles + j, 0)
    prev = lambda b, j: (jnp.maximum((b * tiles + j) * per_tile - 1, b * tiles * per_tile), 0)
    return pl.pallas_call(
        _attn_kernel,
        grid=(batch, tiles),
        in_specs=[pl.BlockSpec(memory_space=pltpu.SMEM),
                  pl.BlockSpec((tq, ATTN_Q_WIDTH), row),
                  pl.BlockSpec((tq, 2 * ATTN_KV_WIDTH), row),
                  pl.BlockSpec((ATTN_BLOCK, 2 * ATTN_KV_WIDTH), prev)],
        out_specs=pl.BlockSpec((tq, ATTN_Q_WIDTH), row),
        out_shape=jax.ShapeDtypeStruct((t, ATTN_Q_WIDTH), BF16),
        compiler_params=pltpu.CompilerParams(dimension_semantics=("arbitrary", "arbitrary")),
        name="attn",
    )(sinks, qa, kv, kv)


def _mlstm_kernel(q_ref, kt_ref, v_ref, og_ref, if_ref, bif_ref, nw_ref, y_ref,
                  cn_ref, mrow_ref, mcol_ref):
    ts = TOKEN_TILE
    L = MLSTM_CHUNK
    D = MLSTM_HEAD_DIM
    j = pl.program_id(1)

    @pl.when(j == 0)
    def _():
        cn_ref[...] = jnp.zeros_like(cn_ref)
        mrow_ref[...] = jnp.zeros_like(mrow_ref)
        mcol_ref[...] = jnp.zeros_like(mcol_ref)

    gates = if_ref[...] + bif_ref[...]
    logf_all = jax.nn.log_sigmoid(pltpu.roll(gates, LANES - MLSTM_HEADS, axis=1))

    ti = lax.broadcasted_iota(jnp.int32, (L, L), 0)
    si = lax.broadcasted_iota(jnp.int32, (L, L), 1)
    causal = si <= ti
    tri = causal.astype(BF16)
    ones_v = jnp.ones((L, LANES), BF16)

    for c in range(ts // L):
        rows = slice(c * L, (c + 1) * L)
        lf = logf_all[rows]
        lf_hi = lf.astype(BF16)
        rem = lf - lf_hi.astype(F32)
        lf_mid = rem.astype(BF16)
        lf_lo = (rem - lf_mid.astype(F32)).astype(BF16)
        b = (jnp.dot(tri, lf_hi, preferred_element_type=F32)
             + jnp.dot(tri, lf_mid, preferred_element_type=F32)
             + jnp.dot(tri, lf_lo, preferred_element_type=F32))
        r = b - gates[rows]
        pm = -r
        sh = 1
        while sh < L:
            pm = jnp.maximum(pm, jnp.where(ti >= sh, pltpu.roll(pm, sh, axis=0), -jnp.inf))
            sh *= 2
        m_row = mrow_ref[0:1, :]
        inter = b + m_row
        mt = jnp.maximum(inter, b + pm)
        e_col = b - mt
        w_inter = jnp.exp(inter - mt)
        e_mt = jnp.exp(-mt)
        b_last = b[L - 1:L, :]
        m_new_row = jnp.maximum(b_last + m_row, jnp.max(b_last - r, axis=0, keepdims=True))
        decay_row = jnp.exp(b_last + m_row - m_new_row)
        mrow_ref[...] = jnp.broadcast_to(m_new_row, mrow_ref.shape)
        b_t = b.T[:SUBLANES]
        r_t = r.T[:SUBLANES]
        m_col = mcol_ref[:, 0:1]
        bl_col = b_t[:, L - 1:L]
        a_t = bl_col - r_t
        m_new_col = jnp.maximum(bl_col + m_col, jnp.max(a_t, axis=1, keepdims=True))
        wk_t = jnp.exp(a_t - m_new_col)
        mcol_ref[...] = jnp.broadcast_to(m_new_col, mcol_ref.shape)

        for h in range(MLSTM_HEADS):
            hs = slice(h * D, (h + 1) * D)
            qh = q_ref[rows, hs]
            kt = kt_ref[hs, rows]
            s = jnp.dot(qh, kt, preferred_element_type=F32)
            w_intra = jnp.where(causal, jnp.exp(e_col[:, h:h + 1] - r_t[h:h + 1, :]), 0.0)
            pr = (s * w_intra).astype(BF16)
            v_ext = jnp.concatenate([v_ref[rows, hs], ones_v], axis=1)
            cn = cn_ref[h]
            nd = (jnp.dot(pr, v_ext, preferred_element_type=F32)
                  + w_inter[:, h:h + 1] * jnp.dot(qh, cn.astype(BF16), preferred_element_type=F32))
            hh = nd[:, :D] / jnp.maximum(jnp.abs(nd[:, D:]), e_mt[:, h:h + 1])
            kw = (kt.astype(F32) * wk_t[h:h + 1, :]).astype(BF16)
            cn_ref[h] = decay_row[:, h:h + 1] * cn + jnp.dot(kw, v_ext, preferred_element_type=F32)
            mu = jnp.mean(hh, axis=-1, keepdims=True)
            xc = hh - mu
            var = jnp.mean(xc * xc, axis=-1, keepdims=True)
            yn = (xc * lax.rsqrt(var + NORM_EPS)) * nw_ref[:, hs]
            y_ref[rows, hs] = (og_ref[rows, hs].astype(F32) * yn).astype(BF16)


def _mlstm(qm, kt, vm, og, gates, bif, norm_w, batch, seq):
    t = qm.shape[0]
    ts = TOKEN_TILE
    tiles = seq // ts
    row = lambda b, j: (b * tiles + j, 0)
    col = lambda b, j: (0, b * tiles + j)
    const = lambda b, j: (0, 0)
    return pl.pallas_call(
        _mlstm_kernel,
        grid=(batch, tiles),
        in_specs=[pl.BlockSpec((ts, MLSTM_WIDTH), row),
                  pl.BlockSpec((MLSTM_WIDTH, ts), col),
                  pl.BlockSpec((ts, MLSTM_WIDTH), row),
                  pl.BlockSpec((ts, MLSTM_WIDTH), row),
                  pl.BlockSpec((ts, LANES), row),
                  pl.BlockSpec((1, LANES), const),
                  pl.BlockSpec((1, MLSTM_WIDTH), const)],
        out_specs=pl.BlockSpec((ts, MLSTM_WIDTH), row),
        out_shape=jax.ShapeDtypeStruct((t, MLSTM_WIDTH), BF16),
        scratch_shapes=[pltpu.VMEM((MLSTM_HEADS, MLSTM_HEAD_DIM, 2 * MLSTM_HEAD_DIM), F32),
                        pltpu.VMEM((SUBLANES, LANES), F32),
                        pltpu.VMEM((SUBLANES, LANES), F32)],
        compiler_params=pltpu.CompilerParams(dimension_semantics=("arbitrary", "arbitrary")),
        name="mlstm",
    )(qm, kt, vm, og, gates, bif, norm_w)


def _out_kernel(x_ref, ya_ref, ym_ref, ga_ref, gm_ref, mod_ref, gpm_ref, gqf_ref, gpf_ref,
                wa_ref, wb_ref, wo_ref, wg_ref, wu_ref, wd_ref, o_ref, act_s):
    gate_m = mod_ref[0, 2:3, :]
    shift_f = mod_ref[0, 3:4, :]
    scale_f = mod_ref[0, 4:5, :]
    gate_f = mod_ref[0, 5:6, :]
    ba = jnp.dot(ya_ref[...], wa_ref[...], preferred_element_type=F32)
    bb = jnp.dot(ym_ref[...], wb_ref[...], preferred_element_type=F32)
    merged = ga_ref[...].astype(F32) * ba + gm_ref[...].astype(F32) * bb
    mix = jnp.dot(merged.astype(BF16), wo_ref[...], preferred_element_type=F32)
    x1 = x_ref[...] + gate_m * _rms(mix, gpm_ref[...])
    h2 = (_rms(x1, gqf_ref[...]) * (1.0 + scale_f) + shift_f).astype(BF16)
    for c in range(D_FF // FF_CHUNK):
        cs = slice(c * FF_CHUNK, (c + 1) * FF_CHUNK)
        g = jnp.dot(h2, wg_ref[:, cs], preferred_element_type=F32)
        u = jnp.dot(h2, wu_ref[:, cs], preferred_element_type=F32)
        act_s[:, cs] = ((g * jax.nn.sigmoid(g)) * u).astype(BF16)
    ff = jnp.dot(act_s[...], wd_ref[...], preferred_element_type=F32)
    o_ref[...] = x1 + gate_f * _rms(ff, gpf_ref[...])


def _out(x2, ya, ym, ga, gm, mod3, g_post_mix, g_pre_ffn, g_post_ffn, wa, wb, wo, wg, wu, wd,
         tiles_per_batch):
    t = x2.shape[0]
    tm = TOKEN_TILE
    row = lambda i: (i, 0)
    const = lambda i: (0, 0)
    once = pl.Buffered(1)
    weight_bytes = 2 * (wa.size + wb.size + wo.size + wg.size + wu.size + wd.size)
    est = (weight_bytes + 2 * 2 * tm * D_MODEL * 4 + 2 * tm * (2 * MLSTM_WIDTH + 2 * D_MODEL) * 2
           + tm * D_FF * 2 + 8 * tm * D_MODEL * 4)
    return pl.pallas_call(
        _out_kernel,
        grid=(t // tm,),
        in_specs=[pl.BlockSpec((tm, D_MODEL), row),
                  pl.BlockSpec((tm, ATTN_Q_WIDTH), row),
                  pl.BlockSpec((tm, MLSTM_WIDTH), row),
                  pl.BlockSpec((tm, D_MODEL), row),
                  pl.BlockSpec((tm, D_MODEL), row),
                  pl.BlockSpec((1, 6, D_MODEL), lambda i: (i // tiles_per_batch, 0, 0)),
                  pl.BlockSpec((1, D_MODEL), const),
                  pl.BlockSpec((1, D_MODEL), const),
                  pl.BlockSpec((1, D_MODEL), const),
                  pl.BlockSpec(wa.shape, const, pipeline_mode=once),
                  pl.BlockSpec(wb.shape, const, pipeline_mode=once),
                  pl.BlockSpec(wo.shape, const, pipeline_mode=once),
                  pl.BlockSpec(wg.shape, const, pipeline_mode=once),
                  pl.BlockSpec(wu.shape, const, pipeline_mode=once),
                  pl.BlockSpec(wd.shape, const, pipeline_mode=once)],
        out_specs=pl.BlockSpec((tm, D_MODEL), row),
        out_shape=jax.ShapeDtypeStruct((t, D_MODEL), F32),
        scratch_shapes=[pltpu.VMEM((tm, D_FF), BF16)],
        compiler_params=pltpu.CompilerParams(dimension_semantics=("arbitrary",),
                                             vmem_limit_bytes=_vmem_limit(est)),
        name="out",
    )(x2, ya, ym, ga, gm, mod3, g_post_mix, g_pre_ffn, g_post_ffn, wa, wb, wo, wg, wu, wd)


def _q_head_order():
    per_kv = N_Q_HEADS // N_KV_HEADS
    return [h for p in range(per_kv) for h in (p, p + per_kv)]


def _layout_w_in(w_in):
    splits = np.cumsum([ATTN_Q_WIDTH, ATTN_KV_WIDTH, ATTN_KV_WIDTH, MLSTM_WIDTH, MLSTM_WIDTH,
                        MLSTM_WIDTH, MLSTM_WIDTH, MLSTM_HEADS, MLSTM_HEADS, D_MODEL]).tolist()
    q_a, k_a, v_a, q_m, k_m, v_m, o_m, i_m, f_m, g_a, g_m = jnp.split(w_in, splits, axis=1)
    q_a = q_a.reshape(D_MODEL, N_Q_HEADS, HEAD_DIM)[:, np.array(_q_head_order())].reshape(D_MODEL, -1)
    pad = jnp.zeros((D_MODEL, LANES - 2 * MLSTM_HEADS), w_in.dtype)
    return jnp.concatenate([q_a, k_a, v_a, q_m, k_m, v_m, o_m, i_m, f_m, pad, g_a, g_m],
                           axis=1).astype(BF16)


def kernel(x, c, positions, w_ada, b_ada, g_pre_mix, g_post_mix, w_in, b_if, conv_w, conv_b,
           attn_sinks, mlstm_norm_w, w_branch_attn, w_branch_mlstm, w_out, g_pre_ffn, g_post_ffn,
           w_ffn_gate, w_ffn_up, w_ffn_down):
    batch, seq, d = x.shape
    depth = w_in.shape[0]
    assert d == D_MODEL and seq % TOKEN_TILE == 0 and (batch * seq) % ROPE_TILE == 0
    assert D_FF % FF_CHUNK == 0
    t = batch * seq
    tiles_per_batch = seq // TOKEN_TILE
    x2 = x.reshape(t, d)

    inv_freq = (ROPE_THETA ** (-2.0 * jnp.arange(HEAD_DIM // 2, dtype=F32) / HEAD_DIM)).reshape(-1, 1)
    cos_t, sin_t = _rope_tables(positions.reshape(1, t), inv_freq)
    c_pad = jnp.pad(c, ((0, SUBLANES - batch % SUBLANES), (0, 0))) if batch % SUBLANES else c
    head_order = np.array(_q_head_order())

    for l in range(depth):
        mod = _ada(c_pad, w_ada[l], b_ada[l].reshape(1, -1))[:batch]
        mod3 = mod.reshape(batch, 6, d)
        qa, kv, qm, kt, vm, og, gates, ga, gm = _inproj(
            x2, mod3, g_pre_mix[l].reshape(1, d), cos_t, sin_t, _layout_w_in(w_in[l]),
            conv_w[l], conv_b[l].reshape(1, -1), tiles_per_batch)
        ya = _attention(attn_sinks[l], qa, kv, batch, seq)
        bif = jnp.pad(b_if[l], (0, LANES - 2 * MLSTM_HEADS)).reshape(1, LANES)
        ym = _mlstm(qm, kt, vm, og, gates, bif, mlstm_norm_w[l].reshape(1, -1), batch, seq)
        wa = w_branch_attn[l].reshape(N_Q_HEADS, HEAD_DIM, d)[head_order].reshape(ATTN_Q_WIDTH, d)
        x2 = _out(x2, ya, ym, ga, gm, mod3, g_post_mix[l].reshape(1, d), g_pre_ffn[l].reshape(1, d),
                  g_post_ffn[l].reshape(1, d), wa.astype(BF16), w_branch_mlstm[l].astype(BF16),
                  w_out[l].astype(BF16), w_ffn_gate[l].astype(BF16), w_ffn_up[l].astype(BF16),
                  w_ffn_down[l].astype(BF16), tiles_per_batch)
    return x2.reshape(batch, seq, d)
```

```python
import functools
import itertools

import numpy as np
import jax
import jax.numpy as jnp
from jax import lax
from jax.experimental import pallas as pl
from jax.experimental.pallas import tpu as pltpu

F32 = jnp.float32
BF16 = jnp.bfloat16

D_MODEL = 1024
N_Q_HEADS = 8
N_KV_HEADS = 2
HEAD_DIM = 64
ROPE_THETA = 10000.0
MLSTM_HEADS = 4
MLSTM_HEAD_DIM = 128
CONV_WIDTH = 4
D_FF = 2816
NORM_EPS = 1e-6
ATTN_Q_WIDTH = N_Q_HEADS * HEAD_DIM
ATTN_KV_WIDTH = N_KV_HEADS * HEAD_DIM
MLSTM_WIDTH = MLSTM_HEADS * MLSTM_HEAD_DIM
ATTN_BLOCK = 128

LANES = 128
SUBLANES = 8
MXU_WIDTH = 256
V7X_SCOPED_VMEM_BYTES = 60000 * 1024

TOKEN_TILE = 512
MLSTM_CHUNK = 128
FF_CHUNK = MXU_WIDTH
ROPE_TILE = 2048

C_QA = 0
C_KV = C_QA + ATTN_Q_WIDTH
C_QK = C_KV + 2 * ATTN_KV_WIDTH
C_VM = C_QK + 2 * MLSTM_WIDTH
C_OM = C_VM + MLSTM_WIDTH
C_IF = C_OM + MLSTM_WIDTH
C_GA = C_IF + LANES
C_GM = C_GA + D_MODEL
IN_COLS = C_GM + D_MODEL


def _vmem_limit(estimate_bytes):
    return int(min(V7X_SCOPED_VMEM_BYTES, max(estimate_bytes, 16 * 1024 * 1024)))


def _rms(x, g):
    return (x * lax.rsqrt(jnp.mean(x * x, axis=-1, keepdims=True) + NORM_EPS)) * g


def _ada_kernel(c_ref, w_ref, b_ref, o_ref):
    o_ref[...] = jnp.dot(c_ref[...], w_ref[...], preferred_element_type=F32,
                         precision=lax.Precision.HIGHEST) + b_ref[...]


def _ada(c_pad, w_ada, b_ada):
    rows = c_pad.shape[0]
    n = w_ada.shape[1]
    return pl.pallas_call(
        _ada_kernel,
        grid=(n // D_MODEL,),
        in_specs=[pl.BlockSpec((rows, D_MODEL), lambda j: (0, 0)),
                  pl.BlockSpec((D_MODEL, D_MODEL), lambda j: (0, j)),
                  pl.BlockSpec((1, D_MODEL), lambda j: (0, j))],
        out_specs=pl.BlockSpec((rows, D_MODEL), lambda j: (0, j)),
        out_shape=jax.ShapeDtypeStruct((rows, n), F32),
        name="ada",
    )(c_pad, w_ada, b_ada)


def _rope_kernel(pos_ref, freq_ref, cos_ref, sin_ref):
    ang = freq_ref[...] * pos_ref[...].astype(F32)
    c = jnp.cos(ang)
    s = jnp.sin(ang)
    cos_ref[...] = jnp.concatenate([c, c, c, c], axis=0).T
    sin_ref[...] = jnp.concatenate([-s, s, -s, s], axis=0).T


def _rope_tables(pos_row, inv_freq):
    t = pos_row.shape[1]
    half = inv_freq.shape[0]
    return pl.pallas_call(
        _rope_kernel,
        grid=(t // ROPE_TILE,),
        in_specs=[pl.BlockSpec((1, ROPE_TILE), lambda i: (0, i)),
                  pl.BlockSpec((half, 1), lambda i: (0, 0))],
        out_specs=[pl.BlockSpec((ROPE_TILE, LANES), lambda i: (i, 0)),
                   pl.BlockSpec((ROPE_TILE, LANES), lambda i: (i, 0))],
        out_shape=[jax.ShapeDtypeStruct((t, LANES), F32),
                   jax.ShapeDtypeStruct((t, LANES), F32)],
        name="rope",
    )(pos_row, inv_freq)


def _inproj_kernel(tiles_per_batch, x_ref, mod_ref, g_ref, cos_ref, sin_ref, w_ref, cw_ref, cb_ref,
                   qa_ref, kv_ref, qm_ref, kt_ref, vm_ref, om_ref, if_ref, ga_ref, gm_ref, raw_s, h_s):
    tm = TOKEN_TILE
    x = x_ref[...]
    shift = mod_ref[0, 0:1, :]
    scale = mod_ref[0, 1:2, :]
    h_s[...] = (_rms(x, g_ref[...]) * (1.0 + scale) + shift).astype(BF16)

    def proj(lo, width):
        return jnp.dot(h_s[...], w_ref[:, lo:lo + width], preferred_element_type=F32)

    cos = cos_ref[...]
    sin = sin_ref[...]
    lane = lax.broadcasted_iota(jnp.int32, (1, LANES), 1)
    first_half = (lane % HEAD_DIM) < (HEAD_DIM // 2)

    def rope(t):
        swapped = jnp.where(first_half, pltpu.roll(t, LANES - HEAD_DIM // 2, axis=1),
                            pltpu.roll(t, HEAD_DIM // 2, axis=1))
        return t * cos + swapped * sin

    q = proj(C_QA, ATTN_Q_WIDTH)
    for p in range(ATTN_Q_WIDTH // LANES):
        sl = slice(p * LANES, (p + 1) * LANES)
        qa_ref[:, sl] = (rope(q[:, sl]) * (HEAD_DIM ** -0.5)).astype(BF16)
    kv = proj(C_KV, 2 * ATTN_KV_WIDTH)
    kv_ref[:, :LANES] = rope(kv[:, :LANES]).astype(BF16)
    kv_ref[:, LANES:] = kv[:, LANES:].astype(BF16)
    @pl.when(pl.program_id(0) % tiles_per_batch == 0)
    def _():
        raw_s[0:SUBLANES, :] = jnp.zeros((SUBLANES, raw_s.shape[1]), F32)

    blk = MXU_WIDTH
    for cbk in range(2 * MLSTM_WIDTH // blk):
        cs = slice(cbk * blk, (cbk + 1) * blk)
        raw = proj(C_QK + cbk * blk, blk)
        raw_s[SUBLANES:, cs] = raw
        acc = cb_ref[:, cs] + cw_ref[CONV_WIDTH - 1:CONV_WIDTH, cs] * raw
        for sh in range(1, CONV_WIDTH):
            acc = acc + (cw_ref[CONV_WIDTH - 1 - sh:CONV_WIDTH - sh, cs]
                         * raw_s[SUBLANES - sh:SUBLANES - sh + tm, cs])
        raw_s[0:SUBLANES, cs] = raw[tm - SUBLANES:]
        a = acc * jax.nn.sigmoid(acc)
        if cbk * blk < MLSTM_WIDTH:
            qm_ref[:, cs] = a.astype(BF16)
        else:
            a = a * (MLSTM_HEAD_DIM ** -0.5)
            for d0 in range(0, blk, LANES):
                r0 = cbk * blk - MLSTM_WIDTH + d0
                for t0 in range(0, tm, LANES):
                    kt_ref[r0:r0 + LANES, t0:t0 + LANES] = a[t0:t0 + LANES, d0:d0 + LANES].T.astype(BF16)
    vm_ref[...] = proj(C_VM, MLSTM_WIDTH).astype(BF16)
    om_ref[...] = jax.nn.sigmoid(proj(C_OM, MLSTM_WIDTH)).astype(BF16)
    if_ref[...] = proj(C_IF, LANES)
    ga_ref[...] = jax.nn.sigmoid(proj(C_GA, D_MODEL)).astype(BF16)
    gm_ref[...] = jax.nn.sigmoid(proj(C_GM, D_MODEL)).astype(BF16)


def _inproj(x2, mod3, g_pre, cos_t, sin_t, w_all, conv_w, conv_b, tiles_per_batch):
    t = x2.shape[0]
    tm = TOKEN_TILE
    row = lambda i: (i, 0)
    const = lambda i: (0, 0)
    widths = [ATTN_Q_WIDTH, 2 * ATTN_KV_WIDTH, MLSTM_WIDTH, None, MLSTM_WIDTH, MLSTM_WIDTH,
              LANES, D_MODEL, D_MODEL]
    dtypes = [BF16, BF16, BF16, BF16, BF16, BF16, F32, BF16, BF16]
    out_specs = [pl.BlockSpec((MLSTM_WIDTH, tm), lambda i: (0, i)) if w is None
                 else pl.BlockSpec((tm, w), row) for w in widths]
    out_shape = [jax.ShapeDtypeStruct((MLSTM_WIDTH, t) if w is None else (t, w), d)
                 for w, d in zip(widths, dtypes)]
    out_bytes = sum((w or MLSTM_WIDTH) * np.dtype(d).itemsize for w, d in zip(widths, dtypes)) * tm
    est = (2 * w_all.size * 2 + 2 * tm * D_MODEL * 4 + 2 * out_bytes
           + 4 * tm * LANES * 4 + 8 * tm * D_MODEL * 4)
    return pl.pallas_call(
        functools.partial(_inproj_kernel, tiles_per_batch),
        grid=(t // tm,),
        in_specs=[pl.BlockSpec((tm, D_MODEL), row),
                  pl.BlockSpec((1, 6, D_MODEL), lambda i: (i // tiles_per_batch, 0, 0)),
                  pl.BlockSpec((1, D_MODEL), const),
                  pl.BlockSpec((tm, LANES), row),
                  pl.BlockSpec((tm, LANES), row),
                  pl.BlockSpec((D_MODEL, IN_COLS), const),
                  pl.BlockSpec((CONV_WIDTH, 2 * MLSTM_WIDTH), const),
                  pl.BlockSpec((1, 2 * MLSTM_WIDTH), const)],
        out_specs=out_specs,
        out_shape=out_shape,
        scratch_shapes=[pltpu.VMEM((tm + SUBLANES, 2 * MLSTM_WIDTH), F32),
                        pltpu.VMEM((tm, D_MODEL), BF16)],
        compiler_params=pltpu.CompilerParams(dimension_semantics=("arbitrary",),
                                             vmem_limit_bytes=_vmem_limit(est)),
        name="inproj",
    )(x2, mod3, g_pre, cos_t, sin_t, w_all, conv_w, conv_b)


def _attn_tile(j, sink_ref, q_ref, kv_ref, kvp_ref, o_ref):
    blk = ATTN_BLOCK
    nblk = TOKEN_TILE // blk
    lane = lax.broadcasted_iota(jnp.int32, (1, LANES), 1)
    low = lane < HEAD_DIM
    qi = lax.broadcasted_iota(jnp.int32, (blk, 2 * blk), 0)
    kj = lax.broadcasted_iota(jnp.int32, (blk, 2 * blk), 1)
    in_cur = (kj >= blk) & (kj - blk <= qi)
    in_prev = (kj < blk) & (kj > qi)
    in_prev_first = (kj < blk) & (kj > qi + jnp.where(j > 0, 0, blk))
    ones = jnp.ones((2 * blk, LANES), F32)
    zeros = jnp.zeros((2 * blk, LANES), F32)
    one_lo = jnp.where(low, ones, zeros).astype(BF16)
    one_hi = jnp.where(low, zeros, ones).astype(BF16)

    for i in range(nblk):
        cur = kv_ref[i * blk:(i + 1) * blk, :]
        prev = kvp_ref[...] if i == 0 else kv_ref[(i - 1) * blk:i * blk, :]
        band = jnp.concatenate([prev, cur], axis=0).astype(F32)
        k = band[:, :LANES]
        v = band[:, LANES:]
        k_cat = jnp.concatenate([jnp.where(low, k, 0.0), jnp.where(low, 0.0, k)], axis=0).astype(BF16)
        v_cat = jnp.concatenate(
            [jnp.concatenate([jnp.where(low, v, 0.0).astype(BF16), one_lo], axis=1),
             jnp.concatenate([jnp.where(low, 0.0, v).astype(BF16), one_hi], axis=1)], axis=0)
        mask = in_cur | (in_prev_first if i == 0 else in_prev)
        for p in range(ATTN_Q_WIDTH // LANES):
            q = q_ref[i * blk:(i + 1) * blk, p * LANES:(p + 1) * LANES]
            s = lax.dot_general(q, k_cat, (((1,), (1,)), ((), ())), preferred_element_type=F32)
            s0 = jnp.where(mask, s[:, :2 * blk], -jnp.inf)
            s1 = jnp.where(mask, s[:, 2 * blk:], -jnp.inf)
            sink0 = sink_ref[p]
            sink1 = sink_ref[p + N_Q_HEADS // N_KV_HEADS]
            m0 = jnp.maximum(jnp.max(s0, axis=-1, keepdims=True), sink0)
            m1 = jnp.maximum(jnp.max(s1, axis=-1, keepdims=True), sink1)
            pr = jnp.concatenate([jnp.exp(s0 - m0), jnp.exp(s1 - m1)], axis=1).astype(BF16)
            r = jnp.dot(pr, v_cat, preferred_element_type=F32)
            den = r[:, LANES:] + jnp.where(low, jnp.exp(sink0 - m0), jnp.exp(sink1 - m1))
            o_ref[i * blk:(i + 1) * blk, p * LANES:(p + 1) * LANES] = (r[:, :LANES] / den).astype(BF16)
            yield


def _mlstm_tile(q_ref, kt_ref, v_ref, og_ref, if_ref, bif_ref, nw_ref, y_ref,
                cn_ref, mrow_ref, mcol_ref):
    ts = TOKEN_TILE
    L = MLSTM_CHUNK
    D = MLSTM_HEAD_DIM

    gates = if_ref[...] + bif_ref[...]
    logf_all = jax.nn.log_sigmoid(pltpu.roll(gates, LANES - MLSTM_HEADS, axis=1))

    ti = lax.broadcasted_iota(jnp.int32, (L, L), 0)
    si = lax.broadcasted_iota(jnp.int32, (L, L), 1)
    causal = si <= ti
    tri = causal.astype(BF16)
    ones_v = jnp.ones((L, LANES), BF16)

    for c in range(ts // L):
        rows = slice(c * L, (c + 1) * L)
        lf = logf_all[rows]
        lf_hi = lf.astype(BF16)
        rem = lf - lf_hi.astype(F32)
        lf_mid = rem.astype(BF16)
        lf_lo = (rem - lf_mid.astype(F32)).astype(BF16)
        b = (jnp.dot(tri, lf_hi, preferred_element_type=F32)
             + jnp.dot(tri, lf_mid, preferred_element_type=F32)
             + jnp.dot(tri, lf_lo, preferred_element_type=F32))
        r = b - gates[rows]
        pm = -r
        sh = 1
        while sh < L:
            pm = jnp.maximum(pm, jnp.where(ti >= sh, pltpu.roll(pm, sh, axis=0), -jnp.inf))
            sh *= 2
        m_row = mrow_ref[0:1, :]
        inter = b + m_row
        mt = jnp.maximum(inter, b + pm)
        e_col = b - mt
        w_inter = jnp.exp(inter - mt)
        e_mt = jnp.exp(-mt)
        b_last = b[L - 1:L, :]
        m_new_row = jnp.maximum(b_last + m_row, jnp.max(b_last - r, axis=0, keepdims=True))
        decay_row = jnp.exp(b_last + m_row - m_new_row)
        mrow_ref[...] = jnp.broadcast_to(m_new_row, mrow_ref.shape)
        b_t = b.T[:SUBLANES]
        r_t = r.T[:SUBLANES]
        m_col = mcol_ref[:, 0:1]
        bl_col = b_t[:, L - 1:L]
        a_t = bl_col - r_t
        m_new_col = jnp.maximum(bl_col + m_col, jnp.max(a_t, axis=1, keepdims=True))
        wk_t = jnp.exp(a_t - m_new_col)
        mcol_ref[...] = jnp.broadcast_to(m_new_col, mcol_ref.shape)
        yield

        for h in range(MLSTM_HEADS):
            hs = slice(h * D, (h + 1) * D)
            qh = q_ref[rows, hs]
            kt = kt_ref[hs, rows]
            s = jnp.dot(qh, kt, preferred_element_type=F32)
            w_intra = jnp.where(causal, jnp.exp(e_col[:, h:h + 1] - r_t[h:h + 1, :]), 0.0)
            pr = (s * w_intra).astype(BF16)
            v_ext = jnp.concatenate([v_ref[rows, hs], ones_v], axis=1)
            cn = cn_ref[h]
            nd = (jnp.dot(pr, v_ext, preferred_element_type=F32)
                  + w_inter[:, h:h + 1] * jnp.dot(qh, cn.astype(BF16), preferred_element_type=F32))
            hh = nd[:, :D] / jnp.maximum(jnp.abs(nd[:, D:]), e_mt[:, h:h + 1])
            kw = (kt.astype(F32) * wk_t[h:h + 1, :]).astype(BF16)
            cn_ref[h] = decay_row[:, h:h + 1] * cn + jnp.dot(kw, v_ext, preferred_element_type=F32)
            mu = jnp.mean(hh, axis=-1, keepdims=True)
            xc = hh - mu
            var = jnp.mean(xc * xc, axis=-1, keepdims=True)
            yn = (xc * lax.rsqrt(var + NORM_EPS)) * nw_ref[:, hs]
            y_ref[rows, hs] = (og_ref[rows, hs].astype(F32) * yn).astype(BF16)
            yield


def _ffn_tile(x_ref, ya_ref, ym_ref, ga_ref, gm_ref, mod_ref, gpm_ref, gqf_ref, gpf_ref,
              wa_ref, wb_ref, wo_ref, wg_ref, wu_ref, wd_ref, o_ref, act_s):
    gate_m = mod_ref[0, 2:3, :]
    shift_f = mod_ref[0, 3:4, :]
    scale_f = mod_ref[0, 4:5, :]
    gate_f = mod_ref[0, 5:6, :]
    ba = jnp.dot(ya_ref[...], wa_ref[...], preferred_element_type=F32)
    bb = jnp.dot(ym_ref[...], wb_ref[...], preferred_element_type=F32)
    merged = (ga_ref[...].astype(F32) * ba + gm_ref[...].astype(F32) * bb).astype(BF16)
    yield
    mix = jnp.dot(merged, wo_ref[...], preferred_element_type=F32)
    x1 = x_ref[...] + gate_m * _rms(mix, gpm_ref[...])
    h2 = (_rms(x1, gqf_ref[...]) * (1.0 + scale_f) + shift_f).astype(BF16)
    yield
    for c in range(D_FF // FF_CHUNK):
        cs = slice(c * FF_CHUNK, (c + 1) * FF_CHUNK)
        g = jnp.dot(h2, wg_ref[:, cs], preferred_element_type=F32)
        u = jnp.dot(h2, wu_ref[:, cs], preferred_element_type=F32)
        act_s[:, cs] = ((g * jax.nn.sigmoid(g)) * u).astype(BF16)
        yield
    ff = []
    for c in range(D_MODEL // FF_CHUNK):
        cs = slice(c * FF_CHUNK, (c + 1) * FF_CHUNK)
        ff.append(jnp.dot(act_s[...], wd_ref[:, cs], preferred_element_type=F32))
        yield
    o_ref[...] = x1 + gate_f * _rms(jnp.concatenate(ff, axis=1), gpf_ref[...])
    yield


def _interleave(main, fill):
    n_main = 1 + 1 + D_FF // FF_CHUNK + D_MODEL // FF_CHUNK + 1
    n_fill = ((TOKEN_TILE // ATTN_BLOCK) * (ATTN_Q_WIDTH // LANES)
              + (TOKEN_TILE // MLSTM_CHUNK) * (1 + MLSTM_HEADS))
    done = 0
    for k in range(n_main):
        next(main)
        want = ((k + 1) * n_fill) // n_main
        while done < want:
            next(fill)
            done += 1
    assert next(main, None) is None and next(fill, None) is None


def _back_kernel(n_tiles, tiles_per_batch, sink_ref,
                 qa_ref, kv_ref, kvp_ref, qm_ref, kt_ref, vm_ref, og_ref, if_ref, bif_ref, nw_ref,
                 x_ref, ga_ref, gm_ref, mod_ref, gpm_ref, gqf_ref, gpf_ref,
                 wa_ref, wb_ref, wo_ref, wg_ref, wu_ref, wd_ref, o_ref,
                 ya_s, ym_s, act_s, cn_ref, mrow_ref, mcol_ref):
    i = pl.program_id(0)
    tile = jnp.minimum(i, n_tiles - 1)
    j = tile % tiles_per_batch

    @pl.when(i == 0)
    def _():
        ya_s[...] = jnp.zeros_like(ya_s)
        ym_s[...] = jnp.zeros_like(ym_s)

    @pl.when(j == 0)
    def _():
        cn_ref[...] = jnp.zeros_like(cn_ref)
        mrow_ref[...] = jnp.zeros_like(mrow_ref)
        mcol_ref[...] = jnp.zeros_like(mcol_ref)

    ffn = _ffn_tile(x_ref, ya_s, ym_s, ga_ref, gm_ref, mod_ref, gpm_ref, gqf_ref, gpf_ref,
                    wa_ref, wb_ref, wo_ref, wg_ref, wu_ref, wd_ref, o_ref, act_s)
    mixers = itertools.chain(
        _attn_tile(j, sink_ref, qa_ref, kv_ref, kvp_ref, ya_s),
        _mlstm_tile(qm_ref, kt_ref, vm_ref, og_ref, if_ref, bif_ref, nw_ref, ym_s,
                    cn_ref, mrow_ref, mcol_ref))
    _interleave(ffn, mixers)


def _back(sinks, qa, kv, qm, kt, vm, og, gates, bif, norm_w, x2, ga, gm, mod3,
          g_post_mix, g_pre_ffn, g_post_ffn, wa, wb, wo, wg, wu, wd, tiles_per_batch):
    t = x2.shape[0]
    tm = TOKEN_TILE
    n = t // tm
    per_tile = tm // ATTN_BLOCK
    cur = lambda i: (jnp.minimum(i, n - 1), 0)
    cur_t = lambda i: (0, jnp.minimum(i, n - 1))

    def prev_block(i):
        tile = jnp.minimum(i, n - 1)
        first = (tile // tiles_per_batch) * tiles_per_batch * per_tile
        return (jnp.maximum(tile * per_tile - 1, first), 0)

    last = lambda i: (jnp.maximum(i - 1, 0), 0)
    const = lambda i: (0, 0)
    once = pl.Buffered(1)
    weight_bytes = 2 * (wa.size + wb.size + wo.size + wg.size + wu.size + wd.size)
    mixer_in = tm * (ATTN_Q_WIDTH + 2 * ATTN_KV_WIDTH + 4 * MLSTM_WIDTH) * 2 + tm * LANES * 4
    est = (weight_bytes + 2 * 2 * tm * D_MODEL * 4 + 2 * tm * 2 * D_MODEL * 2 + 2 * mixer_in
           + tm * (D_FF + 2 * MLSTM_WIDTH) * 2 + 8 * tm * D_MODEL * 4)
    return pl.pallas_call(
        functools.partial(_back_kernel, n, tiles_per_batch),
        grid=(n + 1,),
        in_specs=[pl.BlockSpec(memory_space=pltpu.SMEM),
                  pl.BlockSpec((tm, ATTN_Q_WIDTH), cur),
                  pl.BlockSpec((tm, 2 * ATTN_KV_WIDTH), cur),
                  pl.BlockSpec((ATTN_BLOCK, 2 * ATTN_KV_WIDTH), prev_block),
                  pl.BlockSpec((tm, MLSTM_WIDTH), cur),
                  pl.BlockSpec((MLSTM_WIDTH, tm), cur_t),
                  pl.BlockSpec((tm, MLSTM_WIDTH), cur),
                  pl.BlockSpec((tm, MLSTM_WIDTH), cur),
                  pl.BlockSpec((tm, LANES), cur),
                  pl.BlockSpec((1, LANES), const),
                  pl.BlockSpec((1, MLSTM_WIDTH), const),
                  pl.BlockSpec((tm, D_MODEL), last),
                  pl.BlockSpec((tm, D_MODEL), last),
                  pl.BlockSpec((tm, D_MODEL), last),
                  pl.BlockSpec((1, 6, D_MODEL),
                               lambda i: (jnp.maximum(i - 1, 0) // tiles_per_batch, 0, 0)),
                  pl.BlockSpec((1, D_MODEL), const),
                  pl.BlockSpec((1, D_MODEL), const),
                  pl.BlockSpec((1, D_MODEL), const),
                  pl.BlockSpec(wa.shape, const, pipeline_mode=once),
                  pl.BlockSpec(wb.shape, const, pipeline_mode=once),
                  pl.BlockSpec(wo.shape, const, pipeline_mode=once),
                  pl.BlockSpec(wg.shape, const, pipeline_mode=once),
                  pl.BlockSpec(wu.shape, const, pipeline_mode=once),
                  pl.BlockSpec(wd.shape, const, pipeline_mode=once)],
        out_specs=pl.BlockSpec((tm, D_MODEL), last),
        out_shape=jax.ShapeDtypeStruct((t, D_MODEL), F32),
        scratch_shapes=[pltpu.VMEM((tm, ATTN_Q_WIDTH), BF16),
                        pltpu.VMEM((tm, MLSTM_WIDTH), BF16),
                        pltpu.VMEM((tm, D_FF), BF16),
                        pltpu.VMEM((MLSTM_HEADS, MLSTM_HEAD_DIM, 2 * MLSTM_HEAD_DIM), F32),
                        pltpu.VMEM((SUBLANES, LANES), F32),
                        pltpu.VMEM((SUBLANES, LANES), F32)],
        compiler_params=pltpu.CompilerParams(dimension_semantics=("arbitrary",),
                                             vmem_limit_bytes=_vmem_limit(est)),
        name="back",
    )(sinks, qa, kv, kv, qm, kt, vm, og, gates, bif, norm_w, x2, ga, gm, mod3,
      g_post_mix, g_pre_ffn, g_post_ffn, wa, wb, wo, wg, wu, wd)


def _q_head_order():
    per_kv = N_Q_HEADS // N_KV_HEADS
    return [h for p in range(per_kv) for h in (p, p + per_kv)]


def _layout_w_in(w_in):
    splits = np.cumsum([ATTN_Q_WIDTH, ATTN_KV_WIDTH, ATTN_KV_WIDTH, MLSTM_WIDTH, MLSTM_WIDTH,
                        MLSTM_WIDTH, MLSTM_WIDTH, MLSTM_HEADS, MLSTM_HEADS, D_MODEL]).tolist()
    q_a, k_a, v_a, q_m, k_m, v_m, o_m, i_m, f_m, g_a, g_m = jnp.split(w_in, splits, axis=1)
    q_a = q_a.reshape(D_MODEL, N_Q_HEADS, HEAD_DIM)[:, np.array(_q_head_order())].reshape(D_MODEL, -1)
    pad = jnp.zeros((D_MODEL, LANES - 2 * MLSTM_HEADS), w_in.dtype)
    return jnp.concatenate([q_a, k_a, v_a, q_m, k_m, v_m, o_m, i_m, f_m, pad, g_a, g_m],
                           axis=1).astype(BF16)


def kernel(x, c, positions, w_ada, b_ada, g_pre_mix, g_post_mix, w_in, b_if, conv_w, conv_b,
           attn_sinks, mlstm_norm_w, w_branch_attn, w_branch_mlstm, w_out, g_pre_ffn, g_post_ffn,
           w_ffn_gate, w_ffn_up, w_ffn_down):
    batch, seq, d = x.shape
    depth = w_in.shape[0]
    assert d == D_MODEL and seq % TOKEN_TILE == 0 and (batch * seq) % ROPE_TILE == 0
    assert D_FF % FF_CHUNK == 0
    t = batch * seq
    tiles_per_batch = seq // TOKEN_TILE
    x2 = x.reshape(t, d)

    inv_freq = (ROPE_THETA ** (-2.0 * jnp.arange(HEAD_DIM // 2, dtype=F32) / HEAD_DIM)).reshape(-1, 1)
    cos_t, sin_t = _rope_tables(positions.reshape(1, t), inv_freq)
    c_pad = jnp.pad(c, ((0, SUBLANES - batch % SUBLANES), (0, 0))) if batch % SUBLANES else c
    head_order = np.array(_q_head_order())

    for l in range(depth):
        mod = _ada(c_pad, w_ada[l], b_ada[l].reshape(1, -1))[:batch]
        mod3 = mod.reshape(batch, 6, d)
        qa, kv, qm, kt, vm, og, gates, ga, gm = _inproj(
            x2, mod3, g_pre_mix[l].reshape(1, d), cos_t, sin_t, _layout_w_in(w_in[l]),
            conv_w[l], conv_b[l].reshape(1, -1), tiles_per_batch)
        bif = jnp.pad(b_if[l], (0, LANES - 2 * MLSTM_HEADS)).reshape(1, LANES)
        wa = w_branch_attn[l].reshape(N_Q_HEADS, HEAD_DIM, d)[head_order].reshape(ATTN_Q_WIDTH, d)
        x2 = _back(attn_sinks[l], qa, kv, qm, kt, vm, og, gates, bif, mlstm_norm_w[l].reshape(1, -1),
                   x2, ga, gm, mod3, g_post_mix[l].reshape(1, d), g_pre_ffn[l].reshape(1, d),
                   g_post_ffn[l].reshape(1, d), wa.astype(BF16), w_branch_mlstm[l].astype(BF16),
                   w_out[l].astype(BF16), w_ffn_gate[l].astype(BF16), w_ffn_up[l].astype(BF16),
                   w_ffn_down[l].astype(BF16), tiles_per_batch)
    return x2.reshape(batch, seq, d)
```

```python
import functools

import numpy as np
import jax
import jax.numpy as jnp
from jax import lax
from jax.experimental import pallas as pl
from jax.experimental.pallas import tpu as pltpu

F32 = jnp.float32
BF16 = jnp.bfloat16

D_MODEL = 1024
N_Q_HEADS = 8
N_KV_HEADS = 2
HEAD_DIM = 64
ROPE_THETA = 10000.0
MLSTM_HEADS = 4
MLSTM_HEAD_DIM = 128
CONV_WIDTH = 4
D_FF = 2816
NORM_EPS = 1e-6
ATTN_Q_WIDTH = N_Q_HEADS * HEAD_DIM
ATTN_KV_WIDTH = N_KV_HEADS * HEAD_DIM
MLSTM_WIDTH = MLSTM_HEADS * MLSTM_HEAD_DIM
ATTN_BLOCK = 128

LANES = 128
SUBLANES = 8
MXU_WIDTH = 256
V7X_SCOPED_VMEM_BYTES = 60000 * 1024

TOKEN_TILE = 512
MLSTM_CHUNK = 128
FF_CHUNK = MXU_WIDTH
ROPE_TILE = 2048

C_QA = 0
C_KV = C_QA + ATTN_Q_WIDTH
C_QK = C_KV + 2 * ATTN_KV_WIDTH
C_VM = C_QK + 2 * MLSTM_WIDTH
C_OM = C_VM + MLSTM_WIDTH
C_IF = C_OM + MLSTM_WIDTH
C_GA = C_IF + LANES
C_GM = C_GA + D_MODEL
IN_COLS = C_GM + D_MODEL


def _vmem_limit(estimate_bytes):
    return int(min(V7X_SCOPED_VMEM_BYTES, max(estimate_bytes, 16 * 1024 * 1024)))


def _rms(x, g):
    return (x * lax.rsqrt(jnp.mean(x * x, axis=-1, keepdims=True) + NORM_EPS)) * g


def _ada_kernel(c_ref, w_ref, b_ref, o_ref):
    o_ref[...] = jnp.dot(c_ref[...], w_ref[...], preferred_element_type=F32,
                         precision=lax.Precision.HIGHEST) + b_ref[...]


def _ada(c_pad, w_ada, b_ada):
    rows = c_pad.shape[0]
    n = w_ada.shape[1]
    return pl.pallas_call(
        _ada_kernel,
        grid=(n // D_MODEL,),
        in_specs=[pl.BlockSpec((rows, D_MODEL), lambda j: (0, 0)),
                  pl.BlockSpec((D_MODEL, D_MODEL), lambda j: (0, j)),
                  pl.BlockSpec((1, D_MODEL), lambda j: (0, j))],
        out_specs=pl.BlockSpec((rows, D_MODEL), lambda j: (0, j)),
        out_shape=jax.ShapeDtypeStruct((rows, n), F32),
        name="ada",
    )(c_pad, w_ada, b_ada)


def _rope_kernel(pos_ref, freq_ref, cos_ref, sin_ref):
    ang = freq_ref[...] * pos_ref[...].astype(F32)
    c = jnp.cos(ang)
    s = jnp.sin(ang)
    cos_ref[...] = jnp.concatenate([c, c, c, c], axis=0).T
    sin_ref[...] = jnp.concatenate([-s, s, -s, s], axis=0).T


def _rope_tables(pos_row, inv_freq):
    t = pos_row.shape[1]
    half = inv_freq.shape[0]
    return pl.pallas_call(
        _rope_kernel,
        grid=(t // ROPE_TILE,),
        in_specs=[pl.BlockSpec((1, ROPE_TILE), lambda i: (0, i)),
                  pl.BlockSpec((half, 1), lambda i: (0, 0))],
        out_specs=[pl.BlockSpec((ROPE_TILE, LANES), lambda i: (i, 0)),
                   pl.BlockSpec((ROPE_TILE, LANES), lambda i: (i, 0))],
        out_shape=[jax.ShapeDtypeStruct((t, LANES), F32),
                   jax.ShapeDtypeStruct((t, LANES), F32)],
        name="rope",
    )(pos_row, inv_freq)


def _inproj_kernel(tiles_per_batch, x_ref, mod_ref, g_ref, cos_ref, sin_ref, w_ref, cw_ref, cb_ref,
                   qa_ref, kv_ref, qm_ref, kt_ref, vm_ref, om_ref, if_ref, ga_ref, gm_ref, raw_s, h_s):
    tm = TOKEN_TILE
    x = x_ref[...]
    shift = mod_ref[0, 0:1, :]
    scale = mod_ref[0, 1:2, :]
    h_s[...] = (_rms(x, g_ref[...]) * (1.0 + scale) + shift).astype(BF16)

    def proj(lo, width):
        return jnp.dot(h_s[...], w_ref[:, lo:lo + width], preferred_element_type=F32)

    cos = cos_ref[...]
    sin = sin_ref[...]
    lane = lax.broadcasted_iota(jnp.int32, (1, LANES), 1)
    first_half = (lane % HEAD_DIM) < (HEAD_DIM // 2)

    def rope(t):
        swapped = jnp.where(first_half, pltpu.roll(t, LANES - HEAD_DIM // 2, axis=1),
                            pltpu.roll(t, HEAD_DIM // 2, axis=1))
        return t * cos + swapped * sin

    q = proj(C_QA, ATTN_Q_WIDTH)
    for p in range(ATTN_Q_WIDTH // LANES):
        sl = slice(p * LANES, (p + 1) * LANES)
        qa_ref[:, sl] = (rope(q[:, sl]) * (HEAD_DIM ** -0.5)).astype(BF16)
    kv = proj(C_KV, 2 * ATTN_KV_WIDTH)
    kv_ref[:, :LANES] = rope(kv[:, :LANES]).astype(BF16)
    kv_ref[:, LANES:] = kv[:, LANES:].astype(BF16)
    @pl.when(pl.program_id(0) % tiles_per_batch == 0)
    def _():
        raw_s[0:SUBLANES, :] = jnp.zeros((SUBLANES, raw_s.shape[1]), F32)

    blk = MXU_WIDTH
    for cbk in range(2 * MLSTM_WIDTH // blk):
        cs = slice(cbk * blk, (cbk + 1) * blk)
        raw = proj(C_QK + cbk * blk, blk)
        raw_s[SUBLANES:, cs] = raw
        acc = cb_ref[:, cs] + cw_ref[CONV_WIDTH - 1:CONV_WIDTH, cs] * raw
        for sh in range(1, CONV_WIDTH):
            acc = acc + (cw_ref[CONV_WIDTH - 1 - sh:CONV_WIDTH - sh, cs]
                         * raw_s[SUBLANES - sh:SUBLANES - sh + tm, cs])
        raw_s[0:SUBLANES, cs] = raw[tm - SUBLANES:]
        a = acc * jax.nn.sigmoid(acc)
        if cbk * blk < MLSTM_WIDTH:
            qm_ref[:, cs] = a.astype(BF16)
        else:
            a = a * (MLSTM_HEAD_DIM ** -0.5)
            for d0 in range(0, blk, LANES):
                r0 = cbk * blk - MLSTM_WIDTH + d0
                for t0 in range(0, tm, LANES):
                    kt_ref[r0:r0 + LANES, t0:t0 + LANES] = a[t0:t0 + LANES, d0:d0 + LANES].T.astype(BF16)
    vm_ref[...] = proj(C_VM, MLSTM_WIDTH).astype(BF16)
    om_ref[...] = jax.nn.sigmoid(proj(C_OM, MLSTM_WIDTH)).astype(BF16)
    if_ref[...] = proj(C_IF, LANES)
    ga_ref[...] = jax.nn.sigmoid(proj(C_GA, D_MODEL)).astype(BF16)
    gm_ref[...] = jax.nn.sigmoid(proj(C_GM, D_MODEL)).astype(BF16)


def _inproj(x2, mod3, g_pre, cos_t, sin_t, w_all, conv_w, conv_b, tiles_per_batch):
    t = x2.shape[0]
    tm = TOKEN_TILE
    row = lambda i: (i, 0)
    const = lambda i: (0, 0)
    widths = [ATTN_Q_WIDTH, 2 * ATTN_KV_WIDTH, MLSTM_WIDTH, None, MLSTM_WIDTH, MLSTM_WIDTH,
              LANES, D_MODEL, D_MODEL]
    dtypes = [BF16, BF16, BF16, BF16, BF16, BF16, F32, BF16, BF16]
    out_specs = [pl.BlockSpec((MLSTM_WIDTH, tm), lambda i: (0, i)) if w is None
                 else pl.BlockSpec((tm, w), row) for w in widths]
    out_shape = [jax.ShapeDtypeStruct((MLSTM_WIDTH, t) if w is None else (t, w), d)
                 for w, d in zip(widths, dtypes)]
    out_bytes = sum((w or MLSTM_WIDTH) * np.dtype(d).itemsize for w, d in zip(widths, dtypes)) * tm
    est = (2 * w_all.size * 2 + 2 * tm * D_MODEL * 4 + 2 * out_bytes
           + 4 * tm * LANES * 4 + 8 * tm * D_MODEL * 4)
    return pl.pallas_call(
        functools.partial(_inproj_kernel, tiles_per_batch),
        grid=(t // tm,),
        in_specs=[pl.BlockSpec((tm, D_MODEL), row),
                  pl.BlockSpec((1, 6, D_MODEL), lambda i: (i // tiles_per_batch, 0, 0)),
                  pl.BlockSpec((1, D_MODEL), const),
                  pl.BlockSpec((tm, LANES), row),
                  pl.BlockSpec((tm, LANES), row),
                  pl.BlockSpec((D_MODEL, IN_COLS), const),
                  pl.BlockSpec((CONV_WIDTH, 2 * MLSTM_WIDTH), const),
                  pl.BlockSpec((1, 2 * MLSTM_WIDTH), const)],
        out_specs=out_specs,
        out_shape=out_shape,
        scratch_shapes=[pltpu.VMEM((tm + SUBLANES, 2 * MLSTM_WIDTH), F32),
                        pltpu.VMEM((tm, D_MODEL), BF16)],
        compiler_params=pltpu.CompilerParams(dimension_semantics=("arbitrary",),
                                             vmem_limit_bytes=_vmem_limit(est)),
        name="inproj",
    )(x2, mod3, g_pre, cos_t, sin_t, w_all, conv_w, conv_b)


def _attn_thread(j, sink_ref, q_ref, kv_ref, kvp_ref, o_ref):
    blk = ATTN_BLOCK
    nblk = TOKEN_TILE // blk
    ngrp = ATTN_Q_WIDTH // LANES
    lane = lax.broadcasted_iota(jnp.int32, (1, LANES), 1)
    low = lane < HEAD_DIM
    qi = lax.broadcasted_iota(jnp.int32, (blk, 2 * blk), 0)
    kj = lax.broadcasted_iota(jnp.int32, (blk, 2 * blk), 1)
    in_cur = (kj >= blk) & (kj - blk <= qi)
    in_prev = (kj < blk) & (kj > qi)
    in_prev_first = (kj < blk) & (kj > qi + jnp.where(j > 0, 0, blk))
    ones = jnp.ones((2 * blk, LANES), F32)
    zeros = jnp.zeros((2 * blk, LANES), F32)
    one_lo = jnp.where(low, ones, zeros).astype(BF16)
    one_hi = jnp.where(low, zeros, ones).astype(BF16)
    band = {}
    live = {}

    def prep(i):
        cur = kv_ref[i * blk:(i + 1) * blk, :]
        prev = kvp_ref[...] if i == 0 else kv_ref[(i - 1) * blk:i * blk, :]
        kv = jnp.concatenate([prev, cur], axis=0).astype(F32)
        k = kv[:, :LANES]
        v = kv[:, LANES:]
        k_cat = jnp.concatenate([jnp.where(low, k, 0.0), jnp.where(low, 0.0, k)], axis=0).astype(BF16)
        v_cat = jnp.concatenate(
            [jnp.concatenate([jnp.where(low, v, 0.0).astype(BF16), one_lo], axis=1),
             jnp.concatenate([jnp.where(low, 0.0, v).astype(BF16), one_hi], axis=1)], axis=0)
        band[i] = (k_cat, v_cat, in_cur | (in_prev_first if i == 0 else in_prev))

    def scores(u):
        i, p = divmod(u, ngrp)
        q = q_ref[i * blk:(i + 1) * blk, p * LANES:(p + 1) * LANES]
        live[u] = lax.dot_general(q, band[i][0], (((1,), (1,)), ((), ())), preferred_element_type=F32)

    def softmax(u):
        i, p = divmod(u, ngrp)
        s = live[u]
        mask = band[i][2]
        s0 = jnp.where(mask, s[:, :2 * blk], -jnp.inf)
        s1 = jnp.where(mask, s[:, 2 * blk:], -jnp.inf)
        m0 = jnp.maximum(jnp.max(s0, axis=-1, keepdims=True), sink_ref[p])
        m1 = jnp.maximum(jnp.max(s1, axis=-1, keepdims=True), sink_ref[p + N_Q_HEADS // N_KV_HEADS])
        pr = jnp.concatenate([jnp.exp(s0 - m0), jnp.exp(s1 - m1)], axis=1).astype(BF16)
        live[u] = (pr, m0, m1)

    def values(u):
        i, p = divmod(u, ngrp)
        pr, m0, m1 = live.pop(u)
        r = jnp.dot(pr, band[i][1], preferred_element_type=F32)
        den = r[:, LANES:] + jnp.where(low, jnp.exp(sink_ref[p] - m0),
                                       jnp.exp(sink_ref[p + N_Q_HEADS // N_KV_HEADS] - m1))
        o_ref[i * blk:(i + 1) * blk, p * LANES:(p + 1) * LANES] = (r[:, :LANES] / den).astype(BF16)

    n = nblk * ngrp
    prep(0)
    for k in range(n + 2):
        if k < n:
            scores(k)
            if k + 1 < n and (k + 1) % ngrp == 0:
                prep((k + 1) // ngrp)
        if 0 <= k - 2 < n:
            values(k - 2)
        if 0 <= k - 1 < n:
            softmax(k - 1)
        yield


ATTN_STAGES = (TOKEN_TILE // ATTN_BLOCK) * (ATTN_Q_WIDTH // LANES) + 2


def _mlstm_thread(q_ref, kt_ref, v_ref, og_ref, if_ref, bif_ref, nw_ref, y_ref,
                  cn_ref, mrow_ref, mcol_ref):
    ts = TOKEN_TILE
    L = MLSTM_CHUNK
    D = MLSTM_HEAD_DIM
    nch = ts // L

    gates = if_ref[...] + bif_ref[...]
    logf_all = jax.nn.log_sigmoid(pltpu.roll(gates, LANES - MLSTM_HEADS, axis=1))

    ti = lax.broadcasted_iota(jnp.int32, (L, L), 0)
    si = lax.broadcasted_iota(jnp.int32, (L, L), 1)
    causal = si <= ti
    tri = causal.astype(BF16)
    ones_v = jnp.ones((L, LANES), BF16)
    chunk = {}
    head = {}

    def rows_of(c):
        return slice(c * L, (c + 1) * L)

    def p1(c):
        lf = logf_all[rows_of(c)]
        lf_hi = lf.astype(BF16)
        rem = lf - lf_hi.astype(F32)
        lf_mid = rem.astype(BF16)
        lf_lo = (rem - lf_mid.astype(F32)).astype(BF16)
        chunk[c] = (jnp.dot(tri, lf_hi, preferred_element_type=F32)
                    + jnp.dot(tri, lf_mid, preferred_element_type=F32)
                    + jnp.dot(tri, lf_lo, preferred_element_type=F32))

    def p2(c):
        b = chunk[c]
        r = b - gates[rows_of(c)]
        pm = -r
        sh = 1
        while sh < L:
            pm = jnp.maximum(pm, jnp.where(ti >= sh, pltpu.roll(pm, sh, axis=0), -jnp.inf))
            sh *= 2
        m_row = mrow_ref[0:1, :]
        inter = b + m_row
        mt = jnp.maximum(inter, b + pm)
        b_last = b[L - 1:L, :]
        m_new_row = jnp.maximum(b_last + m_row, jnp.max(b_last - r, axis=0, keepdims=True))
        mrow_ref[...] = jnp.broadcast_to(m_new_row, mrow_ref.shape)
        b_t = b.T[:SUBLANES]
        r_t = r.T[:SUBLANES]
        m_col = mcol_ref[:, 0:1]
        bl_col = b_t[:, L - 1:L]
        a_t = bl_col - r_t
        m_new_col = jnp.maximum(bl_col + m_col, jnp.max(a_t, axis=1, keepdims=True))
        mcol_ref[...] = jnp.broadcast_to(m_new_col, mcol_ref.shape)
        chunk[c] = dict(e_col=b - mt, r_t=r_t, w_inter=jnp.exp(inter - mt), e_mt=jnp.exp(-mt),
                        decay_row=jnp.exp(b_last + m_row - m_new_row),
                        wk_t=jnp.exp(a_t - m_new_col))

    def ha(c, h):
        hs = slice(h * D, (h + 1) * D)
        qh = q_ref[rows_of(c), hs]
        kt = kt_ref[hs, rows_of(c)]
        head[c, h] = dict(qh=qh, kt=kt, s=jnp.dot(qh, kt, preferred_element_type=F32))

    def hb(c, h):
        st, ch = head[c, h], chunk[c]
        w_intra = jnp.where(causal, jnp.exp(ch["e_col"][:, h:h + 1] - ch["r_t"][h:h + 1, :]), 0.0)
        st["pr"] = (st.pop("s") * w_intra).astype(BF16)
        st["kw"] = (st.pop("kt").astype(F32) * ch["wk_t"][h:h + 1, :]).astype(BF16)

    def hc(c, h):
        st = head[c, h]
        hs = slice(h * D, (h + 1) * D)
        v_ext = jnp.concatenate([v_ref[rows_of(c), hs], ones_v], axis=1)
        st["pv"] = jnp.dot(st.pop("pr"), v_ext, preferred_element_type=F32)
        st["qc"] = jnp.dot(st.pop("qh"), cn_ref[h].astype(BF16), preferred_element_type=F32)
        st["kv"] = jnp.dot(st.pop("kw"), v_ext, preferred_element_type=F32)

    def hd(c, h):
        st, ch = head.pop((c, h)), chunk[c]
        hs = slice(h * D, (h + 1) * D)
        nd = st["pv"] + ch["w_inter"][:, h:h + 1] * st["qc"]
        hh = nd[:, :D] / jnp.maximum(jnp.abs(nd[:, D:]), ch["e_mt"][:, h:h + 1])
        cn_ref[h] = ch["decay_row"][:, h:h + 1] * cn_ref[h] + st["kv"]
        mu = jnp.mean(hh, axis=-1, keepdims=True)
        xc = hh - mu
        var = jnp.mean(xc * xc, axis=-1, keepdims=True)
        yn = (xc * lax.rsqrt(var + NORM_EPS)) * nw_ref[:, hs]
        y_ref[rows_of(c), hs] = (og_ref[rows_of(c), hs].astype(F32) * yn).astype(BF16)

    half = MLSTM_HEADS // 2
    for g in range(MLSTM_STAGES):
        for c in range(nch):
            o = g - MLSTM_CHUNK_SKEW * c
            if o == 0:
                p1(c)
            elif o == 1:
                for h in range(MLSTM_HEADS):
                    ha(c, h)
                p2(c)
            elif o == 2:
                for h in range(half):
                    hb(c, h)
            elif o == 3:
                for h in range(half):
                    hc(c, h)
                for h in range(half, MLSTM_HEADS):
                    hb(c, h)
            elif o == 4:
                for h in range(half, MLSTM_HEADS):
                    hc(c, h)
                for h in range(half):
                    hd(c, h)
            elif o == 5:
                for h in range(half, MLSTM_HEADS):
                    hd(c, h)
        yield


MLSTM_CHUNK_SKEW = 4
MLSTM_STAGES = MLSTM_CHUNK_SKEW * (TOKEN_TILE // MLSTM_CHUNK - 1) + 6


def _ffn_thread(x_ref, ya_ref, ym_ref, ga_ref, gm_ref, mod_ref, gpm_ref, gqf_ref, gpf_ref,
                wa_ref, wb_ref, wo_ref, wg_ref, wu_ref, wd_ref, o_ref, act_s):
    gate_m = mod_ref[0, 2:3, :]
    shift_f = mod_ref[0, 3:4, :]
    scale_f = mod_ref[0, 4:5, :]
    gate_f = mod_ref[0, 5:6, :]
    ba = jnp.dot(ya_ref[...], wa_ref[...], preferred_element_type=F32)
    yield
    bb = jnp.dot(ym_ref[...], wb_ref[...], preferred_element_type=F32)
    merged = (ga_ref[...].astype(F32) * ba + gm_ref[...].astype(F32) * bb).astype(BF16)
    yield
    mix = jnp.dot(merged, wo_ref[...], preferred_element_type=F32)
    x1 = x_ref[...] + gate_m * _rms(mix, gpm_ref[...])
    h2 = (_rms(x1, gqf_ref[...]) * (1.0 + scale_f) + shift_f).astype(BF16)
    yield
    for c in range(D_FF // FF_CHUNK):
        cs = slice(c * FF_CHUNK, (c + 1) * FF_CHUNK)
        g = jnp.dot(h2, wg_ref[:, cs], preferred_element_type=F32)
        yield
        u = jnp.dot(h2, wu_ref[:, cs], preferred_element_type=F32)
        act_s[:, cs] = ((g * jax.nn.sigmoid(g)) * u).astype(BF16)
        yield
    ff = []
    for c in range(D_MODEL // FF_CHUNK):
        cs = slice(c * FF_CHUNK, (c + 1) * FF_CHUNK)
        ff.append(jnp.dot(act_s[...], wd_ref[:, cs], preferred_element_type=F32))
        yield
    o_ref[...] = x1 + gate_f * _rms(jnp.concatenate(ff, axis=1), gpf_ref[...])
    yield


FFN_STAGES = 3 + 2 * (D_FF // FF_CHUNK) + D_MODEL // FF_CHUNK + 1


def _interleave(main, n_main, fills, first, last):
    done = [0] * len(fills)
    span = last - first + 1
    for k in range(n_main):
        next(main)
        for f, (gen, count) in enumerate(fills):
            want = 0 if k < first else min(count, ((k - first + 1) * count + span - 1) // span)
            while done[f] < want:
                next(gen)
                done[f] += 1
    assert all(d == c for d, (_, c) in zip(done, fills))
    assert next(main, None) is None and all(next(g, None) is None for g, _ in fills)


def _back_kernel(n_tiles, tiles_per_batch, sink_ref,
                 qa_ref, kv_ref, kvp_ref, qm_ref, kt_ref, vm_ref, og_ref, if_ref, bif_ref, nw_ref,
                 x_ref, ga_ref, gm_ref, mod_ref, gpm_ref, gqf_ref, gpf_ref,
                 wa_ref, wb_ref, wo_ref, wg_ref, wu_ref, wd_ref, o_ref,
                 ya_s, ym_s, act_s, cn_ref, mrow_ref, mcol_ref):
    i = pl.program_id(0)
    tile = jnp.minimum(i, n_tiles - 1)
    j = tile % tiles_per_batch

    @pl.when(i == 0)
    def _():
        ya_s[...] = jnp.zeros_like(ya_s)
        ym_s[...] = jnp.zeros_like(ym_s)

    @pl.when(j == 0)
    def _():
        cn_ref[...] = jnp.zeros_like(cn_ref)
        mrow_ref[...] = jnp.zeros_like(mrow_ref)
        mcol_ref[...] = jnp.zeros_like(mcol_ref)

    ffn = _ffn_thread(x_ref, ya_s, ym_s, ga_ref, gm_ref, mod_ref, gpm_ref, gqf_ref, gpf_ref,
                      wa_ref, wb_ref, wo_ref, wg_ref, wu_ref, wd_ref, o_ref, act_s)
    mixers = [(_attn_thread(j, sink_ref, qa_ref, kv_ref, kvp_ref, ya_s), ATTN_STAGES),
              (_mlstm_thread(qm_ref, kt_ref, vm_ref, og_ref, if_ref, bif_ref, nw_ref, ym_s,
                             cn_ref, mrow_ref, mcol_ref), MLSTM_STAGES)]
    _interleave(ffn, FFN_STAGES, mixers, first=1, last=FFN_STAGES - 2)


def _back(sinks, qa, kv, qm, kt, vm, og, gates, bif, norm_w, x2, ga, gm, mod3,
          g_post_mix, g_pre_ffn, g_post_ffn, wa, wb, wo, wg, wu, wd, tiles_per_batch):
    t = x2.shape[0]
    tm = TOKEN_TILE
    n = t // tm
    per_tile = tm // ATTN_BLOCK
    cur = lambda i: (jnp.minimum(i, n - 1), 0)
    cur_t = lambda i: (0, jnp.minimum(i, n - 1))

    def prev_block(i):
        tile = jnp.minimum(i, n - 1)
        first = (tile // tiles_per_batch) * tiles_per_batch * per_tile
        return (jnp.maximum(tile * per_tile - 1, first), 0)

    last = lambda i: (jnp.maximum(i - 1, 0), 0)
    const = lambda i: (0, 0)
    once = pl.Buffered(1)
    weight_bytes = 2 * (wa.size + wb.size + wo.size + wg.size + wu.size + wd.size)
    mixer_in = tm * (ATTN_Q_WIDTH + 2 * ATTN_KV_WIDTH + 4 * MLSTM_WIDTH) * 2 + tm * LANES * 4
    est = (weight_bytes + 2 * 2 * tm * D_MODEL * 4 + 2 * tm * 2 * D_MODEL * 2 + 2 * mixer_in
           + tm * (D_FF + 2 * MLSTM_WIDTH) * 2 + 8 * tm * D_MODEL * 4)
    return pl.pallas_call(
        functools.partial(_back_kernel, n, tiles_per_batch),
        grid=(n + 1,),
        in_specs=[pl.BlockSpec(memory_space=pltpu.SMEM),
                  pl.BlockSpec((tm, ATTN_Q_WIDTH), cur),
                  pl.BlockSpec((tm, 2 * ATTN_KV_WIDTH), cur),
                  pl.BlockSpec((ATTN_BLOCK, 2 * ATTN_KV_WIDTH), prev_block),
                  pl.BlockSpec((tm, MLSTM_WIDTH), cur),
                  pl.BlockSpec((MLSTM_WIDTH, tm), cur_t),
                  pl.BlockSpec((tm, MLSTM_WIDTH), cur),
                  pl.BlockSpec((tm, MLSTM_WIDTH), cur),
                  pl.BlockSpec((tm, LANES), cur),
                  pl.BlockSpec((1, LANES), const),
                  pl.BlockSpec((1, MLSTM_WIDTH), const),
                  pl.BlockSpec((tm, D_MODEL), last),
                  pl.BlockSpec((tm, D_MODEL), last),
                  pl.BlockSpec((tm, D_MODEL), last),
                  pl.BlockSpec((1, 6, D_MODEL),
                               lambda i: (jnp.maximum(i - 1, 0) // tiles_per_batch, 0, 0)),
                  pl.BlockSpec((1, D_MODEL), const),
                  pl.BlockSpec((1, D_MODEL), const),
                  pl.BlockSpec((1, D_MODEL), const),
                  pl.BlockSpec(wa.shape, const, pipeline_mode=once),
                  pl.BlockSpec(wb.shape, const, pipeline_mode=once),
                  pl.BlockSpec(wo.shape, const, pipeline_mode=once),
                  pl.BlockSpec(wg.shape, const, pipeline_mode=once),
                  pl.BlockSpec(wu.shape, const, pipeline_mode=once),
                  pl.BlockSpec(wd.shape, const, pipeline_mode=once)],
        out_specs=pl.BlockSpec((tm, D_MODEL), last),
        out_shape=jax.ShapeDtypeStruct((t, D_MODEL), F32),
        scratch_shapes=[pltpu.VMEM((tm, ATTN_Q_WIDTH), BF16),
                        pltpu.VMEM((tm, MLSTM_WIDTH), BF16),
                        pltpu.VMEM((tm, D_FF), BF16),
                        pltpu.VMEM((MLSTM_HEADS, MLSTM_HEAD_DIM, 2 * MLSTM_HEAD_DIM), F32),
                        pltpu.VMEM((SUBLANES, LANES), F32),
                        pltpu.VMEM((SUBLANES, LANES), F32)],
        compiler_params=pltpu.CompilerParams(dimension_semantics=("arbitrary",),
                                             vmem_limit_bytes=_vmem_limit(est)),
        name="back",
    )(sinks, qa, kv, kv, qm, kt, vm, og, gates, bif, norm_w, x2, ga, gm, mod3,
      g_post_mix, g_pre_ffn, g_post_ffn, wa, wb, wo, wg, wu, wd)


def _q_head_order():
    per_kv = N_Q_HEADS // N_KV_HEADS
    return [h for p in range(per_kv) for h in (p, p + per_kv)]


def _layout_w_in(w_in):
    splits = np.cumsum([ATTN_Q_WIDTH, ATTN_KV_WIDTH, ATTN_KV_WIDTH, MLSTM_WIDTH, MLSTM_WIDTH,
                        MLSTM_WIDTH, MLSTM_WIDTH, MLSTM_HEADS, MLSTM_HEADS, D_MODEL]).tolist()
    q_a, k_a, v_a, q_m, k_m, v_m, o_m, i_m, f_m, g_a, g_m = jnp.split(w_in, splits, axis=1)
    q_a = q_a.reshape(D_MODEL, N_Q_HEADS, HEAD_DIM)[:, np.array(_q_head_order())].reshape(D_MODEL, -1)
    pad = jnp.zeros((D_MODEL, LANES - 2 * MLSTM_HEADS), w_in.dtype)
    return jnp.concatenate([q_a, k_a, v_a, q_m, k_m, v_m, o_m, i_m, f_m, pad, g_a, g_m],
                           axis=1).astype(BF16)


def kernel(x, c, positions, w_ada, b_ada, g_pre_mix, g_post_mix, w_in, b_if, conv_w, conv_b,
           attn_sinks, mlstm_norm_w, w_branch_attn, w_branch_mlstm, w_out, g_pre_ffn, g_post_ffn,
           w_ffn_gate, w_ffn_up, w_ffn_down):
    batch, seq, d = x.shape
    depth = w_in.shape[0]
    assert d == D_MODEL and seq % TOKEN_TILE == 0 and (batch * seq) % ROPE_TILE == 0
    assert D_FF % FF_CHUNK == 0
    t = batch * seq
    tiles_per_batch = seq // TOKEN_TILE
    x2 = x.reshape(t, d)

    inv_freq = (ROPE_THETA ** (-2.0 * jnp.arange(HEAD_DIM // 2, dtype=F32) / HEAD_DIM)).reshape(-1, 1)
    cos_t, sin_t = _rope_tables(positions.reshape(1, t), inv_freq)
    c_pad = jnp.pad(c, ((0, SUBLANES - batch % SUBLANES), (0, 0))) if batch % SUBLANES else c
    head_order = np.array(_q_head_order())

    for l in range(depth):
        mod = _ada(c_pad, w_ada[l], b_ada[l].reshape(1, -1))[:batch]
        mod3 = mod.reshape(batch, 6, d)
        qa, kv, qm, kt, vm, og, gates, ga, gm = _inproj(
            x2, mod3, g_pre_mix[l].reshape(1, d), cos_t, sin_t, _layout_w_in(w_in[l]),
            conv_w[l], conv_b[l].reshape(1, -1), tiles_per_batch)
        bif = jnp.pad(b_if[l], (0, LANES - 2 * MLSTM_HEADS)).reshape(1, LANES)
        wa = w_branch_attn[l].reshape(N_Q_HEADS, HEAD_DIM, d)[head_order].reshape(ATTN_Q_WIDTH, d)
        x2 = _back(attn_sinks[l], qa, kv, qm, kt, vm, og, gates, bif, mlstm_norm_w[l].reshape(1, -1),
                   x2, ga, gm, mod3, g_post_mix[l].reshape(1, d), g_pre_ffn[l].reshape(1, d),
                   g_post_ffn[l].reshape(1, d), wa.astype(BF16), w_branch_mlstm[l].astype(BF16),
                   w_out[l].astype(BF16), w_ffn_gate[l].astype(BF16), w_ffn_up[l].astype(BF16),
                   w_ffn_down[l].astype(BF16), tiles_per_batch)
    return x2.reshape(batch, seq, d)
```

```python
import functools

import numpy as np
import jax
import jax.numpy as jnp
from jax import lax
from jax.experimental import pallas as pl
from jax.experimental.pallas import tpu as pltpu

F32 = jnp.float32
BF16 = jnp.bfloat16

D_MODEL = 1024
N_Q_HEADS = 8
N_KV_HEADS = 2
HEAD_DIM = 64
ROPE_THETA = 10000.0
MLSTM_HEADS = 4
MLSTM_HEAD_DIM = 128
CONV_WIDTH = 4
D_FF = 2816
NORM_EPS = 1e-6
ATTN_Q_WIDTH = N_Q_HEADS * HEAD_DIM
ATTN_KV_WIDTH = N_KV_HEADS * HEAD_DIM
MLSTM_WIDTH = MLSTM_HEADS * MLSTM_HEAD_DIM
ATTN_BLOCK = 128

LANES = 128
SUBLANES = 8
MXU_WIDTH = 256
V7X_SCOPED_VMEM_BYTES = 60000 * 1024

TOKEN_TILE = 512
MLSTM_CHUNK = 128
FF_CHUNK = MXU_WIDTH
ROPE_TILE = 2048

C_QA = 0
C_KV = C_QA + ATTN_Q_WIDTH
C_QK = C_KV + 2 * ATTN_KV_WIDTH
C_VM = C_QK + 2 * MLSTM_WIDTH
C_OM = C_VM + MLSTM_WIDTH
C_IF = C_OM + MLSTM_WIDTH
C_GA = C_IF + LANES
C_GM = C_GA + D_MODEL
IN_COLS = C_GM + D_MODEL


def _vmem_limit(estimate_bytes):
    return int(min(V7X_SCOPED_VMEM_BYTES, max(estimate_bytes, 16 * 1024 * 1024)))


def _rms(x, g):
    return (x * lax.rsqrt(jnp.mean(x * x, axis=-1, keepdims=True) + NORM_EPS)) * g


def _ada_kernel(c_ref, w_ref, b_ref, o_ref):
    o_ref[...] = jnp.dot(c_ref[...], w_ref[...], preferred_element_type=F32,
                         precision=lax.Precision.HIGHEST) + b_ref[...]


def _ada(c_pad, w_ada, b_ada):
    rows = c_pad.shape[0]
    n = w_ada.shape[1]
    return pl.pallas_call(
        _ada_kernel,
        grid=(n // D_MODEL,),
        in_specs=[pl.BlockSpec((rows, D_MODEL), lambda j: (0, 0)),
                  pl.BlockSpec((D_MODEL, D_MODEL), lambda j: (0, j)),
                  pl.BlockSpec((1, D_MODEL), lambda j: (0, j))],
        out_specs=pl.BlockSpec((rows, D_MODEL), lambda j: (0, j)),
        out_shape=jax.ShapeDtypeStruct((rows, n), F32),
        name="ada",
    )(c_pad, w_ada, b_ada)


def _rope_kernel(pos_ref, freq_ref, cos_ref, sin_ref):
    ang = freq_ref[...] * pos_ref[...].astype(F32)
    c = jnp.cos(ang)
    s = jnp.sin(ang)
    cos_ref[...] = jnp.concatenate([c, c, c, c], axis=0).T
    sin_ref[...] = jnp.concatenate([-s, s, -s, s], axis=0).T


def _rope_tables(pos_row, inv_freq):
    t = pos_row.shape[1]
    half = inv_freq.shape[0]
    return pl.pallas_call(
        _rope_kernel,
        grid=(t // ROPE_TILE,),
        in_specs=[pl.BlockSpec((1, ROPE_TILE), lambda i: (0, i)),
                  pl.BlockSpec((half, 1), lambda i: (0, 0))],
        out_specs=[pl.BlockSpec((ROPE_TILE, LANES), lambda i: (i, 0)),
                   pl.BlockSpec((ROPE_TILE, LANES), lambda i: (i, 0))],
        out_shape=[jax.ShapeDtypeStruct((t, LANES), F32),
                   jax.ShapeDtypeStruct((t, LANES), F32)],
        name="rope",
    )(pos_row, inv_freq)


def _modulated_norm(x, g, mod_ref, row):
    return (_rms(x, g) * (1.0 + mod_ref[0, row + 1:row + 2, :]) + mod_ref[0, row:row + 1, :]).astype(BF16)


def _inproj_kernel(n_tiles, tiles_per_batch, x0_ref, mod0_ref, xn_ref, modn_ref, g_ref, cos_ref, sin_ref,
                   w_ref, cw_ref, cb_ref,
                   qa_ref, kv_ref, qm_ref, kt_ref, vm_ref, om_ref, if_ref, ga_ref, gm_ref,
                   h_cur, h_next, raw_s, rot_s):
    tm = TOKEN_TILE
    i = pl.program_id(0)
    post_tile = jnp.maximum(i - 1, 0)

    @pl.when(i == 0)
    def _():
        h_cur[...] = _modulated_norm(x0_ref[...], g_ref[...], mod0_ref, 0)
        raw_s[...] = jnp.zeros_like(raw_s)
        rot_s[...] = jnp.zeros_like(rot_s)

    @pl.when(post_tile % tiles_per_batch == 0)
    def _():
        raw_s[0:SUBLANES, :] = jnp.zeros((SUBLANES, raw_s.shape[1]), F32)

    def proj(lo, width):
        return jnp.dot(h_cur[...], w_ref[:, lo:lo + width], preferred_element_type=F32)

    lane = lax.broadcasted_iota(jnp.int32, (1, LANES), 1)
    first_half = (lane % HEAD_DIM) < (HEAD_DIM // 2)

    def rope(t):
        swapped = jnp.where(first_half, pltpu.roll(t, LANES - HEAD_DIM // 2, axis=1),
                            pltpu.roll(t, HEAD_DIM // 2, axis=1))
        return t * cos_ref[...] + swapped * sin_ref[...]

    def post_q():
        for p in range(ATTN_Q_WIDTH // LANES):
            sl = slice(p * LANES, (p + 1) * LANES)
            qa_ref[:, sl] = (rope(rot_s[:, sl]) * (HEAD_DIM ** -0.5)).astype(BF16)

    def post_kv():
        kv_ref[:, :LANES] = rope(rot_s[:, ATTN_Q_WIDTH:ATTN_Q_WIDTH + LANES]).astype(BF16)
        kv_ref[:, LANES:] = rot_s[:, ATTN_Q_WIDTH + LANES:].astype(BF16)

    blk = MXU_WIDTH

    def post_conv(cbk):
        cs = slice(cbk * blk, (cbk + 1) * blk)
        acc = cb_ref[:, cs]
        for sh in range(CONV_WIDTH):
            acc = acc + (cw_ref[CONV_WIDTH - 1 - sh:CONV_WIDTH - sh, cs]
                         * raw_s[SUBLANES - sh:SUBLANES - sh + tm, cs])
        raw_s[0:SUBLANES, cs] = raw_s[tm:tm + SUBLANES, cs]
        a = acc * jax.nn.sigmoid(acc)
        if cbk * blk < MLSTM_WIDTH:
            qm_ref[:, cs] = a.astype(BF16)
        else:
            a = a * (MLSTM_HEAD_DIM ** -0.5)
            for d0 in range(0, blk, LANES):
                r0 = cbk * blk - MLSTM_WIDTH + d0
                for t0 in range(0, tm, LANES):
                    kt_ref[r0:r0 + LANES, t0:t0 + LANES] = a[t0:t0 + LANES, d0:d0 + LANES].T.astype(BF16)

    def norm_next(part):
        rows = slice(part * tm // INPROJ_NORM_PARTS, (part + 1) * tm // INPROJ_NORM_PARTS)
        h_next[rows, :] = _modulated_norm(xn_ref[rows, :], g_ref[...], modn_ref, 0)

    def main_rot(lo_w, lo_s, width):
        rot_s[:, lo_s:lo_s + width] = proj(lo_w, width)

    def main_conv(cbk):
        raw_s[SUBLANES:, cbk * blk:(cbk + 1) * blk] = proj(C_QK + cbk * blk, blk)

    def main_direct(out_ref, lo_w, lo, width):
        out_ref[:, lo:lo + width] = proj(lo_w + lo, width).astype(out_ref.dtype)

    half = D_MODEL // 2
    post_q()
    main_rot(C_QA, 0, ATTN_Q_WIDTH)
    post_kv()
    post_conv(0)
    main_rot(C_KV, ATTN_Q_WIDTH, 2 * ATTN_KV_WIDTH)
    main_conv(0)
    post_conv(1)
    main_conv(1)
    main_direct(vm_ref, C_VM, 0, MLSTM_WIDTH)
    post_conv(2)
    main_conv(2)
    main_direct(om_ref, C_OM, 0, MLSTM_WIDTH)
    post_conv(3)
    main_conv(3)
    main_direct(if_ref, C_IF, 0, LANES)
    norm_next(0)
    main_direct(ga_ref, C_GA, 0, half)
    norm_next(1)
    main_direct(ga_ref, C_GA, half, half)
    norm_next(2)
    main_direct(gm_ref, C_GM, 0, half)
    norm_next(3)
    main_direct(gm_ref, C_GM, half, half)
    h_cur[...] = h_next[...]


INPROJ_NORM_PARTS = 4


def _inproj(x2, mod3, g_pre, cos_t, sin_t, w_all, conv_w, conv_b, tiles_per_batch):
    t = x2.shape[0]
    tm = TOKEN_TILE
    n = t // tm
    main = lambda i: (jnp.minimum(i, n - 1), 0)
    post = lambda i: (jnp.maximum(i - 1, 0), 0)
    nxt = lambda i: (jnp.minimum(i + 1, n - 1), 0)
    const = lambda i: (0, 0)
    once = pl.Buffered(1)
    outs = [(ATTN_Q_WIDTH, BF16, post), (2 * ATTN_KV_WIDTH, BF16, post), (MLSTM_WIDTH, BF16, post),
            (None, BF16, None), (MLSTM_WIDTH, BF16, main), (MLSTM_WIDTH, BF16, main),
            (LANES, F32, main), (D_MODEL, BF16, main), (D_MODEL, BF16, main)]
    out_specs = [pl.BlockSpec((MLSTM_WIDTH, tm), lambda i: (0, jnp.maximum(i - 1, 0))) if w is None
                 else pl.BlockSpec((tm, w), m) for w, _, m in outs]
    out_shape = [jax.ShapeDtypeStruct((MLSTM_WIDTH, t) if w is None else (t, w), d) for w, d, _ in outs]
    out_bytes = sum((w or MLSTM_WIDTH) * np.dtype(d).itemsize for w, d, _ in outs) * tm
    rot_cols = ATTN_Q_WIDTH + 2 * ATTN_KV_WIDTH
    scratch_bytes = (2 * tm * D_MODEL * 2 + (tm + SUBLANES) * 2 * MLSTM_WIDTH * 4 + tm * rot_cols * 4)
    est = (w_all.size * 2 + 3 * tm * D_MODEL * 4 + 2 * out_bytes + 4 * tm * LANES * 4
           + scratch_bytes + 4 * tm * D_MODEL * 4)
    return pl.pallas_call(
        functools.partial(_inproj_kernel, n, tiles_per_batch),
        grid=(n + 1,),
        in_specs=[pl.BlockSpec((tm, D_MODEL), const, pipeline_mode=once),
                  pl.BlockSpec((1, 6, D_MODEL), lambda i: (0, 0, 0)),
                  pl.BlockSpec((tm, D_MODEL), nxt),
                  pl.BlockSpec((1, 6, D_MODEL),
                               lambda i: (jnp.minimum(i + 1, n - 1) // tiles_per_batch, 0, 0)),
                  pl.BlockSpec((1, D_MODEL), const),
                  pl.BlockSpec((tm, LANES), post),
                  pl.BlockSpec((tm, LANES), post),
                  pl.BlockSpec((D_MODEL, IN_COLS), const, pipeline_mode=once),
                  pl.BlockSpec((CONV_WIDTH, 2 * MLSTM_WIDTH), const),
                  pl.BlockSpec((1, 2 * MLSTM_WIDTH), const)],
        out_specs=out_specs,
        out_shape=out_shape,
        scratch_shapes=[pltpu.VMEM((tm, D_MODEL), BF16),
                        pltpu.VMEM((tm, D_MODEL), BF16),
                        pltpu.VMEM((tm + SUBLANES, 2 * MLSTM_WIDTH), F32),
                        pltpu.VMEM((tm, rot_cols), F32)],
        compiler_params=pltpu.CompilerParams(dimension_semantics=("arbitrary",),
                                             vmem_limit_bytes=_vmem_limit(est)),
        name="inproj",
    )(x2, mod3, x2, mod3, g_pre, cos_t, sin_t, w_all, conv_w, conv_b)


def _attn_thread(j, sink_ref, q_ref, kv_ref, kvp_ref, o_ref):
    blk = ATTN_BLOCK
    nblk = TOKEN_TILE // blk
    ngrp = ATTN_Q_WIDTH // LANES
    lane = lax.broadcasted_iota(jnp.int32, (1, LANES), 1)
    low = lane < HEAD_DIM
    qi = lax.broadcasted_iota(jnp.int32, (blk, 2 * blk), 0)
    kj = lax.broadcasted_iota(jnp.int32, (blk, 2 * blk), 1)
    in_cur = (kj >= blk) & (kj - blk <= qi)
    in_prev = (kj < blk) & (kj > qi)
    in_prev_first = (kj < blk) & (kj > qi + jnp.where(j > 0, 0, blk))
    ones = jnp.ones((2 * blk, LANES), F32)
    zeros = jnp.zeros((2 * blk, LANES), F32)
    one_lo = jnp.where(low, ones, zeros).astype(BF16)
    one_hi = jnp.where(low, zeros, ones).astype(BF16)
    band = {}
    live = {}

    def prep(i):
        cur = kv_ref[i * blk:(i + 1) * blk, :]
        prev = kvp_ref[...] if i == 0 else kv_ref[(i - 1) * blk:i * blk, :]
        kv = jnp.concatenate([prev, cur], axis=0).astype(F32)
        k = kv[:, :LANES]
        v = kv[:, LANES:]
        k_cat = jnp.concatenate([jnp.where(low, k, 0.0), jnp.where(low, 0.0, k)], axis=0).astype(BF16)
        v_cat = jnp.concatenate(
            [jnp.concatenate([jnp.where(low, v, 0.0).astype(BF16), one_lo], axis=1),
             jnp.concatenate([jnp.where(low, 0.0, v).astype(BF16), one_hi], axis=1)], axis=0)
        band[i] = (k_cat, v_cat, in_cur | (in_prev_first if i == 0 else in_prev))

    def scores(u):
        i, p = divmod(u, ngrp)
        q = q_ref[i * blk:(i + 1) * blk, p * LANES:(p + 1) * LANES]
        live[u] = lax.dot_general(q, band[i][0], (((1,), (1,)), ((), ())), preferred_element_type=F32)

    def softmax(u):
        i, p = divmod(u, ngrp)
        s = live[u]
        mask = band[i][2]
        s0 = jnp.where(mask, s[:, :2 * blk], -jnp.inf)
        s1 = jnp.where(mask, s[:, 2 * blk:], -jnp.inf)
        m0 = jnp.maximum(jnp.max(s0, axis=-1, keepdims=True), sink_ref[p])
        m1 = jnp.maximum(jnp.max(s1, axis=-1, keepdims=True), sink_ref[p + N_Q_HEADS // N_KV_HEADS])
        pr = jnp.concatenate([jnp.exp(s0 - m0), jnp.exp(s1 - m1)], axis=1).astype(BF16)
        live[u] = (pr, m0, m1)

    def values(u):
        i, p = divmod(u, ngrp)
        pr, m0, m1 = live.pop(u)
        r = jnp.dot(pr, band[i][1], preferred_element_type=F32)
        den = r[:, LANES:] + jnp.where(low, jnp.exp(sink_ref[p] - m0),
                                       jnp.exp(sink_ref[p + N_Q_HEADS // N_KV_HEADS] - m1))
        o_ref[i * blk:(i + 1) * blk, p * LANES:(p + 1) * LANES] = (r[:, :LANES] / den).astype(BF16)

    n = nblk * ngrp
    prep(0)
    for k in range(n + 2):
        if k < n:
            scores(k)
            if k + 1 < n and (k + 1) % ngrp == 0:
                prep((k + 1) // ngrp)
        if 0 <= k - 2 < n:
            values(k - 2)
        if 0 <= k - 1 < n:
            softmax(k - 1)
        yield


ATTN_STAGES = (TOKEN_TILE // ATTN_BLOCK) * (ATTN_Q_WIDTH // LANES) + 2


def _mlstm_thread(q_ref, kt_ref, v_ref, om_ref, if_ref, bif_ref, nw_ref, y_ref,
                  cn_ref, mrow_ref, mcol_ref):
    ts = TOKEN_TILE
    L = MLSTM_CHUNK
    D = MLSTM_HEAD_DIM
    nch = ts // L

    gates = if_ref[...] + bif_ref[...]
    logf_all = jax.nn.log_sigmoid(pltpu.roll(gates, LANES - MLSTM_HEADS, axis=1))

    ti = lax.broadcasted_iota(jnp.int32, (L, L), 0)
    si = lax.broadcasted_iota(jnp.int32, (L, L), 1)
    causal = si <= ti
    tri = causal.astype(BF16)
    ones_v = jnp.ones((L, LANES), BF16)
    chunk = {}
    head = {}

    def rows_of(c):
        return slice(c * L, (c + 1) * L)

    def p1(c):
        lf = logf_all[rows_of(c)]
        lf_hi = lf.astype(BF16)
        rem = lf - lf_hi.astype(F32)
        lf_mid = rem.astype(BF16)
        lf_lo = (rem - lf_mid.astype(F32)).astype(BF16)
        chunk[c] = (jnp.dot(tri, lf_hi, preferred_element_type=F32)
                    + jnp.dot(tri, lf_mid, preferred_element_type=F32)
                    + jnp.dot(tri, lf_lo, preferred_element_type=F32))

    def p2(c):
        b = chunk[c]
        r = b - gates[rows_of(c)]
        pm = -r
        sh = 1
        while sh < L:
            pm = jnp.maximum(pm, jnp.where(ti >= sh, pltpu.roll(pm, sh, axis=0), -jnp.inf))
            sh *= 2
        m_row = mrow_ref[0:1, :]
        inter = b + m_row
        mt = jnp.maximum(inter, b + pm)
        b_last = b[L - 1:L, :]
        m_new_row = jnp.maximum(b_last + m_row, jnp.max(b_last - r, axis=0, keepdims=True))
        mrow_ref[...] = jnp.broadcast_to(m_new_row, mrow_ref.shape)
        b_t = b.T[:SUBLANES]
        r_t = r.T[:SUBLANES]
        m_col = mcol_ref[:, 0:1]
        bl_col = b_t[:, L - 1:L]
        a_t = bl_col - r_t
        m_new_col = jnp.maximum(bl_col + m_col, jnp.max(a_t, axis=1, keepdims=True))
        mcol_ref[...] = jnp.broadcast_to(m_new_col, mcol_ref.shape)
        chunk[c] = dict(e_col=b - mt, r_t=r_t, w_inter=jnp.exp(inter - mt), e_mt=jnp.exp(-mt),
                        decay_row=jnp.exp(b_last + m_row - m_new_row),
                        wk_t=jnp.exp(a_t - m_new_col))

    def ha(c, h):
        hs = slice(h * D, (h + 1) * D)
        qh = q_ref[rows_of(c), hs]
        kt = kt_ref[hs, rows_of(c)]
        head[c, h] = dict(qh=qh, kt=kt, s=jnp.dot(qh, kt, preferred_element_type=F32))

    def hb(c, h):
        st, ch = head[c, h], chunk[c]
        w_intra = jnp.where(causal, jnp.exp(ch["e_col"][:, h:h + 1] - ch["r_t"][h:h + 1, :]), 0.0)
        st["pr"] = (st.pop("s") * w_intra).astype(BF16)
        st["kw"] = (st.pop("kt").astype(F32) * ch["wk_t"][h:h + 1, :]).astype(BF16)

    def hc(c, h):
        st = head[c, h]
        hs = slice(h * D, (h + 1) * D)
        v_ext = jnp.concatenate([v_ref[rows_of(c), hs], ones_v], axis=1)
        st["pv"] = jnp.dot(st.pop("pr"), v_ext, preferred_element_type=F32)
        st["qc"] = jnp.dot(st.pop("qh"), cn_ref[h].astype(BF16), preferred_element_type=F32)
        st["kv"] = jnp.dot(st.pop("kw"), v_ext, preferred_element_type=F32)

    def hd(c, h):
        st, ch = head.pop((c, h)), chunk[c]
        hs = slice(h * D, (h + 1) * D)
        nd = st["pv"] + ch["w_inter"][:, h:h + 1] * st["qc"]
        hh = nd[:, :D] / jnp.maximum(jnp.abs(nd[:, D:]), ch["e_mt"][:, h:h + 1])
        cn_ref[h] = ch["decay_row"][:, h:h + 1] * cn_ref[h] + st["kv"]
        mu = jnp.mean(hh, axis=-1, keepdims=True)
        xc = hh - mu
        var = jnp.mean(xc * xc, axis=-1, keepdims=True)
        yn = (xc * lax.rsqrt(var + NORM_EPS)) * nw_ref[:, hs]
        y_ref[rows_of(c), hs] = (jax.nn.sigmoid(om_ref[rows_of(c), hs].astype(F32)) * yn).astype(BF16)

    half = MLSTM_HEADS // 2
    for g in range(MLSTM_STAGES):
        for c in range(nch):
            o = g - MLSTM_CHUNK_SKEW * c
            if o == 0:
                p1(c)
            elif o == 1:
                for h in range(MLSTM_HEADS):
                    ha(c, h)
                p2(c)
            elif o == 2:
                for h in range(half):
                    hb(c, h)
            elif o == 3:
                for h in range(half):
                    hc(c, h)
                for h in range(half, MLSTM_HEADS):
                    hb(c, h)
            elif o == 4:
                for h in range(half, MLSTM_HEADS):
                    hc(c, h)
                for h in range(half):
                    hd(c, h)
            elif o == 5:
                for h in range(half, MLSTM_HEADS):
                    hd(c, h)
        yield


MLSTM_CHUNK_SKEW = 4
MLSTM_STAGES = MLSTM_CHUNK_SKEW * (TOKEN_TILE // MLSTM_CHUNK - 1) + 6


def _merge_thread(x_ref, ya_ref, ym_ref, ga_ref, gm_ref, mod_ref, gpm_ref, gqf_ref,
                  wa_ref, wb_ref, wo_ref, x1_ref, h2_ref):
    tm = TOKEN_TILE
    parts = [slice(r * tm // MERGE_ROW_PARTS, (r + 1) * tm // MERGE_ROW_PARTS)
             for r in range(MERGE_ROW_PARTS)]
    gate_m = mod_ref[0, 2:3, :]
    ba = jnp.dot(ya_ref[...], wa_ref[...], preferred_element_type=F32)
    yield
    bb = jnp.dot(ym_ref[...], wb_ref[...], preferred_element_type=F32)
    yield
    merged = []
    for rows in parts:
        merged.append((jax.nn.sigmoid(ga_ref[rows, :].astype(F32)) * ba[rows]
                       + jax.nn.sigmoid(gm_ref[rows, :].astype(F32)) * bb[rows]).astype(BF16))
        yield
    mix = jnp.dot(jnp.concatenate(merged, axis=0), wo_ref[...], preferred_element_type=F32)
    yield
    for rows in parts:
        x1 = x_ref[rows, :] + gate_m * _rms(mix[rows], gpm_ref[...])
        x1_ref[rows, :] = x1
        yield
        h2_ref[rows, :] = _modulated_norm(x1, gqf_ref[...], mod_ref, 3)
        yield


MERGE_ROW_PARTS = 4
MERGE_STAGES = 3 + 3 * MERGE_ROW_PARTS


def _ffn_thread(x1_ref, h2_ref, mod_ref, gpf_ref, wg_ref, wu_ref, wd_ref, o_ref, act_s, after_up):
    gate_f = mod_ref[0, 5:6, :]
    for c in range(D_FF // FF_CHUNK):
        cs = slice(c * FF_CHUNK, (c + 1) * FF_CHUNK)
        g = jnp.dot(h2_ref[...], wg_ref[:, cs], preferred_element_type=F32)
        yield
        u = jnp.dot(h2_ref[...], wu_ref[:, cs], preferred_element_type=F32)
        act_s[:, cs] = ((g * jax.nn.sigmoid(g)) * u).astype(BF16)
        yield
    after_up()
    ff = []
    for c in range(D_MODEL // FF_CHUNK):
        cs = slice(c * FF_CHUNK, (c + 1) * FF_CHUNK)
        ff.append(jnp.dot(act_s[...], wd_ref[:, cs], preferred_element_type=F32))
        yield
    o_ref[...] = x1_ref[...] + gate_f * _rms(jnp.concatenate(ff, axis=1), gpf_ref[...])
    yield


FFN_UP_STAGES = 2 * (D_FF // FF_CHUNK)
FFN_STAGES = FFN_UP_STAGES + D_MODEL // FF_CHUNK + 1


def _interleave(main, n_main, fills):
    done = [0] * len(fills)
    for k in range(n_main):
        next(main)
        for f, (gen, count, first, last) in enumerate(fills):
            span = last - first + 1
            want = 0 if k < first else min(count, ((k - first + 1) * count + span - 1) // span)
            while done[f] < want:
                next(gen)
                done[f] += 1
    assert all(d == f[1] for d, f in zip(done, fills))
    assert next(main, None) is None and all(next(f[0], None) is None for f in fills)


def _back_kernel(n_tiles, tiles_per_batch, sink_ref,
                 qa_ref, kv_ref, kvp_ref, qm_ref, kt_ref, vm_ref, om_ref, if_ref, bif_ref, nw_ref,
                 x_ref, ga_ref, gm_ref, modm_ref, gpm_ref, gqf_ref,
                 modf_ref, gpf_ref,
                 wa_ref, wb_ref, wo_ref, wg_ref, wu_ref, wd_ref, o_ref,
                 ya_s, ym_s, x1_s, h2_cur, h2_next, act_s, cn_ref, mrow_ref, mcol_ref):
    i = pl.program_id(0)
    j = jnp.minimum(i, n_tiles - 1) % tiles_per_batch
    slot = i % 2

    @pl.when(i == 0)
    def _():
        ya_s[...] = jnp.zeros_like(ya_s)
        ym_s[...] = jnp.zeros_like(ym_s)
        x1_s[...] = jnp.zeros_like(x1_s)
        h2_cur[...] = jnp.zeros_like(h2_cur)

    @pl.when(j == 0)
    def _():
        cn_ref[...] = jnp.zeros_like(cn_ref)
        mrow_ref[...] = jnp.zeros_like(mrow_ref)
        mcol_ref[...] = jnp.zeros_like(mcol_ref)

    def hand_over():
        h2_cur[...] = h2_next[...]

    ffn = _ffn_thread(x1_s.at[slot], h2_cur, modf_ref, gpf_ref, wg_ref, wu_ref, wd_ref, o_ref, act_s,
                      hand_over)
    merge = _merge_thread(x_ref, ya_s, ym_s, ga_ref, gm_ref, modm_ref, gpm_ref, gqf_ref,
                          wa_ref, wb_ref, wo_ref, x1_s.at[1 - slot], h2_next)
    attn = _attn_thread(j, sink_ref, qa_ref, kv_ref, kvp_ref, ya_s)
    mlstm = _mlstm_thread(qm_ref, kt_ref, vm_ref, om_ref, if_ref, bif_ref, nw_ref, ym_s,
                          cn_ref, mrow_ref, mcol_ref)
    _interleave(ffn, FFN_STAGES,
                [(merge, MERGE_STAGES, 0, FFN_UP_STAGES - 2),
                 (attn, ATTN_STAGES, 2, FFN_STAGES - 2),
                 (mlstm, MLSTM_STAGES, 2, FFN_STAGES - 2)])


def _back(sinks, qa, kv, qm, kt, vm, om, gates, bif, norm_w, x2, ga, gm, mod3,
          g_post_mix, g_pre_ffn, g_post_ffn, wa, wb, wo, wg, wu, wd, tiles_per_batch):
    t = x2.shape[0]
    tm = TOKEN_TILE
    n = t // tm
    per_tile = tm // ATTN_BLOCK
    clamp = lambda v: jnp.clip(v, 0, n - 1)
    cur = lambda i: (clamp(i), 0)
    cur_t = lambda i: (0, clamp(i))

    def prev_block(i):
        tile = clamp(i)
        first = (tile // tiles_per_batch) * tiles_per_batch * per_tile
        return (jnp.maximum(tile * per_tile - 1, first), 0)

    mid = lambda i: (clamp(i - 1), 0)
    last = lambda i: (clamp(i - 2), 0)
    const = lambda i: (0, 0)
    once = pl.Buffered(1)
    weight_bytes = 2 * (wa.size + wb.size + wo.size + wg.size + wu.size + wd.size)
    mixer_in = tm * (ATTN_Q_WIDTH + 2 * ATTN_KV_WIDTH + 4 * MLSTM_WIDTH) * 2 + tm * LANES * 4
    scratch_bytes = tm * (D_FF + 2 * MLSTM_WIDTH + 2 * D_MODEL) * 2 + 2 * tm * D_MODEL * 4
    est = (weight_bytes + 2 * 2 * tm * D_MODEL * 4 + 2 * tm * 2 * D_MODEL * 2 + 2 * mixer_in
           + scratch_bytes + 6 * tm * D_MODEL * 4)
    return pl.pallas_call(
        functools.partial(_back_kernel, n, tiles_per_batch),
        grid=(n + 2,),
        in_specs=[pl.BlockSpec(memory_space=pltpu.SMEM),
                  pl.BlockSpec((tm, ATTN_Q_WIDTH), cur),
                  pl.BlockSpec((tm, 2 * ATTN_KV_WIDTH), cur),
                  pl.BlockSpec((ATTN_BLOCK, 2 * ATTN_KV_WIDTH), prev_block),
                  pl.BlockSpec((tm, MLSTM_WIDTH), cur),
                  pl.BlockSpec((MLSTM_WIDTH, tm), cur_t),
                  pl.BlockSpec((tm, MLSTM_WIDTH), cur),
                  pl.BlockSpec((tm, MLSTM_WIDTH), cur),
                  pl.BlockSpec((tm, LANES), cur),
                  pl.BlockSpec((1, LANES), const),
                  pl.BlockSpec((1, MLSTM_WIDTH), const),
                  pl.BlockSpec((tm, D_MODEL), mid),
                  pl.BlockSpec((tm, D_MODEL), mid),
                  pl.BlockSpec((tm, D_MODEL), mid),
                  pl.BlockSpec((1, 6, D_MODEL), lambda i: (clamp(i - 1) // tiles_per_batch, 0, 0)),
                  pl.BlockSpec((1, D_MODEL), const),
                  pl.BlockSpec((1, D_MODEL), const),
                  pl.BlockSpec((1, 6, D_MODEL), lambda i: (clamp(i - 2) // tiles_per_batch, 0, 0)),
                  pl.BlockSpec((1, D_MODEL), const),
                  pl.BlockSpec(wa.shape, const, pipeline_mode=once),
                  pl.BlockSpec(wb.shape, const, pipeline_mode=once),
                  pl.BlockSpec(wo.shape, const, pipeline_mode=once),
                  pl.BlockSpec(wg.shape, const, pipeline_mode=once),
                  pl.BlockSpec(wu.shape, const, pipeline_mode=once),
                  pl.BlockSpec(wd.shape, const, pipeline_mode=once)],
        out_specs=pl.BlockSpec((tm, D_MODEL), last),
        out_shape=jax.ShapeDtypeStruct((t, D_MODEL), F32),
        scratch_shapes=[pltpu.VMEM((tm, ATTN_Q_WIDTH), BF16),
                        pltpu.VMEM((tm, MLSTM_WIDTH), BF16),
                        pltpu.VMEM((2, tm, D_MODEL), F32),
                        pltpu.VMEM((tm, D_MODEL), BF16),
                        pltpu.VMEM((tm, D_MODEL), BF16),
                        pltpu.VMEM((tm, D_FF), BF16),
                        pltpu.VMEM((MLSTM_HEADS, MLSTM_HEAD_DIM, 2 * MLSTM_HEAD_DIM), F32),
                        pltpu.VMEM((SUBLANES, LANES), F32),
                        pltpu.VMEM((SUBLANES, LANES), F32)],
        compiler_params=pltpu.CompilerParams(dimension_semantics=("arbitrary",),
                                             vmem_limit_bytes=_vmem_limit(est)),
        name="back",
    )(sinks, qa, kv, kv, qm, kt, vm, om, gates, bif, norm_w, x2, ga, gm, mod3,
      g_post_mix, g_pre_ffn, mod3, g_post_ffn, wa, wb, wo, wg, wu, wd)


def _q_head_order():
    per_kv = N_Q_HEADS // N_KV_HEADS
    return [h for p in range(per_kv) for h in (p, p + per_kv)]


def _layout_w_in(w_in):
    splits = np.cumsum([ATTN_Q_WIDTH, ATTN_KV_WIDTH, ATTN_KV_WIDTH, MLSTM_WIDTH, MLSTM_WIDTH,
                        MLSTM_WIDTH, MLSTM_WIDTH, MLSTM_HEADS, MLSTM_HEADS, D_MODEL]).tolist()
    q_a, k_a, v_a, q_m, k_m, v_m, o_m, i_m, f_m, g_a, g_m = jnp.split(w_in, splits, axis=1)
    q_a = q_a.reshape(D_MODEL, N_Q_HEADS, HEAD_DIM)[:, np.array(_q_head_order())].reshape(D_MODEL, -1)
    pad = jnp.zeros((D_MODEL, LANES - 2 * MLSTM_HEADS), w_in.dtype)
    return jnp.concatenate([q_a, k_a, v_a, q_m, k_m, v_m, o_m, i_m, f_m, pad, g_a, g_m],
                           axis=1).astype(BF16)


def kernel(x, c, positions, w_ada, b_ada, g_pre_mix, g_post_mix, w_in, b_if, conv_w, conv_b,
           attn_sinks, mlstm_norm_w, w_branch_attn, w_branch_mlstm, w_out, g_pre_ffn, g_post_ffn,
           w_ffn_gate, w_ffn_up, w_ffn_down):
    batch, seq, d = x.shape
    depth = w_in.shape[0]
    assert d == D_MODEL and seq % TOKEN_TILE == 0 and (batch * seq) % ROPE_TILE == 0
    assert D_FF % FF_CHUNK == 0
    t = batch * seq
    tiles_per_batch = seq // TOKEN_TILE
    x2 = x.reshape(t, d)

    inv_freq = (ROPE_THETA ** (-2.0 * jnp.arange(HEAD_DIM // 2, dtype=F32) / HEAD_DIM)).reshape(-1, 1)
    cos_t, sin_t = _rope_tables(positions.reshape(1, t), inv_freq)
    c_pad = jnp.pad(c, ((0, SUBLANES - batch % SUBLANES), (0, 0))) if batch % SUBLANES else c
    head_order = np.array(_q_head_order())

    for l in range(depth):
        mod = _ada(c_pad, w_ada[l], b_ada[l].reshape(1, -1))[:batch]
        mod3 = mod.reshape(batch, 6, d)
        qa, kv, qm, kt, vm, om, gates, ga, gm = _inproj(
            x2, mod3, g_pre_mix[l].reshape(1, d), cos_t, sin_t, _layout_w_in(w_in[l]),
            conv_w[l], conv_b[l].reshape(1, -1), tiles_per_batch)
        bif = jnp.pad(b_if[l], (0, LANES - 2 * MLSTM_HEADS)).reshape(1, LANES)
        wa = w_branch_attn[l].reshape(N_Q_HEADS, HEAD_DIM, d)[head_order].reshape(ATTN_Q_WIDTH, d)
        x2 = _back(attn_sinks[l], qa, kv, qm, kt, vm, om, gates, bif, mlstm_norm_w[l].reshape(1, -1),
                   x2, ga, gm, mod3, g_post_mix[l].reshape(1, d), g_pre_ffn[l].reshape(1, d),
                   g_post_ffn[l].reshape(1, d), wa.astype(BF16), w_branch_mlstm[l].astype(BF16),
                   w_out[l].astype(BF16), w_ffn_gate[l].astype(BF16), w_ffn_up[l].astype(BF16),
                   w_ffn_down[l].astype(BF16), tiles_per_batch)
    return x2.reshape(batch, seq, d)
```

```python
import functools

import numpy as np
import jax
import jax.numpy as jnp
from jax import lax
from jax.experimental import pallas as pl
from jax.experimental.pallas import tpu as pltpu

F32 = jnp.float32
BF16 = jnp.bfloat16

D_MODEL = 1024
N_Q_HEADS = 8
N_KV_HEADS = 2
HEAD_DIM = 64
ROPE_THETA = 10000.0
MLSTM_HEADS = 4
MLSTM_HEAD_DIM = 128
CONV_WIDTH = 4
D_FF = 2816
NORM_EPS = 1e-6
ATTN_Q_WIDTH = N_Q_HEADS * HEAD_DIM
ATTN_KV_WIDTH = N_KV_HEADS * HEAD_DIM
MLSTM_WIDTH = MLSTM_HEADS * MLSTM_HEAD_DIM
ATTN_BLOCK = 128

LANES = 128
SUBLANES = 8
MXU_WIDTH = 256
V7X_SCOPED_VMEM_BYTES = 60000 * 1024

TOKEN_TILE = 512
MLSTM_CHUNK = 128
FF_CHUNK = MXU_WIDTH
ROPE_TILE = 2048

C_QA = 0
C_KV = C_QA + ATTN_Q_WIDTH
C_QK = C_KV + 2 * ATTN_KV_WIDTH
C_VM = C_QK + 2 * MLSTM_WIDTH
C_OM = C_VM + MLSTM_WIDTH
C_IF = C_OM + MLSTM_WIDTH
C_GA = C_IF + LANES
C_GM = C_GA + D_MODEL
TAIL_COLS = C_GM + D_MODEL - C_IF


def _vmem_limit(estimate_bytes):
    return int(min(V7X_SCOPED_VMEM_BYTES, max(estimate_bytes, 16 * 1024 * 1024)))


def _rms(x, g):
    return (x * lax.rsqrt(jnp.mean(x * x, axis=-1, keepdims=True) + NORM_EPS)) * g


def _prep_kernel(pos_ref, freq_ref, c_ref, w_ref, b_ref, cos_ref, sin_ref, mod_ref):
    mod_ref[...] = jnp.dot(c_ref[...], w_ref[...], preferred_element_type=F32,
                           precision=lax.Precision.HIGHEST) + b_ref[...]
    ang = freq_ref[...] * pos_ref[...].astype(F32)
    c = jnp.cos(ang)
    s = jnp.sin(ang)
    cos_ref[...] = jnp.concatenate([c, c, c, c], axis=0).T
    sin_ref[...] = jnp.concatenate([-s, s, -s, s], axis=0).T


def _prep(pos_row, inv_freq, c_pad, w_ada, b_ada):
    t = pos_row.shape[1]
    half = inv_freq.shape[0]
    rows = c_pad.shape[0]
    n = w_ada.shape[1]
    steps = t // ROPE_TILE
    cols = n // steps
    assert n % steps == 0 and cols % LANES == 0
    return pl.pallas_call(
        _prep_kernel,
        grid=(steps,),
        in_specs=[pl.BlockSpec((1, ROPE_TILE), lambda i: (0, i)),
                  pl.BlockSpec((half, 1), lambda i: (0, 0)),
                  pl.BlockSpec((rows, D_MODEL), lambda i: (0, 0)),
                  pl.BlockSpec((D_MODEL, cols), lambda i: (0, i)),
                  pl.BlockSpec((1, cols), lambda i: (0, i))],
        out_specs=[pl.BlockSpec((ROPE_TILE, LANES), lambda i: (i, 0)),
                   pl.BlockSpec((ROPE_TILE, LANES), lambda i: (i, 0)),
                   pl.BlockSpec((rows, cols), lambda i: (0, i))],
        out_shape=[jax.ShapeDtypeStruct((t, LANES), F32),
                   jax.ShapeDtypeStruct((t, LANES), F32),
                   jax.ShapeDtypeStruct((rows, n), F32)],
        name="prep",
    )(pos_row, inv_freq, c_pad, w_ada, b_ada)


def _modulated_norm(x, g, mod_ref, row):
    return (_rms(x, g) * (1.0 + mod_ref[0, row + 1:row + 2, :]) + mod_ref[0, row:row + 1, :]).astype(BF16)


def _inproj_kernel(n_tiles, tiles_per_batch, x0_ref, mod0_ref, xn_ref, modn_ref, g_ref, cos_ref, sin_ref,
                   wh_ref, wt_ref, cw_ref, cb_ref,
                   qa_ref, kv_ref, qm_ref, kt_ref, vm_ref, om_ref, if_ref, ga_ref, gm_ref,
                   h_cur, h_next, raw_s, rot_s, wt_s):
    tm = TOKEN_TILE
    i = pl.program_id(0)
    post_tile = jnp.maximum(i - 1, 0)

    @pl.when(i == 0)
    def _():
        h_cur[...] = _modulated_norm(x0_ref[...], g_ref[...], mod0_ref, 0)
        raw_s[...] = jnp.zeros_like(raw_s)
        rot_s[...] = jnp.zeros_like(rot_s)
        n_if = 2 * MLSTM_HEADS
        lane_t = lax.broadcasted_iota(jnp.int32, (1, LANES), 1)
        for r0 in range(0, D_MODEL, LANES):
            wt = wt_ref[r0:r0 + LANES, :]
            wt_s[r0:r0 + LANES, :LANES] = jnp.where(lane_t < n_if, wt[:, :LANES], 0.0).astype(BF16)
            shifted = pltpu.roll(wt, TAIL_COLS - n_if, axis=1)
            wt_s[r0:r0 + LANES, LANES:] = shifted[:, :TAIL_COLS - LANES].astype(BF16)

    @pl.when(post_tile % tiles_per_batch == 0)
    def _():
        raw_s[0:SUBLANES, :] = jnp.zeros((SUBLANES, raw_s.shape[1]), F32)

    def proj(lo, width):
        w = wh_ref[:, lo:lo + width] if lo < C_IF else wt_s[:, lo - C_IF:lo - C_IF + width]
        return jnp.dot(h_cur[...], w, preferred_element_type=F32)

    lane = lax.broadcasted_iota(jnp.int32, (1, LANES), 1)
    first_half = (lane % HEAD_DIM) < (HEAD_DIM // 2)

    def rope(t):
        swapped = jnp.where(first_half, pltpu.roll(t, LANES - HEAD_DIM // 2, axis=1),
                            pltpu.roll(t, HEAD_DIM // 2, axis=1))
        return t * cos_ref[...] + swapped * sin_ref[...]

    def post_q():
        for p in range(ATTN_Q_WIDTH // LANES):
            sl = slice(p * LANES, (p + 1) * LANES)
            qa_ref[:, sl] = (rope(rot_s[:, sl]) * (HEAD_DIM ** -0.5)).astype(BF16)

    def post_kv():
        kv_ref[:, :LANES] = rope(rot_s[:, ATTN_Q_WIDTH:ATTN_Q_WIDTH + LANES]).astype(BF16)
        kv_ref[:, LANES:] = rot_s[:, ATTN_Q_WIDTH + LANES:].astype(BF16)

    blk = MXU_WIDTH

    def post_conv(cbk):
        cs = slice(cbk * blk, (cbk + 1) * blk)
        acc = cb_ref[:, cs]
        for sh in range(CONV_WIDTH):
            acc = acc + (cw_ref[CONV_WIDTH - 1 - sh:CONV_WIDTH - sh, cs]
                         * raw_s[SUBLANES - sh:SUBLANES - sh + tm, cs])
        raw_s[0:SUBLANES, cs] = raw_s[tm:tm + SUBLANES, cs]
        a = acc * jax.nn.sigmoid(acc)
        if cbk * blk < MLSTM_WIDTH:
            qm_ref[:, cs] = a.astype(BF16)
        else:
            a = a * (MLSTM_HEAD_DIM ** -0.5)
            for d0 in range(0, blk, LANES):
                r0 = cbk * blk - MLSTM_WIDTH + d0
                for t0 in range(0, tm, LANES):
                    kt_ref[r0:r0 + LANES, t0:t0 + LANES] = a[t0:t0 + LANES, d0:d0 + LANES].T.astype(BF16)

    def norm_next(part):
        rows = slice(part * tm // INPROJ_NORM_PARTS, (part + 1) * tm // INPROJ_NORM_PARTS)
        h_next[rows, :] = _modulated_norm(xn_ref[rows, :], g_ref[...], modn_ref, 0)

    def main_rot(lo_w, lo_s, width):
        rot_s[:, lo_s:lo_s + width] = proj(lo_w, width)

    def main_conv(cbk):
        raw_s[SUBLANES:, cbk * blk:(cbk + 1) * blk] = proj(C_QK + cbk * blk, blk)

    def main_direct(out_ref, lo_w, lo, width):
        out_ref[:, lo:lo + width] = proj(lo_w + lo, width).astype(out_ref.dtype)

    half = D_MODEL // 2
    post_q()
    main_rot(C_QA, 0, ATTN_Q_WIDTH)
    post_kv()
    post_conv(0)
    main_rot(C_KV, ATTN_Q_WIDTH, 2 * ATTN_KV_WIDTH)
    main_conv(0)
    post_conv(1)
    main_conv(1)
    main_direct(vm_ref, C_VM, 0, MLSTM_WIDTH)
    post_conv(2)
    main_conv(2)
    main_direct(om_ref, C_OM, 0, MLSTM_WIDTH)
    post_conv(3)
    main_conv(3)
    main_direct(if_ref, C_IF, 0, LANES)
    norm_next(0)
    main_direct(ga_ref, C_GA, 0, half)
    norm_next(1)
    main_direct(ga_ref, C_GA, half, half)
    norm_next(2)
    main_direct(gm_ref, C_GM, 0, half)
    norm_next(3)
    main_direct(gm_ref, C_GM, half, half)
    h_cur[...] = h_next[...]


INPROJ_NORM_PARTS = 4


def _inproj(x2, mod3, g_pre, cos_t, sin_t, w_head, w_tail, conv_w, conv_b, tiles_per_batch):
    t = x2.shape[0]
    tm = TOKEN_TILE
    n = t // tm
    main = lambda i: (jnp.minimum(i, n - 1), 0)
    post = lambda i: (jnp.maximum(i - 1, 0), 0)
    nxt = lambda i: (jnp.minimum(i + 1, n - 1), 0)
    const = lambda i: (0, 0)
    once = pl.Buffered(1)
    outs = [(ATTN_Q_WIDTH, BF16, post), (2 * ATTN_KV_WIDTH, BF16, post), (MLSTM_WIDTH, BF16, post),
            (None, BF16, None), (MLSTM_WIDTH, BF16, main), (MLSTM_WIDTH, BF16, main),
            (LANES, F32, main), (D_MODEL, BF16, main), (D_MODEL, BF16, main)]
    out_specs = [pl.BlockSpec((MLSTM_WIDTH, tm), lambda i: (0, jnp.maximum(i - 1, 0))) if w is None
                 else pl.BlockSpec((tm, w), m) for w, _, m in outs]
    out_shape = [jax.ShapeDtypeStruct((MLSTM_WIDTH, t) if w is None else (t, w), d) for w, d, _ in outs]
    out_bytes = sum((w or MLSTM_WIDTH) * np.dtype(d).itemsize for w, d, _ in outs) * tm
    rot_cols = ATTN_Q_WIDTH + 2 * ATTN_KV_WIDTH
    scratch_bytes = (2 * tm * D_MODEL * 2 + (tm + SUBLANES) * 2 * MLSTM_WIDTH * 4 + tm * rot_cols * 4
                     + D_MODEL * TAIL_COLS * 2)
    est = (w_head.size * 2 + w_tail.size * 4 + 3 * tm * D_MODEL * 4 + 2 * out_bytes
           + 4 * tm * LANES * 4 + scratch_bytes + 4 * tm * D_MODEL * 4)
    return pl.pallas_call(
        functools.partial(_inproj_kernel, n, tiles_per_batch),
        grid=(n + 1,),
        in_specs=[pl.BlockSpec((tm, D_MODEL), const, pipeline_mode=once),
                  pl.BlockSpec((1, 6, D_MODEL), lambda i: (0, 0, 0)),
                  pl.BlockSpec((tm, D_MODEL), nxt),
                  pl.BlockSpec((1, 6, D_MODEL),
                               lambda i: (jnp.minimum(i + 1, n - 1) // tiles_per_batch, 0, 0)),
                  pl.BlockSpec((1, D_MODEL), const),
                  pl.BlockSpec((tm, LANES), post),
                  pl.BlockSpec((tm, LANES), post),
                  pl.BlockSpec((D_MODEL, C_IF), const, pipeline_mode=once),
                  pl.BlockSpec((D_MODEL, TAIL_COLS), const, pipeline_mode=once),
                  pl.BlockSpec((CONV_WIDTH, 2 * MLSTM_WIDTH), const),
                  pl.BlockSpec((1, 2 * MLSTM_WIDTH), const)],
        out_specs=out_specs,
        out_shape=out_shape,
        scratch_shapes=[pltpu.VMEM((tm, D_MODEL), BF16),
                        pltpu.VMEM((tm, D_MODEL), BF16),
                        pltpu.VMEM((tm + SUBLANES, 2 * MLSTM_WIDTH), F32),
                        pltpu.VMEM((tm, rot_cols), F32),
                        pltpu.VMEM((D_MODEL, TAIL_COLS), BF16)],
        compiler_params=pltpu.CompilerParams(dimension_semantics=("arbitrary",),
                                             vmem_limit_bytes=_vmem_limit(est)),
        name="inproj",
    )(x2, mod3, x2, mod3, g_pre, cos_t, sin_t, w_head, w_tail, conv_w, conv_b)


def _attn_thread(j, sink_ref, q_ref, kv_ref, kvp_ref, o_ref):
    blk = ATTN_BLOCK
    nblk = TOKEN_TILE // blk
    ngrp = ATTN_Q_WIDTH // LANES
    lane = lax.broadcasted_iota(jnp.int32, (1, LANES), 1)
    low = lane < HEAD_DIM
    qi = lax.broadcasted_iota(jnp.int32, (blk, 2 * blk), 0)
    kj = lax.broadcasted_iota(jnp.int32, (blk, 2 * blk), 1)
    in_cur = (kj >= blk) & (kj - blk <= qi)
    in_prev = (kj < blk) & (kj > qi)
    in_prev_first = (kj < blk) & (kj > qi + jnp.where(j > 0, 0, blk))
    ones = jnp.ones((2 * blk, LANES), F32)
    zeros = jnp.zeros((2 * blk, LANES), F32)
    one_lo = jnp.where(low, ones, zeros).astype(BF16)
    one_hi = jnp.where(low, zeros, ones).astype(BF16)
    band = {}
    live = {}

    def prep(i):
        cur = kv_ref[i * blk:(i + 1) * blk, :]
        prev = kvp_ref[...] if i == 0 else kv_ref[(i - 1) * blk:i * blk, :]
        kv = jnp.concatenate([prev, cur], axis=0).astype(F32)
        k = kv[:, :LANES]
        v = kv[:, LANES:]
        k_cat = jnp.concatenate([jnp.where(low, k, 0.0), jnp.where(low, 0.0, k)], axis=0).astype(BF16)
        v_cat = jnp.concatenate(
            [jnp.concatenate([jnp.where(low, v, 0.0).astype(BF16), one_lo], axis=1),
             jnp.concatenate([jnp.where(low, 0.0, v).astype(BF16), one_hi], axis=1)], axis=0)
        band[i] = (k_cat, v_cat, in_cur | (in_prev_first if i == 0 else in_prev))

    def scores(u):
        i, p = divmod(u, ngrp)
        q = q_ref[i * blk:(i + 1) * blk, p * LANES:(p + 1) * LANES]
        live[u] = lax.dot_general(q, band[i][0], (((1,), (1,)), ((), ())), preferred_element_type=F32)

    def softmax(u):
        i, p = divmod(u, ngrp)
        s = live[u]
        mask = band[i][2]
        s0 = jnp.where(mask, s[:, :2 * blk], -jnp.inf)
        s1 = jnp.where(mask, s[:, 2 * blk:], -jnp.inf)
        m0 = jnp.maximum(jnp.max(s0, axis=-1, keepdims=True), sink_ref[p])
        m1 = jnp.maximum(jnp.max(s1, axis=-1, keepdims=True), sink_ref[p + N_Q_HEADS // N_KV_HEADS])
        pr = jnp.concatenate([jnp.exp(s0 - m0), jnp.exp(s1 - m1)], axis=1).astype(BF16)
        live[u] = (pr, m0, m1)

    def values(u):
        i, p = divmod(u, ngrp)
        pr, m0, m1 = live.pop(u)
        r = jnp.dot(pr, band[i][1], preferred_element_type=F32)
        den = r[:, LANES:] + jnp.where(low, jnp.exp(sink_ref[p] - m0),
                                       jnp.exp(sink_ref[p + N_Q_HEADS // N_KV_HEADS] - m1))
        o_ref[i * blk:(i + 1) * blk, p * LANES:(p + 1) * LANES] = (r[:, :LANES] / den).astype(BF16)

    n = nblk * ngrp
    prep(0)
    for k in range(n + 2):
        if k < n:
            scores(k)
            if k + 1 < n and (k + 1) % ngrp == 0:
                prep((k + 1) // ngrp)
        if 0 <= k - 2 < n:
            values(k - 2)
        if 0 <= k - 1 < n:
            softmax(k - 1)
        yield


ATTN_STAGES = (TOKEN_TILE // ATTN_BLOCK) * (ATTN_Q_WIDTH // LANES) + 2


def _mlstm_thread(q_ref, kt_ref, v_ref, om_ref, if_ref, bif_ref, nw_ref, y_ref,
                  cn_ref, mrow_ref, mcol_ref):
    ts = TOKEN_TILE
    L = MLSTM_CHUNK
    D = MLSTM_HEAD_DIM
    nch = ts // L

    gates = if_ref[...] + bif_ref[...]
    logf_all = jax.nn.log_sigmoid(pltpu.roll(gates, LANES - MLSTM_HEADS, axis=1))

    ti = lax.broadcasted_iota(jnp.int32, (L, L), 0)
    si = lax.broadcasted_iota(jnp.int32, (L, L), 1)
    causal = si <= ti
    tri = causal.astype(BF16)
    ones_v = jnp.ones((L, LANES), BF16)
    chunk = {}
    head = {}

    def rows_of(c):
        return slice(c * L, (c + 1) * L)

    def p1(c):
        lf = logf_all[rows_of(c)]
        lf_hi = lf.astype(BF16)
        rem = lf - lf_hi.astype(F32)
        lf_mid = rem.astype(BF16)
        lf_lo = (rem - lf_mid.astype(F32)).astype(BF16)
        chunk[c] = (jnp.dot(tri, lf_hi, preferred_element_type=F32)
                    + jnp.dot(tri, lf_mid, preferred_element_type=F32)
                    + jnp.dot(tri, lf_lo, preferred_element_type=F32))

    def p2(c):
        b = chunk[c]
        r = b - gates[rows_of(c)]
        pm = -r
        sh = 1
        while sh < L:
            pm = jnp.maximum(pm, jnp.where(ti >= sh, pltpu.roll(pm, sh, axis=0), -jnp.inf))
            sh *= 2
        m_row = mrow_ref[0:1, :]
        inter = b + m_row
        mt = jnp.maximum(inter, b + pm)
        b_last = b[L - 1:L, :]
        m_new_row = jnp.maximum(b_last + m_row, jnp.max(b_last - r, axis=0, keepdims=True))
        mrow_ref[...] = jnp.broadcast_to(m_new_row, mrow_ref.shape)
        b_t = b.T[:SUBLANES]
        r_t = r.T[:SUBLANES]
        m_col = mcol_ref[:, 0:1]
        bl_col = b_t[:, L - 1:L]
        a_t = bl_col - r_t
        m_new_col = jnp.maximum(bl_col + m_col, jnp.max(a_t, axis=1, keepdims=True))
        mcol_ref[...] = jnp.broadcast_to(m_new_col, mcol_ref.shape)
        chunk[c] = dict(e_col=b - mt, r_t=r_t, w_inter=jnp.exp(inter - mt), e_mt=jnp.exp(-mt),
                        decay_row=jnp.exp(b_last + m_row - m_new_row),
                        wk_t=jnp.exp(a_t - m_new_col))

    def ha(c, h):
        hs = slice(h * D, (h + 1) * D)
        qh = q_ref[rows_of(c), hs]
        kt = kt_ref[hs, rows_of(c)]
        head[c, h] = dict(qh=qh, kt=kt, s=jnp.dot(qh, kt, preferred_element_type=F32))

    def hb(c, h):
        st, ch = head[c, h], chunk[c]
        w_intra = jnp.where(causal, jnp.exp(ch["e_col"][:, h:h + 1] - ch["r_t"][h:h + 1, :]), 0.0)
        pr = (st.pop("s") * w_intra).astype(BF16)
        qw = (st.pop("qh").astype(F32) * ch["w_inter"][:, h:h + 1]).astype(BF16)
        st["lhs"] = jnp.concatenate([pr, qw], axis=1)
        st["kw"] = (st.pop("kt").astype(F32) * ch["wk_t"][h:h + 1, :]).astype(BF16)

    def hc(c, h):
        st = head[c, h]
        hs = slice(h * D, (h + 1) * D)
        v_ext = jnp.concatenate([v_ref[rows_of(c), hs], ones_v], axis=1)
        rhs = jnp.concatenate([v_ext, cn_ref[h].astype(BF16)], axis=0)
        st["nd"] = jnp.dot(st.pop("lhs"), rhs, preferred_element_type=F32)
        st["kv"] = jnp.dot(st.pop("kw"), v_ext, preferred_element_type=F32)

    def hd(c, h):
        st, ch = head.pop((c, h)), chunk[c]
        hs = slice(h * D, (h + 1) * D)
        nd = st["nd"]
        hh = nd[:, :D] / jnp.maximum(jnp.abs(nd[:, D:]), ch["e_mt"][:, h:h + 1])
        cn_ref[h] = ch["decay_row"][:, h:h + 1] * cn_ref[h] + st["kv"]
        mu = jnp.mean(hh, axis=-1, keepdims=True)
        xc = hh - mu
        var = jnp.mean(xc * xc, axis=-1, keepdims=True)
        yn = (xc * lax.rsqrt(var + NORM_EPS)) * nw_ref[:, hs]
        y_ref[rows_of(c), hs] = (jax.nn.sigmoid(om_ref[rows_of(c), hs].astype(F32)) * yn).astype(BF16)

    half = MLSTM_HEADS // 2
    for g in range(MLSTM_STAGES):
        for c in range(nch):
            o = g - MLSTM_CHUNK_SKEW * c
            if o == 0:
                p1(c)
            elif o == 1:
                for h in range(MLSTM_HEADS):
                    ha(c, h)
                p2(c)
            elif o == 2:
                for h in range(half):
                    hb(c, h)
            elif o == 3:
                for h in range(half):
                    hc(c, h)
                for h in range(half, MLSTM_HEADS):
                    hb(c, h)
            elif o == 4:
                for h in range(half, MLSTM_HEADS):
                    hc(c, h)
                for h in range(half):
                    hd(c, h)
            elif o == 5:
                for h in range(half, MLSTM_HEADS):
                    hd(c, h)
        yield


MLSTM_CHUNK_SKEW = 4
MLSTM_STAGES = MLSTM_CHUNK_SKEW * (TOKEN_TILE // MLSTM_CHUNK - 1) + 6


def _merge_thread(x_ref, ya_ref, ym_ref, ga_ref, gm_ref, mod_ref, gpm_ref, gqf_ref,
                  wa_ref, wb_ref, wo_ref, x1_ref, h2_ref):
    tm = TOKEN_TILE
    parts = [slice(r * tm // MERGE_ROW_PARTS, (r + 1) * tm // MERGE_ROW_PARTS)
             for r in range(MERGE_ROW_PARTS)]
    gate_m = mod_ref[0, 2:3, :]
    ba = jnp.dot(ya_ref[...], wa_ref[...], preferred_element_type=F32)
    yield
    bb = jnp.dot(ym_ref[...], wb_ref[...], preferred_element_type=F32)
    yield
    merged = []
    for rows in parts:
        merged.append((jax.nn.sigmoid(ga_ref[rows, :].astype(F32)) * ba[rows]
                       + jax.nn.sigmoid(gm_ref[rows, :].astype(F32)) * bb[rows]).astype(BF16))
        yield
    mix = jnp.dot(jnp.concatenate(merged, axis=0), wo_ref[...], preferred_element_type=F32)
    yield
    for rows in parts:
        x1 = x_ref[rows, :] + gate_m * _rms(mix[rows], gpm_ref[...])
        x1_ref[rows, :] = x1
        yield
        h2_ref[rows, :] = _modulated_norm(x1, gqf_ref[...], mod_ref, 3)
        yield


MERGE_ROW_PARTS = 4
MERGE_STAGES = 3 + 3 * MERGE_ROW_PARTS


def _ffn_thread(x1_ref, h2_ref, mod_ref, gpf_ref, wg_ref, wu_ref, wd_ref, o_ref, act_s, after_up):
    gate_f = mod_ref[0, 5:6, :]
    for c in range(D_FF // FF_CHUNK):
        cs = slice(c * FF_CHUNK, (c + 1) * FF_CHUNK)
        g = jnp.dot(h2_ref[...], wg_ref[:, cs], preferred_element_type=F32)
        yield
        u = jnp.dot(h2_ref[...], wu_ref[:, cs], preferred_element_type=F32)
        act_s[:, cs] = ((g * jax.nn.sigmoid(g)) * u).astype(BF16)
        yield
    after_up()
    ff = []
    for c in range(D_MODEL // FF_CHUNK):
        cs = slice(c * FF_CHUNK, (c + 1) * FF_CHUNK)
        ff.append(jnp.dot(act_s[...], wd_ref[:, cs], preferred_element_type=F32))
        yield
    o_ref[...] = x1_ref[...] + gate_f * _rms(jnp.concatenate(ff, axis=1), gpf_ref[...])
    yield


FFN_UP_STAGES = 2 * (D_FF // FF_CHUNK)
FFN_STAGES = FFN_UP_STAGES + D_MODEL // FF_CHUNK + 1


def _interleave(main, n_main, fills):
    done = [0] * len(fills)
    for k in range(n_main):
        next(main)
        for f, (gen, count, first, last) in enumerate(fills):
            span = last - first + 1
            want = 0 if k < first else min(count, ((k - first + 1) * count + span - 1) // span)
            while done[f] < want:
                next(gen)
                done[f] += 1
    assert all(d == f[1] for d, f in zip(done, fills))
    assert next(main, None) is None and all(next(f[0], None) is None for f in fills)


def _back_kernel(n_tiles, tiles_per_batch, sink_ref,
                 qa_ref, kv_ref, kvp_ref, qm_ref, kt_ref, vm_ref, om_ref, if_ref, bif_ref, nw_ref,
                 x_ref, ga_ref, gm_ref, modm_ref, gpm_ref, gqf_ref,
                 modf_ref, gpf_ref,
                 wa_ref, wb_ref, wo_ref, wg_ref, wu_ref, wd_ref, o_ref,
                 ya_s, ym_s, x1_s, h2_cur, h2_next, act_s, cn_ref, mrow_ref, mcol_ref):
    i = pl.program_id(0)
    j = jnp.minimum(i, n_tiles - 1) % tiles_per_batch
    slot = i % 2

    @pl.when(i == 0)
    def _():
        ya_s[...] = jnp.zeros_like(ya_s)
        ym_s[...] = jnp.zeros_like(ym_s)
        x1_s[...] = jnp.zeros_like(x1_s)
        h2_cur[...] = jnp.zeros_like(h2_cur)

    @pl.when(j == 0)
    def _():
        cn_ref[...] = jnp.zeros_like(cn_ref)
        mrow_ref[...] = jnp.zeros_like(mrow_ref)
        mcol_ref[...] = jnp.zeros_like(mcol_ref)

    def hand_over():
        h2_cur[...] = h2_next[...]

    ffn = _ffn_thread(x1_s.at[slot], h2_cur, modf_ref, gpf_ref, wg_ref, wu_ref, wd_ref, o_ref, act_s,
                      hand_over)
    merge = _merge_thread(x_ref, ya_s, ym_s, ga_ref, gm_ref, modm_ref, gpm_ref, gqf_ref,
                          wa_ref, wb_ref, wo_ref, x1_s.at[1 - slot], h2_next)
    attn = _attn_thread(j, sink_ref, qa_ref, kv_ref, kvp_ref, ya_s)
    mlstm = _mlstm_thread(qm_ref, kt_ref, vm_ref, om_ref, if_ref, bif_ref, nw_ref, ym_s,
                          cn_ref, mrow_ref, mcol_ref)
    _interleave(ffn, FFN_STAGES,
                [(merge, MERGE_STAGES, 0, FFN_UP_STAGES - 2),
                 (attn, ATTN_STAGES, 2, FFN_STAGES - 2),
                 (mlstm, MLSTM_STAGES, 2, FFN_STAGES - 2)])


def _back(sinks, qa, kv, qm, kt, vm, om, gates, bif, norm_w, x2, ga, gm, mod3,
          g_post_mix, g_pre_ffn, g_post_ffn, wa, wb, wo, wg, wu, wd, tiles_per_batch):
    t = x2.shape[0]
    tm = TOKEN_TILE
    n = t // tm
    per_tile = tm // ATTN_BLOCK
    clamp = lambda v: jnp.clip(v, 0, n - 1)
    cur = lambda i: (clamp(i), 0)
    cur_t = lambda i: (0, clamp(i))

    def prev_block(i):
        tile = clamp(i)
        first = (tile // tiles_per_batch) * tiles_per_batch * per_tile
        return (jnp.maximum(tile * per_tile - 1, first), 0)

    mid = lambda i: (clamp(i - 1), 0)
    last = lambda i: (clamp(i - 2), 0)
    const = lambda i: (0, 0)
    once = pl.Buffered(1)
    weight_bytes = 2 * (wa.size + wb.size + wo.size + wg.size + wu.size + wd.size)
    mixer_in = tm * (ATTN_Q_WIDTH + 2 * ATTN_KV_WIDTH + 4 * MLSTM_WIDTH) * 2 + tm * LANES * 4
    scratch_bytes = tm * (D_FF + 2 * MLSTM_WIDTH + 2 * D_MODEL) * 2 + 2 * tm * D_MODEL * 4
    est = (weight_bytes + 2 * 2 * tm * D_MODEL * 4 + 2 * tm * 2 * D_MODEL * 2 + 2 * mixer_in
           + scratch_bytes + 6 * tm * D_MODEL * 4)
    return pl.pallas_call(
        functools.partial(_back_kernel, n, tiles_per_batch),
        grid=(n + 2,),
        in_specs=[pl.BlockSpec(memory_space=pltpu.SMEM),
                  pl.BlockSpec((tm, ATTN_Q_WIDTH), cur),
                  pl.BlockSpec((tm, 2 * ATTN_KV_WIDTH), cur),
                  pl.BlockSpec((ATTN_BLOCK, 2 * ATTN_KV_WIDTH), prev_block),
                  pl.BlockSpec((tm, MLSTM_WIDTH), cur),
                  pl.BlockSpec((MLSTM_WIDTH, tm), cur_t),
                  pl.BlockSpec((tm, MLSTM_WIDTH), cur),
                  pl.BlockSpec((tm, MLSTM_WIDTH), cur),
                  pl.BlockSpec((tm, LANES), cur),
                  pl.BlockSpec((1, LANES), const),
                  pl.BlockSpec((1, MLSTM_WIDTH), const),
                  pl.BlockSpec((tm, D_MODEL), mid),
                  pl.BlockSpec((tm, D_MODEL), mid),
                  pl.BlockSpec((tm, D_MODEL), mid),
                  pl.BlockSpec((1, 6, D_MODEL), lambda i: (clamp(i - 1) // tiles_per_batch, 0, 0)),
                  pl.BlockSpec((1, D_MODEL), const),
                  pl.BlockSpec((1, D_MODEL), const),
                  pl.BlockSpec((1, 6, D_MODEL), lambda i: (clamp(i - 2) // tiles_per_batch, 0, 0)),
                  pl.BlockSpec((1, D_MODEL), const),
                  pl.BlockSpec(wa.shape, const, pipeline_mode=once),
                  pl.BlockSpec(wb.shape, const, pipeline_mode=once),
                  pl.BlockSpec(wo.shape, const, pipeline_mode=once),
                  pl.BlockSpec(wg.shape, const, pipeline_mode=once),
                  pl.BlockSpec(wu.shape, const, pipeline_mode=once),
                  pl.BlockSpec(wd.shape, const, pipeline_mode=once)],
        out_specs=pl.BlockSpec((tm, D_MODEL), last),
        out_shape=jax.ShapeDtypeStruct((t, D_MODEL), F32),
        scratch_shapes=[pltpu.VMEM((tm, ATTN_Q_WIDTH), BF16),
                        pltpu.VMEM((tm, MLSTM_WIDTH), BF16),
                        pltpu.VMEM((2, tm, D_MODEL), F32),
                        pltpu.VMEM((tm, D_MODEL), BF16),
                        pltpu.VMEM((tm, D_MODEL), BF16),
                        pltpu.VMEM((tm, D_FF), BF16),
                        pltpu.VMEM((MLSTM_HEADS, MLSTM_HEAD_DIM, 2 * MLSTM_HEAD_DIM), F32),
                        pltpu.VMEM((SUBLANES, LANES), F32),
                        pltpu.VMEM((SUBLANES, LANES), F32)],
        compiler_params=pltpu.CompilerParams(dimension_semantics=("arbitrary",),
                                             vmem_limit_bytes=_vmem_limit(est)),
        name="back",
    )(sinks, qa, kv, kv, qm, kt, vm, om, gates, bif, norm_w, x2, ga, gm, mod3,
      g_post_mix, g_pre_ffn, mod3, g_post_ffn, wa, wb, wo, wg, wu, wd)


def _q_head_order():
    per_kv = N_Q_HEADS // N_KV_HEADS
    return [h for p in range(per_kv) for h in (p, p + per_kv)]


def _layout_w_in(w_in):
    q_a = w_in[:, :ATTN_Q_WIDTH].reshape(D_MODEL, N_Q_HEADS, HEAD_DIM)[:, np.array(_q_head_order())]
    head = jnp.concatenate([q_a.reshape(D_MODEL, ATTN_Q_WIDTH), w_in[:, ATTN_Q_WIDTH:C_IF]], axis=1)
    tail = jnp.pad(w_in[:, C_IF:], ((0, 0), (0, TAIL_COLS - (w_in.shape[1] - C_IF))))
    return head.astype(BF16), tail


def kernel(x, c, positions, w_ada, b_ada, g_pre_mix, g_post_mix, w_in, b_if, conv_w, conv_b,
           attn_sinks, mlstm_norm_w, w_branch_attn, w_branch_mlstm, w_out, g_pre_ffn, g_post_ffn,
           w_ffn_gate, w_ffn_up, w_ffn_down):
    batch, seq, d = x.shape
    depth = w_in.shape[0]
    assert d == D_MODEL and seq % TOKEN_TILE == 0 and (batch * seq) % ROPE_TILE == 0
    assert D_FF % FF_CHUNK == 0
    t = batch * seq
    tiles_per_batch = seq // TOKEN_TILE
    x2 = x.reshape(t, d)

    inv_freq = (ROPE_THETA ** (-2.0 * jnp.arange(HEAD_DIM // 2, dtype=F32) / HEAD_DIM)).reshape(-1, 1)
    c_pad = jnp.pad(c, ((0, SUBLANES - batch % SUBLANES), (0, 0))) if batch % SUBLANES else c
    head_order = np.array(_q_head_order())

    for l in range(depth):
        cos_t, sin_t, mod = _prep(positions.reshape(1, t), inv_freq, c_pad, w_ada[l],
                                  b_ada[l].reshape(1, -1))
        mod3 = mod[:batch].reshape(batch, 6, d)
        w_head, w_tail = _layout_w_in(w_in[l])
        qa, kv, qm, kt, vm, om, gates, ga, gm = _inproj(
            x2, mod3, g_pre_mix[l].reshape(1, d), cos_t, sin_t, w_head, w_tail,
            conv_w[l], conv_b[l].reshape(1, -1), tiles_per_batch)
        bif = jnp.pad(b_if[l], (0, LANES - 2 * MLSTM_HEADS)).reshape(1, LANES)
        wa = w_branch_attn[l].reshape(N_Q_HEADS, HEAD_DIM, d)[head_order].reshape(ATTN_Q_WIDTH, d)
        x2 = _back(attn_sinks[l], qa, kv, qm, kt, vm, om, gates, bif, mlstm_norm_w[l].reshape(1, -1),
                   x2, ga, gm, mod3, g_post_mix[l].reshape(1, d), g_pre_ffn[l].reshape(1, d),
                   g_post_ffn[l].reshape(1, d), wa.astype(BF16), w_branch_mlstm[l].astype(BF16),
                   w_out[l].astype(BF16), w_ffn_gate[l].astype(BF16), w_ffn_up[l].astype(BF16),
                   w_ffn_down[l].astype(BF16), tiles_per_batch)
    return x2.reshape(batch, seq, d)
```

```python
import functools

import numpy as np
import jax
import jax.numpy as jnp
from jax import lax
from jax.experimental import pallas as pl
from jax.experimental.pallas import tpu as pltpu

F32 = jnp.float32
BF16 = jnp.bfloat16

D_MODEL = 1024
N_Q_HEADS = 8
N_KV_HEADS = 2
HEAD_DIM = 64
ROPE_THETA = 10000.0
MLSTM_HEADS = 4
MLSTM_HEAD_DIM = 128
CONV_WIDTH = 4
D_FF = 2816
NORM_EPS = 1e-6
ATTN_Q_WIDTH = N_Q_HEADS * HEAD_DIM
ATTN_KV_WIDTH = N_KV_HEADS * HEAD_DIM
MLSTM_WIDTH = MLSTM_HEADS * MLSTM_HEAD_DIM
ATTN_BLOCK = 128

LANES = 128
SUBLANES = 8
MXU_WIDTH = 256
V7X_SCOPED_VMEM_BYTES = 60000 * 1024

TOKEN_TILE = 512
MLSTM_CHUNK = 128
FF_CHUNK = MXU_WIDTH
ROPE_TILE = 2048

C_QA = 0
C_KV = C_QA + ATTN_Q_WIDTH
C_QK = C_KV + 2 * ATTN_KV_WIDTH
C_VM = C_QK + 2 * MLSTM_WIDTH
C_OM = C_VM + MLSTM_WIDTH
C_IF = C_OM + MLSTM_WIDTH
C_GA = C_IF + LANES
C_GM = C_GA + D_MODEL
IN_COLS = C_GM + D_MODEL


def _vmem_limit(estimate_bytes):
    return int(min(V7X_SCOPED_VMEM_BYTES, max(estimate_bytes, 16 * 1024 * 1024)))


def _rms(x, g):
    return (x * lax.rsqrt(jnp.mean(x * x, axis=-1, keepdims=True) + NORM_EPS)) * g


def _prep_kernel(pos_ref, freq_ref, c_ref, w_ref, b_ref, cos_ref, sin_ref, mod_ref):
    mod_ref[...] = jnp.dot(c_ref[...], w_ref[...], preferred_element_type=F32,
                           precision=lax.Precision.HIGHEST) + b_ref[...]
    ang = freq_ref[...] * pos_ref[...].astype(F32)
    c = jnp.cos(ang)
    s = jnp.sin(ang)
    cos_ref[...] = jnp.concatenate([c, c, c, c], axis=0).T
    sin_ref[...] = jnp.concatenate([-s, s, -s, s], axis=0).T


def _prep(pos_row, inv_freq, c_pad, w_ada, b_ada):
    t = pos_row.shape[1]
    half = inv_freq.shape[0]
    rows = c_pad.shape[0]
    n = w_ada.shape[1]
    steps = t // ROPE_TILE
    cols = n // steps
    assert n % steps == 0 and cols % LANES == 0
    return pl.pallas_call(
        _prep_kernel,
        grid=(steps,),
        in_specs=[pl.BlockSpec((1, ROPE_TILE), lambda i: (0, i)),
                  pl.BlockSpec((half, 1), lambda i: (0, 0)),
                  pl.BlockSpec((rows, D_MODEL), lambda i: (0, 0)),
                  pl.BlockSpec((D_MODEL, cols), lambda i: (0, i)),
                  pl.BlockSpec((1, cols), lambda i: (0, i))],
        out_specs=[pl.BlockSpec((ROPE_TILE, LANES), lambda i: (i, 0)),
                   pl.BlockSpec((ROPE_TILE, LANES), lambda i: (i, 0)),
                   pl.BlockSpec((rows, cols), lambda i: (0, i))],
        out_shape=[jax.ShapeDtypeStruct((t, LANES), F32),
                   jax.ShapeDtypeStruct((t, LANES), F32),
                   jax.ShapeDtypeStruct((rows, n), F32)],
        name="prep",
    )(pos_row, inv_freq, c_pad, w_ada, b_ada)


def _modulated_norm(x, g, mod_ref, row):
    return (_rms(x, g) * (1.0 + mod_ref[0, row + 1:row + 2, :]) + mod_ref[0, row:row + 1, :]).astype(BF16)


def _inproj_kernel(n_tiles, tiles_per_batch, x0_ref, mod0_ref, xn_ref, modn_ref, g_ref, cos_ref, sin_ref,
                   w_ref, cw_ref, cb_ref,
                   qa_ref, kv_ref, qm_ref, kt_ref, vm_ref, om_ref, if_ref, ga_ref, gm_ref,
                   h_cur, h_next, raw_s, rot_s):
    tm = TOKEN_TILE
    i = pl.program_id(0)
    post_tile = jnp.maximum(i - 1, 0)

    @pl.when(i == 0)
    def _():
        h_cur[...] = _modulated_norm(x0_ref[...], g_ref[...], mod0_ref, 0)
        raw_s[...] = jnp.zeros_like(raw_s)
        rot_s[...] = jnp.zeros_like(rot_s)

    @pl.when(post_tile % tiles_per_batch == 0)
    def _():
        raw_s[0:SUBLANES, :] = jnp.zeros((SUBLANES, raw_s.shape[1]), F32)

    def proj(lo, width):
        return lax.dot_general(h_cur[...], w_ref[lo:lo + width, :], (((1,), (1,)), ((), ())),
                               preferred_element_type=F32)

    lane = lax.broadcasted_iota(jnp.int32, (1, LANES), 1)
    first_half = (lane % HEAD_DIM) < (HEAD_DIM // 2)

    def rope(t):
        swapped = jnp.where(first_half, pltpu.roll(t, LANES - HEAD_DIM // 2, axis=1),
                            pltpu.roll(t, HEAD_DIM // 2, axis=1))
        return t * cos_ref[...] + swapped * sin_ref[...]

    def post_q():
        for p in range(ATTN_Q_WIDTH // LANES):
            sl = slice(p * LANES, (p + 1) * LANES)
            qa_ref[:, sl] = (rope(rot_s[:, sl]) * (HEAD_DIM ** -0.5)).astype(BF16)

    def post_kv():
        kv_ref[:, :LANES] = rope(rot_s[:, ATTN_Q_WIDTH:ATTN_Q_WIDTH + LANES]).astype(BF16)
        kv_ref[:, LANES:] = rot_s[:, ATTN_Q_WIDTH + LANES:].astype(BF16)

    blk = MXU_WIDTH

    def post_conv(cbk):
        cs = slice(cbk * blk, (cbk + 1) * blk)
        acc = cb_ref[:, cs]
        for sh in range(CONV_WIDTH):
            acc = acc + (cw_ref[CONV_WIDTH - 1 - sh:CONV_WIDTH - sh, cs]
                         * raw_s[SUBLANES - sh:SUBLANES - sh + tm, cs])
        raw_s[0:SUBLANES, cs] = raw_s[tm:tm + SUBLANES, cs]
        a = acc * jax.nn.sigmoid(acc)
        if cbk * blk < MLSTM_WIDTH:
            qm_ref[:, cs] = a.astype(BF16)
        else:
            a = a * (MLSTM_HEAD_DIM ** -0.5)
            for d0 in range(0, blk, LANES):
                r0 = cbk * blk - MLSTM_WIDTH + d0
                for t0 in range(0, tm, LANES):
                    kt_ref[r0:r0 + LANES, t0:t0 + LANES] = a[t0:t0 + LANES, d0:d0 + LANES].T.astype(BF16)

    def norm_next(part):
        rows = slice(part * tm // INPROJ_NORM_PARTS, (part + 1) * tm // INPROJ_NORM_PARTS)
        h_next[rows, :] = _modulated_norm(xn_ref[rows, :], g_ref[...], modn_ref, 0)

    def main_rot(lo_w, lo_s, width):
        rot_s[:, lo_s:lo_s + width] = proj(lo_w, width)

    def main_conv(cbk):
        raw_s[SUBLANES:, cbk * blk:(cbk + 1) * blk] = proj(C_QK + cbk * blk, blk)

    def main_direct(out_ref, lo_w, lo, width):
        out_ref[:, lo:lo + width] = proj(lo_w + lo, width).astype(out_ref.dtype)

    half = D_MODEL // 2
    post_q()
    main_rot(C_QA, 0, ATTN_Q_WIDTH)
    post_kv()
    post_conv(0)
    main_rot(C_KV, ATTN_Q_WIDTH, 2 * ATTN_KV_WIDTH)
    main_conv(0)
    post_conv(1)
    main_conv(1)
    main_direct(vm_ref, C_VM, 0, MLSTM_WIDTH)
    post_conv(2)
    main_conv(2)
    main_direct(om_ref, C_OM, 0, MLSTM_WIDTH)
    post_conv(3)
    main_conv(3)
    main_direct(if_ref, C_IF, 0, LANES)
    norm_next(0)
    main_direct(ga_ref, C_GA, 0, half)
    norm_next(1)
    main_direct(ga_ref, C_GA, half, half)
    norm_next(2)
    main_direct(gm_ref, C_GM, 0, half)
    norm_next(3)
    main_direct(gm_ref, C_GM, half, half)
    h_cur[...] = h_next[...]


INPROJ_NORM_PARTS = 4


def _inproj(x2, mod3, g_pre, cos_t, sin_t, w_t, conv_w, conv_b, tiles_per_batch):
    t = x2.shape[0]
    tm = TOKEN_TILE
    n = t // tm
    main = lambda i: (jnp.minimum(i, n - 1), 0)
    post = lambda i: (jnp.maximum(i - 1, 0), 0)
    nxt = lambda i: (jnp.minimum(i + 1, n - 1), 0)
    const = lambda i: (0, 0)
    once = pl.Buffered(1)
    outs = [(ATTN_Q_WIDTH, BF16, post), (2 * ATTN_KV_WIDTH, BF16, post), (MLSTM_WIDTH, BF16, post),
            (None, BF16, None), (MLSTM_WIDTH, BF16, main), (MLSTM_WIDTH, BF16, main),
            (LANES, F32, main), (D_MODEL, BF16, main), (D_MODEL, BF16, main)]
    out_specs = [pl.BlockSpec((MLSTM_WIDTH, tm), lambda i: (0, jnp.maximum(i - 1, 0))) if w is None
                 else pl.BlockSpec((tm, w), m) for w, _, m in outs]
    out_shape = [jax.ShapeDtypeStruct((MLSTM_WIDTH, t) if w is None else (t, w), d) for w, d, _ in outs]
    out_bytes = sum((w or MLSTM_WIDTH) * np.dtype(d).itemsize for w, d, _ in outs) * tm
    rot_cols = ATTN_Q_WIDTH + 2 * ATTN_KV_WIDTH
    scratch_bytes = (2 * tm * D_MODEL * 2 + (tm + SUBLANES) * 2 * MLSTM_WIDTH * 4 + tm * rot_cols * 4)
    est = (w_t.size * 2 + 3 * tm * D_MODEL * 4 + 2 * out_bytes + 4 * tm * LANES * 4
           + scratch_bytes + 4 * tm * D_MODEL * 4)
    return pl.pallas_call(
        functools.partial(_inproj_kernel, n, tiles_per_batch),
        grid=(n + 1,),
        in_specs=[pl.BlockSpec((tm, D_MODEL), const, pipeline_mode=once),
                  pl.BlockSpec((1, 6, D_MODEL), lambda i: (0, 0, 0)),
                  pl.BlockSpec((tm, D_MODEL), nxt),
                  pl.BlockSpec((1, 6, D_MODEL),
                               lambda i: (jnp.minimum(i + 1, n - 1) // tiles_per_batch, 0, 0)),
                  pl.BlockSpec((1, D_MODEL), const),
                  pl.BlockSpec((tm, LANES), post),
                  pl.BlockSpec((tm, LANES), post),
                  pl.BlockSpec((IN_COLS, D_MODEL), const, pipeline_mode=once),
                  pl.BlockSpec((CONV_WIDTH, 2 * MLSTM_WIDTH), const),
                  pl.BlockSpec((1, 2 * MLSTM_WIDTH), const)],
        out_specs=out_specs,
        out_shape=out_shape,
        scratch_shapes=[pltpu.VMEM((tm, D_MODEL), BF16),
                        pltpu.VMEM((tm, D_MODEL), BF16),
                        pltpu.VMEM((tm + SUBLANES, 2 * MLSTM_WIDTH), F32),
                        pltpu.VMEM((tm, rot_cols), F32)],
        compiler_params=pltpu.CompilerParams(dimension_semantics=("arbitrary",),
                                             vmem_limit_bytes=_vmem_limit(est)),
        name="inproj",
    )(x2, mod3, x2, mod3, g_pre, cos_t, sin_t, w_t, conv_w, conv_b)


def _attn_thread(j, sink_ref, q_ref, kv_ref, kvp_ref, o_ref):
    blk = ATTN_BLOCK
    nblk = TOKEN_TILE // blk
    ngrp = ATTN_Q_WIDTH // LANES
    lane = lax.broadcasted_iota(jnp.int32, (1, LANES), 1)
    low = lane < HEAD_DIM
    qi = lax.broadcasted_iota(jnp.int32, (blk, 2 * blk), 0)
    kj = lax.broadcasted_iota(jnp.int32, (blk, 2 * blk), 1)
    in_cur = (kj >= blk) & (kj - blk <= qi)
    in_prev = (kj < blk) & (kj > qi)
    in_prev_first = (kj < blk) & (kj > qi + jnp.where(j > 0, 0, blk))
    ones = jnp.ones((2 * blk, LANES), F32)
    zeros = jnp.zeros((2 * blk, LANES), F32)
    one_lo = jnp.where(low, ones, zeros).astype(BF16)
    one_hi = jnp.where(low, zeros, ones).astype(BF16)
    band = {}
    live = {}

    def prep(i):
        cur = kv_ref[i * blk:(i + 1) * blk, :]
        prev = kvp_ref[...] if i == 0 else kv_ref[(i - 1) * blk:i * blk, :]
        kv = jnp.concatenate([prev, cur], axis=0).astype(F32)
        k = kv[:, :LANES]
        v = kv[:, LANES:]
        k_cat = jnp.concatenate([jnp.where(low, k, 0.0), jnp.where(low, 0.0, k)], axis=0).astype(BF16)
        v_cat = jnp.concatenate(
            [jnp.concatenate([jnp.where(low, v, 0.0).astype(BF16), one_lo], axis=1),
             jnp.concatenate([jnp.where(low, 0.0, v).astype(BF16), one_hi], axis=1)], axis=0)
        band[i] = (k_cat, v_cat, in_cur | (in_prev_first if i == 0 else in_prev))

    def scores(u):
        i, p = divmod(u, ngrp)
        q = q_ref[i * blk:(i + 1) * blk, p * LANES:(p + 1) * LANES]
        live[u] = lax.dot_general(q, band[i][0], (((1,), (1,)), ((), ())), preferred_element_type=F32)

    def softmax(u):
        i, p = divmod(u, ngrp)
        s = live[u]
        mask = band[i][2]
        s0 = jnp.where(mask, s[:, :2 * blk], -jnp.inf)
        s1 = jnp.where(mask, s[:, 2 * blk:], -jnp.inf)
        m0 = jnp.maximum(jnp.max(s0, axis=-1, keepdims=True), sink_ref[p])
        m1 = jnp.maximum(jnp.max(s1, axis=-1, keepdims=True), sink_ref[p + N_Q_HEADS // N_KV_HEADS])
        pr = jnp.concatenate([jnp.exp(s0 - m0), jnp.exp(s1 - m1)], axis=1).astype(BF16)
        live[u] = (pr, m0, m1)

    def values(u):
        i, p = divmod(u, ngrp)
        pr, m0, m1 = live.pop(u)
        r = jnp.dot(pr, band[i][1], preferred_element_type=F32)
        den = r[:, LANES:] + jnp.where(low, jnp.exp(sink_ref[p] - m0),
                                       jnp.exp(sink_ref[p + N_Q_HEADS // N_KV_HEADS] - m1))
        o_ref[i * blk:(i + 1) * blk, p * LANES:(p + 1) * LANES] = (r[:, :LANES] / den).astype(BF16)

    n = nblk * ngrp
    prep(0)
    for k in range(n + 2):
        if k < n:
            scores(k)
            if k + 1 < n and (k + 1) % ngrp == 0:
                prep((k + 1) // ngrp)
        if 0 <= k - 2 < n:
            values(k - 2)
        if 0 <= k - 1 < n:
            softmax(k - 1)
        yield


ATTN_STAGES = (TOKEN_TILE // ATTN_BLOCK) * (ATTN_Q_WIDTH // LANES) + 2


def _mlstm_thread(q_ref, kt_ref, v_ref, om_ref, if_ref, bif_ref, nw_ref, y_ref,
                  cn_ref, mrow_ref, mcol_ref):
    ts = TOKEN_TILE
    L = MLSTM_CHUNK
    D = MLSTM_HEAD_DIM
    nch = ts // L

    gates = if_ref[...] + bif_ref[...]
    logf_all = jax.nn.log_sigmoid(pltpu.roll(gates, LANES - MLSTM_HEADS, axis=1))

    ti = lax.broadcasted_iota(jnp.int32, (L, L), 0)
    si = lax.broadcasted_iota(jnp.int32, (L, L), 1)
    causal = si <= ti
    tri = causal.astype(BF16)
    ones_v = jnp.ones((L, LANES), BF16)
    chunk = {}
    head = {}

    def rows_of(c):
        return slice(c * L, (c + 1) * L)

    def p1(c):
        lf = logf_all[rows_of(c)]
        lf_hi = lf.astype(BF16)
        rem = lf - lf_hi.astype(F32)
        lf_mid = rem.astype(BF16)
        lf_lo = (rem - lf_mid.astype(F32)).astype(BF16)
        chunk[c] = (jnp.dot(tri, lf_hi, preferred_element_type=F32)
                    + jnp.dot(tri, lf_mid, preferred_element_type=F32)
                    + jnp.dot(tri, lf_lo, preferred_element_type=F32))

    def p2(c):
        b = chunk[c]
        r = b - gates[rows_of(c)]
        pm = -r
        sh = 1
        while sh < L:
            pm = jnp.maximum(pm, jnp.where(ti >= sh, pltpu.roll(pm, sh, axis=0), -jnp.inf))
            sh *= 2
        m_row = mrow_ref[0:1, :]
        inter = b + m_row
        mt = jnp.maximum(inter, b + pm)
        b_last = b[L - 1:L, :]
        m_new_row = jnp.maximum(b_last + m_row, jnp.max(b_last - r, axis=0, keepdims=True))
        mrow_ref[...] = jnp.broadcast_to(m_new_row, mrow_ref.shape)
        b_t = b.T[:SUBLANES]
        r_t = r.T[:SUBLANES]
        m_col = mcol_ref[:, 0:1]
        bl_col = b_t[:, L - 1:L]
        a_t = bl_col - r_t
        m_new_col = jnp.maximum(bl_col + m_col, jnp.max(a_t, axis=1, keepdims=True))
        mcol_ref[...] = jnp.broadcast_to(m_new_col, mcol_ref.shape)
        chunk[c] = dict(e_col=b - mt, r_t=r_t, w_inter=jnp.exp(inter - mt), e_mt=jnp.exp(-mt),
                        decay_row=jnp.exp(b_last + m_row - m_new_row),
                        wk_t=jnp.exp(a_t - m_new_col))

    def ha(c, h):
        hs = slice(h * D, (h + 1) * D)
        qh = q_ref[rows_of(c), hs]
        kt = kt_ref[hs, rows_of(c)]
        head[c, h] = dict(qh=qh, kt=kt, s=jnp.dot(qh, kt, preferred_element_type=F32))

    def hb(c, h):
        st, ch = head[c, h], chunk[c]
        w_intra = jnp.where(causal, jnp.exp(ch["e_col"][:, h:h + 1] - ch["r_t"][h:h + 1, :]), 0.0)
        pr = (st.pop("s") * w_intra).astype(BF16)
        qw = (st.pop("qh").astype(F32) * ch["w_inter"][:, h:h + 1]).astype(BF16)
        st["lhs"] = jnp.concatenate([pr, qw], axis=1)
        st["kw"] = (st.pop("kt").astype(F32) * ch["wk_t"][h:h + 1, :]).astype(BF16)

    def hc(c, h):
        st = head[c, h]
        hs = slice(h * D, (h + 1) * D)
        v_ext = jnp.concatenate([v_ref[rows_of(c), hs], ones_v], axis=1)
        rhs = jnp.concatenate([v_ext, cn_ref[h].astype(BF16)], axis=0)
        st["nd"] = jnp.dot(st.pop("lhs"), rhs, preferred_element_type=F32)
        st["kv"] = jnp.dot(st.pop("kw"), v_ext, preferred_element_type=F32)

    def hd(c, h):
        st, ch = head.pop((c, h)), chunk[c]
        hs = slice(h * D, (h + 1) * D)
        nd = st["nd"]
        hh = nd[:, :D] / jnp.maximum(jnp.abs(nd[:, D:]), ch["e_mt"][:, h:h + 1])
        cn_ref[h] = ch["decay_row"][:, h:h + 1] * cn_ref[h] + st["kv"]
        mu = jnp.mean(hh, axis=-1, keepdims=True)
        xc = hh - mu
        var = jnp.mean(xc * xc, axis=-1, keepdims=True)
        yn = (xc * lax.rsqrt(var + NORM_EPS)) * nw_ref[:, hs]
        y_ref[rows_of(c), hs] = (jax.nn.sigmoid(om_ref[rows_of(c), hs].astype(F32)) * yn).astype(BF16)

    half = MLSTM_HEADS // 2
    for g in range(MLSTM_STAGES):
        for c in range(nch):
            o = g - MLSTM_CHUNK_SKEW * c
            if o == 0:
                p1(c)
            elif o == 1:
                for h in range(MLSTM_HEADS):
                    ha(c, h)
                p2(c)
            elif o == 2:
                for h in range(half):
                    hb(c, h)
            elif o == 3:
                for h in range(half):
                    hc(c, h)
                for h in range(half, MLSTM_HEADS):
                    hb(c, h)
            elif o == 4:
                for h in range(half, MLSTM_HEADS):
                    hc(c, h)
                for h in range(half):
                    hd(c, h)
            elif o == 5:
                for h in range(half, MLSTM_HEADS):
                    hd(c, h)
        yield


MLSTM_CHUNK_SKEW = 4
MLSTM_STAGES = MLSTM_CHUNK_SKEW * (TOKEN_TILE // MLSTM_CHUNK - 1) + 6


def _merge_thread(x_ref, ya_ref, ym_ref, ga_ref, gm_ref, mod_ref, gpm_ref, gqf_ref,
                  wa_ref, wb_ref, wo_ref, x1_ref, h2_ref):
    tm = TOKEN_TILE
    parts = [slice(r * tm // MERGE_ROW_PARTS, (r + 1) * tm // MERGE_ROW_PARTS)
             for r in range(MERGE_ROW_PARTS)]
    gate_m = mod_ref[0, 2:3, :]
    ba = jnp.dot(ya_ref[...], wa_ref[...], preferred_element_type=F32)
    yield
    bb = jnp.dot(ym_ref[...], wb_ref[...], preferred_element_type=F32)
    yield
    merged = []
    for rows in parts:
        merged.append((jax.nn.sigmoid(ga_ref[rows, :].astype(F32)) * ba[rows]
                       + jax.nn.sigmoid(gm_ref[rows, :].astype(F32)) * bb[rows]).astype(BF16))
        yield
    mix = jnp.dot(jnp.concatenate(merged, axis=0), wo_ref[...], preferred_element_type=F32)
    yield
    for rows in parts:
        x1 = x_ref[rows, :] + gate_m * _rms(mix[rows], gpm_ref[...])
        x1_ref[rows, :] = x1
        yield
        h2_ref[rows, :] = _modulated_norm(x1, gqf_ref[...], mod_ref, 3)
        yield


MERGE_ROW_PARTS = 4
MERGE_STAGES = 3 + 3 * MERGE_ROW_PARTS


def _ffn_thread(x1_ref, h2_ref, mod_ref, gpf_ref, wg_ref, wu_ref, wd_ref, o_ref, act_s, after_up):
    gate_f = mod_ref[0, 5:6, :]
    for c in range(D_FF // FF_CHUNK):
        cs = slice(c * FF_CHUNK, (c + 1) * FF_CHUNK)
        g = jnp.dot(h2_ref[...], wg_ref[:, cs], preferred_element_type=F32)
        yield
        u = jnp.dot(h2_ref[...], wu_ref[:, cs], preferred_element_type=F32)
        act_s[:, cs] = ((g * jax.nn.sigmoid(g)) * u).astype(BF16)
        yield
    after_up()
    ff = []
    for c in range(D_MODEL // FF_CHUNK):
        cs = slice(c * FF_CHUNK, (c + 1) * FF_CHUNK)
        ff.append(jnp.dot(act_s[...], wd_ref[:, cs], preferred_element_type=F32))
        yield
    o_ref[...] = x1_ref[...] + gate_f * _rms(jnp.concatenate(ff, axis=1), gpf_ref[...])
    yield


FFN_UP_STAGES = 2 * (D_FF // FF_CHUNK)
FFN_STAGES = FFN_UP_STAGES + D_MODEL // FF_CHUNK + 1


def _interleave(main, n_main, fills):
    done = [0] * len(fills)
    for k in range(n_main):
        next(main)
        for f, (gen, count, first, last) in enumerate(fills):
            span = last - first + 1
            want = 0 if k < first else min(count, ((k - first + 1) * count + span - 1) // span)
            while done[f] < want:
                next(gen)
                done[f] += 1
    assert all(d == f[1] for d, f in zip(done, fills))
    assert next(main, None) is None and all(next(f[0], None) is None for f in fills)


def _back_kernel(n_tiles, tiles_per_batch, sink_ref,
                 qa_ref, kv_ref, kvp_ref, qm_ref, kt_ref, vm_ref, om_ref, if_ref, bif_ref, nw_ref,
                 x_ref, ga_ref, gm_ref, modm_ref, gpm_ref, gqf_ref,
                 modf_ref, gpf_ref,
                 wa_ref, wb_ref, wo_ref, wg_ref, wu_ref, wd_ref, o_ref,
                 ya_s, ym_s, x1_s, h2_cur, h2_next, act_s, cn_ref, mrow_ref, mcol_ref):
    i = pl.program_id(0)
    j = jnp.minimum(i, n_tiles - 1) % tiles_per_batch
    slot = i % 2

    @pl.when(i == 0)
    def _():
        ya_s[...] = jnp.zeros_like(ya_s)
        ym_s[...] = jnp.zeros_like(ym_s)
        x1_s[...] = jnp.zeros_like(x1_s)
        h2_cur[...] = jnp.zeros_like(h2_cur)

    @pl.when(j == 0)
    def _():
        cn_ref[...] = jnp.zeros_like(cn_ref)
        mrow_ref[...] = jnp.zeros_like(mrow_ref)
        mcol_ref[...] = jnp.zeros_like(mcol_ref)

    def hand_over():
        h2_cur[...] = h2_next[...]

    ffn = _ffn_thread(x1_s.at[slot], h2_cur, modf_ref, gpf_ref, wg_ref, wu_ref, wd_ref, o_ref, act_s,
                      hand_over)
    merge = _merge_thread(x_ref, ya_s, ym_s, ga_ref, gm_ref, modm_ref, gpm_ref, gqf_ref,
                          wa_ref, wb_ref, wo_ref, x1_s.at[1 - slot], h2_next)
    attn = _attn_thread(j, sink_ref, qa_ref, kv_ref, kvp_ref, ya_s)
    mlstm = _mlstm_thread(qm_ref, kt_ref, vm_ref, om_ref, if_ref, bif_ref, nw_ref, ym_s,
                          cn_ref, mrow_ref, mcol_ref)
    _interleave(ffn, FFN_STAGES,
                [(merge, MERGE_STAGES, 0, FFN_UP_STAGES - 2),
                 (attn, ATTN_STAGES, 2, FFN_STAGES - 2),
                 (mlstm, MLSTM_STAGES, 2, FFN_STAGES - 2)])


def _back(sinks, qa, kv, qm, kt, vm, om, gates, bif, norm_w, x2, ga, gm, mod3,
          g_post_mix, g_pre_ffn, g_post_ffn, wa, wb, wo, wg, wu, wd, tiles_per_batch):
    t = x2.shape[0]
    tm = TOKEN_TILE
    n = t // tm
    per_tile = tm // ATTN_BLOCK
    clamp = lambda v: jnp.clip(v, 0, n - 1)
    cur = lambda i: (clamp(i), 0)
    cur_t = lambda i: (0, clamp(i))

    def prev_block(i):
        tile = clamp(i)
        first = (tile // tiles_per_batch) * tiles_per_batch * per_tile
        return (jnp.maximum(tile * per_tile - 1, first), 0)

    mid = lambda i: (clamp(i - 1), 0)
    last = lambda i: (clamp(i - 2), 0)
    const = lambda i: (0, 0)
    once = pl.Buffered(1)
    weight_bytes = 2 * (wa.size + wb.size + wo.size + wg.size + wu.size + wd.size)
    mixer_in = tm * (ATTN_Q_WIDTH + 2 * ATTN_KV_WIDTH + 4 * MLSTM_WIDTH) * 2 + tm * LANES * 4
    scratch_bytes = tm * (D_FF + 2 * MLSTM_WIDTH + 2 * D_MODEL) * 2 + 2 * tm * D_MODEL * 4
    est = (weight_bytes + 2 * 2 * tm * D_MODEL * 4 + 2 * tm * 2 * D_MODEL * 2 + 2 * mixer_in
           + scratch_bytes + 6 * tm * D_MODEL * 4)
    return pl.pallas_call(
        functools.partial(_back_kernel, n, tiles_per_batch),
        grid=(n + 2,),
        in_specs=[pl.BlockSpec(memory_space=pltpu.SMEM),
                  pl.BlockSpec((tm, ATTN_Q_WIDTH), cur),
                  pl.BlockSpec((tm, 2 * ATTN_KV_WIDTH), cur),
                  pl.BlockSpec((ATTN_BLOCK, 2 * ATTN_KV_WIDTH), prev_block),
                  pl.BlockSpec((tm, MLSTM_WIDTH), cur),
                  pl.BlockSpec((MLSTM_WIDTH, tm), cur_t),
                  pl.BlockSpec((tm, MLSTM_WIDTH), cur),
                  pl.BlockSpec((tm, MLSTM_WIDTH), cur),
                  pl.BlockSpec((tm, LANES), cur),
                  pl.BlockSpec((1, LANES), const),
                  pl.BlockSpec((1, MLSTM_WIDTH), const),
                  pl.BlockSpec((tm, D_MODEL), mid),
                  pl.BlockSpec((tm, D_MODEL), mid),
                  pl.BlockSpec((tm, D_MODEL), mid),
                  pl.BlockSpec((1, 6, D_MODEL), lambda i: (clamp(i - 1) // tiles_per_batch, 0, 0)),
                  pl.BlockSpec((1, D_MODEL), const),
                  pl.BlockSpec((1, D_MODEL), const),
                  pl.BlockSpec((1, 6, D_MODEL), lambda i: (clamp(i - 2) // tiles_per_batch, 0, 0)),
                  pl.BlockSpec((1, D_MODEL), const),
                  pl.BlockSpec(wa.shape, const, pipeline_mode=once),
                  pl.BlockSpec(wb.shape, const, pipeline_mode=once),
                  pl.BlockSpec(wo.shape, const, pipeline_mode=once),
                  pl.BlockSpec(wg.shape, const, pipeline_mode=once),
                  pl.BlockSpec(wu.shape, const, pipeline_mode=once),
                  pl.BlockSpec(wd.shape, const, pipeline_mode=once)],
        out_specs=pl.BlockSpec((tm, D_MODEL), last),
        out_shape=jax.ShapeDtypeStruct((t, D_MODEL), F32),
        scratch_shapes=[pltpu.VMEM((tm, ATTN_Q_WIDTH), BF16),
                        pltpu.VMEM((tm, MLSTM_WIDTH), BF16),
                        pltpu.VMEM((2, tm, D_MODEL), F32),
                        pltpu.VMEM((tm, D_MODEL), BF16),
                        pltpu.VMEM((tm, D_MODEL), BF16),
                        pltpu.VMEM((tm, D_FF), BF16),
                        pltpu.VMEM((MLSTM_HEADS, MLSTM_HEAD_DIM, 2 * MLSTM_HEAD_DIM), F32),
                        pltpu.VMEM((SUBLANES, LANES), F32),
                        pltpu.VMEM((SUBLANES, LANES), F32)],
        compiler_params=pltpu.CompilerParams(dimension_semantics=("arbitrary",),
                                             vmem_limit_bytes=_vmem_limit(est)),
        name="back",
    )(sinks, qa, kv, kv, qm, kt, vm, om, gates, bif, norm_w, x2, ga, gm, mod3,
      g_post_mix, g_pre_ffn, mod3, g_post_ffn, wa, wb, wo, wg, wu, wd)


def _q_head_order():
    per_kv = N_Q_HEADS // N_KV_HEADS
    return [h for p in range(per_kv) for h in (p, p + per_kv)]


def _layout_w_in(w_in):
    wt = w_in.T
    q_a = wt[:ATTN_Q_WIDTH].reshape(N_Q_HEADS, HEAD_DIM, D_MODEL)[np.array(_q_head_order())]
    n_if = 2 * MLSTM_HEADS
    gates = jnp.pad(wt[C_IF:C_IF + n_if], ((0, LANES - n_if), (0, 0)))
    return jnp.concatenate([q_a.reshape(ATTN_Q_WIDTH, D_MODEL), wt[ATTN_Q_WIDTH:C_IF], gates,
                            wt[C_IF + n_if:]], axis=0).astype(BF16)


def kernel(x, c, positions, w_ada, b_ada, g_pre_mix, g_post_mix, w_in, b_if, conv_w, conv_b,
           attn_sinks, mlstm_norm_w, w_branch_attn, w_branch_mlstm, w_out, g_pre_ffn, g_post_ffn,
           w_ffn_gate, w_ffn_up, w_ffn_down):
    batch, seq, d = x.shape
    depth = w_in.shape[0]
    assert d == D_MODEL and seq % TOKEN_TILE == 0 and (batch * seq) % ROPE_TILE == 0
    assert D_FF % FF_CHUNK == 0
    t = batch * seq
    tiles_per_batch = seq // TOKEN_TILE
    x2 = x.reshape(t, d)

    inv_freq = (ROPE_THETA ** (-2.0 * jnp.arange(HEAD_DIM // 2, dtype=F32) / HEAD_DIM)).reshape(-1, 1)
    c_pad = jnp.pad(c, ((0, SUBLANES - batch % SUBLANES), (0, 0))) if batch % SUBLANES else c
    head_order = np.array(_q_head_order())

    for l in range(depth):
        cos_t, sin_t, mod = _prep(positions.reshape(1, t), inv_freq, c_pad, w_ada[l],
                                  b_ada[l].reshape(1, -1))
        mod3 = mod[:batch].reshape(batch, 6, d)
        qa, kv, qm, kt, vm, om, gates, ga, gm = _inproj(
            x2, mod3, g_pre_mix[l].reshape(1, d), cos_t, sin_t, _layout_w_in(w_in[l]),
            conv_w[l], conv_b[l].reshape(1, -1), tiles_per_batch)
        bif = jnp.pad(b_if[l], (0, LANES - 2 * MLSTM_HEADS)).reshape(1, LANES)
        wa = w_branch_attn[l].reshape(N_Q_HEADS, HEAD_DIM, d)[head_order].reshape(ATTN_Q_WIDTH, d)
        x2 = _back(attn_sinks[l], qa, kv, qm, kt, vm, om, gates, bif, mlstm_norm_w[l].reshape(1, -1),
                   x2, ga, gm, mod3, g_post_mix[l].reshape(1, d), g_pre_ffn[l].reshape(1, d),
                   g_post_ffn[l].reshape(1, d), wa.astype(BF16), w_branch_mlstm[l].astype(BF16),
                   w_out[l].astype(BF16), w_ffn_gate[l].astype(BF16), w_ffn_up[l].astype(BF16),
                   w_ffn_down[l].astype(BF16), tiles_per_batch)
    return x2.reshape(batch, seq, d)
```

```python
import functools

import numpy as np
import jax
import jax.numpy as jnp
from jax import lax
from jax.experimental import pallas as pl
from jax.experimental.pallas import tpu as pltpu

F32 = jnp.float32
BF16 = jnp.bfloat16

D_MODEL = 1024
N_Q_HEADS = 8
N_KV_HEADS = 2
HEAD_DIM = 64
ROPE_THETA = 10000.0
MLSTM_HEADS = 4
MLSTM_HEAD_DIM = 128
CONV_WIDTH = 4
D_FF = 2816
NORM_EPS = 1e-6
ATTN_Q_WIDTH = N_Q_HEADS * HEAD_DIM
ATTN_KV_WIDTH = N_KV_HEADS * HEAD_DIM
MLSTM_WIDTH = MLSTM_HEADS * MLSTM_HEAD_DIM
ATTN_BLOCK = 128

LANES = 128
SUBLANES = 8
MXU_WIDTH = 256
V7X_SCOPED_VMEM_BYTES = 60000 * 1024

TOKEN_TILE = 512
MLSTM_CHUNK = 128
FF_CHUNK = LANES
ROPE_TILE = 2048

C_QA = 0
C_KV = C_QA + ATTN_Q_WIDTH
C_QK = C_KV + 2 * ATTN_KV_WIDTH
C_VM = C_QK + 2 * MLSTM_WIDTH
C_OM = C_VM + MLSTM_WIDTH
C_IF = C_OM + MLSTM_WIDTH
C_GA = C_IF + LANES
C_GM = C_GA + D_MODEL
IN_COLS = C_GM + D_MODEL


def _vmem_limit(estimate_bytes):
    return int(min(V7X_SCOPED_VMEM_BYTES, max(estimate_bytes, 16 * 1024 * 1024)))


def _rms(x, g):
    return (x * lax.rsqrt(jnp.mean(x * x, axis=-1, keepdims=True) + NORM_EPS)) * g


def _prep_kernel(pos_ref, freq_ref, c_ref, w_ref, b_ref, cos_ref, sin_ref, mod_ref):
    mod_ref[...] = jnp.dot(c_ref[...], w_ref[...], preferred_element_type=F32,
                           precision=lax.Precision.HIGHEST) + b_ref[...]
    ang = freq_ref[...] * pos_ref[...].astype(F32)
    c = jnp.cos(ang)
    s = jnp.sin(ang)
    cos_ref[...] = jnp.concatenate([c, c, c, c], axis=0).T
    sin_ref[...] = jnp.concatenate([-s, s, -s, s], axis=0).T


def _prep(pos_row, inv_freq, c_pad, w_ada, b_ada):
    t = pos_row.shape[1]
    half = inv_freq.shape[0]
    rows = c_pad.shape[0]
    n = w_ada.shape[1]
    steps = t // ROPE_TILE
    cols = n // steps
    assert n % steps == 0 and cols % LANES == 0
    return pl.pallas_call(
        _prep_kernel,
        grid=(steps,),
        in_specs=[pl.BlockSpec((1, ROPE_TILE), lambda i: (0, i)),
                  pl.BlockSpec((half, 1), lambda i: (0, 0)),
                  pl.BlockSpec((rows, D_MODEL), lambda i: (0, 0)),
                  pl.BlockSpec((D_MODEL, cols), lambda i: (0, i)),
                  pl.BlockSpec((1, cols), lambda i: (0, i))],
        out_specs=[pl.BlockSpec((ROPE_TILE, LANES), lambda i: (i, 0)),
                   pl.BlockSpec((ROPE_TILE, LANES), lambda i: (i, 0)),
                   pl.BlockSpec((rows, cols), lambda i: (0, i))],
        out_shape=[jax.ShapeDtypeStruct((t, LANES), F32),
                   jax.ShapeDtypeStruct((t, LANES), F32),
                   jax.ShapeDtypeStruct((rows, n), F32)],
        name="prep",
    )(pos_row, inv_freq, c_pad, w_ada, b_ada)


def _modulated_norm(x, g, mod_ref, row):
    return (_rms(x, g) * (1.0 + mod_ref[0, row + 1:row + 2, :]) + mod_ref[0, row:row + 1, :]).astype(BF16)


def _inproj_kernel(n_tiles, tiles_per_batch, x0_ref, mod0_ref, xn_ref, modn_ref, g_ref, cos_ref, sin_ref,
                   w_ref, cw_ref, cb_ref,
                   qa_ref, kv_ref, qm_ref, kt_ref, vm_ref, om_ref, if_ref, ga_ref, gm_ref,
                   h_cur, h_next, raw_s, rot_s):
    tm = TOKEN_TILE
    i = pl.program_id(0)
    post_tile = jnp.maximum(i - 1, 0)

    @pl.when(i == 0)
    def _():
        h_cur[...] = _modulated_norm(x0_ref[...], g_ref[...], mod0_ref, 0)
        raw_s[...] = jnp.zeros_like(raw_s)
        rot_s[...] = jnp.zeros_like(rot_s)

    @pl.when(post_tile % tiles_per_batch == 0)
    def _():
        raw_s[0:SUBLANES, :] = jnp.zeros((SUBLANES, raw_s.shape[1]), F32)

    def proj(lo, width):
        return lax.dot_general(h_cur[...], w_ref[lo:lo + width, :], (((1,), (1,)), ((), ())),
                               preferred_element_type=F32)

    lane = lax.broadcasted_iota(jnp.int32, (1, LANES), 1)
    first_half = (lane % HEAD_DIM) < (HEAD_DIM // 2)

    def rope(t):
        swapped = jnp.where(first_half, pltpu.roll(t, LANES - HEAD_DIM // 2, axis=1),
                            pltpu.roll(t, HEAD_DIM // 2, axis=1))
        return t * cos_ref[...] + swapped * sin_ref[...]

    def post_q():
        for p in range(ATTN_Q_WIDTH // LANES):
            sl = slice(p * LANES, (p + 1) * LANES)
            qa_ref[:, sl] = (rope(rot_s[:, sl]) * (HEAD_DIM ** -0.5)).astype(BF16)

    def post_kv():
        kv_ref[:, :LANES] = rope(rot_s[:, ATTN_Q_WIDTH:ATTN_Q_WIDTH + LANES]).astype(BF16)
        kv_ref[:, LANES:] = rot_s[:, ATTN_Q_WIDTH + LANES:].astype(BF16)

    blk = MXU_WIDTH

    def post_conv(cbk):
        cs = slice(cbk * blk, (cbk + 1) * blk)
        acc = cb_ref[:, cs]
        for sh in range(CONV_WIDTH):
            acc = acc + (cw_ref[CONV_WIDTH - 1 - sh:CONV_WIDTH - sh, cs]
                         * raw_s[SUBLANES - sh:SUBLANES - sh + tm, cs])
        raw_s[0:SUBLANES, cs] = raw_s[tm:tm + SUBLANES, cs]
        a = acc * jax.nn.sigmoid(acc)
        if cbk * blk < MLSTM_WIDTH:
            qm_ref[:, cs] = a.astype(BF16)
        else:
            a = a * (MLSTM_HEAD_DIM ** -0.5)
            for d0 in range(0, blk, LANES):
                r0 = cbk * blk - MLSTM_WIDTH + d0
                for t0 in range(0, tm, LANES):
                    kt_ref[r0:r0 + LANES, t0:t0 + LANES] = a[t0:t0 + LANES, d0:d0 + LANES].T.astype(BF16)

    def norm_next(part):
        rows = slice(part * tm // INPROJ_NORM_PARTS, (part + 1) * tm // INPROJ_NORM_PARTS)
        h_next[rows, :] = _modulated_norm(xn_ref[rows, :], g_ref[...], modn_ref, 0)

    def main_rot(lo_w, lo_s, width):
        rot_s[:, lo_s:lo_s + width] = proj(lo_w, width)

    def main_conv(cbk):
        raw_s[SUBLANES:, cbk * blk:(cbk + 1) * blk] = proj(C_QK + cbk * blk, blk)

    def main_direct(out_ref, lo_w, lo, width):
        out_ref[:, lo:lo + width] = proj(lo_w + lo, width).astype(out_ref.dtype)

    half = D_MODEL // 2
    post_q()
    main_rot(C_QA, 0, ATTN_Q_WIDTH)
    post_kv()
    post_conv(0)
    main_rot(C_KV, ATTN_Q_WIDTH, 2 * ATTN_KV_WIDTH)
    main_conv(0)
    post_conv(1)
    main_conv(1)
    main_direct(vm_ref, C_VM, 0, MLSTM_WIDTH)
    post_conv(2)
    main_conv(2)
    main_direct(om_ref, C_OM, 0, MLSTM_WIDTH)
    post_conv(3)
    main_conv(3)
    main_direct(if_ref, C_IF, 0, LANES)
    norm_next(0)
    main_direct(ga_ref, C_GA, 0, half)
    norm_next(1)
    main_direct(ga_ref, C_GA, half, half)
    norm_next(2)
    main_direct(gm_ref, C_GM, 0, half)
    norm_next(3)
    main_direct(gm_ref, C_GM, half, half)
    h_cur[...] = h_next[...]


INPROJ_NORM_PARTS = 4


def _inproj(x2, mod3, g_pre, cos_t, sin_t, w_t, conv_w, conv_b, tiles_per_batch):
    t = x2.shape[0]
    tm = TOKEN_TILE
    n = t // tm
    main = lambda i: (jnp.minimum(i, n - 1), 0)
    post = lambda i: (jnp.maximum(i - 1, 0), 0)
    nxt = lambda i: (jnp.minimum(i + 1, n - 1), 0)
    const = lambda i: (0, 0)
    once = pl.Buffered(1)
    outs = [(ATTN_Q_WIDTH, BF16, post), (2 * ATTN_KV_WIDTH, BF16, post), (MLSTM_WIDTH, BF16, post),
            (None, BF16, None), (MLSTM_WIDTH, BF16, main), (MLSTM_WIDTH, BF16, main),
            (LANES, F32, main), (D_MODEL, BF16, main), (D_MODEL, BF16, main)]
    out_specs = [pl.BlockSpec((MLSTM_WIDTH, tm), lambda i: (0, jnp.maximum(i - 1, 0))) if w is None
                 else pl.BlockSpec((tm, w), m) for w, _, m in outs]
    out_shape = [jax.ShapeDtypeStruct((MLSTM_WIDTH, t) if w is None else (t, w), d) for w, d, _ in outs]
    out_bytes = sum((w or MLSTM_WIDTH) * np.dtype(d).itemsize for w, d, _ in outs) * tm
    rot_cols = ATTN_Q_WIDTH + 2 * ATTN_KV_WIDTH
    scratch_bytes = (2 * tm * D_MODEL * 2 + (tm + SUBLANES) * 2 * MLSTM_WIDTH * 4 + tm * rot_cols * 4)
    est = (w_t.size * 2 + 3 * tm * D_MODEL * 4 + 2 * out_bytes + 4 * tm * LANES * 4
           + scratch_bytes + 4 * tm * D_MODEL * 4)
    return pl.pallas_call(
        functools.partial(_inproj_kernel, n, tiles_per_batch),
        grid=(n + 1,),
        in_specs=[pl.BlockSpec((tm, D_MODEL), const, pipeline_mode=once),
                  pl.BlockSpec((1, 6, D_MODEL), lambda i: (0, 0, 0)),
                  pl.BlockSpec((tm, D_MODEL), nxt),
                  pl.BlockSpec((1, 6, D_MODEL),
                               lambda i: (jnp.minimum(i + 1, n - 1) // tiles_per_batch, 0, 0)),
                  pl.BlockSpec((1, D_MODEL), const),
                  pl.BlockSpec((tm, LANES), post),
                  pl.BlockSpec((tm, LANES), post),
                  pl.BlockSpec((IN_COLS, D_MODEL), const, pipeline_mode=once),
                  pl.BlockSpec((CONV_WIDTH, 2 * MLSTM_WIDTH), const),
                  pl.BlockSpec((1, 2 * MLSTM_WIDTH), const)],
        out_specs=out_specs,
        out_shape=out_shape,
        scratch_shapes=[pltpu.VMEM((tm, D_MODEL), BF16),
                        pltpu.VMEM((tm, D_MODEL), BF16),
                        pltpu.VMEM((tm + SUBLANES, 2 * MLSTM_WIDTH), F32),
                        pltpu.VMEM((tm, rot_cols), F32)],
        compiler_params=pltpu.CompilerParams(dimension_semantics=("arbitrary",),
                                             vmem_limit_bytes=_vmem_limit(est)),
        name="inproj",
    )(x2, mod3, x2, mod3, g_pre, cos_t, sin_t, w_t, conv_w, conv_b)


def _attn_thread(j, sink_ref, q_ref, kv_ref, kvp_ref, o_ref):
    blk = ATTN_BLOCK
    nblk = TOKEN_TILE // blk
    ngrp = ATTN_Q_WIDTH // LANES
    lane = lax.broadcasted_iota(jnp.int32, (1, LANES), 1)
    low = lane < HEAD_DIM
    qi = lax.broadcasted_iota(jnp.int32, (blk, 2 * blk), 0)
    kj = lax.broadcasted_iota(jnp.int32, (blk, 2 * blk), 1)
    in_cur = (kj >= blk) & (kj - blk <= qi)
    in_prev = (kj < blk) & (kj > qi)
    in_prev_first = (kj < blk) & (kj > qi + jnp.where(j > 0, 0, blk))
    ones = jnp.ones((2 * blk, LANES), F32)
    zeros = jnp.zeros((2 * blk, LANES), F32)
    one_lo = jnp.where(low, ones, zeros).astype(BF16)
    one_hi = jnp.where(low, zeros, ones).astype(BF16)
    band = {}
    live = {}

    def prep(i):
        cur = kv_ref[i * blk:(i + 1) * blk, :]
        prev = kvp_ref[...] if i == 0 else kv_ref[(i - 1) * blk:i * blk, :]
        kv = jnp.concatenate([prev, cur], axis=0).astype(F32)
        k = kv[:, :LANES]
        v = kv[:, LANES:]
        k_cat = jnp.concatenate([jnp.where(low, k, 0.0), jnp.where(low, 0.0, k)], axis=0).astype(BF16)
        v_cat = jnp.concatenate(
            [jnp.concatenate([jnp.where(low, v, 0.0).astype(BF16), one_lo], axis=1),
             jnp.concatenate([jnp.where(low, 0.0, v).astype(BF16), one_hi], axis=1)], axis=0)
        band[i] = (k_cat, v_cat, in_cur | (in_prev_first if i == 0 else in_prev))

    def scores(u):
        i, p = divmod(u, ngrp)
        q = q_ref[i * blk:(i + 1) * blk, p * LANES:(p + 1) * LANES]
        live[u] = lax.dot_general(q, band[i][0], (((1,), (1,)), ((), ())), preferred_element_type=F32)

    def softmax(u):
        i, p = divmod(u, ngrp)
        s = live[u]
        mask = band[i][2]
        s0 = jnp.where(mask, s[:, :2 * blk], -jnp.inf)
        s1 = jnp.where(mask, s[:, 2 * blk:], -jnp.inf)
        m0 = jnp.maximum(jnp.max(s0, axis=-1, keepdims=True), sink_ref[p])
        m1 = jnp.maximum(jnp.max(s1, axis=-1, keepdims=True), sink_ref[p + N_Q_HEADS // N_KV_HEADS])
        pr = jnp.concatenate([jnp.exp(s0 - m0), jnp.exp(s1 - m1)], axis=1).astype(BF16)
        live[u] = (pr, m0, m1)

    def values(u):
        i, p = divmod(u, ngrp)
        pr, m0, m1 = live.pop(u)
        r = jnp.dot(pr, band[i][1], preferred_element_type=F32)
        den = r[:, LANES:] + jnp.where(low, jnp.exp(sink_ref[p] - m0),
                                       jnp.exp(sink_ref[p + N_Q_HEADS // N_KV_HEADS] - m1))
        o_ref[i * blk:(i + 1) * blk, p * LANES:(p + 1) * LANES] = (r[:, :LANES] / den).astype(BF16)

    n = nblk * ngrp
    prep(0)
    for k in range(n + 2):
        if k < n:
            scores(k)
            if k + 1 < n and (k + 1) % ngrp == 0:
                prep((k + 1) // ngrp)
        if 0 <= k - 2 < n:
            values(k - 2)
        if 0 <= k - 1 < n:
            softmax(k - 1)
        yield


ATTN_STAGES = (TOKEN_TILE // ATTN_BLOCK) * (ATTN_Q_WIDTH // LANES) + 2


def _mlstm_thread(q_ref, kt_ref, v_ref, om_ref, if_ref, bif_ref, nw_ref, y_ref,
                  cn_ref, mrow_ref, mcol_ref):
    ts = TOKEN_TILE
    L = MLSTM_CHUNK
    D = MLSTM_HEAD_DIM
    nch = ts // L

    gates = if_ref[...] + bif_ref[...]
    logf_all = jax.nn.log_sigmoid(pltpu.roll(gates, LANES - MLSTM_HEADS, axis=1))

    ti = lax.broadcasted_iota(jnp.int32, (L, L), 0)
    si = lax.broadcasted_iota(jnp.int32, (L, L), 1)
    causal = si <= ti
    tri = causal.astype(BF16)
    ones_v = jnp.ones((L, LANES), BF16)
    chunk = {}
    head = {}

    def rows_of(c):
        return slice(c * L, (c + 1) * L)

    def p1(c):
        lf = logf_all[rows_of(c)]
        lf_hi = lf.astype(BF16)
        rem = lf - lf_hi.astype(F32)
        lf_mid = rem.astype(BF16)
        lf_lo = (rem - lf_mid.astype(F32)).astype(BF16)
        chunk[c] = (jnp.dot(tri, lf_hi, preferred_element_type=F32)
                    + jnp.dot(tri, lf_mid, preferred_element_type=F32)
                    + jnp.dot(tri, lf_lo, preferred_element_type=F32))

    def p2(c):
        b = chunk[c]
        r = b - gates[rows_of(c)]
        pm = -r
        sh = 1
        while sh < L:
            pm = jnp.maximum(pm, jnp.where(ti >= sh, pltpu.roll(pm, sh, axis=0), -jnp.inf))
            sh *= 2
        m_row = mrow_ref[0:1, :]
        inter = b + m_row
        mt = jnp.maximum(inter, b + pm)
        b_last = b[L - 1:L, :]
        m_new_row = jnp.maximum(b_last + m_row, jnp.max(b_last - r, axis=0, keepdims=True))
        mrow_ref[...] = jnp.broadcast_to(m_new_row, mrow_ref.shape)
        b_t = b.T[:SUBLANES]
        r_t = r.T[:SUBLANES]
        m_col = mcol_ref[:, 0:1]
        bl_col = b_t[:, L - 1:L]
        a_t = bl_col - r_t
        m_new_col = jnp.maximum(bl_col + m_col, jnp.max(a_t, axis=1, keepdims=True))
        mcol_ref[...] = jnp.broadcast_to(m_new_col, mcol_ref.shape)
        chunk[c] = dict(e_col=b - mt, r_t=r_t, w_inter=jnp.exp(inter - mt), e_mt=jnp.exp(-mt),
                        decay_row=jnp.exp(b_last + m_row - m_new_row),
                        wk_t=jnp.exp(a_t - m_new_col))

    def ha(c, h):
        hs = slice(h * D, (h + 1) * D)
        qh = q_ref[rows_of(c), hs]
        kt = kt_ref[hs, rows_of(c)]
        head[c, h] = dict(qh=qh, kt=kt, s=jnp.dot(qh, kt, preferred_element_type=F32))

    def hb(c, h):
        st, ch = head[c, h], chunk[c]
        w_intra = jnp.where(causal, jnp.exp(ch["e_col"][:, h:h + 1] - ch["r_t"][h:h + 1, :]), 0.0)
        pr = (st.pop("s") * w_intra).astype(BF16)
        qw = (st.pop("qh").astype(F32) * ch["w_inter"][:, h:h + 1]).astype(BF16)
        st["lhs"] = jnp.concatenate([pr, qw], axis=1)
        st["kw"] = (st.pop("kt").astype(F32) * ch["wk_t"][h:h + 1, :]).astype(BF16)

    def hc(c, h):
        st = head[c, h]
        hs = slice(h * D, (h + 1) * D)
        v_ext = jnp.concatenate([v_ref[rows_of(c), hs], ones_v], axis=1)
        rhs = jnp.concatenate([v_ext, cn_ref[h].astype(BF16)], axis=0)
        st["nd"] = jnp.dot(st.pop("lhs"), rhs, preferred_element_type=F32)
        st["kv"] = jnp.dot(st.pop("kw"), v_ext, preferred_element_type=F32)

    def hd(c, h):
        st, ch = head.pop((c, h)), chunk[c]
        hs = slice(h * D, (h + 1) * D)
        nd = st["nd"]
        hh = nd[:, :D] / jnp.maximum(jnp.abs(nd[:, D:]), ch["e_mt"][:, h:h + 1])
        cn_ref[h] = ch["decay_row"][:, h:h + 1] * cn_ref[h] + st["kv"]
        mu = jnp.mean(hh, axis=-1, keepdims=True)
        xc = hh - mu
        var = jnp.mean(xc * xc, axis=-1, keepdims=True)
        yn = (xc * lax.rsqrt(var + NORM_EPS)) * nw_ref[:, hs]
        y_ref[rows_of(c), hs] = (jax.nn.sigmoid(om_ref[rows_of(c), hs].astype(F32)) * yn).astype(BF16)

    half = MLSTM_HEADS // 2
    for g in range(MLSTM_STAGES):
        for c in range(nch):
            o = g - MLSTM_CHUNK_SKEW * c
            if o == 0:
                p1(c)
            elif o == 1:
                for h in range(MLSTM_HEADS):
                    ha(c, h)
                p2(c)
            elif o == 2:
                for h in range(half):
                    hb(c, h)
            elif o == 3:
                for h in range(half):
                    hc(c, h)
                for h in range(half, MLSTM_HEADS):
                    hb(c, h)
            elif o == 4:
                for h in range(half, MLSTM_HEADS):
                    hc(c, h)
                for h in range(half):
                    hd(c, h)
            elif o == 5:
                for h in range(half, MLSTM_HEADS):
                    hd(c, h)
        yield


MLSTM_CHUNK_SKEW = 4
MLSTM_STAGES = MLSTM_CHUNK_SKEW * (TOKEN_TILE // MLSTM_CHUNK - 1) + 6


def _merge_thread(x_ref, ya_ref, ym_ref, ga_ref, gm_ref, mod_ref, gpm_ref, gqf_ref,
                  wa_ref, wb_ref, wo_ref, x1_ref, h2_ref):
    tm = TOKEN_TILE
    parts = [slice(r * tm // MERGE_ROW_PARTS, (r + 1) * tm // MERGE_ROW_PARTS)
             for r in range(MERGE_ROW_PARTS)]
    gate_m = mod_ref[0, 2:3, :]
    ba = jnp.dot(ya_ref[...], wa_ref[...], preferred_element_type=F32)
    yield
    bb = jnp.dot(ym_ref[...], wb_ref[...], preferred_element_type=F32)
    yield
    merged = []
    for rows in parts:
        merged.append((jax.nn.sigmoid(ga_ref[rows, :].astype(F32)) * ba[rows]
                       + jax.nn.sigmoid(gm_ref[rows, :].astype(F32)) * bb[rows]).astype(BF16))
        yield
    mix = jnp.dot(jnp.concatenate(merged, axis=0), wo_ref[...], preferred_element_type=F32)
    yield
    for rows in parts:
        x1 = x_ref[rows, :] + gate_m * _rms(mix[rows], gpm_ref[...])
        x1_ref[rows, :] = x1
        yield
        h2_ref[rows, :] = _modulated_norm(x1, gqf_ref[...], mod_ref, 3)
        yield


MERGE_ROW_PARTS = 4
MERGE_STAGES = 3 + 3 * MERGE_ROW_PARTS


def _ffn_thread(x1_ref, h2_ref, mod_ref, gpf_ref, wgu_ref, wd_ref, o_ref, act_s, after_up):
    gate_f = mod_ref[0, 5:6, :]
    for c in range(D_FF // FF_CHUNK):
        gu = jnp.dot(h2_ref[...], wgu_ref[:, 2 * c * FF_CHUNK:2 * (c + 1) * FF_CHUNK],
                     preferred_element_type=F32)
        g = gu[:, :FF_CHUNK]
        act_s[:, c * FF_CHUNK:(c + 1) * FF_CHUNK] = ((g * jax.nn.sigmoid(g)) * gu[:, FF_CHUNK:]).astype(BF16)
        yield
    after_up()
    ff = []
    for c in range(FFN_DOWN_SPLIT):
        cs = slice(c * D_MODEL // FFN_DOWN_SPLIT, (c + 1) * D_MODEL // FFN_DOWN_SPLIT)
        ff.append(jnp.dot(act_s[...], wd_ref[:, cs], preferred_element_type=F32))
        yield
    o_ref[...] = x1_ref[...] + gate_f * _rms(jnp.concatenate(ff, axis=1), gpf_ref[...])
    yield


FFN_DOWN_SPLIT = 2
FFN_UP_STAGES = D_FF // FF_CHUNK
FFN_STAGES = FFN_UP_STAGES + FFN_DOWN_SPLIT + 1


def _interleave(main, n_main, fills):
    done = [0] * len(fills)
    for k in range(n_main):
        next(main)
        for f, (gen, count, first, last) in enumerate(fills):
            span = last - first + 1
            want = 0 if k < first else min(count, ((k - first + 1) * count + span - 1) // span)
            while done[f] < want:
                next(gen)
                done[f] += 1
    assert all(d == f[1] for d, f in zip(done, fills))
    assert next(main, None) is None and all(next(f[0], None) is None for f in fills)


def _back_kernel(n_tiles, tiles_per_batch, sink_ref,
                 qa_ref, kv_ref, kvp_ref, qm_ref, kt_ref, vm_ref, om_ref, if_ref, bif_ref, nw_ref,
                 x_ref, ga_ref, gm_ref, modm_ref, gpm_ref, gqf_ref,
                 modf_ref, gpf_ref,
                 wa_ref, wb_ref, wo_ref, wgu_ref, wd_ref, o_ref,
                 ya_s, ym_s, x1_s, h2_cur, h2_next, act_s, cn_ref, mrow_ref, mcol_ref):
    i = pl.program_id(0)
    j = jnp.minimum(i, n_tiles - 1) % tiles_per_batch
    slot = i % 2

    @pl.when(i == 0)
    def _():
        ya_s[...] = jnp.zeros_like(ya_s)
        ym_s[...] = jnp.zeros_like(ym_s)
        x1_s[...] = jnp.zeros_like(x1_s)
        h2_cur[...] = jnp.zeros_like(h2_cur)

    @pl.when(j == 0)
    def _():
        cn_ref[...] = jnp.zeros_like(cn_ref)
        mrow_ref[...] = jnp.zeros_like(mrow_ref)
        mcol_ref[...] = jnp.zeros_like(mcol_ref)

    def hand_over():
        h2_cur[...] = h2_next[...]

    ffn = _ffn_thread(x1_s.at[slot], h2_cur, modf_ref, gpf_ref, wgu_ref, wd_ref, o_ref, act_s,
                      hand_over)
    merge = _merge_thread(x_ref, ya_s, ym_s, ga_ref, gm_ref, modm_ref, gpm_ref, gqf_ref,
                          wa_ref, wb_ref, wo_ref, x1_s.at[1 - slot], h2_next)
    attn = _attn_thread(j, sink_ref, qa_ref, kv_ref, kvp_ref, ya_s)
    mlstm = _mlstm_thread(qm_ref, kt_ref, vm_ref, om_ref, if_ref, bif_ref, nw_ref, ym_s,
                          cn_ref, mrow_ref, mcol_ref)
    _interleave(ffn, FFN_STAGES,
                [(merge, MERGE_STAGES, 0, FFN_UP_STAGES - 2),
                 (attn, ATTN_STAGES, 2, FFN_STAGES - 2),
                 (mlstm, MLSTM_STAGES, 2, FFN_STAGES - 2)])


def _back(sinks, qa, kv, qm, kt, vm, om, gates, bif, norm_w, x2, ga, gm, mod3,
          g_post_mix, g_pre_ffn, g_post_ffn, wa, wb, wo, wgu, wd, tiles_per_batch):
    t = x2.shape[0]
    tm = TOKEN_TILE
    n = t // tm
    per_tile = tm // ATTN_BLOCK
    clamp = lambda v: jnp.clip(v, 0, n - 1)
    cur = lambda i: (clamp(i), 0)
    cur_t = lambda i: (0, clamp(i))

    def prev_block(i):
        tile = clamp(i)
        first = (tile // tiles_per_batch) * tiles_per_batch * per_tile
        return (jnp.maximum(tile * per_tile - 1, first), 0)

    mid = lambda i: (clamp(i - 1), 0)
    last = lambda i: (clamp(i - 2), 0)
    const = lambda i: (0, 0)
    once = pl.Buffered(1)
    weight_bytes = 2 * (wa.size + wb.size + wo.size + wgu.size + wd.size)
    mixer_in = tm * (ATTN_Q_WIDTH + 2 * ATTN_KV_WIDTH + 4 * MLSTM_WIDTH) * 2 + tm * LANES * 4
    scratch_bytes = tm * (D_FF + 2 * MLSTM_WIDTH + 2 * D_MODEL) * 2 + 2 * tm * D_MODEL * 4
    est = (weight_bytes + 2 * 2 * tm * D_MODEL * 4 + 2 * tm * 2 * D_MODEL * 2 + 2 * mixer_in
           + scratch_bytes + 6 * tm * D_MODEL * 4)
    return pl.pallas_call(
        functools.partial(_back_kernel, n, tiles_per_batch),
        grid=(n + 2,),
        in_specs=[pl.BlockSpec(memory_space=pltpu.SMEM),
                  pl.BlockSpec((tm, ATTN_Q_WIDTH), cur),
                  pl.BlockSpec((tm, 2 * ATTN_KV_WIDTH), cur),
                  pl.BlockSpec((ATTN_BLOCK, 2 * ATTN_KV_WIDTH), prev_block),
                  pl.BlockSpec((tm, MLSTM_WIDTH), cur),
                  pl.BlockSpec((MLSTM_WIDTH, tm), cur_t),
                  pl.BlockSpec((tm, MLSTM_WIDTH), cur),
                  pl.BlockSpec((tm, MLSTM_WIDTH), cur),
                  pl.BlockSpec((tm, LANES), cur),
                  pl.BlockSpec((1, LANES), const),
                  pl.BlockSpec((1, MLSTM_WIDTH), const),
                  pl.BlockSpec((tm, D_MODEL), mid),
                  pl.BlockSpec((tm, D_MODEL), mid),
                  pl.BlockSpec((tm, D_MODEL), mid),
                  pl.BlockSpec((1, 6, D_MODEL), lambda i: (clamp(i - 1) // tiles_per_batch, 0, 0)),
                  pl.BlockSpec((1, D_MODEL), const),
                  pl.BlockSpec((1, D_MODEL), const),
                  pl.BlockSpec((1, 6, D_MODEL), lambda i: (clamp(i - 2) // tiles_per_batch, 0, 0)),
                  pl.BlockSpec((1, D_MODEL), const),
                  pl.BlockSpec(wa.shape, const, pipeline_mode=once),
                  pl.BlockSpec(wb.shape, const, pipeline_mode=once),
                  pl.BlockSpec(wo.shape, const, pipeline_mode=once),
                  pl.BlockSpec(wgu.shape, const, pipeline_mode=once),
                  pl.BlockSpec(wd.shape, const, pipeline_mode=once)],
        out_specs=pl.BlockSpec((tm, D_MODEL), last),
        out_shape=jax.ShapeDtypeStruct((t, D_MODEL), F32),
        scratch_shapes=[pltpu.VMEM((tm, ATTN_Q_WIDTH), BF16),
                        pltpu.VMEM((tm, MLSTM_WIDTH), BF16),
                        pltpu.VMEM((2, tm, D_MODEL), F32),
                        pltpu.VMEM((tm, D_MODEL), BF16),
                        pltpu.VMEM((tm, D_MODEL), BF16),
                        pltpu.VMEM((tm, D_FF), BF16),
                        pltpu.VMEM((MLSTM_HEADS, MLSTM_HEAD_DIM, 2 * MLSTM_HEAD_DIM), F32),
                        pltpu.VMEM((SUBLANES, LANES), F32),
                        pltpu.VMEM((SUBLANES, LANES), F32)],
        compiler_params=pltpu.CompilerParams(dimension_semantics=("arbitrary",),
                                             vmem_limit_bytes=_vmem_limit(est)),
        name="back",
    )(sinks, qa, kv, kv, qm, kt, vm, om, gates, bif, norm_w, x2, ga, gm, mod3,
      g_post_mix, g_pre_ffn, mod3, g_post_ffn, wa, wb, wo, wgu, wd)


def _q_head_order():
    per_kv = N_Q_HEADS // N_KV_HEADS
    return [h for p in range(per_kv) for h in (p, p + per_kv)]


def _layout_w_in(w_in):
    wt = w_in.T
    q_a = wt[:ATTN_Q_WIDTH].reshape(N_Q_HEADS, HEAD_DIM, D_MODEL)[np.array(_q_head_order())]
    n_if = 2 * MLSTM_HEADS
    gates = jnp.pad(wt[C_IF:C_IF + n_if], ((0, LANES - n_if), (0, 0)))
    return jnp.concatenate([q_a.reshape(ATTN_Q_WIDTH, D_MODEL), wt[ATTN_Q_WIDTH:C_IF], gates,
                            wt[C_IF + n_if:]], axis=0).astype(BF16)


def kernel(x, c, positions, w_ada, b_ada, g_pre_mix, g_post_mix, w_in, b_if, conv_w, conv_b,
           attn_sinks, mlstm_norm_w, w_branch_attn, w_branch_mlstm, w_out, g_pre_ffn, g_post_ffn,
           w_ffn_gate, w_ffn_up, w_ffn_down):
    batch, seq, d = x.shape
    depth = w_in.shape[0]
    assert d == D_MODEL and seq % TOKEN_TILE == 0 and (batch * seq) % ROPE_TILE == 0
    assert D_FF % FF_CHUNK == 0
    t = batch * seq
    tiles_per_batch = seq // TOKEN_TILE
    x2 = x.reshape(t, d)

    inv_freq = (ROPE_THETA ** (-2.0 * jnp.arange(HEAD_DIM // 2, dtype=F32) / HEAD_DIM)).reshape(-1, 1)
    c_pad = jnp.pad(c, ((0, SUBLANES - batch % SUBLANES), (0, 0))) if batch % SUBLANES else c
    head_order = np.array(_q_head_order())

    for l in range(depth):
        cos_t, sin_t, mod = _prep(positions.reshape(1, t), inv_freq, c_pad, w_ada[l],
                                  b_ada[l].reshape(1, -1))
        mod3 = mod[:batch].reshape(batch, 6, d)
        qa, kv, qm, kt, vm, om, gates, ga, gm = _inproj(
            x2, mod3, g_pre_mix[l].reshape(1, d), cos_t, sin_t, _layout_w_in(w_in[l]),
            conv_w[l], conv_b[l].reshape(1, -1), tiles_per_batch)
        bif = jnp.pad(b_if[l], (0, LANES - 2 * MLSTM_HEADS)).reshape(1, LANES)
        wa = w_branch_attn[l].reshape(N_Q_HEADS, HEAD_DIM, d)[head_order].reshape(ATTN_Q_WIDTH, d)
        n_ff = D_FF // FF_CHUNK
        wgu = jnp.stack([w_ffn_gate[l].astype(BF16).reshape(d, n_ff, FF_CHUNK),
                         w_ffn_up[l].astype(BF16).reshape(d, n_ff, FF_CHUNK)], axis=2).reshape(d, 2 * D_FF)
        x2 = _back(attn_sinks[l], qa, kv, qm, kt, vm, om, gates, bif, mlstm_norm_w[l].reshape(1, -1),
                   x2, ga, gm, mod3, g_post_mix[l].reshape(1, d), g_pre_ffn[l].reshape(1, d),
                   g_post_ffn[l].reshape(1, d), wa.astype(BF16), w_branch_mlstm[l].astype(BF16),
                   w_out[l].astype(BF16), wgu, w_ffn_down[l].astype(BF16), tiles_per_batch)
    return x2.reshape(batch, seq, d)
```

```python
import functools

import numpy as np
import jax
import jax.numpy as jnp
from jax import lax
from jax.experimental import pallas as pl
from jax.experimental.pallas import tpu as pltpu

F32 = jnp.float32
BF16 = jnp.bfloat16

D_MODEL = 1024
N_Q_HEADS = 8
N_KV_HEADS = 2
HEAD_DIM = 64
ROPE_THETA = 10000.0
MLSTM_HEADS = 4
MLSTM_HEAD_DIM = 128
CONV_WIDTH = 4
D_FF = 2816
NORM_EPS = 1e-6
ATTN_Q_WIDTH = N_Q_HEADS * HEAD_DIM
ATTN_KV_WIDTH = N_KV_HEADS * HEAD_DIM
MLSTM_WIDTH = MLSTM_HEADS * MLSTM_HEAD_DIM
ATTN_BLOCK = 128

LANES = 128
SUBLANES = 8
MXU_WIDTH = 256
V7X_SCOPED_VMEM_BYTES = 60000 * 1024

TOKEN_TILE = 512
MLSTM_CHUNK = 128
FF_CHUNK = LANES
ROPE_TILE = 2048

C_QA = 0
C_KV = C_QA + ATTN_Q_WIDTH
C_QK = C_KV + 2 * ATTN_KV_WIDTH
C_VM = C_QK + 2 * MLSTM_WIDTH
C_OM = C_VM + MLSTM_WIDTH
C_IF = C_OM + MLSTM_WIDTH
C_GA = C_IF + LANES
C_GM = C_GA + D_MODEL
IN_COLS = C_GM + D_MODEL


def _vmem_limit(estimate_bytes):
    return int(min(V7X_SCOPED_VMEM_BYTES, max(estimate_bytes, 16 * 1024 * 1024)))


def _rms(x, g):
    return (x * lax.rsqrt(jnp.mean(x * x, axis=-1, keepdims=True) + NORM_EPS)) * g


def _prep_kernel(pos_ref, freq_ref, c_ref, w_ref, b_ref, cos_ref, sin_ref, mod_ref):
    mod_ref[...] = jnp.dot(c_ref[...], w_ref[...], preferred_element_type=F32,
                           precision=lax.Precision.HIGHEST) + b_ref[...]
    ang = freq_ref[...] * pos_ref[...].astype(F32)
    c = jnp.cos(ang)
    s = jnp.sin(ang)
    cos_ref[...] = jnp.concatenate([c, c, c, c], axis=0).T
    sin_ref[...] = jnp.concatenate([-s, s, -s, s], axis=0).T


def _prep(pos_row, inv_freq, c_pad, w_ada, b_ada):
    t = pos_row.shape[1]
    half = inv_freq.shape[0]
    rows = c_pad.shape[0]
    n = w_ada.shape[1]
    steps = t // ROPE_TILE
    cols = n // steps
    assert n % steps == 0 and cols % LANES == 0
    return pl.pallas_call(
        _prep_kernel,
        grid=(steps,),
        in_specs=[pl.BlockSpec((1, ROPE_TILE), lambda i: (0, i)),
                  pl.BlockSpec((half, 1), lambda i: (0, 0)),
                  pl.BlockSpec((rows, D_MODEL), lambda i: (0, 0)),
                  pl.BlockSpec((D_MODEL, cols), lambda i: (0, i)),
                  pl.BlockSpec((1, cols), lambda i: (0, i))],
        out_specs=[pl.BlockSpec((ROPE_TILE, LANES), lambda i: (i, 0)),
                   pl.BlockSpec((ROPE_TILE, LANES), lambda i: (i, 0)),
                   pl.BlockSpec((rows, cols), lambda i: (0, i))],
        out_shape=[jax.ShapeDtypeStruct((t, LANES), F32),
                   jax.ShapeDtypeStruct((t, LANES), F32),
                   jax.ShapeDtypeStruct((rows, n), F32)],
        name="prep",
    )(pos_row, inv_freq, c_pad, w_ada, b_ada)


def _modulated_norm(x, g, mod_ref, row):
    return (_rms(x, g) * (1.0 + mod_ref[0, row + 1:row + 2, :]) + mod_ref[0, row:row + 1, :]).astype(BF16)


def _inproj_kernel(n_tiles, tiles_per_batch, x0_ref, mod0_ref, xn_ref, modn_ref, g_ref, cos_ref, sin_ref,
                   w_ref, cw_ref, cb_ref,
                   qa_ref, kv_ref, qm_ref, kt_ref, vm_ref, om_ref, if_ref, ga_ref, gm_ref,
                   h_cur, h_next, raw_s, rot_s):
    tm = TOKEN_TILE
    i = pl.program_id(0)
    post_tile = jnp.maximum(i - 1, 0)

    @pl.when(i == 0)
    def _():
        h_cur[...] = _modulated_norm(x0_ref[...], g_ref[...], mod0_ref, 0)
        raw_s[...] = jnp.zeros_like(raw_s)
        rot_s[...] = jnp.zeros_like(rot_s)

    @pl.when(post_tile % tiles_per_batch == 0)
    def _():
        raw_s[0:SUBLANES, :] = jnp.zeros((SUBLANES, raw_s.shape[1]), F32)

    def proj(lo, width):
        return lax.dot_general(h_cur[...], w_ref[lo:lo + width, :], (((1,), (1,)), ((), ())),
                               preferred_element_type=F32)

    lane = lax.broadcasted_iota(jnp.int32, (1, LANES), 1)
    first_half = (lane % HEAD_DIM) < (HEAD_DIM // 2)

    def rope(t):
        swapped = jnp.where(first_half, pltpu.roll(t, LANES - HEAD_DIM // 2, axis=1),
                            pltpu.roll(t, HEAD_DIM // 2, axis=1))
        return t * cos_ref[...] + swapped * sin_ref[...]

    def post_q():
        for p in range(ATTN_Q_WIDTH // LANES):
            sl = slice(p * LANES, (p + 1) * LANES)
            qa_ref[:, sl] = (rope(rot_s[:, sl]) * (HEAD_DIM ** -0.5)).astype(BF16)

    def post_kv():
        kv_ref[:, :LANES] = rope(rot_s[:, ATTN_Q_WIDTH:ATTN_Q_WIDTH + LANES]).astype(BF16)
        kv_ref[:, LANES:] = rot_s[:, ATTN_Q_WIDTH + LANES:].astype(BF16)

    blk = MXU_WIDTH

    def post_conv(cbk):
        cs = slice(cbk * blk, (cbk + 1) * blk)
        acc = cb_ref[:, cs]
        for sh in range(CONV_WIDTH):
            acc = acc + (cw_ref[CONV_WIDTH - 1 - sh:CONV_WIDTH - sh, cs]
                         * raw_s[SUBLANES - sh:SUBLANES - sh + tm, cs])
        raw_s[0:SUBLANES, cs] = raw_s[tm:tm + SUBLANES, cs]
        a = acc * jax.nn.sigmoid(acc)
        if cbk * blk < MLSTM_WIDTH:
            qm_ref[:, cs] = a.astype(BF16)
        else:
            a = a * (MLSTM_HEAD_DIM ** -0.5)
            for d0 in range(0, blk, LANES):
                r0 = cbk * blk - MLSTM_WIDTH + d0
                for t0 in range(0, tm, LANES):
                    kt_ref[r0:r0 + LANES, t0:t0 + LANES] = a[t0:t0 + LANES, d0:d0 + LANES].T.astype(BF16)

    def norm_next(part):
        rows = slice(part * tm // INPROJ_NORM_PARTS, (part + 1) * tm // INPROJ_NORM_PARTS)
        h_next[rows, :] = _modulated_norm(xn_ref[rows, :], g_ref[...], modn_ref, 0)

    def main_rot(lo_w, lo_s, width):
        rot_s[:, lo_s:lo_s + width] = proj(lo_w, width)

    def main_conv(cbk):
        raw_s[SUBLANES:, cbk * blk:(cbk + 1) * blk] = proj(C_QK + cbk * blk, blk)

    def main_direct(out_ref, lo_w, lo, width):
        out_ref[:, lo:lo + width] = proj(lo_w + lo, width).astype(out_ref.dtype)

    half = D_MODEL // 2
    post_q()
    main_rot(C_QA, 0, ATTN_Q_WIDTH)
    post_kv()
    post_conv(0)
    main_rot(C_KV, ATTN_Q_WIDTH, 2 * ATTN_KV_WIDTH)
    main_conv(0)
    post_conv(1)
    main_conv(1)
    main_direct(vm_ref, C_VM, 0, MLSTM_WIDTH)
    post_conv(2)
    main_conv(2)
    main_direct(om_ref, C_OM, 0, MLSTM_WIDTH)
    post_conv(3)
    main_conv(3)
    main_direct(if_ref, C_IF, 0, LANES)
    norm_next(0)
    main_direct(ga_ref, C_GA, 0, half)
    norm_next(1)
    main_direct(ga_ref, C_GA, half, half)
    norm_next(2)
    main_direct(gm_ref, C_GM, 0, half)
    norm_next(3)
    main_direct(gm_ref, C_GM, half, half)
    h_cur[...] = h_next[...]


INPROJ_NORM_PARTS = 4


def _inproj(x2, mod3, g_pre, cos_t, sin_t, w_t, conv_w, conv_b, tiles_per_batch):
    t = x2.shape[0]
    tm = TOKEN_TILE
    n = t // tm
    main = lambda i: (jnp.minimum(i, n - 1), 0)
    post = lambda i: (jnp.maximum(i - 1, 0), 0)
    nxt = lambda i: (jnp.minimum(i + 1, n - 1), 0)
    const = lambda i: (0, 0)
    once = pl.Buffered(1)
    outs = [(ATTN_Q_WIDTH, BF16, post), (2 * ATTN_KV_WIDTH, BF16, post), (MLSTM_WIDTH, BF16, post),
            (None, BF16, None), (MLSTM_WIDTH, BF16, main), (MLSTM_WIDTH, BF16, main),
            (LANES, F32, main), (D_MODEL, BF16, main), (D_MODEL, BF16, main)]
    out_specs = [pl.BlockSpec((MLSTM_WIDTH, tm), lambda i: (0, jnp.maximum(i - 1, 0))) if w is None
                 else pl.BlockSpec((tm, w), m) for w, _, m in outs]
    out_shape = [jax.ShapeDtypeStruct((MLSTM_WIDTH, t) if w is None else (t, w), d) for w, d, _ in outs]
    out_bytes = sum((w or MLSTM_WIDTH) * np.dtype(d).itemsize for w, d, _ in outs) * tm
    rot_cols = ATTN_Q_WIDTH + 2 * ATTN_KV_WIDTH
    scratch_bytes = (2 * tm * D_MODEL * 2 + (tm + SUBLANES) * 2 * MLSTM_WIDTH * 4 + tm * rot_cols * 4)
    est = (w_t.size * 2 + 3 * tm * D_MODEL * 4 + 2 * out_bytes + 4 * tm * LANES * 4
           + scratch_bytes + 4 * tm * D_MODEL * 4)
    return pl.pallas_call(
        functools.partial(_inproj_kernel, n, tiles_per_batch),
        grid=(n + 1,),
        in_specs=[pl.BlockSpec((tm, D_MODEL), const, pipeline_mode=once),
                  pl.BlockSpec((1, 6, D_MODEL), lambda i: (0, 0, 0)),
                  pl.BlockSpec((tm, D_MODEL), nxt),
                  pl.BlockSpec((1, 6, D_MODEL),
                               lambda i: (jnp.minimum(i + 1, n - 1) // tiles_per_batch, 0, 0)),
                  pl.BlockSpec((1, D_MODEL), const),
                  pl.BlockSpec((tm, LANES), post),
                  pl.BlockSpec((tm, LANES), post),
                  pl.BlockSpec((IN_COLS, D_MODEL), const, pipeline_mode=once),
                  pl.BlockSpec((CONV_WIDTH, 2 * MLSTM_WIDTH), const),
                  pl.BlockSpec((1, 2 * MLSTM_WIDTH), const)],
        out_specs=out_specs,
        out_shape=out_shape,
        scratch_shapes=[pltpu.VMEM((tm, D_MODEL), BF16),
                        pltpu.VMEM((tm, D_MODEL), BF16),
                        pltpu.VMEM((tm + SUBLANES, 2 * MLSTM_WIDTH), F32),
                        pltpu.VMEM((tm, rot_cols), F32)],
        compiler_params=pltpu.CompilerParams(dimension_semantics=("arbitrary",),
                                             vmem_limit_bytes=_vmem_limit(est)),
        name="inproj",
    )(x2, mod3, x2, mod3, g_pre, cos_t, sin_t, w_t, conv_w, conv_b)


def _attn_thread(j, sink_ref, q_ref, kv_ref, kvp_ref, o_ref):
    blk = ATTN_BLOCK
    nblk = TOKEN_TILE // blk
    ngrp = ATTN_Q_WIDTH // LANES
    lane = lax.broadcasted_iota(jnp.int32, (1, LANES), 1)
    low = lane < HEAD_DIM
    qi = lax.broadcasted_iota(jnp.int32, (blk, 2 * blk), 0)
    kj = lax.broadcasted_iota(jnp.int32, (blk, 2 * blk), 1)
    in_cur = (kj >= blk) & (kj - blk <= qi)
    in_prev = (kj < blk) & (kj > qi)
    in_prev_first = (kj < blk) & (kj > qi + jnp.where(j > 0, 0, blk))
    ones = jnp.ones((2 * blk, LANES), F32)
    zeros = jnp.zeros((2 * blk, LANES), F32)
    one_lo = jnp.where(low, ones, zeros).astype(BF16)
    one_hi = jnp.where(low, zeros, ones).astype(BF16)
    band = {}
    live = {}

    def prep(i):
        cur = kv_ref[i * blk:(i + 1) * blk, :]
        prev = kvp_ref[...] if i == 0 else kv_ref[(i - 1) * blk:i * blk, :]
        kv = jnp.concatenate([prev, cur], axis=0).astype(F32)
        k = kv[:, :LANES]
        v = kv[:, LANES:]
        k_cat = jnp.concatenate([jnp.where(low, k, 0.0), jnp.where(low, 0.0, k)], axis=0).astype(BF16)
        v_cat = jnp.concatenate(
            [jnp.concatenate([jnp.where(low, v, 0.0).astype(BF16), one_lo], axis=1),
             jnp.concatenate([jnp.where(low, 0.0, v).astype(BF16), one_hi], axis=1)], axis=0)
        band[i] = (k_cat, v_cat, in_cur | (in_prev_first if i == 0 else in_prev))

    def scores(u):
        i, p = divmod(u, ngrp)
        q = q_ref[i * blk:(i + 1) * blk, p * LANES:(p + 1) * LANES]
        live[u] = lax.dot_general(q, band[i][0], (((1,), (1,)), ((), ())), preferred_element_type=F32)

    def softmax(u):
        i, p = divmod(u, ngrp)
        s = live[u]
        mask = band[i][2]
        s0 = jnp.where(mask, s[:, :2 * blk], -jnp.inf)
        s1 = jnp.where(mask, s[:, 2 * blk:], -jnp.inf)
        m0 = jnp.maximum(jnp.max(s0, axis=-1, keepdims=True), sink_ref[p])
        m1 = jnp.maximum(jnp.max(s1, axis=-1, keepdims=True), sink_ref[p + N_Q_HEADS // N_KV_HEADS])
        pr = jnp.concatenate([jnp.exp(s0 - m0), jnp.exp(s1 - m1)], axis=1).astype(BF16)
        live[u] = (pr, m0, m1)

    def values(u):
        i, p = divmod(u, ngrp)
        pr, m0, m1 = live.pop(u)
        r = jnp.dot(pr, band[i][1], preferred_element_type=F32)
        den = r[:, LANES:] + jnp.where(low, jnp.exp(sink_ref[p] - m0),
                                       jnp.exp(sink_ref[p + N_Q_HEADS // N_KV_HEADS] - m1))
        o_ref[i * blk:(i + 1) * blk, p * LANES:(p + 1) * LANES] = (r[:, :LANES] / den).astype(BF16)

    n = nblk * ngrp
    prep(0)
    for k in range(n + 2):
        if k < n:
            scores(k)
            if k + 1 < n and (k + 1) % ngrp == 0:
                prep((k + 1) // ngrp)
        if 0 <= k - 2 < n:
            values(k - 2)
        if 0 <= k - 1 < n:
            softmax(k - 1)
        yield


ATTN_STAGES = (TOKEN_TILE // ATTN_BLOCK) * (ATTN_Q_WIDTH // LANES) + 2


def _mlstm_thread(q_ref, kt_ref, v_ref, om_ref, if_ref, bif_ref, nw_ref, y_ref,
                  cn_ref, mrow_ref, mcol_ref):
    ts = TOKEN_TILE
    L = MLSTM_CHUNK
    D = MLSTM_HEAD_DIM
    nch = ts // L

    gates = if_ref[...] + bif_ref[...]
    logf_all = jax.nn.log_sigmoid(pltpu.roll(gates, LANES - MLSTM_HEADS, axis=1))

    ti = lax.broadcasted_iota(jnp.int32, (L, L), 0)
    si = lax.broadcasted_iota(jnp.int32, (L, L), 1)
    causal = si <= ti
    tri = causal.astype(BF16)
    ones_v = jnp.ones((L, LANES), BF16)
    chunk = {}
    head = {}

    def rows_of(c):
        return slice(c * L, (c + 1) * L)

    def p1(c):
        lf = logf_all[rows_of(c)]
        lf_hi = lf.astype(BF16)
        rem = lf - lf_hi.astype(F32)
        lf_mid = rem.astype(BF16)
        lf_lo = (rem - lf_mid.astype(F32)).astype(BF16)
        chunk[c] = (jnp.dot(tri, lf_hi, preferred_element_type=F32)
                    + jnp.dot(tri, lf_mid, preferred_element_type=F32)
                    + jnp.dot(tri, lf_lo, preferred_element_type=F32))

    def p2(c):
        b = chunk[c]
        r = b - gates[rows_of(c)]
        pm = -r
        sh = 1
        while sh < L:
            pm = jnp.maximum(pm, jnp.where(ti >= sh, pltpu.roll(pm, sh, axis=0), -jnp.inf))
            sh *= 2
        m_row = mrow_ref[0:1, :]
        inter = b + m_row
        mt = jnp.maximum(inter, b + pm)
        b_last = b[L - 1:L, :]
        m_new_row = jnp.maximum(b_last + m_row, jnp.max(b_last - r, axis=0, keepdims=True))
        mrow_ref[...] = jnp.broadcast_to(m_new_row, mrow_ref.shape)
        b_t = b.T[:SUBLANES]
        r_t = r.T[:SUBLANES]
        m_col = mcol_ref[:, 0:1]
        bl_col = b_t[:, L - 1:L]
        a_t = bl_col - r_t
        m_new_col = jnp.maximum(bl_col + m_col, jnp.max(a_t, axis=1, keepdims=True))
        mcol_ref[...] = jnp.broadcast_to(m_new_col, mcol_ref.shape)
        chunk[c] = dict(e_col=b - mt, r_t=r_t, w_inter=jnp.exp(inter - mt), e_mt=jnp.exp(-mt),
                        decay_row=jnp.exp(b_last + m_row - m_new_row),
                        wk_t=jnp.exp(a_t - m_new_col))

    def ha(c, h):
        hs = slice(h * D, (h + 1) * D)
        qh = q_ref[rows_of(c), hs]
        kt = kt_ref[hs, rows_of(c)]
        head[c, h] = dict(qh=qh, kt=kt, s=jnp.dot(qh, kt, preferred_element_type=F32))

    def hb(c, h):
        st, ch = head[c, h], chunk[c]
        w_intra = jnp.where(causal, jnp.exp(ch["e_col"][:, h:h + 1] - ch["r_t"][h:h + 1, :]), 0.0)
        pr = (st.pop("s") * w_intra).astype(BF16)
        qw = (st.pop("qh").astype(F32) * ch["w_inter"][:, h:h + 1]).astype(BF16)
        st["lhs"] = jnp.concatenate([pr, qw], axis=1)
        st["kw"] = (st.pop("kt").astype(F32) * ch["wk_t"][h:h + 1, :]).astype(BF16)

    def hc(c, h):
        st = head[c, h]
        hs = slice(h * D, (h + 1) * D)
        v_ext = jnp.concatenate([v_ref[rows_of(c), hs], ones_v], axis=1)
        rhs = jnp.concatenate([v_ext, cn_ref[h].astype(BF16)], axis=0)
        st["nd"] = jnp.dot(st.pop("lhs"), rhs, preferred_element_type=F32)
        st["kv"] = jnp.dot(st.pop("kw"), v_ext, preferred_element_type=F32)

    def hd(c, h):
        st, ch = head.pop((c, h)), chunk[c]
        hs = slice(h * D, (h + 1) * D)
        nd = st["nd"]
        hh = nd[:, :D] / jnp.maximum(jnp.abs(nd[:, D:]), ch["e_mt"][:, h:h + 1])
        cn_ref[h] = ch["decay_row"][:, h:h + 1] * cn_ref[h] + st["kv"]
        mu = jnp.mean(hh, axis=-1, keepdims=True)
        xc = hh - mu
        var = jnp.mean(xc * xc, axis=-1, keepdims=True)
        yn = (xc * lax.rsqrt(var + NORM_EPS)) * nw_ref[:, hs]
        y_ref[rows_of(c), hs] = (jax.nn.sigmoid(om_ref[rows_of(c), hs].astype(F32)) * yn).astype(BF16)

    half = MLSTM_HEADS // 2
    for g in range(MLSTM_STAGES):
        for c in range(nch):
            o = g - MLSTM_CHUNK_SKEW * c
            if o == 0:
                p1(c)
            elif o == 1:
                for h in range(MLSTM_HEADS):
                    ha(c, h)
                p2(c)
            elif o == 2:
                for h in range(half):
                    hb(c, h)
            elif o == 3:
                for h in range(half):
                    hc(c, h)
                for h in range(half, MLSTM_HEADS):
                    hb(c, h)
            elif o == 4:
                for h in range(half, MLSTM_HEADS):
                    hc(c, h)
                for h in range(half):
                    hd(c, h)
            elif o == 5:
                for h in range(half, MLSTM_HEADS):
                    hd(c, h)
        yield


MLSTM_CHUNK_SKEW = 4
MLSTM_STAGES = MLSTM_CHUNK_SKEW * (TOKEN_TILE // MLSTM_CHUNK - 1) + 6


def _merge_thread(x_ref, ya_ref, ym_ref, ga_ref, gm_ref, mod_ref, gpm_ref, gqf_ref,
                  wa_ref, wb_ref, wo_ref, x1_ref, h2_ref):
    tm = TOKEN_TILE
    parts = [slice(r * tm // MERGE_ROW_PARTS, (r + 1) * tm // MERGE_ROW_PARTS)
             for r in range(MERGE_ROW_PARTS)]
    gate_m = mod_ref[0, 2:3, :]
    ba = jnp.dot(ya_ref[...], wa_ref[...], preferred_element_type=F32)
    yield
    bb = jnp.dot(ym_ref[...], wb_ref[...], preferred_element_type=F32)
    yield
    merged = []
    for rows in parts:
        merged.append((jax.nn.sigmoid(ga_ref[rows, :].astype(F32)) * ba[rows]
                       + jax.nn.sigmoid(gm_ref[rows, :].astype(F32)) * bb[rows]).astype(BF16))
        yield
    mix = jnp.dot(jnp.concatenate(merged, axis=0), wo_ref[...], preferred_element_type=F32)
    yield
    for rows in parts:
        x1 = x_ref[rows, :] + gate_m * _rms(mix[rows], gpm_ref[...])
        x1_ref[rows, :] = x1
        yield
        h2_ref[rows, :] = _modulated_norm(x1, gqf_ref[...], mod_ref, 3)
        yield


MERGE_ROW_PARTS = 4
MERGE_STAGES = 3 + 3 * MERGE_ROW_PARTS


def _ffn_thread(x1_ref, h2_ref, mod_ref, gpf_ref, wg_ref, wu_ref, wd_ref, o_ref, act_s, after_up):
    gate_f = mod_ref[0, 5:6, :]
    for c in range(D_FF // FF_CHUNK):
        cs = slice(c * FF_CHUNK, (c + 1) * FF_CHUNK)
        gu = jnp.dot(h2_ref[...], jnp.concatenate([wg_ref[:, cs], wu_ref[:, cs]], axis=1),
                     preferred_element_type=F32)
        g = gu[:, :FF_CHUNK]
        act_s[:, c * FF_CHUNK:(c + 1) * FF_CHUNK] = ((g * jax.nn.sigmoid(g)) * gu[:, FF_CHUNK:]).astype(BF16)
        yield
    after_up()
    ff = []
    for c in range(FFN_DOWN_SPLIT):
        cs = slice(c * D_MODEL // FFN_DOWN_SPLIT, (c + 1) * D_MODEL // FFN_DOWN_SPLIT)
        ff.append(jnp.dot(act_s[...], wd_ref[:, cs], preferred_element_type=F32))
        yield
    o_ref[...] = x1_ref[...] + gate_f * _rms(jnp.concatenate(ff, axis=1), gpf_ref[...])
    yield


FFN_DOWN_SPLIT = 2
FFN_UP_STAGES = D_FF // FF_CHUNK
FFN_STAGES = FFN_UP_STAGES + FFN_DOWN_SPLIT + 1


def _interleave(main, n_main, fills):
    done = [0] * len(fills)
    for k in range(n_main):
        next(main)
        for f, (gen, count, first, last) in enumerate(fills):
            span = last - first + 1
            want = 0 if k < first else min(count, ((k - first + 1) * count + span - 1) // span)
            while done[f] < want:
                next(gen)
                done[f] += 1
    assert all(d == f[1] for d, f in zip(done, fills))
    assert next(main, None) is None and all(next(f[0], None) is None for f in fills)


def _back_kernel(n_tiles, tiles_per_batch, sink_ref,
                 qa_ref, kv_ref, kvp_ref, qm_ref, kt_ref, vm_ref, om_ref, if_ref, bif_ref, nw_ref,
                 x_ref, ga_ref, gm_ref, modm_ref, gpm_ref, gqf_ref,
                 modf_ref, gpf_ref,
                 wa_ref, wb_ref, wo_ref, wg_ref, wu_ref, wd_ref, o_ref,
                 ya_s, ym_s, x1_s, h2_cur, h2_next, act_s, cn_ref, mrow_ref, mcol_ref):
    i = pl.program_id(0)
    j = jnp.minimum(i, n_tiles - 1) % tiles_per_batch
    slot = i % 2

    @pl.when(i == 0)
    def _():
        ya_s[...] = jnp.zeros_like(ya_s)
        ym_s[...] = jnp.zeros_like(ym_s)
        x1_s[...] = jnp.zeros_like(x1_s)
        h2_cur[...] = jnp.zeros_like(h2_cur)

    @pl.when(j == 0)
    def _():
        cn_ref[...] = jnp.zeros_like(cn_ref)
        mrow_ref[...] = jnp.zeros_like(mrow_ref)
        mcol_ref[...] = jnp.zeros_like(mcol_ref)

    def hand_over():
        h2_cur[...] = h2_next[...]

    ffn = _ffn_thread(x1_s.at[slot], h2_cur, modf_ref, gpf_ref, wg_ref, wu_ref, wd_ref, o_ref, act_s,
                      hand_over)
    merge = _merge_thread(x_ref, ya_s, ym_s, ga_ref, gm_ref, modm_ref, gpm_ref, gqf_ref,
                          wa_ref, wb_ref, wo_ref, x1_s.at[1 - slot], h2_next)
    attn = _attn_thread(j, sink_ref, qa_ref, kv_ref, kvp_ref, ya_s)
    mlstm = _mlstm_thread(qm_ref, kt_ref, vm_ref, om_ref, if_ref, bif_ref, nw_ref, ym_s,
                          cn_ref, mrow_ref, mcol_ref)
    _interleave(ffn, FFN_STAGES,
                [(merge, MERGE_STAGES, 0, FFN_UP_STAGES - 2),
                 (attn, ATTN_STAGES, 2, FFN_STAGES - 2),
                 (mlstm, MLSTM_STAGES, 2, FFN_STAGES - 2)])


def _back(sinks, qa, kv, qm, kt, vm, om, gates, bif, norm_w, x2, ga, gm, mod3,
          g_post_mix, g_pre_ffn, g_post_ffn, wa, wb, wo, wg, wu, wd, tiles_per_batch):
    t = x2.shape[0]
    tm = TOKEN_TILE
    n = t // tm
    per_tile = tm // ATTN_BLOCK
    clamp = lambda v: jnp.clip(v, 0, n - 1)
    cur = lambda i: (clamp(i), 0)
    cur_t = lambda i: (0, clamp(i))

    def prev_block(i):
        tile = clamp(i)
        first = (tile // tiles_per_batch) * tiles_per_batch * per_tile
        return (jnp.maximum(tile * per_tile - 1, first), 0)

    mid = lambda i: (clamp(i - 1), 0)
    last = lambda i: (clamp(i - 2), 0)
    const = lambda i: (0, 0)
    once = pl.Buffered(1)
    weight_bytes = 2 * (wa.size + wb.size + wo.size + wg.size + wu.size + wd.size)
    mixer_in = tm * (ATTN_Q_WIDTH + 2 * ATTN_KV_WIDTH + 4 * MLSTM_WIDTH) * 2 + tm * LANES * 4
    scratch_bytes = tm * (D_FF + 2 * MLSTM_WIDTH + 2 * D_MODEL) * 2 + 2 * tm * D_MODEL * 4
    est = (weight_bytes + 2 * 2 * tm * D_MODEL * 4 + 2 * tm * 2 * D_MODEL * 2 + 2 * mixer_in
           + scratch_bytes + 6 * tm * D_MODEL * 4)
    return pl.pallas_call(
        functools.partial(_back_kernel, n, tiles_per_batch),
        grid=(n + 2,),
        in_specs=[pl.BlockSpec(memory_space=pltpu.SMEM),
                  pl.BlockSpec((tm, ATTN_Q_WIDTH), cur),
                  pl.BlockSpec((tm, 2 * ATTN_KV_WIDTH), cur),
                  pl.BlockSpec((ATTN_BLOCK, 2 * ATTN_KV_WIDTH), prev_block),
                  pl.BlockSpec((tm, MLSTM_WIDTH), cur),
                  pl.BlockSpec((MLSTM_WIDTH, tm), cur_t),
                  pl.BlockSpec((tm, MLSTM_WIDTH), cur),
                  pl.BlockSpec((tm, MLSTM_WIDTH), cur),
                  pl.BlockSpec((tm, LANES), cur),
                  pl.BlockSpec((1, LANES), const),
                  pl.BlockSpec((1, MLSTM_WIDTH), const),
                  pl.BlockSpec((tm, D_MODEL), mid),
                  pl.BlockSpec((tm, D_MODEL), mid),
                  pl.BlockSpec((tm, D_MODEL), mid),
                  pl.BlockSpec((1, 6, D_MODEL), lambda i: (clamp(i - 1) // tiles_per_batch, 0, 0)),
                  pl.BlockSpec((1, D_MODEL), const),
                  pl.BlockSpec((1, D_MODEL), const),
                  pl.BlockSpec((1, 6, D_MODEL), lambda i: (clamp(i - 2) // tiles_per_batch, 0, 0)),
                  pl.BlockSpec((1, D_MODEL), const),
                  pl.BlockSpec(wa.shape, const, pipeline_mode=once),
                  pl.BlockSpec(wb.shape, const, pipeline_mode=once),
                  pl.BlockSpec(wo.shape, const, pipeline_mode=once),
                  pl.BlockSpec(wg.shape, const, pipeline_mode=once),
                  pl.BlockSpec(wu.shape, const, pipeline_mode=once),
                  pl.BlockSpec(wd.shape, const, pipeline_mode=once)],
        out_specs=pl.BlockSpec((tm, D_MODEL), last),
        out_shape=jax.ShapeDtypeStruct((t, D_MODEL), F32),
        scratch_shapes=[pltpu.VMEM((tm, ATTN_Q_WIDTH), BF16),
                        pltpu.VMEM((tm, MLSTM_WIDTH), BF16),
                        pltpu.VMEM((2, tm, D_MODEL), F32),
                        pltpu.VMEM((tm, D_MODEL), BF16),
                        pltpu.VMEM((tm, D_MODEL), BF16),
                        pltpu.VMEM((tm, D_FF), BF16),
                        pltpu.VMEM((MLSTM_HEADS, MLSTM_HEAD_DIM, 2 * MLSTM_HEAD_DIM), F32),
                        pltpu.VMEM((SUBLANES, LANES), F32),
                        pltpu.VMEM((SUBLANES, LANES), F32)],
        compiler_params=pltpu.CompilerParams(dimension_semantics=("arbitrary",),
                                             vmem_limit_bytes=_vmem_limit(est)),
        name="back",
    )(sinks, qa, kv, kv, qm, kt, vm, om, gates, bif, norm_w, x2, ga, gm, mod3,
      g_post_mix, g_pre_ffn, mod3, g_post_ffn, wa, wb, wo, wg, wu, wd)


def _q_head_order():
    per_kv = N_Q_HEADS // N_KV_HEADS
    return [h for p in range(per_kv) for h in (p, p + per_kv)]


def _layout_w_in(w_in):
    wt = w_in.T
    q_a = wt[:ATTN_Q_WIDTH].reshape(N_Q_HEADS, HEAD_DIM, D_MODEL)[np.array(_q_head_order())]
    n_if = 2 * MLSTM_HEADS
    gates = jnp.pad(wt[C_IF:C_IF + n_if], ((0, LANES - n_if), (0, 0)))
    return jnp.concatenate([q_a.reshape(ATTN_Q_WIDTH, D_MODEL), wt[ATTN_Q_WIDTH:C_IF], gates,
                            wt[C_IF + n_if:]], axis=0).astype(BF16)


def kernel(x, c, positions, w_ada, b_ada, g_pre_mix, g_post_mix, w_in, b_if, conv_w, conv_b,
           attn_sinks, mlstm_norm_w, w_branch_attn, w_branch_mlstm, w_out, g_pre_ffn, g_post_ffn,
           w_ffn_gate, w_ffn_up, w_ffn_down):
    batch, seq, d = x.shape
    depth = w_in.shape[0]
    assert d == D_MODEL and seq % TOKEN_TILE == 0 and (batch * seq) % ROPE_TILE == 0
    assert D_FF % FF_CHUNK == 0
    t = batch * seq
    tiles_per_batch = seq // TOKEN_TILE
    x2 = x.reshape(t, d)

    inv_freq = (ROPE_THETA ** (-2.0 * jnp.arange(HEAD_DIM // 2, dtype=F32) / HEAD_DIM)).reshape(-1, 1)
    c_pad = jnp.pad(c, ((0, SUBLANES - batch % SUBLANES), (0, 0))) if batch % SUBLANES else c
    head_order = np.array(_q_head_order())

    for l in range(depth):
        cos_t, sin_t, mod = _prep(positions.reshape(1, t), inv_freq, c_pad, w_ada[l],
                                  b_ada[l].reshape(1, -1))
        mod3 = mod[:batch].reshape(batch, 6, d)
        qa, kv, qm, kt, vm, om, gates, ga, gm = _inproj(
            x2, mod3, g_pre_mix[l].reshape(1, d), cos_t, sin_t, _layout_w_in(w_in[l]),
            conv_w[l], conv_b[l].reshape(1, -1), tiles_per_batch)
        bif = jnp.pad(b_if[l], (0, LANES - 2 * MLSTM_HEADS)).reshape(1, LANES)
        wa = w_branch_attn[l].reshape(N_Q_HEADS, HEAD_DIM, d)[head_order].reshape(ATTN_Q_WIDTH, d)
        x2 = _back(attn_sinks[l], qa, kv, qm, kt, vm, om, gates, bif, mlstm_norm_w[l].reshape(1, -1),
                   x2, ga, gm, mod3, g_post_mix[l].reshape(1, d), g_pre_ffn[l].reshape(1, d),
                   g_post_ffn[l].reshape(1, d), wa.astype(BF16), w_branch_mlstm[l].astype(BF16),
                   w_out[l].astype(BF16), w_ffn_gate[l].astype(BF16), w_ffn_up[l].astype(BF16),
                   w_ffn_down[l].astype(BF16), tiles_per_batch)
    return x2.reshape(batch, seq, d)
```

```python
import functools

import numpy as np
import jax
import jax.numpy as jnp
from jax import lax
from jax.experimental import pallas as pl
from jax.experimental.pallas import tpu as pltpu

F32 = jnp.float32
BF16 = jnp.bfloat16

D_MODEL = 1024
N_Q_HEADS = 8
N_KV_HEADS = 2
HEAD_DIM = 64
ROPE_THETA = 10000.0
MLSTM_HEADS = 4
MLSTM_HEAD_DIM = 128
CONV_WIDTH = 4
D_FF = 2816
NORM_EPS = 1e-6
ATTN_Q_WIDTH = N_Q_HEADS * HEAD_DIM
ATTN_KV_WIDTH = N_KV_HEADS * HEAD_DIM
MLSTM_WIDTH = MLSTM_HEADS * MLSTM_HEAD_DIM
ATTN_BLOCK = 128

LANES = 128
SUBLANES = 8
MXU_WIDTH = 256
V7X_SCOPED_VMEM_BYTES = 60000 * 1024

TOKEN_TILE = 512
MLSTM_CHUNK = 128
FF_CHUNK = LANES
ROPE_TILE = 2048

C_QA = 0
C_KV = C_QA + ATTN_Q_WIDTH
C_QK = C_KV + 2 * ATTN_KV_WIDTH
C_VM = C_QK + 2 * MLSTM_WIDTH
C_OM = C_VM + MLSTM_WIDTH
C_IF = C_OM + MLSTM_WIDTH
C_GA = C_IF + LANES
C_GM = C_GA + D_MODEL
IN_COLS = C_GM + D_MODEL


def _vmem_limit(estimate_bytes):
    return int(min(V7X_SCOPED_VMEM_BYTES, max(estimate_bytes, 16 * 1024 * 1024)))


def _rms(x, g):
    return (x * lax.rsqrt(jnp.mean(x * x, axis=-1, keepdims=True) + NORM_EPS)) * g


def _prep_kernel(pos_ref, freq_ref, c_ref, w_ref, b_ref, cos_ref, sin_ref, mod_ref):
    mod_ref[...] = jnp.dot(c_ref[...], w_ref[...], preferred_element_type=F32,
                           precision=lax.Precision.HIGHEST) + b_ref[...]
    ang = freq_ref[...] * pos_ref[...].astype(F32)
    c = jnp.cos(ang)
    s = jnp.sin(ang)
    cos_ref[...] = jnp.concatenate([c, c, c, c], axis=0).T
    sin_ref[...] = jnp.concatenate([-s, s, -s, s], axis=0).T


def _prep(pos_row, inv_freq, c_pad, w_ada, b_ada):
    t = pos_row.shape[1]
    half = inv_freq.shape[0]
    rows = c_pad.shape[0]
    n = w_ada.shape[1]
    steps = t // ROPE_TILE
    cols = n // steps
    assert n % steps == 0 and cols % LANES == 0
    return pl.pallas_call(
        _prep_kernel,
        grid=(steps,),
        in_specs=[pl.BlockSpec((1, ROPE_TILE), lambda i: (0, i)),
                  pl.BlockSpec((half, 1), lambda i: (0, 0)),
                  pl.BlockSpec((rows, D_MODEL), lambda i: (0, 0)),
                  pl.BlockSpec((D_MODEL, cols), lambda i: (0, i)),
                  pl.BlockSpec((1, cols), lambda i: (0, i))],
        out_specs=[pl.BlockSpec((ROPE_TILE, LANES), lambda i: (i, 0)),
                   pl.BlockSpec((ROPE_TILE, LANES), lambda i: (i, 0)),
                   pl.BlockSpec((rows, cols), lambda i: (0, i))],
        out_shape=[jax.ShapeDtypeStruct((t, LANES), F32),
                   jax.ShapeDtypeStruct((t, LANES), F32),
                   jax.ShapeDtypeStruct((rows, n), F32)],
        name="prep",
    )(pos_row, inv_freq, c_pad, w_ada, b_ada)


def _modulated_norm(x, g, mod_ref, row):
    return (_rms(x, g) * (1.0 + mod_ref[0, row + 1:row + 2, :]) + mod_ref[0, row:row + 1, :]).astype(BF16)


def _inproj_kernel(n_tiles, tiles_per_batch, x0_ref, mod0_ref, xn_ref, modn_ref, g_ref, cos_ref, sin_ref,
                   w_ref, cw_ref, cb_ref,
                   qa_ref, kv_ref, qm_ref, kt_ref, vm_ref, om_ref, if_ref, ga_ref, gm_ref,
                   s_ref, raw_s, rot_s):
    tm = TOKEN_TILE
    i = pl.program_id(0)
    par = i % 2
    post_tile = jnp.maximum(i - 1, 0)
    h_cur = s_ref.at[par]
    h_next = s_ref.at[1 - par]
    out_a = s_ref.at[2 + par]
    out_b = s_ref.at[4 + par]
    kv_lo = ATTN_Q_WIDTH

    @pl.when(i == 0)
    def _():
        s_ref[0] = _modulated_norm(x0_ref[...], g_ref[...], mod0_ref, 0)
        raw_s[...] = jnp.zeros_like(raw_s)
        rot_s[...] = jnp.zeros_like(rot_s)

    @pl.when(post_tile % tiles_per_batch == 0)
    def _():
        raw_s[0:SUBLANES, :] = jnp.zeros((SUBLANES, raw_s.shape[1]), F32)

    def proj(lo, width):
        return lax.dot_general(h_cur[...], w_ref[lo:lo + width, :], (((1,), (1,)), ((), ())),
                               preferred_element_type=F32)

    lane = lax.broadcasted_iota(jnp.int32, (1, LANES), 1)
    first_half = (lane % HEAD_DIM) < (HEAD_DIM // 2)

    def rope(t):
        swapped = jnp.where(first_half, pltpu.roll(t, LANES - HEAD_DIM // 2, axis=1),
                            pltpu.roll(t, HEAD_DIM // 2, axis=1))
        return t * cos_ref[...] + swapped * sin_ref[...]

    def post_q():
        for p in range(ATTN_Q_WIDTH // LANES):
            sl = slice(p * LANES, (p + 1) * LANES)
            out_a[:, sl] = (rope(rot_s[:, sl]) * (HEAD_DIM ** -0.5)).astype(BF16)

    def post_kv():
        out_a[:, kv_lo:kv_lo + LANES] = rope(rot_s[:, kv_lo:kv_lo + LANES]).astype(BF16)
        out_a[:, kv_lo + LANES:kv_lo + 2 * LANES] = rot_s[:, kv_lo + LANES:].astype(BF16)

    blk = MXU_WIDTH

    def post_conv(cbk):
        cs = slice(cbk * blk, (cbk + 1) * blk)
        acc = cb_ref[:, cs]
        for sh in range(CONV_WIDTH):
            acc = acc + (cw_ref[CONV_WIDTH - 1 - sh:CONV_WIDTH - sh, cs]
                         * raw_s[SUBLANES - sh:SUBLANES - sh + tm, cs])
        raw_s[0:SUBLANES, cs] = raw_s[tm:tm + SUBLANES, cs]
        a = acc * jax.nn.sigmoid(acc)
        if cbk * blk < MLSTM_WIDTH:
            out_b[:, cs] = a.astype(BF16)
        else:
            a = a * (MLSTM_HEAD_DIM ** -0.5)
            for d0 in range(0, blk, LANES):
                r0 = cbk * blk - MLSTM_WIDTH + d0
                for t0 in range(0, tm, LANES):
                    out_b[r0:r0 + LANES, MLSTM_WIDTH + t0:MLSTM_WIDTH + t0 + LANES] = (
                        a[t0:t0 + LANES, d0:d0 + LANES].T.astype(BF16))

    def ship_q():
        qa_ref[...] = out_a[:, :ATTN_Q_WIDTH]

    def ship_kv():
        kv_ref[...] = out_a[:, kv_lo:kv_lo + 2 * ATTN_KV_WIDTH]

    def ship_conv(cbk):
        cs = slice(cbk * blk, (cbk + 1) * blk)
        if cbk * blk < MLSTM_WIDTH:
            qm_ref[:, cs] = out_b[:, cs]
        else:
            rs = slice(cbk * blk - MLSTM_WIDTH, (cbk + 1) * blk - MLSTM_WIDTH)
            kt_ref[rs, :] = out_b[rs, MLSTM_WIDTH:MLSTM_WIDTH + tm]

    def norm_next(part):
        rows = slice(part * tm // INPROJ_NORM_PARTS, (part + 1) * tm // INPROJ_NORM_PARTS)
        h_next[rows, :] = _modulated_norm(xn_ref[rows, :], g_ref[...], modn_ref, 0)

    def main_rot(lo_w, lo_s, width):
        rot_s[:, lo_s:lo_s + width] = proj(lo_w, width)

    def main_conv(cbk):
        raw_s[SUBLANES:, cbk * blk:(cbk + 1) * blk] = proj(C_QK + cbk * blk, blk)

    def main_direct(out_ref, lo_w, lo, width):
        out_ref[:, lo:lo + width] = proj(lo_w + lo, width).astype(out_ref.dtype)

    half = D_MODEL // 2
    main_direct(vm_ref, C_VM, 0, MLSTM_WIDTH)
    post_q()
    main_rot(C_QA, 0, ATTN_Q_WIDTH)
    ship_q()
    post_kv()
    post_conv(0)
    main_rot(C_KV, kv_lo, 2 * ATTN_KV_WIDTH)
    ship_kv()
    main_conv(0)
    ship_conv(0)
    post_conv(1)
    main_direct(om_ref, C_OM, 0, MLSTM_WIDTH)
    main_conv(1)
    ship_conv(1)
    post_conv(2)
    main_direct(if_ref, C_IF, 0, LANES)
    main_conv(2)
    ship_conv(2)
    post_conv(3)
    main_direct(ga_ref, C_GA, 0, half)
    main_conv(3)
    ship_conv(3)
    norm_next(0)
    main_direct(ga_ref, C_GA, half, half)
    norm_next(1)
    main_direct(gm_ref, C_GM, 0, half)
    norm_next(2)
    norm_next(3)
    main_direct(gm_ref, C_GM, half, half)


INPROJ_NORM_PARTS = 4


def _inproj(x2, mod3, g_pre, cos_t, sin_t, w_t, conv_w, conv_b, tiles_per_batch):
    t = x2.shape[0]
    tm = TOKEN_TILE
    n = t // tm
    main = lambda i: (jnp.minimum(i, n - 1), 0)
    post = lambda i: (jnp.maximum(i - 1, 0), 0)
    nxt = lambda i: (jnp.minimum(i + 1, n - 1), 0)
    const = lambda i: (0, 0)
    once = pl.Buffered(1)
    outs = [(ATTN_Q_WIDTH, BF16, post), (2 * ATTN_KV_WIDTH, BF16, post), (MLSTM_WIDTH, BF16, post),
            (None, BF16, None), (MLSTM_WIDTH, BF16, main), (MLSTM_WIDTH, BF16, main),
            (LANES, F32, main), (D_MODEL, BF16, main), (D_MODEL, BF16, main)]
    out_specs = [pl.BlockSpec((MLSTM_WIDTH, tm), lambda i: (0, jnp.maximum(i - 1, 0))) if w is None
                 else pl.BlockSpec((tm, w), m) for w, _, m in outs]
    out_shape = [jax.ShapeDtypeStruct((MLSTM_WIDTH, t) if w is None else (t, w), d) for w, d, _ in outs]
    out_bytes = sum((w or MLSTM_WIDTH) * np.dtype(d).itemsize for w, d, _ in outs) * tm
    rot_cols = ATTN_Q_WIDTH + 2 * ATTN_KV_WIDTH
    assert tm == MLSTM_WIDTH
    scratch_bytes = (6 * tm * D_MODEL * 2 + (tm + SUBLANES) * 2 * MLSTM_WIDTH * 4 + tm * rot_cols * 4)
    est = (w_t.size * 2 + 3 * tm * D_MODEL * 4 + 2 * out_bytes + 4 * tm * LANES * 4
           + scratch_bytes + 4 * tm * D_MODEL * 4)
    return pl.pallas_call(
        functools.partial(_inproj_kernel, n, tiles_per_batch),
        grid=(n + 1,),
        in_specs=[pl.BlockSpec((tm, D_MODEL), const, pipeline_mode=once),
                  pl.BlockSpec((1, 6, D_MODEL), lambda i: (0, 0, 0)),
                  pl.BlockSpec((tm, D_MODEL), nxt),
                  pl.BlockSpec((1, 6, D_MODEL),
                               lambda i: (jnp.minimum(i + 1, n - 1) // tiles_per_batch, 0, 0)),
                  pl.BlockSpec((1, D_MODEL), const),
                  pl.BlockSpec((tm, LANES), post),
                  pl.BlockSpec((tm, LANES), post),
                  pl.BlockSpec((IN_COLS, D_MODEL), const, pipeline_mode=once),
                  pl.BlockSpec((CONV_WIDTH, 2 * MLSTM_WIDTH), const),
                  pl.BlockSpec((1, 2 * MLSTM_WIDTH), const)],
        out_specs=out_specs,
        out_shape=out_shape,
        scratch_shapes=[pltpu.VMEM((6, tm, D_MODEL), BF16),
                        pltpu.VMEM((tm + SUBLANES, 2 * MLSTM_WIDTH), F32),
                        pltpu.VMEM((tm, rot_cols), F32)],
        compiler_params=pltpu.CompilerParams(dimension_semantics=("arbitrary",),
                                             vmem_limit_bytes=_vmem_limit(est)),
        name="inproj",
    )(x2, mod3, x2, mod3, g_pre, cos_t, sin_t, w_t, conv_w, conv_b)


def _attn_thread(j, sink_ref, q_ref, kv_ref, kvp_ref, o_ref):
    blk = ATTN_BLOCK
    nblk = TOKEN_TILE // blk
    ngrp = ATTN_Q_WIDTH // LANES
    lane = lax.broadcasted_iota(jnp.int32, (1, LANES), 1)
    low = lane < HEAD_DIM
    qi = lax.broadcasted_iota(jnp.int32, (blk, 2 * blk), 0)
    kj = lax.broadcasted_iota(jnp.int32, (blk, 2 * blk), 1)
    in_cur = (kj >= blk) & (kj - blk <= qi)
    in_prev = (kj < blk) & (kj > qi)
    in_prev_first = (kj < blk) & (kj > qi + jnp.where(j > 0, 0, blk))
    ones = jnp.ones((2 * blk, LANES), F32)
    zeros = jnp.zeros((2 * blk, LANES), F32)
    one_lo = jnp.where(low, ones, zeros).astype(BF16)
    one_hi = jnp.where(low, zeros, ones).astype(BF16)
    band = {}
    live = {}

    def prep(i):
        cur = kv_ref[i * blk:(i + 1) * blk, :]
        prev = kvp_ref[...] if i == 0 else kv_ref[(i - 1) * blk:i * blk, :]
        kv = jnp.concatenate([prev, cur], axis=0).astype(F32)
        k = kv[:, :LANES]
        v = kv[:, LANES:]
        k_cat = jnp.concatenate([jnp.where(low, k, 0.0), jnp.where(low, 0.0, k)], axis=0).astype(BF16)
        v_cat = jnp.concatenate(
            [jnp.concatenate([jnp.where(low, v, 0.0).astype(BF16), one_lo], axis=1),
             jnp.concatenate([jnp.where(low, 0.0, v).astype(BF16), one_hi], axis=1)], axis=0)
        band[i] = (k_cat, v_cat, in_cur | (in_prev_first if i == 0 else in_prev))

    def scores(u):
        i, p = divmod(u, ngrp)
        q = q_ref[i * blk:(i + 1) * blk, p * LANES:(p + 1) * LANES]
        live[u] = lax.dot_general(q, band[i][0], (((1,), (1,)), ((), ())), preferred_element_type=F32)

    def softmax(u):
        i, p = divmod(u, ngrp)
        s = live[u]
        mask = band[i][2]
        s0 = jnp.where(mask, s[:, :2 * blk], -jnp.inf)
        s1 = jnp.where(mask, s[:, 2 * blk:], -jnp.inf)
        m0 = jnp.maximum(jnp.max(s0, axis=-1, keepdims=True), sink_ref[p])
        m1 = jnp.maximum(jnp.max(s1, axis=-1, keepdims=True), sink_ref[p + N_Q_HEADS // N_KV_HEADS])
        pr = jnp.concatenate([jnp.exp(s0 - m0), jnp.exp(s1 - m1)], axis=1).astype(BF16)
        live[u] = (pr, m0, m1)

    def values(u):
        i, p = divmod(u, ngrp)
        pr, m0, m1 = live.pop(u)
        r = jnp.dot(pr, band[i][1], preferred_element_type=F32)
        den = r[:, LANES:] + jnp.where(low, jnp.exp(sink_ref[p] - m0),
                                       jnp.exp(sink_ref[p + N_Q_HEADS // N_KV_HEADS] - m1))
        o_ref[i * blk:(i + 1) * blk, p * LANES:(p + 1) * LANES] = (r[:, :LANES] / den).astype(BF16)

    n = nblk * ngrp
    prep(0)
    for k in range(n + 2):
        if k < n:
            scores(k)
            if k + 1 < n and (k + 1) % ngrp == 0:
                prep((k + 1) // ngrp)
        if 0 <= k - 2 < n:
            values(k - 2)
        if 0 <= k - 1 < n:
            softmax(k - 1)
        yield


ATTN_STAGES = (TOKEN_TILE // ATTN_BLOCK) * (ATTN_Q_WIDTH // LANES) + 2


def _mlstm_thread(q_ref, kt_ref, v_ref, om_ref, if_ref, bif_ref, nw_ref, y_ref,
                  cn_ref, mrow_ref, mcol_ref):
    ts = TOKEN_TILE
    L = MLSTM_CHUNK
    D = MLSTM_HEAD_DIM
    nch = ts // L

    gates = if_ref[...] + bif_ref[...]
    logf_all = jax.nn.log_sigmoid(pltpu.roll(gates, LANES - MLSTM_HEADS, axis=1))

    ti = lax.broadcasted_iota(jnp.int32, (L, L), 0)
    si = lax.broadcasted_iota(jnp.int32, (L, L), 1)
    causal = si <= ti
    tri = causal.astype(BF16)
    ones_v = jnp.ones((L, LANES), BF16)
    chunk = {}
    head = {}

    def rows_of(c):
        return slice(c * L, (c + 1) * L)

    def p1(c):
        lf = logf_all[rows_of(c)]
        lf_hi = lf.astype(BF16)
        rem = lf - lf_hi.astype(F32)
        lf_mid = rem.astype(BF16)
        lf_lo = (rem - lf_mid.astype(F32)).astype(BF16)
        chunk[c] = (jnp.dot(tri, lf_hi, preferred_element_type=F32)
                    + jnp.dot(tri, lf_mid, preferred_element_type=F32)
                    + jnp.dot(tri, lf_lo, preferred_element_type=F32))

    def p2(c):
        b = chunk[c]
        r = b - gates[rows_of(c)]
        pm = -r
        sh = 1
        while sh < L:
            pm = jnp.maximum(pm, jnp.where(ti >= sh, pltpu.roll(pm, sh, axis=0), -jnp.inf))
            sh *= 2
        m_row = mrow_ref[0:1, :]
        inter = b + m_row
        mt = jnp.maximum(inter, b + pm)
        b_last = b[L - 1:L, :]
        m_new_row = jnp.maximum(b_last + m_row, jnp.max(b_last - r, axis=0, keepdims=True))
        mrow_ref[...] = jnp.broadcast_to(m_new_row, mrow_ref.shape)
        b_t = b.T[:SUBLANES]
        r_t = r.T[:SUBLANES]
        m_col = mcol_ref[:, 0:1]
        bl_col = b_t[:, L - 1:L]
        a_t = bl_col - r_t
        m_new_col = jnp.maximum(bl_col + m_col, jnp.max(a_t, axis=1, keepdims=True))
        mcol_ref[...] = jnp.broadcast_to(m_new_col, mcol_ref.shape)
        chunk[c] = dict(e_col=b - mt, r_t=r_t, w_inter=jnp.exp(inter - mt), e_mt=jnp.exp(-mt),
                        decay_row=jnp.exp(b_last + m_row - m_new_row),
                        wk_t=jnp.exp(a_t - m_new_col))

    def ha(c, h):
        hs = slice(h * D, (h + 1) * D)
        qh = q_ref[rows_of(c), hs]
        kt = kt_ref[hs, rows_of(c)]
        head[c, h] = dict(qh=qh, kt=kt, s=jnp.dot(qh, kt, preferred_element_type=F32))

    def hb(c, h):
        st, ch = head[c, h], chunk[c]
        w_intra = jnp.where(causal, jnp.exp(ch["e_col"][:, h:h + 1] - ch["r_t"][h:h + 1, :]), 0.0)
        pr = (st.pop("s") * w_intra).astype(BF16)
        qw = (st.pop("qh").astype(F32) * ch["w_inter"][:, h:h + 1]).astype(BF16)
        st["lhs"] = jnp.concatenate([pr, qw], axis=1)
        st["kw"] = (st.pop("kt").astype(F32) * ch["wk_t"][h:h + 1, :]).astype(BF16)

    def hc(c, h):
        st = head[c, h]
        hs = slice(h * D, (h + 1) * D)
        v_ext = jnp.concatenate([v_ref[rows_of(c), hs], ones_v], axis=1)
        rhs = jnp.concatenate([v_ext, cn_ref[h].astype(BF16)], axis=0)
        st["nd"] = jnp.dot(st.pop("lhs"), rhs, preferred_element_type=F32)
        st["kv"] = jnp.dot(st.pop("kw"), v_ext, preferred_element_type=F32)

    def hd(c, h):
        st, ch = head.pop((c, h)), chunk[c]
        hs = slice(h * D, (h + 1) * D)
        nd = st["nd"]
        hh = nd[:, :D] / jnp.maximum(jnp.abs(nd[:, D:]), ch["e_mt"][:, h:h + 1])
        cn_ref[h] = ch["decay_row"][:, h:h + 1] * cn_ref[h] + st["kv"]
        mu = jnp.mean(hh, axis=-1, keepdims=True)
        xc = hh - mu
        var = jnp.mean(xc * xc, axis=-1, keepdims=True)
        yn = (xc * lax.rsqrt(var + NORM_EPS)) * nw_ref[:, hs]
        y_ref[rows_of(c), hs] = (jax.nn.sigmoid(om_ref[rows_of(c), hs].astype(F32)) * yn).astype(BF16)

    half = MLSTM_HEADS // 2
    for g in range(MLSTM_STAGES):
        for c in range(nch):
            o = g - MLSTM_CHUNK_SKEW * c
            if o == 0:
                p1(c)
            elif o == 1:
                for h in range(MLSTM_HEADS):
                    ha(c, h)
                p2(c)
            elif o == 2:
                for h in range(half):
                    hb(c, h)
            elif o == 3:
                for h in range(half):
                    hc(c, h)
                for h in range(half, MLSTM_HEADS):
                    hb(c, h)
            elif o == 4:
                for h in range(half, MLSTM_HEADS):
                    hc(c, h)
                for h in range(half):
                    hd(c, h)
            elif o == 5:
                for h in range(half, MLSTM_HEADS):
                    hd(c, h)
        yield


MLSTM_CHUNK_SKEW = 4
MLSTM_STAGES = MLSTM_CHUNK_SKEW * (TOKEN_TILE // MLSTM_CHUNK - 1) + 6


def _merge_thread(x_ref, ya_ref, ym_ref, ga_ref, gm_ref, mod_ref, gpm_ref, gqf_ref,
                  wa_ref, wb_ref, wo_ref, x1_ref, h2_ref):
    tm = TOKEN_TILE
    parts = [slice(r * tm // MERGE_ROW_PARTS, (r + 1) * tm // MERGE_ROW_PARTS)
             for r in range(MERGE_ROW_PARTS)]
    gate_m = mod_ref[0, 2:3, :]
    ba = jnp.dot(ya_ref[...], wa_ref[...], preferred_element_type=F32)
    yield
    bb = jnp.dot(ym_ref[...], wb_ref[...], preferred_element_type=F32)
    yield
    merged = []
    for rows in parts:
        merged.append((jax.nn.sigmoid(ga_ref[rows, :].astype(F32)) * ba[rows]
                       + jax.nn.sigmoid(gm_ref[rows, :].astype(F32)) * bb[rows]).astype(BF16))
        yield
    mix = jnp.dot(jnp.concatenate(merged, axis=0), wo_ref[...], preferred_element_type=F32)
    yield
    for rows in parts:
        x1 = x_ref[rows, :] + gate_m * _rms(mix[rows], gpm_ref[...])
        x1_ref[rows, :] = x1
        yield
        h2_ref[rows, :] = _modulated_norm(x1, gqf_ref[...], mod_ref, 3)
        yield


MERGE_ROW_PARTS = 4
MERGE_STAGES = 3 + 3 * MERGE_ROW_PARTS


def _ffn_thread(x1_ref, h2_ref, mod_ref, gpf_ref, wg_ref, wu_ref, wd_ref, o_ref, act_s, after_up):
    gate_f = mod_ref[0, 5:6, :]
    for c in range(D_FF // FF_CHUNK):
        cs = slice(c * FF_CHUNK, (c + 1) * FF_CHUNK)
        gu = jnp.dot(h2_ref[...], jnp.concatenate([wg_ref[:, cs], wu_ref[:, cs]], axis=1),
                     preferred_element_type=F32)
        g = gu[:, :FF_CHUNK]
        act_s[:, c * FF_CHUNK:(c + 1) * FF_CHUNK] = ((g * jax.nn.sigmoid(g)) * gu[:, FF_CHUNK:]).astype(BF16)
        yield
    after_up()
    ff = []
    for c in range(FFN_DOWN_SPLIT):
        cs = slice(c * D_MODEL // FFN_DOWN_SPLIT, (c + 1) * D_MODEL // FFN_DOWN_SPLIT)
        ff.append(jnp.dot(act_s[...], wd_ref[:, cs], preferred_element_type=F32))
        yield
    o_ref[...] = x1_ref[...] + gate_f * _rms(jnp.concatenate(ff, axis=1), gpf_ref[...])
    yield


FFN_DOWN_SPLIT = 2
FFN_UP_STAGES = D_FF // FF_CHUNK
FFN_STAGES = FFN_UP_STAGES + FFN_DOWN_SPLIT + 1


def _interleave(main, n_main, fills):
    done = [0] * len(fills)
    for k in range(n_main):
        next(main)
        for f, (gen, count, first, last) in enumerate(fills):
            span = last - first + 1
            want = 0 if k < first else min(count, ((k - first + 1) * count + span - 1) // span)
            while done[f] < want:
                next(gen)
                done[f] += 1
    assert all(d == f[1] for d, f in zip(done, fills))
    assert next(main, None) is None and all(next(f[0], None) is None for f in fills)


def _back_kernel(n_tiles, tiles_per_batch, sink_ref,
                 qa_ref, kv_ref, kvp_ref, qm_ref, kt_ref, vm_ref, om_ref, if_ref, bif_ref, nw_ref,
                 x_ref, ga_ref, gm_ref, modm_ref, gpm_ref, gqf_ref,
                 modf_ref, gpf_ref,
                 wa_ref, wb_ref, wo_ref, wg_ref, wu_ref, wd_ref, o_ref,
                 ya_s, ym_s, x1_s, h2_cur, h2_next, act_s, cn_ref, mrow_ref, mcol_ref):
    i = pl.program_id(0)
    j = jnp.minimum(i, n_tiles - 1) % tiles_per_batch
    slot = i % 2

    @pl.when(i == 0)
    def _():
        ya_s[...] = jnp.zeros_like(ya_s)
        ym_s[...] = jnp.zeros_like(ym_s)
        x1_s[...] = jnp.zeros_like(x1_s)
        h2_cur[...] = jnp.zeros_like(h2_cur)

    @pl.when(j == 0)
    def _():
        cn_ref[...] = jnp.zeros_like(cn_ref)
        mrow_ref[...] = jnp.zeros_like(mrow_ref)
        mcol_ref[...] = jnp.zeros_like(mcol_ref)

    def hand_over():
        h2_cur[...] = h2_next[...]

    ffn = _ffn_thread(x1_s.at[slot], h2_cur, modf_ref, gpf_ref, wg_ref, wu_ref, wd_ref, o_ref, act_s,
                      hand_over)
    merge = _merge_thread(x_ref, ya_s, ym_s, ga_ref, gm_ref, modm_ref, gpm_ref, gqf_ref,
                          wa_ref, wb_ref, wo_ref, x1_s.at[1 - slot], h2_next)
    attn = _attn_thread(j, sink_ref, qa_ref, kv_ref, kvp_ref, ya_s)
    mlstm = _mlstm_thread(qm_ref, kt_ref, vm_ref, om_ref, if_ref, bif_ref, nw_ref, ym_s,
                          cn_ref, mrow_ref, mcol_ref)
    _interleave(ffn, FFN_STAGES,
                [(merge, MERGE_STAGES, 0, FFN_UP_STAGES - 2),
                 (attn, ATTN_STAGES, 2, FFN_STAGES - 2),
                 (mlstm, MLSTM_STAGES, 2, FFN_STAGES - 2)])


def _back(sinks, qa, kv, qm, kt, vm, om, gates, bif, norm_w, x2, ga, gm, mod3,
          g_post_mix, g_pre_ffn, g_post_ffn, wa, wb, wo, wg, wu, wd, tiles_per_batch):
    t = x2.shape[0]
    tm = TOKEN_TILE
    n = t // tm
    per_tile = tm // ATTN_BLOCK
    clamp = lambda v: jnp.clip(v, 0, n - 1)
    cur = lambda i: (clamp(i), 0)
    cur_t = lambda i: (0, clamp(i))

    def prev_block(i):
        tile = clamp(i)
        first = (tile // tiles_per_batch) * tiles_per_batch * per_tile
        return (jnp.maximum(tile * per_tile - 1, first), 0)

    mid = lambda i: (clamp(i - 1), 0)
    last = lambda i: (clamp(i - 2), 0)
    const = lambda i: (0, 0)
    once = pl.Buffered(1)
    weight_bytes = 2 * (wa.size + wb.size + wo.size + wg.size + wu.size + wd.size)
    mixer_in = tm * (ATTN_Q_WIDTH + 2 * ATTN_KV_WIDTH + 4 * MLSTM_WIDTH) * 2 + tm * LANES * 4
    scratch_bytes = tm * (D_FF + 2 * MLSTM_WIDTH + 2 * D_MODEL) * 2 + 2 * tm * D_MODEL * 4
    est = (weight_bytes + 2 * 2 * tm * D_MODEL * 4 + 2 * tm * 2 * D_MODEL * 2 + 2 * mixer_in
           + scratch_bytes + 6 * tm * D_MODEL * 4)
    return pl.pallas_call(
        functools.partial(_back_kernel, n, tiles_per_batch),
        grid=(n + 2,),
        in_specs=[pl.BlockSpec(memory_space=pltpu.SMEM),
                  pl.BlockSpec((tm, ATTN_Q_WIDTH), cur),
                  pl.BlockSpec((tm, 2 * ATTN_KV_WIDTH), cur),
                  pl.BlockSpec((ATTN_BLOCK, 2 * ATTN_KV_WIDTH), prev_block),
                  pl.BlockSpec((tm, MLSTM_WIDTH), cur),
                  pl.BlockSpec((MLSTM_WIDTH, tm), cur_t),
                  pl.BlockSpec((tm, MLSTM_WIDTH), cur),
                  pl.BlockSpec((tm, MLSTM_WIDTH), cur),
                  pl.BlockSpec((tm, LANES), cur),
                  pl.BlockSpec((1, LANES), const),
                  pl.BlockSpec((1, MLSTM_WIDTH), const),
                  pl.BlockSpec((tm, D_MODEL), mid),
                  pl.BlockSpec((tm, D_MODEL), mid),
                  pl.BlockSpec((tm, D_MODEL), mid),
                  pl.BlockSpec((1, 6, D_MODEL), lambda i: (clamp(i - 1) // tiles_per_batch, 0, 0)),
                  pl.BlockSpec((1, D_MODEL), const),
                  pl.BlockSpec((1, D_MODEL), const),
                  pl.BlockSpec((1, 6, D_MODEL), lambda i: (clamp(i - 2) // tiles_per_batch, 0, 0)),
                  pl.BlockSpec((1, D_MODEL), const),
                  pl.BlockSpec(wa.shape, const, pipeline_mode=once),
                  pl.BlockSpec(wb.shape, const, pipeline_mode=once),
                  pl.BlockSpec(wo.shape, const, pipeline_mode=once),
                  pl.BlockSpec(wg.shape, const, pipeline_mode=once),
                  pl.BlockSpec(wu.shape, const, pipeline_mode=once),
                  pl.BlockSpec(wd.shape, const, pipeline_mode=once)],
        out_specs=pl.BlockSpec((tm, D_MODEL), last),
        out_shape=jax.ShapeDtypeStruct((t, D_MODEL), F32),
        scratch_shapes=[pltpu.VMEM((tm, ATTN_Q_WIDTH), BF16),
                        pltpu.VMEM((tm, MLSTM_WIDTH), BF16),
                        pltpu.VMEM((2, tm, D_MODEL), F32),
                        pltpu.VMEM((tm, D_MODEL), BF16),
                        pltpu.VMEM((tm, D_MODEL), BF16),
                        pltpu.VMEM((tm, D_FF), BF16),
                        pltpu.VMEM((MLSTM_HEADS, MLSTM_HEAD_DIM, 2 * MLSTM_HEAD_DIM), F32),
                        pltpu.VMEM((SUBLANES, LANES), F32),
                        pltpu.VMEM((SUBLANES, LANES), F32)],
        compiler_params=pltpu.CompilerParams(dimension_semantics=("arbitrary",),
                                             vmem_limit_bytes=_vmem_limit(est)),
        name="back",
    )(sinks, qa, kv, kv, qm, kt, vm, om, gates, bif, norm_w, x2, ga, gm, mod3,
      g_post_mix, g_pre_ffn, mod3, g_post_ffn, wa, wb, wo, wg, wu, wd)


def _q_head_order():
    per_kv = N_Q_HEADS // N_KV_HEADS
    return [h for p in range(per_kv) for h in (p, p + per_kv)]


def _layout_w_in(w_in):
    wt = w_in.T
    q_a = wt[:ATTN_Q_WIDTH].reshape(N_Q_HEADS, HEAD_DIM, D_MODEL)[np.array(_q_head_order())]
    n_if = 2 * MLSTM_HEADS
    gates = jnp.pad(wt[C_IF:C_IF + n_if], ((0, LANES - n_if), (0, 0)))
    return jnp.concatenate([q_a.reshape(ATTN_Q_WIDTH, D_MODEL), wt[ATTN_Q_WIDTH:C_IF], gates,
                            wt[C_IF + n_if:]], axis=0).astype(BF16)


def kernel(x, c, positions, w_ada, b_ada, g_pre_mix, g_post_mix, w_in, b_if, conv_w, conv_b,
           attn_sinks, mlstm_norm_w, w_branch_attn, w_branch_mlstm, w_out, g_pre_ffn, g_post_ffn,
           w_ffn_gate, w_ffn_up, w_ffn_down):
    batch, seq, d = x.shape
    depth = w_in.shape[0]
    assert d == D_MODEL and seq % TOKEN_TILE == 0 and (batch * seq) % ROPE_TILE == 0
    assert D_FF % FF_CHUNK == 0
    t = batch * seq
    tiles_per_batch = seq // TOKEN_TILE
    x2 = x.reshape(t, d)

    inv_freq = (ROPE_THETA ** (-2.0 * jnp.arange(HEAD_DIM // 2, dtype=F32) / HEAD_DIM)).reshape(-1, 1)
    c_pad = jnp.pad(c, ((0, SUBLANES - batch % SUBLANES), (0, 0))) if batch % SUBLANES else c
    head_order = np.array(_q_head_order())

    for l in range(depth):
        cos_t, sin_t, mod = _prep(positions.reshape(1, t), inv_freq, c_pad, w_ada[l],
                                  b_ada[l].reshape(1, -1))
        mod3 = mod[:batch].reshape(batch, 6, d)
        qa, kv, qm, kt, vm, om, gates, ga, gm = _inproj(
            x2, mod3, g_pre_mix[l].reshape(1, d), cos_t, sin_t, _layout_w_in(w_in[l]),
            conv_w[l], conv_b[l].reshape(1, -1), tiles_per_batch)
        bif = jnp.pad(b_if[l], (0, LANES - 2 * MLSTM_HEADS)).reshape(1, LANES)
        wa = w_branch_attn[l].reshape(N_Q_HEADS, HEAD_DIM, d)[head_order].reshape(ATTN_Q_WIDTH, d)
        x2 = _back(attn_sinks[l], qa, kv, qm, kt, vm, om, gates, bif, mlstm_norm_w[l].reshape(1, -1),
                   x2, ga, gm, mod3, g_post_mix[l].reshape(1, d), g_pre_ffn[l].reshape(1, d),
                   g_post_ffn[l].reshape(1, d), wa.astype(BF16), w_branch_mlstm[l].astype(BF16),
                   w_out[l].astype(BF16), w_ffn_gate[l].astype(BF16), w_ffn_up[l].astype(BF16),
                   w_ffn_down[l].astype(BF16), tiles_per_batch)
    return x2.reshape(batch, seq, d)
```

```python
import functools

import numpy as np
import jax
import jax.numpy as jnp
from jax import lax
from jax.experimental import pallas as pl
from jax.experimental.pallas import tpu as pltpu

F32 = jnp.float32
BF16 = jnp.bfloat16

D_MODEL = 1024
N_Q_HEADS = 8
N_KV_HEADS = 2
HEAD_DIM = 64
ROPE_THETA = 10000.0
MLSTM_HEADS = 4
MLSTM_HEAD_DIM = 128
CONV_WIDTH = 4
D_FF = 2816
NORM_EPS = 1e-6
ATTN_Q_WIDTH = N_Q_HEADS * HEAD_DIM
ATTN_KV_WIDTH = N_KV_HEADS * HEAD_DIM
MLSTM_WIDTH = MLSTM_HEADS * MLSTM_HEAD_DIM
ATTN_BLOCK = 128

LANES = 128
SUBLANES = 8
MXU_WIDTH = 256
V7X_SCOPED_VMEM_BYTES = 60000 * 1024

TOKEN_TILE = 512
MLSTM_CHUNK = 128
FF_CHUNK = LANES
ROPE_TILE = 2048

C_QA = 0
C_KV = C_QA + ATTN_Q_WIDTH
C_QK = C_KV + 2 * ATTN_KV_WIDTH
C_VM = C_QK + 2 * MLSTM_WIDTH
C_OM = C_VM + MLSTM_WIDTH
C_IF = C_OM + MLSTM_WIDTH
C_GA = C_IF + LANES
C_GM = C_GA + D_MODEL
IN_COLS = C_GM + D_MODEL


def _vmem_limit(estimate_bytes):
    return int(min(V7X_SCOPED_VMEM_BYTES, max(estimate_bytes, 16 * 1024 * 1024)))


def _rms(x, g):
    return (x * lax.rsqrt(jnp.mean(x * x, axis=-1, keepdims=True) + NORM_EPS)) * g


def _prep_kernel(pos_ref, freq_ref, c_ref, w_ref, b_ref, cos_ref, sin_ref, mod_ref):
    mod_ref[...] = jnp.dot(c_ref[...], w_ref[...], preferred_element_type=F32,
                           precision=lax.Precision.HIGHEST) + b_ref[...]
    ang = freq_ref[...] * pos_ref[...].astype(F32)
    c = jnp.cos(ang)
    s = jnp.sin(ang)
    cos_ref[...] = jnp.concatenate([c, c, c, c], axis=0).T
    sin_ref[...] = jnp.concatenate([-s, s, -s, s], axis=0).T


def _prep(pos_row, inv_freq, c_pad, w_ada, b_ada):
    t = pos_row.shape[1]
    half = inv_freq.shape[0]
    rows = c_pad.shape[0]
    n = w_ada.shape[1]
    steps = t // ROPE_TILE
    cols = n // steps
    assert n % steps == 0 and cols % LANES == 0
    return pl.pallas_call(
        _prep_kernel,
        grid=(steps,),
        in_specs=[pl.BlockSpec((1, ROPE_TILE), lambda i: (0, i)),
                  pl.BlockSpec((half, 1), lambda i: (0, 0)),
                  pl.BlockSpec((rows, D_MODEL), lambda i: (0, 0)),
                  pl.BlockSpec((D_MODEL, cols), lambda i: (0, i)),
                  pl.BlockSpec((1, cols), lambda i: (0, i))],
        out_specs=[pl.BlockSpec((ROPE_TILE, LANES), lambda i: (i, 0)),
                   pl.BlockSpec((ROPE_TILE, LANES), lambda i: (i, 0)),
                   pl.BlockSpec((rows, cols), lambda i: (0, i))],
        out_shape=[jax.ShapeDtypeStruct((t, LANES), F32),
                   jax.ShapeDtypeStruct((t, LANES), F32),
                   jax.ShapeDtypeStruct((rows, n), F32)],
        name="prep",
    )(pos_row, inv_freq, c_pad, w_ada, b_ada)


def _modulated_norm(x, g, mod_ref, row):
    return (_rms(x, g) * (1.0 + mod_ref[0, row + 1:row + 2, :]) + mod_ref[0, row:row + 1, :]).astype(BF16)


def _inproj_kernel(n_tiles, tiles_per_batch, x0_ref, mod0_ref, xn_ref, modn_ref, g_ref, cos_ref, sin_ref,
                   wq_ref, wmid_ref, wtail_ref, cw_ref, cb_ref,
                   qa_ref, kv_ref, qm_ref, kt_ref, vm_ref, om_ref, if_ref, ga_ref, gm_ref,
                   s_ref, raw_s, rot_s, wtail_s):
    tm = TOKEN_TILE
    i = pl.program_id(0)
    par = i % 2
    post_tile = jnp.maximum(i - 1, 0)
    h_cur = s_ref.at[par]
    h_next = s_ref.at[1 - par]
    out_a = s_ref.at[2 + par]
    out_b = s_ref.at[4 + par]
    kv_lo = ATTN_Q_WIDTH

    @pl.when(i == 0)
    def _():
        s_ref[0] = _modulated_norm(x0_ref[...], g_ref[...], mod0_ref, 0)
        raw_s[...] = jnp.zeros_like(raw_s)
        rot_s[...] = jnp.zeros_like(rot_s)
        n_if = 2 * MLSTM_HEADS
        wtail_s[0:LANES, :] = jnp.concatenate(
            [wtail_ref[0:n_if, :], jnp.zeros((LANES - n_if, D_MODEL), F32)], axis=0).astype(BF16)
        for r0 in range(0, 2 * D_MODEL, MXU_WIDTH):
            wtail_s[LANES + r0:LANES + r0 + MXU_WIDTH, :] = (
                wtail_ref[n_if + r0:n_if + r0 + MXU_WIDTH, :].astype(BF16))

    @pl.when(post_tile % tiles_per_batch == 0)
    def _():
        raw_s[0:SUBLANES, :] = jnp.zeros((SUBLANES, raw_s.shape[1]), F32)

    def proj(lo, width):
        if lo < C_KV:
            w = wq_ref[lo:lo + width, :]
        elif lo < C_IF:
            w = wmid_ref[lo - C_KV:lo - C_KV + width, :]
        else:
            w = wtail_s[lo - C_IF:lo - C_IF + width, :]
        return lax.dot_general(h_cur[...], w, (((1,), (1,)), ((), ())), preferred_element_type=F32)

    lane = lax.broadcasted_iota(jnp.int32, (1, LANES), 1)
    first_half = (lane % HEAD_DIM) < (HEAD_DIM // 2)

    def rope(t):
        swapped = jnp.where(first_half, pltpu.roll(t, LANES - HEAD_DIM // 2, axis=1),
                            pltpu.roll(t, HEAD_DIM // 2, axis=1))
        return t * cos_ref[...] + swapped * sin_ref[...]

    def post_q():
        for p in range(ATTN_Q_WIDTH // LANES):
            sl = slice(p * LANES, (p + 1) * LANES)
            out_a[:, sl] = (rope(rot_s[:, sl]) * (HEAD_DIM ** -0.5)).astype(BF16)

    def post_kv():
        out_a[:, kv_lo:kv_lo + LANES] = rope(rot_s[:, kv_lo:kv_lo + LANES]).astype(BF16)
        out_a[:, kv_lo + LANES:kv_lo + 2 * LANES] = rot_s[:, kv_lo + LANES:].astype(BF16)

    blk = MXU_WIDTH

    def post_conv(cbk):
        cs = slice(cbk * blk, (cbk + 1) * blk)
        acc = cb_ref[:, cs]
        for sh in range(CONV_WIDTH):
            acc = acc + (cw_ref[CONV_WIDTH - 1 - sh:CONV_WIDTH - sh, cs]
                         * raw_s[SUBLANES - sh:SUBLANES - sh + tm, cs])
        raw_s[0:SUBLANES, cs] = raw_s[tm:tm + SUBLANES, cs]
        a = acc * jax.nn.sigmoid(acc)
        if cbk * blk < MLSTM_WIDTH:
            out_b[:, cs] = a.astype(BF16)
        else:
            a = a * (MLSTM_HEAD_DIM ** -0.5)
            for d0 in range(0, blk, LANES):
                r0 = cbk * blk - MLSTM_WIDTH + d0
                for t0 in range(0, tm, LANES):
                    out_b[r0:r0 + LANES, MLSTM_WIDTH + t0:MLSTM_WIDTH + t0 + LANES] = (
                        a[t0:t0 + LANES, d0:d0 + LANES].T.astype(BF16))

    def ship_q():
        qa_ref[...] = out_a[:, :ATTN_Q_WIDTH]

    def ship_kv():
        kv_ref[...] = out_a[:, kv_lo:kv_lo + 2 * ATTN_KV_WIDTH]

    def ship_conv(cbk):
        cs = slice(cbk * blk, (cbk + 1) * blk)
        if cbk * blk < MLSTM_WIDTH:
            qm_ref[:, cs] = out_b[:, cs]
        else:
            rs = slice(cbk * blk - MLSTM_WIDTH, (cbk + 1) * blk - MLSTM_WIDTH)
            kt_ref[rs, :] = out_b[rs, MLSTM_WIDTH:MLSTM_WIDTH + tm]

    def norm_next(part):
        rows = slice(part * tm // INPROJ_NORM_PARTS, (part + 1) * tm // INPROJ_NORM_PARTS)
        h_next[rows, :] = _modulated_norm(xn_ref[rows, :], g_ref[...], modn_ref, 0)

    def main_rot(lo_w, lo_s, width):
        rot_s[:, lo_s:lo_s + width] = proj(lo_w, width)

    def main_conv(cbk):
        raw_s[SUBLANES:, cbk * blk:(cbk + 1) * blk] = proj(C_QK + cbk * blk, blk)

    def main_direct(out_ref, lo_w, lo, width):
        out_ref[:, lo:lo + width] = proj(lo_w + lo, width).astype(out_ref.dtype)

    half = D_MODEL // 2
    main_direct(vm_ref, C_VM, 0, MLSTM_WIDTH)
    post_q()
    main_rot(C_QA, 0, ATTN_Q_WIDTH)
    ship_q()
    post_kv()
    post_conv(0)
    main_rot(C_KV, kv_lo, 2 * ATTN_KV_WIDTH)
    ship_kv()
    main_conv(0)
    ship_conv(0)
    post_conv(1)
    main_direct(om_ref, C_OM, 0, MLSTM_WIDTH)
    main_conv(1)
    ship_conv(1)
    post_conv(2)
    main_direct(if_ref, C_IF, 0, LANES)
    main_conv(2)
    ship_conv(2)
    post_conv(3)
    main_direct(ga_ref, C_GA, 0, half)
    main_conv(3)
    ship_conv(3)
    norm_next(0)
    main_direct(ga_ref, C_GA, half, half)
    norm_next(1)
    main_direct(gm_ref, C_GM, 0, half)
    norm_next(2)
    norm_next(3)
    main_direct(gm_ref, C_GM, half, half)


INPROJ_NORM_PARTS = 4


def _inproj(x2, mod3, g_pre, cos_t, sin_t, w_q, w_mid, w_t, conv_w, conv_b, tiles_per_batch):
    t = x2.shape[0]
    tm = TOKEN_TILE
    n = t // tm
    main = lambda i: (jnp.minimum(i, n - 1), 0)
    post = lambda i: (jnp.maximum(i - 1, 0), 0)
    nxt = lambda i: (jnp.minimum(i + 1, n - 1), 0)
    const = lambda i: (0, 0)
    once = pl.Buffered(1)
    outs = [(ATTN_Q_WIDTH, BF16, post), (2 * ATTN_KV_WIDTH, BF16, post), (MLSTM_WIDTH, BF16, post),
            (None, BF16, None), (MLSTM_WIDTH, BF16, main), (MLSTM_WIDTH, BF16, main),
            (LANES, F32, main), (D_MODEL, BF16, main), (D_MODEL, BF16, main)]
    out_specs = [pl.BlockSpec((MLSTM_WIDTH, tm), lambda i: (0, jnp.maximum(i - 1, 0))) if w is None
                 else pl.BlockSpec((tm, w), m) for w, _, m in outs]
    out_shape = [jax.ShapeDtypeStruct((MLSTM_WIDTH, t) if w is None else (t, w), d) for w, d, _ in outs]
    out_bytes = sum((w or MLSTM_WIDTH) * np.dtype(d).itemsize for w, d, _ in outs) * tm
    rot_cols = ATTN_Q_WIDTH + 2 * ATTN_KV_WIDTH
    assert tm == MLSTM_WIDTH
    scratch_bytes = (6 * tm * D_MODEL * 2 + (tm + SUBLANES) * 2 * MLSTM_WIDTH * 4 + tm * rot_cols * 4)
    tail_rows = w_t.shape[0] - C_IF
    tail_aligned = LANES + 2 * D_MODEL
    assert tail_rows == 2 * MLSTM_HEADS + 2 * D_MODEL
    weight_bytes = (w_q.size + w_mid.size + tail_aligned * D_MODEL) * 2 + tail_rows * D_MODEL * 4
    est = (weight_bytes + 3 * tm * D_MODEL * 4 + 2 * out_bytes + 4 * tm * LANES * 4
           + scratch_bytes + 4 * tm * D_MODEL * 4)
    return pl.pallas_call(
        functools.partial(_inproj_kernel, n, tiles_per_batch),
        grid=(n + 1,),
        in_specs=[pl.BlockSpec((tm, D_MODEL), const, pipeline_mode=once),
                  pl.BlockSpec((1, 6, D_MODEL), lambda i: (0, 0, 0)),
                  pl.BlockSpec((tm, D_MODEL), nxt),
                  pl.BlockSpec((1, 6, D_MODEL),
                               lambda i: (jnp.minimum(i + 1, n - 1) // tiles_per_batch, 0, 0)),
                  pl.BlockSpec((1, D_MODEL), const),
                  pl.BlockSpec((tm, LANES), post),
                  pl.BlockSpec((tm, LANES), post),
                  pl.BlockSpec(w_q.shape, const, pipeline_mode=once),
                  pl.BlockSpec(w_mid.shape, const, pipeline_mode=once),
                  pl.BlockSpec((pl.Element(tail_rows), pl.Element(D_MODEL)), lambda i: (C_IF, 0),
                               pipeline_mode=once),
                  pl.BlockSpec((CONV_WIDTH, 2 * MLSTM_WIDTH), const),
                  pl.BlockSpec((1, 2 * MLSTM_WIDTH), const)],
        out_specs=out_specs,
        out_shape=out_shape,
        scratch_shapes=[pltpu.VMEM((6, tm, D_MODEL), BF16),
                        pltpu.VMEM((tm + SUBLANES, 2 * MLSTM_WIDTH), F32),
                        pltpu.VMEM((tm, rot_cols), F32),
                        pltpu.VMEM((tail_aligned, D_MODEL), BF16)],
        compiler_params=pltpu.CompilerParams(dimension_semantics=("arbitrary",),
                                             vmem_limit_bytes=_vmem_limit(est)),
        name="inproj",
    )(x2, mod3, x2, mod3, g_pre, cos_t, sin_t, w_q, w_mid, w_t, conv_w, conv_b)


def _attn_thread(j, sink_ref, q_ref, kv_ref, kvp_ref, o_ref):
    blk = ATTN_BLOCK
    nblk = TOKEN_TILE // blk
    ngrp = ATTN_Q_WIDTH // LANES
    lane = lax.broadcasted_iota(jnp.int32, (1, LANES), 1)
    low = lane < HEAD_DIM
    qi = lax.broadcasted_iota(jnp.int32, (blk, 2 * blk), 0)
    kj = lax.broadcasted_iota(jnp.int32, (blk, 2 * blk), 1)
    in_cur = (kj >= blk) & (kj - blk <= qi)
    in_prev = (kj < blk) & (kj > qi)
    in_prev_first = (kj < blk) & (kj > qi + jnp.where(j > 0, 0, blk))
    ones = jnp.ones((2 * blk, LANES), F32)
    zeros = jnp.zeros((2 * blk, LANES), F32)
    one_lo = jnp.where(low, ones, zeros).astype(BF16)
    one_hi = jnp.where(low, zeros, ones).astype(BF16)
    band = {}
    live = {}

    def prep(i):
        cur = kv_ref[i * blk:(i + 1) * blk, :]
        prev = kvp_ref[...] if i == 0 else kv_ref[(i - 1) * blk:i * blk, :]
        kv = jnp.concatenate([prev, cur], axis=0).astype(F32)
        k = kv[:, :LANES]
        v = kv[:, LANES:]
        k_cat = jnp.concatenate([jnp.where(low, k, 0.0), jnp.where(low, 0.0, k)], axis=0).astype(BF16)
        v_cat = jnp.concatenate(
            [jnp.concatenate([jnp.where(low, v, 0.0).astype(BF16), one_lo], axis=1),
             jnp.concatenate([jnp.where(low, 0.0, v).astype(BF16), one_hi], axis=1)], axis=0)
        band[i] = (k_cat, v_cat, in_cur | (in_prev_first if i == 0 else in_prev))

    def scores(u):
        i, p = divmod(u, ngrp)
        q = q_ref[i * blk:(i + 1) * blk, p * LANES:(p + 1) * LANES]
        live[u] = lax.dot_general(q, band[i][0], (((1,), (1,)), ((), ())), preferred_element_type=F32)

    def softmax(u):
        i, p = divmod(u, ngrp)
        s = live[u]
        mask = band[i][2]
        s0 = jnp.where(mask, s[:, :2 * blk], -jnp.inf)
        s1 = jnp.where(mask, s[:, 2 * blk:], -jnp.inf)
        m0 = jnp.maximum(jnp.max(s0, axis=-1, keepdims=True), sink_ref[p])
        m1 = jnp.maximum(jnp.max(s1, axis=-1, keepdims=True), sink_ref[p + N_Q_HEADS // N_KV_HEADS])
        pr = jnp.concatenate([jnp.exp(s0 - m0), jnp.exp(s1 - m1)], axis=1).astype(BF16)
        live[u] = (pr, m0, m1)

    def values(u):
        i, p = divmod(u, ngrp)
        pr, m0, m1 = live.pop(u)
        r = jnp.dot(pr, band[i][1], preferred_element_type=F32)
        den = r[:, LANES:] + jnp.where(low, jnp.exp(sink_ref[p] - m0),
                                       jnp.exp(sink_ref[p + N_Q_HEADS // N_KV_HEADS] - m1))
        o_ref[i * blk:(i + 1) * blk, p * LANES:(p + 1) * LANES] = (r[:, :LANES] / den).astype(BF16)

    n = nblk * ngrp
    prep(0)
    for k in range(n + 2):
        if k < n:
            scores(k)
            if k + 1 < n and (k + 1) % ngrp == 0:
                prep((k + 1) // ngrp)
        if 0 <= k - 2 < n:
            values(k - 2)
        if 0 <= k - 1 < n:
            softmax(k - 1)
        yield


ATTN_STAGES = (TOKEN_TILE // ATTN_BLOCK) * (ATTN_Q_WIDTH // LANES) + 2


def _mlstm_thread(q_ref, kt_ref, v_ref, om_ref, if_ref, bif_ref, nw_ref, y_ref,
                  cn_ref, mrow_ref, mcol_ref):
    ts = TOKEN_TILE
    L = MLSTM_CHUNK
    D = MLSTM_HEAD_DIM
    nch = ts // L

    gates = if_ref[...] + bif_ref[...]
    logf_all = jax.nn.log_sigmoid(pltpu.roll(gates, LANES - MLSTM_HEADS, axis=1))

    ti = lax.broadcasted_iota(jnp.int32, (L, L), 0)
    si = lax.broadcasted_iota(jnp.int32, (L, L), 1)
    causal = si <= ti
    tri = causal.astype(BF16)
    ones_v = jnp.ones((L, LANES), BF16)
    chunk = {}
    head = {}

    def rows_of(c):
        return slice(c * L, (c + 1) * L)

    def p1(c):
        lf = logf_all[rows_of(c)]
        lf_hi = lf.astype(BF16)
        rem = lf - lf_hi.astype(F32)
        lf_mid = rem.astype(BF16)
        lf_lo = (rem - lf_mid.astype(F32)).astype(BF16)
        chunk[c] = (jnp.dot(tri, lf_hi, preferred_element_type=F32)
                    + jnp.dot(tri, lf_mid, preferred_element_type=F32)
                    + jnp.dot(tri, lf_lo, preferred_element_type=F32))

    def p2(c):
        b = chunk[c]
        r = b - gates[rows_of(c)]
        pm = -r
        sh = 1
        while sh < L:
            pm = jnp.maximum(pm, jnp.where(ti >= sh, pltpu.roll(pm, sh, axis=0), -jnp.inf))
            sh *= 2
        m_row = mrow_ref[0:1, :]
        inter = b + m_row
        mt = jnp.maximum(inter, b + pm)
        b_last = b[L - 1:L, :]
        m_new_row = jnp.maximum(b_last + m_row, jnp.max(b_last - r, axis=0, keepdims=True))
        mrow_ref[...] = jnp.broadcast_to(m_new_row, mrow_ref.shape)
        b_t = b.T[:SUBLANES]
        r_t = r.T[:SUBLANES]
        m_col = mcol_ref[:, 0:1]
        bl_col = b_t[:, L - 1:L]
        a_t = bl_col - r_t
        m_new_col = jnp.maximum(bl_col + m_col, jnp.max(a_t, axis=1, keepdims=True))
        mcol_ref[...] = jnp.broadcast_to(m_new_col, mcol_ref.shape)
        chunk[c] = dict(e_col=b - mt, r_t=r_t, w_inter=jnp.exp(inter - mt), e_mt=jnp.exp(-mt),
                        decay_row=jnp.exp(b_last + m_row - m_new_row),
                        wk_t=jnp.exp(a_t - m_new_col))

    def ha(c, h):
        hs = slice(h * D, (h + 1) * D)
        qh = q_ref[rows_of(c), hs]
        kt = kt_ref[hs, rows_of(c)]
        head[c, h] = dict(qh=qh, kt=kt, s=jnp.dot(qh, kt, preferred_element_type=F32))

    def hb(c, h):
        st, ch = head[c, h], chunk[c]
        w_intra = jnp.where(causal, jnp.exp(ch["e_col"][:, h:h + 1] - ch["r_t"][h:h + 1, :]), 0.0)
        pr = (st.pop("s") * w_intra).astype(BF16)
        qw = (st.pop("qh").astype(F32) * ch["w_inter"][:, h:h + 1]).astype(BF16)
        st["lhs"] = jnp.concatenate([pr, qw], axis=1)
        st["kw"] = (st.pop("kt").astype(F32) * ch["wk_t"][h:h + 1, :]).astype(BF16)

    def hc(c, h):
        st = head[c, h]
        hs = slice(h * D, (h + 1) * D)
        v_ext = jnp.concatenate([v_ref[rows_of(c), hs], ones_v], axis=1)
        rhs = jnp.concatenate([v_ext, cn_ref[h].astype(BF16)], axis=0)
        st["nd"] = jnp.dot(st.pop("lhs"), rhs, preferred_element_type=F32)
        st["kv"] = jnp.dot(st.pop("kw"), v_ext, preferred_element_type=F32)

    def hd(c, h):
        st, ch = head.pop((c, h)), chunk[c]
        hs = slice(h * D, (h + 1) * D)
        nd = st["nd"]
        hh = nd[:, :D] / jnp.maximum(jnp.abs(nd[:, D:]), ch["e_mt"][:, h:h + 1])
        cn_ref[h] = ch["decay_row"][:, h:h + 1] * cn_ref[h] + st["kv"]
        mu = jnp.mean(hh, axis=-1, keepdims=True)
        xc = hh - mu
        var = jnp.mean(xc * xc, axis=-1, keepdims=True)
        yn = (xc * lax.rsqrt(var + NORM_EPS)) * nw_ref[:, hs]
        y_ref[rows_of(c), hs] = (jax.nn.sigmoid(om_ref[rows_of(c), hs].astype(F32)) * yn).astype(BF16)

    half = MLSTM_HEADS // 2
    for g in range(MLSTM_STAGES):
        for c in range(nch):
            o = g - MLSTM_CHUNK_SKEW * c
            if o == 0:
                p1(c)
            elif o == 1:
                for h in range(MLSTM_HEADS):
                    ha(c, h)
                p2(c)
            elif o == 2:
                for h in range(half):
                    hb(c, h)
            elif o == 3:
                for h in range(half):
                    hc(c, h)
                for h in range(half, MLSTM_HEADS):
                    hb(c, h)
            elif o == 4:
                for h in range(half, MLSTM_HEADS):
                    hc(c, h)
                for h in range(half):
                    hd(c, h)
            elif o == 5:
                for h in range(half, MLSTM_HEADS):
                    hd(c, h)
        yield


MLSTM_CHUNK_SKEW = 4
MLSTM_STAGES = MLSTM_CHUNK_SKEW * (TOKEN_TILE // MLSTM_CHUNK - 1) + 6


def _merge_thread(x_ref, ya_ref, ym_ref, ga_ref, gm_ref, mod_ref, gpm_ref, gqf_ref,
                  wa_ref, wb_ref, wo_ref, x1_ref, h2_ref):
    tm = TOKEN_TILE
    parts = [slice(r * tm // MERGE_ROW_PARTS, (r + 1) * tm // MERGE_ROW_PARTS)
             for r in range(MERGE_ROW_PARTS)]
    gate_m = mod_ref[0, 2:3, :]
    ba = jnp.dot(ya_ref[...], wa_ref[...], preferred_element_type=F32)
    yield
    bb = jnp.dot(ym_ref[...], wb_ref[...], preferred_element_type=F32)
    yield
    merged = []
    for rows in parts:
        merged.append((jax.nn.sigmoid(ga_ref[rows, :].astype(F32)) * ba[rows]
                       + jax.nn.sigmoid(gm_ref[rows, :].astype(F32)) * bb[rows]).astype(BF16))
        yield
    mix = jnp.dot(jnp.concatenate(merged, axis=0), wo_ref[...], preferred_element_type=F32)
    yield
    for rows in parts:
        x1 = x_ref[rows, :] + gate_m * _rms(mix[rows], gpm_ref[...])
        x1_ref[rows, :] = x1
        yield
        h2_ref[rows, :] = _modulated_norm(x1, gqf_ref[...], mod_ref, 3)
        yield


MERGE_ROW_PARTS = 4
MERGE_STAGES = 3 + 3 * MERGE_ROW_PARTS


def _ffn_thread(x1_ref, h2_ref, mod_ref, gpf_ref, wg_ref, wu_ref, wd_ref, o_ref, act_s, after_up):
    gate_f = mod_ref[0, 5:6, :]
    for c in range(D_FF // FF_CHUNK):
        cs = slice(c * FF_CHUNK, (c + 1) * FF_CHUNK)
        gu = jnp.dot(h2_ref[...], jnp.concatenate([wg_ref[:, cs], wu_ref[:, cs]], axis=1),
                     preferred_element_type=F32)
        g = gu[:, :FF_CHUNK]
        act_s[:, c * FF_CHUNK:(c + 1) * FF_CHUNK] = ((g * jax.nn.sigmoid(g)) * gu[:, FF_CHUNK:]).astype(BF16)
        yield
    after_up()
    ff = []
    for c in range(FFN_DOWN_SPLIT):
        cs = slice(c * D_MODEL // FFN_DOWN_SPLIT, (c + 1) * D_MODEL // FFN_DOWN_SPLIT)
        ff.append(jnp.dot(act_s[...], wd_ref[:, cs], preferred_element_type=F32))
        yield
    o_ref[...] = x1_ref[...] + gate_f * _rms(jnp.concatenate(ff, axis=1), gpf_ref[...])
    yield


FFN_DOWN_SPLIT = 2
FFN_UP_STAGES = D_FF // FF_CHUNK
FFN_STAGES = FFN_UP_STAGES + FFN_DOWN_SPLIT + 1


def _interleave(main, n_main, fills):
    done = [0] * len(fills)
    for k in range(n_main):
        next(main)
        for f, (gen, count, first, last) in enumerate(fills):
            span = last - first + 1
            want = 0 if k < first else min(count, ((k - first + 1) * count + span - 1) // span)
            while done[f] < want:
                next(gen)
                done[f] += 1
    assert all(d == f[1] for d, f in zip(done, fills))
    assert next(main, None) is None and all(next(f[0], None) is None for f in fills)


def _back_kernel(n_tiles, tiles_per_batch, sink_ref,
                 qa_ref, kv_ref, kvp_ref, qm_ref, kt_ref, vm_ref, om_ref, if_ref, bif_ref, nw_ref,
                 x_ref, ga_ref, gm_ref, modm_ref, gpm_ref, gqf_ref,
                 modf_ref, gpf_ref,
                 wa_ref, wb_ref, wo_ref, wg_ref, wu_ref, wd_ref, o_ref,
                 ya_s, ym_s, x1_s, h2_cur, h2_next, act_s, cn_ref, mrow_ref, mcol_ref):
    i = pl.program_id(0)
    j = jnp.minimum(i, n_tiles - 1) % tiles_per_batch
    slot = i % 2

    @pl.when(i == 0)
    def _():
        ya_s[...] = jnp.zeros_like(ya_s)
        ym_s[...] = jnp.zeros_like(ym_s)
        x1_s[...] = jnp.zeros_like(x1_s)
        h2_cur[...] = jnp.zeros_like(h2_cur)

    @pl.when(j == 0)
    def _():
        cn_ref[...] = jnp.zeros_like(cn_ref)
        mrow_ref[...] = jnp.zeros_like(mrow_ref)
        mcol_ref[...] = jnp.zeros_like(mcol_ref)

    def hand_over():
        h2_cur[...] = h2_next[...]

    ffn = _ffn_thread(x1_s.at[slot], h2_cur, modf_ref, gpf_ref, wg_ref, wu_ref, wd_ref, o_ref, act_s,
                      hand_over)
    merge = _merge_thread(x_ref, ya_s, ym_s, ga_ref, gm_ref, modm_ref, gpm_ref, gqf_ref,
                          wa_ref, wb_ref, wo_ref, x1_s.at[1 - slot], h2_next)
    attn = _attn_thread(j, sink_ref, qa_ref, kv_ref, kvp_ref, ya_s)
    mlstm = _mlstm_thread(qm_ref, kt_ref, vm_ref, om_ref, if_ref, bif_ref, nw_ref, ym_s,
                          cn_ref, mrow_ref, mcol_ref)
    _interleave(ffn, FFN_STAGES,
                [(merge, MERGE_STAGES, 0, FFN_UP_STAGES - 2),
                 (attn, ATTN_STAGES, 2, FFN_STAGES - 2),
                 (mlstm, MLSTM_STAGES, 2, FFN_STAGES - 2)])


def _back(sinks, qa, kv, qm, kt, vm, om, gates, bif, norm_w, x2, ga, gm, mod3,
          g_post_mix, g_pre_ffn, g_post_ffn, wa, wb, wo, wg, wu, wd, tiles_per_batch):
    t = x2.shape[0]
    tm = TOKEN_TILE
    n = t // tm
    per_tile = tm // ATTN_BLOCK
    clamp = lambda v: jnp.clip(v, 0, n - 1)
    cur = lambda i: (clamp(i), 0)
    cur_t = lambda i: (0, clamp(i))

    def prev_block(i):
        tile = clamp(i)
        first = (tile // tiles_per_batch) * tiles_per_batch * per_tile
        return (jnp.maximum(tile * per_tile - 1, first), 0)

    mid = lambda i: (clamp(i - 1), 0)
    last = lambda i: (clamp(i - 2), 0)
    const = lambda i: (0, 0)
    once = pl.Buffered(1)
    weight_bytes = 2 * (wa.size + wb.size + wo.size + wg.size + wu.size + wd.size)
    mixer_in = tm * (ATTN_Q_WIDTH + 2 * ATTN_KV_WIDTH + 4 * MLSTM_WIDTH) * 2 + tm * LANES * 4
    scratch_bytes = tm * (D_FF + 2 * MLSTM_WIDTH + 2 * D_MODEL) * 2 + 2 * tm * D_MODEL * 4
    est = (weight_bytes + 2 * 2 * tm * D_MODEL * 4 + 2 * tm * 2 * D_MODEL * 2 + 2 * mixer_in
           + scratch_bytes + 6 * tm * D_MODEL * 4)
    return pl.pallas_call(
        functools.partial(_back_kernel, n, tiles_per_batch),
        grid=(n + 2,),
        in_specs=[pl.BlockSpec(memory_space=pltpu.SMEM),
                  pl.BlockSpec((tm, ATTN_Q_WIDTH), cur),
                  pl.BlockSpec((tm, 2 * ATTN_KV_WIDTH), cur),
                  pl.BlockSpec((ATTN_BLOCK, 2 * ATTN_KV_WIDTH), prev_block),
                  pl.BlockSpec((tm, MLSTM_WIDTH), cur),
                  pl.BlockSpec((MLSTM_WIDTH, tm), cur_t),
                  pl.BlockSpec((tm, MLSTM_WIDTH), cur),
                  pl.BlockSpec((tm, MLSTM_WIDTH), cur),
                  pl.BlockSpec((tm, LANES), cur),
                  pl.BlockSpec((1, LANES), const),
                  pl.BlockSpec((1, MLSTM_WIDTH), const),
                  pl.BlockSpec((tm, D_MODEL), mid),
                  pl.BlockSpec((tm, D_MODEL), mid),
                  pl.BlockSpec((tm, D_MODEL), mid),
                  pl.BlockSpec((1, 6, D_MODEL), lambda i: (clamp(i - 1) // tiles_per_batch, 0, 0)),
                  pl.BlockSpec((1, D_MODEL), const),
                  pl.BlockSpec((1, D_MODEL), const),
                  pl.BlockSpec((1, 6, D_MODEL), lambda i: (clamp(i - 2) // tiles_per_batch, 0, 0)),
                  pl.BlockSpec((1, D_MODEL), const),
                  pl.BlockSpec(wa.shape, const, pipeline_mode=once),
                  pl.BlockSpec(wb.shape, const, pipeline_mode=once),
                  pl.BlockSpec(wo.shape, const, pipeline_mode=once),
                  pl.BlockSpec(wg.shape, const, pipeline_mode=once),
                  pl.BlockSpec(wu.shape, const, pipeline_mode=once),
                  pl.BlockSpec(wd.shape, const, pipeline_mode=once)],
        out_specs=pl.BlockSpec((tm, D_MODEL), last),
        out_shape=jax.ShapeDtypeStruct((t, D_MODEL), F32),
        scratch_shapes=[pltpu.VMEM((tm, ATTN_Q_WIDTH), BF16),
                        pltpu.VMEM((tm, MLSTM_WIDTH), BF16),
                        pltpu.VMEM((2, tm, D_MODEL), F32),
                        pltpu.VMEM((tm, D_MODEL), BF16),
                        pltpu.VMEM((tm, D_MODEL), BF16),
                        pltpu.VMEM((tm, D_FF), BF16),
                        pltpu.VMEM((MLSTM_HEADS, MLSTM_HEAD_DIM, 2 * MLSTM_HEAD_DIM), F32),
                        pltpu.VMEM((SUBLANES, LANES), F32),
                        pltpu.VMEM((SUBLANES, LANES), F32)],
        compiler_params=pltpu.CompilerParams(dimension_semantics=("arbitrary",),
                                             vmem_limit_bytes=_vmem_limit(est)),
        name="back",
    )(sinks, qa, kv, kv, qm, kt, vm, om, gates, bif, norm_w, x2, ga, gm, mod3,
      g_post_mix, g_pre_ffn, mod3, g_post_ffn, wa, wb, wo, wg, wu, wd)


def _q_head_order():
    per_kv = N_Q_HEADS // N_KV_HEADS
    return [h for p in range(per_kv) for h in (p, p + per_kv)]


def _layout_w_in(w_in):
    wt = w_in.T
    q_a = wt[:ATTN_Q_WIDTH].reshape(N_Q_HEADS, HEAD_DIM, D_MODEL)[np.array(_q_head_order())]
    return q_a.reshape(ATTN_Q_WIDTH, D_MODEL).astype(BF16), wt[ATTN_Q_WIDTH:C_IF].astype(BF16), wt


def kernel(x, c, positions, w_ada, b_ada, g_pre_mix, g_post_mix, w_in, b_if, conv_w, conv_b,
           attn_sinks, mlstm_norm_w, w_branch_attn, w_branch_mlstm, w_out, g_pre_ffn, g_post_ffn,
           w_ffn_gate, w_ffn_up, w_ffn_down):
    batch, seq, d = x.shape
    depth = w_in.shape[0]
    assert d == D_MODEL and seq % TOKEN_TILE == 0 and (batch * seq) % ROPE_TILE == 0
    assert D_FF % FF_CHUNK == 0
    t = batch * seq
    tiles_per_batch = seq // TOKEN_TILE
    x2 = x.reshape(t, d)

    inv_freq = (ROPE_THETA ** (-2.0 * jnp.arange(HEAD_DIM // 2, dtype=F32) / HEAD_DIM)).reshape(-1, 1)
    c_pad = jnp.pad(c, ((0, SUBLANES - batch % SUBLANES), (0, 0))) if batch % SUBLANES else c
    head_order = np.array(_q_head_order())

    for l in range(depth):
        cos_t, sin_t, mod = _prep(positions.reshape(1, t), inv_freq, c_pad, w_ada[l],
                                  b_ada[l].reshape(1, -1))
        mod3 = mod[:batch].reshape(batch, 6, d)
        qa, kv, qm, kt, vm, om, gates, ga, gm = _inproj(
            x2, mod3, g_pre_mix[l].reshape(1, d), cos_t, sin_t, *_layout_w_in(w_in[l]),
            conv_w[l], conv_b[l].reshape(1, -1), tiles_per_batch)
        bif = jnp.pad(b_if[l], (0, LANES - 2 * MLSTM_HEADS)).reshape(1, LANES)
        wa = w_branch_attn[l].reshape(N_Q_HEADS, HEAD_DIM, d)[head_order].reshape(ATTN_Q_WIDTH, d)
        x2 = _back(attn_sinks[l], qa, kv, qm, kt, vm, om, gates, bif, mlstm_norm_w[l].reshape(1, -1),
                   x2, ga, gm, mod3, g_post_mix[l].reshape(1, d), g_pre_ffn[l].reshape(1, d),
                   g_post_ffn[l].reshape(1, d), wa.astype(BF16), w_branch_mlstm[l].astype(BF16),
                   w_out[l].astype(BF16), w_ffn_gate[l].astype(BF16), w_ffn_up[l].astype(BF16),
                   w_ffn_down[l].astype(BF16), tiles_per_batch)
    return x2.reshape(batch, seq, d)
```

```python
import functools

import numpy as np
import jax
import jax.numpy as jnp
from jax import lax
from jax.experimental import pallas as pl
from jax.experimental.pallas import tpu as pltpu

F32 = jnp.float32
BF16 = jnp.bfloat16

D_MODEL = 1024
N_Q_HEADS = 8
N_KV_HEADS = 2
HEAD_DIM = 64
ROPE_THETA = 10000.0
MLSTM_HEADS = 4
MLSTM_HEAD_DIM = 128
CONV_WIDTH = 4
D_FF = 2816
NORM_EPS = 1e-6
ATTN_Q_WIDTH = N_Q_HEADS * HEAD_DIM
ATTN_KV_WIDTH = N_KV_HEADS * HEAD_DIM
MLSTM_WIDTH = MLSTM_HEADS * MLSTM_HEAD_DIM
ATTN_BLOCK = 128

LANES = 128
SUBLANES = 8
MXU_WIDTH = 256
V7X_SCOPED_VMEM_BYTES = 60000 * 1024

TOKEN_TILE = 512
MLSTM_CHUNK = 128
FF_CHUNK = LANES
ROPE_TILE = 2048

C_QA = 0
C_KV = C_QA + ATTN_Q_WIDTH
C_QK = C_KV + 2 * ATTN_KV_WIDTH
C_VM = C_QK + 2 * MLSTM_WIDTH
C_OM = C_VM + MLSTM_WIDTH
C_IF = C_OM + MLSTM_WIDTH
C_GA = C_IF + LANES
C_GM = C_GA + D_MODEL
IN_COLS = C_GM + D_MODEL


def _vmem_limit(estimate_bytes):
    return int(min(V7X_SCOPED_VMEM_BYTES, max(estimate_bytes, 16 * 1024 * 1024)))


def _rms(x, g):
    return (x * lax.rsqrt(jnp.mean(x * x, axis=-1, keepdims=True) + NORM_EPS)) * g


def _prep_kernel(pos_ref, freq_ref, c_ref, w_ref, b_ref, wq_f32, wmid_f32,
                 cos_ref, sin_ref, mod_ref, wq_ref, wmid_ref):
    wq_ref[...] = wq_f32[...].astype(BF16)
    wmid_ref[...] = wmid_f32[...].astype(BF16)
    mod_ref[...] = jnp.dot(c_ref[...], w_ref[...], preferred_element_type=F32,
                           precision=lax.Precision.HIGHEST) + b_ref[...]
    ang = freq_ref[...] * pos_ref[...].astype(F32)
    c = jnp.cos(ang)
    s = jnp.sin(ang)
    cos_ref[...] = jnp.concatenate([c, c, c, c], axis=0).T
    sin_ref[...] = jnp.concatenate([-s, s, -s, s], axis=0).T


def _q_source_block(out_block, blocks_per_head):
    per_kv = N_Q_HEADS // N_KV_HEADS
    head = out_block // blocks_per_head
    src_head = (head % N_KV_HEADS) * per_kv + head // N_KV_HEADS
    return src_head * blocks_per_head + out_block % blocks_per_head


def _prep(pos_row, inv_freq, c_pad, w_ada, b_ada, w_t):
    t = pos_row.shape[1]
    half = inv_freq.shape[0]
    rows = c_pad.shape[0]
    n = w_ada.shape[1]
    steps = t // ROPE_TILE
    cols = n // steps
    q_rows = ATTN_Q_WIDTH // steps
    mid_rows = (C_IF - C_KV) // steps
    assert n % steps == 0 and cols % LANES == 0
    assert ATTN_Q_WIDTH % steps == 0 and HEAD_DIM % q_rows == 0 and q_rows % (2 * SUBLANES) == 0
    assert (C_IF - C_KV) % steps == 0 and mid_rows % (2 * SUBLANES) == 0
    return pl.pallas_call(
        _prep_kernel,
        grid=(steps,),
        in_specs=[pl.BlockSpec((1, ROPE_TILE), lambda i: (0, i)),
                  pl.BlockSpec((half, 1), lambda i: (0, 0)),
                  pl.BlockSpec((rows, D_MODEL), lambda i: (0, 0)),
                  pl.BlockSpec((D_MODEL, cols), lambda i: (0, i)),
                  pl.BlockSpec((1, cols), lambda i: (0, i)),
                  pl.BlockSpec((q_rows, D_MODEL), lambda i: (_q_source_block(i, HEAD_DIM // q_rows), 0)),
                  pl.BlockSpec((pl.Element(mid_rows), pl.Element(D_MODEL)),
                               lambda i: (pl.multiple_of(C_KV + i * mid_rows, 2 * SUBLANES), 0))],
        out_specs=[pl.BlockSpec((ROPE_TILE, LANES), lambda i: (i, 0)),
                   pl.BlockSpec((ROPE_TILE, LANES), lambda i: (i, 0)),
                   pl.BlockSpec((rows, cols), lambda i: (0, i)),
                   pl.BlockSpec((q_rows, D_MODEL), lambda i: (i, 0)),
                   pl.BlockSpec((mid_rows, D_MODEL), lambda i: (i, 0))],
        out_shape=[jax.ShapeDtypeStruct((t, LANES), F32),
                   jax.ShapeDtypeStruct((t, LANES), F32),
                   jax.ShapeDtypeStruct((rows, n), F32),
                   jax.ShapeDtypeStruct((ATTN_Q_WIDTH, D_MODEL), BF16),
                   jax.ShapeDtypeStruct((C_IF - C_KV, D_MODEL), BF16)],
        name="prep",
    )(pos_row, inv_freq, c_pad, w_ada, b_ada, w_t, w_t)


def _modulated_norm(x, g, mod_ref, row):
    return (_rms(x, g) * (1.0 + mod_ref[0, row + 1:row + 2, :]) + mod_ref[0, row:row + 1, :]).astype(BF16)


def _inproj_kernel(n_tiles, tiles_per_batch, x0_ref, mod0_ref, xn_ref, modn_ref, g_ref, cos_ref, sin_ref,
                   wq_ref, wmid_ref, wtail_ref, cw_ref, cb_ref, *rest):
    n_cast = len(BACK_WEIGHT_BLOCKS)
    cast_in, rest = rest[:n_cast], rest[n_cast:]
    (qa_ref, kv_ref, qm_ref, kt_ref, vm_ref, om_ref, if_ref, ga_ref, gm_ref), rest = rest[:9], rest[9:]
    cast_out, (s_ref, raw_s, rot_s, wtail_s) = rest[:n_cast], rest[n_cast:]
    for src, dst in zip(cast_in, cast_out):
        dst[...] = src[...].astype(BF16)
    _inproj_body(n_tiles, tiles_per_batch, x0_ref, mod0_ref, xn_ref, modn_ref, g_ref, cos_ref, sin_ref,
                 wq_ref, wmid_ref, wtail_ref, cw_ref, cb_ref,
                 qa_ref, kv_ref, qm_ref, kt_ref, vm_ref, om_ref, if_ref, ga_ref, gm_ref,
                 s_ref, raw_s, rot_s, wtail_s)


BACK_WEIGHT_BLOCKS = (16, 16, 16, 16, 16, 64)


def _inproj_body(n_tiles, tiles_per_batch, x0_ref, mod0_ref, xn_ref, modn_ref, g_ref, cos_ref, sin_ref,
                 wq_ref, wmid_ref, wtail_ref, cw_ref, cb_ref,
                 qa_ref, kv_ref, qm_ref, kt_ref, vm_ref, om_ref, if_ref, ga_ref, gm_ref,
                 s_ref, raw_s, rot_s, wtail_s):
    tm = TOKEN_TILE
    i = pl.program_id(0)
    par = i % 2
    post_tile = jnp.maximum(i - 1, 0)
    h_cur = s_ref.at[par]
    h_next = s_ref.at[1 - par]
    out_a = s_ref.at[2 + par]
    out_b = s_ref.at[4 + par]
    kv_lo = ATTN_Q_WIDTH

    @pl.when(i == 0)
    def _():
        s_ref[0] = _modulated_norm(x0_ref[...], g_ref[...], mod0_ref, 0)
        raw_s[...] = jnp.zeros_like(raw_s)
        rot_s[...] = jnp.zeros_like(rot_s)
        n_if = 2 * MLSTM_HEADS
        wtail_s[0:LANES, :] = jnp.concatenate(
            [wtail_ref[0:n_if, :], jnp.zeros((LANES - n_if, D_MODEL), F32)], axis=0).astype(BF16)
        for r0 in range(0, 2 * D_MODEL, MXU_WIDTH):
            wtail_s[LANES + r0:LANES + r0 + MXU_WIDTH, :] = (
                wtail_ref[n_if + r0:n_if + r0 + MXU_WIDTH, :].astype(BF16))

    @pl.when(post_tile % tiles_per_batch == 0)
    def _():
        raw_s[0:SUBLANES, :] = jnp.zeros((SUBLANES, raw_s.shape[1]), F32)

    def proj(lo, width):
        if lo < C_KV:
            w = wq_ref[lo:lo + width, :]
        elif lo < C_IF:
            w = wmid_ref[lo - C_KV:lo - C_KV + width, :]
        else:
            w = wtail_s[lo - C_IF:lo - C_IF + width, :]
        return lax.dot_general(h_cur[...], w, (((1,), (1,)), ((), ())), preferred_element_type=F32)

    lane = lax.broadcasted_iota(jnp.int32, (1, LANES), 1)
    first_half = (lane % HEAD_DIM) < (HEAD_DIM // 2)

    def rope(t):
        swapped = jnp.where(first_half, pltpu.roll(t, LANES - HEAD_DIM // 2, axis=1),
                            pltpu.roll(t, HEAD_DIM // 2, axis=1))
        return t * cos_ref[...] + swapped * sin_ref[...]

    def post_q():
        for p in range(ATTN_Q_WIDTH // LANES):
            sl = slice(p * LANES, (p + 1) * LANES)
            out_a[:, sl] = (rope(rot_s[:, sl]) * (HEAD_DIM ** -0.5)).astype(BF16)

    def post_kv():
        out_a[:, kv_lo:kv_lo + LANES] = rope(rot_s[:, kv_lo:kv_lo + LANES]).astype(BF16)
        out_a[:, kv_lo + LANES:kv_lo + 2 * LANES] = rot_s[:, kv_lo + LANES:].astype(BF16)

    blk = MXU_WIDTH

    def post_conv(cbk):
        cs = slice(cbk * blk, (cbk + 1) * blk)
        acc = cb_ref[:, cs]
        for sh in range(CONV_WIDTH):
            acc = acc + (cw_ref[CONV_WIDTH - 1 - sh:CONV_WIDTH - sh, cs]
                         * raw_s[SUBLANES - sh:SUBLANES - sh + tm, cs])
        raw_s[0:SUBLANES, cs] = raw_s[tm:tm + SUBLANES, cs]
        a = acc * jax.nn.sigmoid(acc)
        if cbk * blk < MLSTM_WIDTH:
            out_b[:, cs] = a.astype(BF16)
        else:
            a = a * (MLSTM_HEAD_DIM ** -0.5)
            for d0 in range(0, blk, LANES):
                r0 = cbk * blk - MLSTM_WIDTH + d0
                for t0 in range(0, tm, LANES):
                    out_b[r0:r0 + LANES, MLSTM_WIDTH + t0:MLSTM_WIDTH + t0 + LANES] = (
                        a[t0:t0 + LANES, d0:d0 + LANES].T.astype(BF16))

    def ship_q():
        qa_ref[...] = out_a[:, :ATTN_Q_WIDTH]

    def ship_kv():
        kv_ref[...] = out_a[:, kv_lo:kv_lo + 2 * ATTN_KV_WIDTH]

    def ship_conv(cbk):
        cs = slice(cbk * blk, (cbk + 1) * blk)
        if cbk * blk < MLSTM_WIDTH:
            qm_ref[:, cs] = out_b[:, cs]
        else:
            rs = slice(cbk * blk - MLSTM_WIDTH, (cbk + 1) * blk - MLSTM_WIDTH)
            kt_ref[rs, :] = out_b[rs, MLSTM_WIDTH:MLSTM_WIDTH + tm]

    def norm_next(part):
        rows = slice(part * tm // INPROJ_NORM_PARTS, (part + 1) * tm // INPROJ_NORM_PARTS)
        h_next[rows, :] = _modulated_norm(xn_ref[rows, :], g_ref[...], modn_ref, 0)

    def main_rot(lo_w, lo_s, width):
        rot_s[:, lo_s:lo_s + width] = proj(lo_w, width)

    def main_conv(cbk):
        raw_s[SUBLANES:, cbk * blk:(cbk + 1) * blk] = proj(C_QK + cbk * blk, blk)

    def main_direct(out_ref, lo_w, lo, width):
        out_ref[:, lo:lo + width] = proj(lo_w + lo, width).astype(out_ref.dtype)

    half = D_MODEL // 2
    main_direct(vm_ref, C_VM, 0, MLSTM_WIDTH)
    post_q()
    main_rot(C_QA, 0, ATTN_Q_WIDTH)
    ship_q()
    post_kv()
    post_conv(0)
    main_rot(C_KV, kv_lo, 2 * ATTN_KV_WIDTH)
    ship_kv()
    main_conv(0)
    ship_conv(0)
    post_conv(1)
    main_direct(om_ref, C_OM, 0, MLSTM_WIDTH)
    main_conv(1)
    ship_conv(1)
    post_conv(2)
    main_direct(if_ref, C_IF, 0, LANES)
    main_conv(2)
    ship_conv(2)
    post_conv(3)
    main_direct(ga_ref, C_GA, 0, half)
    main_conv(3)
    ship_conv(3)
    norm_next(0)
    main_direct(ga_ref, C_GA, half, half)
    norm_next(1)
    main_direct(gm_ref, C_GM, 0, half)
    norm_next(2)
    norm_next(3)
    main_direct(gm_ref, C_GM, half, half)


INPROJ_NORM_PARTS = 4


def _inproj(x2, mod3, g_pre, cos_t, sin_t, w_q, w_mid, w_t, conv_w, conv_b, back_weights, tiles_per_batch):
    t = x2.shape[0]
    tm = TOKEN_TILE
    n = t // tm
    main = lambda i: (jnp.minimum(i, n - 1), 0)
    post = lambda i: (jnp.maximum(i - 1, 0), 0)
    nxt = lambda i: (jnp.minimum(i + 1, n - 1), 0)
    const = lambda i: (0, 0)
    once = pl.Buffered(1)
    outs = [(ATTN_Q_WIDTH, BF16, post), (2 * ATTN_KV_WIDTH, BF16, post), (MLSTM_WIDTH, BF16, post),
            (None, BF16, None), (MLSTM_WIDTH, BF16, main), (MLSTM_WIDTH, BF16, main),
            (LANES, F32, main), (D_MODEL, BF16, main), (D_MODEL, BF16, main)]
    out_specs = [pl.BlockSpec((MLSTM_WIDTH, tm), lambda i: (0, jnp.maximum(i - 1, 0))) if w is None
                 else pl.BlockSpec((tm, w), m) for w, _, m in outs]
    out_shape = [jax.ShapeDtypeStruct((MLSTM_WIDTH, t) if w is None else (t, w), d) for w, d, _ in outs]
    out_bytes = sum((w or MLSTM_WIDTH) * np.dtype(d).itemsize for w, d, _ in outs) * tm
    rot_cols = ATTN_Q_WIDTH + 2 * ATTN_KV_WIDTH
    assert tm == MLSTM_WIDTH
    scratch_bytes = (6 * tm * D_MODEL * 2 + (tm + SUBLANES) * 2 * MLSTM_WIDTH * 4 + tm * rot_cols * 4)
    tail_rows = w_t.shape[0] - C_IF
    tail_aligned = LANES + 2 * D_MODEL
    assert tail_rows == 2 * MLSTM_HEADS + 2 * D_MODEL
    weight_bytes = (w_q.size + w_mid.size + tail_aligned * D_MODEL) * 2 + tail_rows * D_MODEL * 4
    cast_in, cast_out, cast_bytes = [], [], 0
    for k, (w, rows) in enumerate(zip(back_weights, BACK_WEIGHT_BLOCKS)):
        blocks = w.shape[0] // rows
        assert w.shape[0] % rows == 0 and blocks <= n and rows % (2 * SUBLANES) == 0
        dst = lambda i, blocks=blocks: (jnp.minimum(i, blocks - 1), 0)
        src = dst if k else (lambda i, blocks=blocks, per_head=HEAD_DIM // rows:
                             (_q_source_block(jnp.minimum(i, blocks - 1), per_head), 0))
        cast_in.append(pl.BlockSpec((rows, w.shape[1]), src))
        cast_out.append(pl.BlockSpec((rows, w.shape[1]), dst))
        out_shape.append(jax.ShapeDtypeStruct(w.shape, BF16))
        cast_bytes += 2 * rows * w.shape[1] * 6
    est = (weight_bytes + 3 * tm * D_MODEL * 4 + 2 * out_bytes + 4 * tm * LANES * 4
           + scratch_bytes + 4 * tm * D_MODEL * 4 + cast_bytes)
    return pl.pallas_call(
        functools.partial(_inproj_kernel, n, tiles_per_batch),
        grid=(n + 1,),
        in_specs=[pl.BlockSpec((tm, D_MODEL), const, pipeline_mode=once),
                  pl.BlockSpec((1, 6, D_MODEL), lambda i: (0, 0, 0)),
                  pl.BlockSpec((tm, D_MODEL), nxt),
                  pl.BlockSpec((1, 6, D_MODEL),
                               lambda i: (jnp.minimum(i + 1, n - 1) // tiles_per_batch, 0, 0)),
                  pl.BlockSpec((1, D_MODEL), const),
                  pl.BlockSpec((tm, LANES), post),
                  pl.BlockSpec((tm, LANES), post),
                  pl.BlockSpec(w_q.shape, const, pipeline_mode=once),
                  pl.BlockSpec(w_mid.shape, const, pipeline_mode=once),
                  pl.BlockSpec((pl.Element(tail_rows), pl.Element(D_MODEL)), lambda i: (C_IF, 0),
                               pipeline_mode=once),
                  pl.BlockSpec((CONV_WIDTH, 2 * MLSTM_WIDTH), const),
                  pl.BlockSpec((1, 2 * MLSTM_WIDTH), const)] + cast_in,
        out_specs=out_specs + cast_out,
        out_shape=out_shape,
        scratch_shapes=[pltpu.VMEM((6, tm, D_MODEL), BF16),
                        pltpu.VMEM((tm + SUBLANES, 2 * MLSTM_WIDTH), F32),
                        pltpu.VMEM((tm, rot_cols), F32),
                        pltpu.VMEM((tail_aligned, D_MODEL), BF16)],
        compiler_params=pltpu.CompilerParams(dimension_semantics=("arbitrary",),
                                             vmem_limit_bytes=_vmem_limit(est)),
        name="inproj",
    )(x2, mod3, x2, mod3, g_pre, cos_t, sin_t, w_q, w_mid, w_t, conv_w, conv_b, *back_weights)


def _attn_thread(j, sink_ref, q_ref, kv_ref, kvp_ref, o_ref):
    blk = ATTN_BLOCK
    nblk = TOKEN_TILE // blk
    ngrp = ATTN_Q_WIDTH // LANES
    lane = lax.broadcasted_iota(jnp.int32, (1, LANES), 1)
    low = lane < HEAD_DIM
    qi = lax.broadcasted_iota(jnp.int32, (blk, 2 * blk), 0)
    kj = lax.broadcasted_iota(jnp.int32, (blk, 2 * blk), 1)
    in_cur = (kj >= blk) & (kj - blk <= qi)
    in_prev = (kj < blk) & (kj > qi)
    in_prev_first = (kj < blk) & (kj > qi + jnp.where(j > 0, 0, blk))
    ones = jnp.ones((2 * blk, LANES), F32)
    zeros = jnp.zeros((2 * blk, LANES), F32)
    one_lo = jnp.where(low, ones, zeros).astype(BF16)
    one_hi = jnp.where(low, zeros, ones).astype(BF16)
    band = {}
    live = {}

    def prep(i):
        cur = kv_ref[i * blk:(i + 1) * blk, :]
        prev = kvp_ref[...] if i == 0 else kv_ref[(i - 1) * blk:i * blk, :]
        kv = jnp.concatenate([prev, cur], axis=0).astype(F32)
        k = kv[:, :LANES]
        v = kv[:, LANES:]
        k_cat = jnp.concatenate([jnp.where(low, k, 0.0), jnp.where(low, 0.0, k)], axis=0).astype(BF16)
        v_cat = jnp.concatenate(
            [jnp.concatenate([jnp.where(low, v, 0.0).astype(BF16), one_lo], axis=1),
             jnp.concatenate([jnp.where(low, 0.0, v).astype(BF16), one_hi], axis=1)], axis=0)
        band[i] = (k_cat, v_cat, in_cur | (in_prev_first if i == 0 else in_prev))

    def scores(u):
        i, p = divmod(u, ngrp)
        q = q_ref[i * blk:(i + 1) * blk, p * LANES:(p + 1) * LANES]
        live[u] = lax.dot_general(q, band[i][0], (((1,), (1,)), ((), ())), preferred_element_type=F32)

    def softmax(u):
        i, p = divmod(u, ngrp)
        s = live[u]
        mask = band[i][2]
        s0 = jnp.where(mask, s[:, :2 * blk], -jnp.inf)
        s1 = jnp.where(mask, s[:, 2 * blk:], -jnp.inf)
        m0 = jnp.maximum(jnp.max(s0, axis=-1, keepdims=True), sink_ref[p])
        m1 = jnp.maximum(jnp.max(s1, axis=-1, keepdims=True), sink_ref[p + N_Q_HEADS // N_KV_HEADS])
        pr = jnp.concatenate([jnp.exp(s0 - m0), jnp.exp(s1 - m1)], axis=1).astype(BF16)
        live[u] = (pr, m0, m1)

    def values(u):
        i, p = divmod(u, ngrp)
        pr, m0, m1 = live.pop(u)
        r = jnp.dot(pr, band[i][1], preferred_element_type=F32)
        den = r[:, LANES:] + jnp.where(low, jnp.exp(sink_ref[p] - m0),
                                       jnp.exp(sink_ref[p + N_Q_HEADS // N_KV_HEADS] - m1))
        o_ref[i * blk:(i + 1) * blk, p * LANES:(p + 1) * LANES] = (r[:, :LANES] / den).astype(BF16)

    n = nblk * ngrp
    prep(0)
    for k in range(n + 2):
        if k < n:
            scores(k)
            if k + 1 < n and (k + 1) % ngrp == 0:
                prep((k + 1) // ngrp)
        if 0 <= k - 2 < n:
            values(k - 2)
        if 0 <= k - 1 < n:
            softmax(k - 1)
        yield


ATTN_STAGES = (TOKEN_TILE // ATTN_BLOCK) * (ATTN_Q_WIDTH // LANES) + 2


def _mlstm_thread(q_ref, kt_ref, v_ref, om_ref, if_ref, bif_ref, nw_ref, y_ref,
                  cn_ref, mrow_ref, mcol_ref):
    ts = TOKEN_TILE
    L = MLSTM_CHUNK
    D = MLSTM_HEAD_DIM
    nch = ts // L

    gates = if_ref[...] + bif_ref[...]
    logf_all = jax.nn.log_sigmoid(pltpu.roll(gates, LANES - MLSTM_HEADS, axis=1))

    ti = lax.broadcasted_iota(jnp.int32, (L, L), 0)
    si = lax.broadcasted_iota(jnp.int32, (L, L), 1)
    causal = si <= ti
    tri = causal.astype(BF16)
    ones_v = jnp.ones((L, LANES), BF16)
    chunk = {}
    head = {}

    def rows_of(c):
        return slice(c * L, (c + 1) * L)

    def p1(c):
        lf = logf_all[rows_of(c)]
        lf_hi = lf.astype(BF16)
        rem = lf - lf_hi.astype(F32)
        lf_mid = rem.astype(BF16)
        lf_lo = (rem - lf_mid.astype(F32)).astype(BF16)
        chunk[c] = (jnp.dot(tri, lf_hi, preferred_element_type=F32)
                    + jnp.dot(tri, lf_mid, preferred_element_type=F32)
                    + jnp.dot(tri, lf_lo, preferred_element_type=F32))

    def p2(c):
        b = chunk[c]
        r = b - gates[rows_of(c)]
        pm = -r
        sh = 1
        while sh < L:
            pm = jnp.maximum(pm, jnp.where(ti >= sh, pltpu.roll(pm, sh, axis=0), -jnp.inf))
            sh *= 2
        m_row = mrow_ref[0:1, :]
        inter = b + m_row
        mt = jnp.maximum(inter, b + pm)
        b_last = b[L - 1:L, :]
        m_new_row = jnp.maximum(b_last + m_row, jnp.max(b_last - r, axis=0, keepdims=True))
        mrow_ref[...] = jnp.broadcast_to(m_new_row, mrow_ref.shape)
        b_t = b.T[:SUBLANES]
        r_t = r.T[:SUBLANES]
        m_col = mcol_ref[:, 0:1]
        bl_col = b_t[:, L - 1:L]
        a_t = bl_col - r_t
        m_new_col = jnp.maximum(bl_col + m_col, jnp.max(a_t, axis=1, keepdims=True))
        mcol_ref[...] = jnp.broadcast_to(m_new_col, mcol_ref.shape)
        chunk[c] = dict(e_col=b - mt, r_t=r_t, w_inter=jnp.exp(inter - mt), e_mt=jnp.exp(-mt),
                        decay_row=jnp.exp(b_last + m_row - m_new_row),
                        wk_t=jnp.exp(a_t - m_new_col))

    def ha(c, h):
        hs = slice(h * D, (h + 1) * D)
        qh = q_ref[rows_of(c), hs]
        kt = kt_ref[hs, rows_of(c)]
        head[c, h] = dict(qh=qh, kt=kt, s=jnp.dot(qh, kt, preferred_element_type=F32))

    def hb(c, h):
        st, ch = head[c, h], chunk[c]
        w_intra = jnp.where(causal, jnp.exp(ch["e_col"][:, h:h + 1] - ch["r_t"][h:h + 1, :]), 0.0)
        pr = (st.pop("s") * w_intra).astype(BF16)
        qw = (st.pop("qh").astype(F32) * ch["w_inter"][:, h:h + 1]).astype(BF16)
        st["lhs"] = jnp.concatenate([pr, qw], axis=1)
        st["kw"] = (st.pop("kt").astype(F32) * ch["wk_t"][h:h + 1, :]).astype(BF16)

    def hc(c, h):
        st = head[c, h]
        hs = slice(h * D, (h + 1) * D)
        v_ext = jnp.concatenate([v_ref[rows_of(c), hs], ones_v], axis=1)
        rhs = jnp.concatenate([v_ext, cn_ref[h].astype(BF16)], axis=0)
        st["nd"] = jnp.dot(st.pop("lhs"), rhs, preferred_element_type=F32)
        st["kv"] = jnp.dot(st.pop("kw"), v_ext, preferred_element_type=F32)

    def hd(c, h):
        st, ch = head.pop((c, h)), chunk[c]
        hs = slice(h * D, (h + 1) * D)
        nd = st["nd"]
        hh = nd[:, :D] / jnp.maximum(jnp.abs(nd[:, D:]), ch["e_mt"][:, h:h + 1])
        cn_ref[h] = ch["decay_row"][:, h:h + 1] * cn_ref[h] + st["kv"]
        mu = jnp.mean(hh, axis=-1, keepdims=True)
        xc = hh - mu
        var = jnp.mean(xc * xc, axis=-1, keepdims=True)
        yn = (xc * lax.rsqrt(var + NORM_EPS)) * nw_ref[:, hs]
        y_ref[rows_of(c), hs] = (jax.nn.sigmoid(om_ref[rows_of(c), hs].astype(F32)) * yn).astype(BF16)

    half = MLSTM_HEADS // 2
    for g in range(MLSTM_STAGES):
        for c in range(nch):
            o = g - MLSTM_CHUNK_SKEW * c
            if o == 0:
                p1(c)
            elif o == 1:
                for h in range(MLSTM_HEADS):
                    ha(c, h)
                p2(c)
            elif o == 2:
                for h in range(half):
                    hb(c, h)
            elif o == 3:
                for h in range(half):
                    hc(c, h)
                for h in range(half, MLSTM_HEADS):
                    hb(c, h)
            elif o == 4:
                for h in range(half, MLSTM_HEADS):
                    hc(c, h)
                for h in range(half):
                    hd(c, h)
            elif o == 5:
                for h in range(half, MLSTM_HEADS):
                    hd(c, h)
        yield


MLSTM_CHUNK_SKEW = 4
MLSTM_STAGES = MLSTM_CHUNK_SKEW * (TOKEN_TILE // MLSTM_CHUNK - 1) + 6


def _merge_thread(x_ref, ya_ref, ym_ref, ga_ref, gm_ref, mod_ref, gpm_ref, gqf_ref,
                  wa_ref, wb_ref, wo_ref, x1_ref, h2_ref):
    tm = TOKEN_TILE
    parts = [slice(r * tm // MERGE_ROW_PARTS, (r + 1) * tm // MERGE_ROW_PARTS)
             for r in range(MERGE_ROW_PARTS)]
    gate_m = mod_ref[0, 2:3, :]
    ba = jnp.dot(ya_ref[...], wa_ref[...], preferred_element_type=F32)
    yield
    bb = jnp.dot(ym_ref[...], wb_ref[...], preferred_element_type=F32)
    yield
    merged = []
    for rows in parts:
        merged.append((jax.nn.sigmoid(ga_ref[rows, :].astype(F32)) * ba[rows]
                       + jax.nn.sigmoid(gm_ref[rows, :].astype(F32)) * bb[rows]).astype(BF16))
        yield
    mix = jnp.dot(jnp.concatenate(merged, axis=0), wo_ref[...], preferred_element_type=F32)
    yield
    for rows in parts:
        x1 = x_ref[rows, :] + gate_m * _rms(mix[rows], gpm_ref[...])
        x1_ref[rows, :] = x1
        yield
        h2_ref[rows, :] = _modulated_norm(x1, gqf_ref[...], mod_ref, 3)
        yield


MERGE_ROW_PARTS = 4
MERGE_STAGES = 3 + 3 * MERGE_ROW_PARTS


def _ffn_thread(x1_ref, h2_ref, mod_ref, gpf_ref, wg_ref, wu_ref, wd_ref, o_ref, act_s, after_up):
    gate_f = mod_ref[0, 5:6, :]
    for c in range(D_FF // FF_CHUNK):
        cs = slice(c * FF_CHUNK, (c + 1) * FF_CHUNK)
        gu = jnp.dot(h2_ref[...], jnp.concatenate([wg_ref[:, cs], wu_ref[:, cs]], axis=1),
                     preferred_element_type=F32)
        g = gu[:, :FF_CHUNK]
        act_s[:, c * FF_CHUNK:(c + 1) * FF_CHUNK] = ((g * jax.nn.sigmoid(g)) * gu[:, FF_CHUNK:]).astype(BF16)
        yield
    after_up()
    ff = []
    for c in range(FFN_DOWN_SPLIT):
        cs = slice(c * D_MODEL // FFN_DOWN_SPLIT, (c + 1) * D_MODEL // FFN_DOWN_SPLIT)
        ff.append(jnp.dot(act_s[...], wd_ref[:, cs], preferred_element_type=F32))
        yield
    o_ref[...] = x1_ref[...] + gate_f * _rms(jnp.concatenate(ff, axis=1), gpf_ref[...])
    yield


FFN_DOWN_SPLIT = 2
FFN_UP_STAGES = D_FF // FF_CHUNK
FFN_STAGES = FFN_UP_STAGES + FFN_DOWN_SPLIT + 1


def _interleave(main, n_main, fills):
    done = [0] * len(fills)
    for k in range(n_main):
        next(main)
        for f, (gen, count, first, last) in enumerate(fills):
            span = last - first + 1
            want = 0 if k < first else min(count, ((k - first + 1) * count + span - 1) // span)
            while done[f] < want:
                next(gen)
                done[f] += 1
    assert all(d == f[1] for d, f in zip(done, fills))
    assert next(main, None) is None and all(next(f[0], None) is None for f in fills)


def _back_kernel(n_tiles, tiles_per_batch, sink_ref,
                 qa_ref, kv_ref, kvp_ref, qm_ref, kt_ref, vm_ref, om_ref, if_ref, bif_ref, nw_ref,
                 x_ref, ga_ref, gm_ref, modm_ref, gpm_ref, gqf_ref,
                 modf_ref, gpf_ref,
                 wa_ref, wb_ref, wo_ref, wg_ref, wu_ref, wd_ref, o_ref,
                 ya_s, ym_s, x1_s, h2_cur, h2_next, act_s, cn_ref, mrow_ref, mcol_ref):
    i = pl.program_id(0)
    j = jnp.minimum(i, n_tiles - 1) % tiles_per_batch
    slot = i % 2

    @pl.when(i == 0)
    def _():
        ya_s[...] = jnp.zeros_like(ya_s)
        ym_s[...] = jnp.zeros_like(ym_s)
        x1_s[...] = jnp.zeros_like(x1_s)
        h2_cur[...] = jnp.zeros_like(h2_cur)

    @pl.when(j == 0)
    def _():
        cn_ref[...] = jnp.zeros_like(cn_ref)
        mrow_ref[...] = jnp.zeros_like(mrow_ref)
        mcol_ref[...] = jnp.zeros_like(mcol_ref)

    def hand_over():
        h2_cur[...] = h2_next[...]

    ffn = _ffn_thread(x1_s.at[slot], h2_cur, modf_ref, gpf_ref, wg_ref, wu_ref, wd_ref, o_ref, act_s,
                      hand_over)
    merge = _merge_thread(x_ref, ya_s, ym_s, ga_ref, gm_ref, modm_ref, gpm_ref, gqf_ref,
                          wa_ref, wb_ref, wo_ref, x1_s.at[1 - slot], h2_next)
    attn = _attn_thread(j, sink_ref, qa_ref, kv_ref, kvp_ref, ya_s)
    mlstm = _mlstm_thread(qm_ref, kt_ref, vm_ref, om_ref, if_ref, bif_ref, nw_ref, ym_s,
                          cn_ref, mrow_ref, mcol_ref)
    _interleave(ffn, FFN_STAGES,
                [(merge, MERGE_STAGES, 0, FFN_UP_STAGES - 2),
                 (attn, ATTN_STAGES, 2, FFN_STAGES - 2),
                 (mlstm, MLSTM_STAGES, 2, FFN_STAGES - 2)])


def _back(sinks, qa, kv, qm, kt, vm, om, gates, bif, norm_w, x2, ga, gm, mod3,
          g_post_mix, g_pre_ffn, g_post_ffn, wa, wb, wo, wg, wu, wd, tiles_per_batch):
    t = x2.shape[0]
    tm = TOKEN_TILE
    n = t // tm
    per_tile = tm // ATTN_BLOCK
    clamp = lambda v: jnp.clip(v, 0, n - 1)
    cur = lambda i: (clamp(i), 0)
    cur_t = lambda i: (0, clamp(i))

    def prev_block(i):
        tile = clamp(i)
        first = (tile // tiles_per_batch) * tiles_per_batch * per_tile
        return (jnp.maximum(tile * per_tile - 1, first), 0)

    mid = lambda i: (clamp(i - 1), 0)
    last = lambda i: (clamp(i - 2), 0)
    const = lambda i: (0, 0)
    once = pl.Buffered(1)
    weight_bytes = 2 * (wa.size + wb.size + wo.size + wg.size + wu.size + wd.size)
    mixer_in = tm * (ATTN_Q_WIDTH + 2 * ATTN_KV_WIDTH + 4 * MLSTM_WIDTH) * 2 + tm * LANES * 4
    scratch_bytes = tm * (D_FF + 2 * MLSTM_WIDTH + 2 * D_MODEL) * 2 + 2 * tm * D_MODEL * 4
    est = (weight_bytes + 2 * 2 * tm * D_MODEL * 4 + 2 * tm * 2 * D_MODEL * 2 + 2 * mixer_in
           + scratch_bytes + 6 * tm * D_MODEL * 4)
    return pl.pallas_call(
        functools.partial(_back_kernel, n, tiles_per_batch),
        grid=(n + 2,),
        in_specs=[pl.BlockSpec(memory_space=pltpu.SMEM),
                  pl.BlockSpec((tm, ATTN_Q_WIDTH), cur),
                  pl.BlockSpec((tm, 2 * ATTN_KV_WIDTH), cur),
                  pl.BlockSpec((ATTN_BLOCK, 2 * ATTN_KV_WIDTH), prev_block),
                  pl.BlockSpec((tm, MLSTM_WIDTH), cur),
                  pl.BlockSpec((MLSTM_WIDTH, tm), cur_t),
                  pl.BlockSpec((tm, MLSTM_WIDTH), cur),
                  pl.BlockSpec((tm, MLSTM_WIDTH), cur),
                  pl.BlockSpec((tm, LANES), cur),
                  pl.BlockSpec((1, LANES), const),
                  pl.BlockSpec((1, MLSTM_WIDTH), const),
                  pl.BlockSpec((tm, D_MODEL), mid),
                  pl.BlockSpec((tm, D_MODEL), mid),
                  pl.BlockSpec((tm, D_MODEL), mid),
                  pl.BlockSpec((1, 6, D_MODEL), lambda i: (clamp(i - 1) // tiles_per_batch, 0, 0)),
                  pl.BlockSpec((1, D_MODEL), const),
                  pl.BlockSpec((1, D_MODEL), const),
                  pl.BlockSpec((1, 6, D_MODEL), lambda i: (clamp(i - 2) // tiles_per_batch, 0, 0)),
                  pl.BlockSpec((1, D_MODEL), const),
                  pl.BlockSpec(wa.shape, const, pipeline_mode=once),
                  pl.BlockSpec(wb.shape, const, pipeline_mode=once),
                  pl.BlockSpec(wo.shape, const, pipeline_mode=once),
                  pl.BlockSpec(wg.shape, const, pipeline_mode=once),
                  pl.BlockSpec(wu.shape, const, pipeline_mode=once),
                  pl.BlockSpec(wd.shape, const, pipeline_mode=once)],
        out_specs=pl.BlockSpec((tm, D_MODEL), last),
        out_shape=jax.ShapeDtypeStruct((t, D_MODEL), F32),
        scratch_shapes=[pltpu.VMEM((tm, ATTN_Q_WIDTH), BF16),
                        pltpu.VMEM((tm, MLSTM_WIDTH), BF16),
                        pltpu.VMEM((2, tm, D_MODEL), F32),
                        pltpu.VMEM((tm, D_MODEL), BF16),
                        pltpu.VMEM((tm, D_MODEL), BF16),
                        pltpu.VMEM((tm, D_FF), BF16),
                        pltpu.VMEM((MLSTM_HEADS, MLSTM_HEAD_DIM, 2 * MLSTM_HEAD_DIM), F32),
                        pltpu.VMEM((SUBLANES, LANES), F32),
                        pltpu.VMEM((SUBLANES, LANES), F32)],
        compiler_params=pltpu.CompilerParams(dimension_semantics=("arbitrary",),
                                             vmem_limit_bytes=_vmem_limit(est)),
        name="back",
    )(sinks, qa, kv, kv, qm, kt, vm, om, gates, bif, norm_w, x2, ga, gm, mod3,
      g_post_mix, g_pre_ffn, mod3, g_post_ffn, wa, wb, wo, wg, wu, wd)


def kernel(x, c, positions, w_ada, b_ada, g_pre_mix, g_post_mix, w_in, b_if, conv_w, conv_b,
           attn_sinks, mlstm_norm_w, w_branch_attn, w_branch_mlstm, w_out, g_pre_ffn, g_post_ffn,
           w_ffn_gate, w_ffn_up, w_ffn_down):
    batch, seq, d = x.shape
    depth = w_in.shape[0]
    assert d == D_MODEL and seq % TOKEN_TILE == 0 and (batch * seq) % ROPE_TILE == 0
    assert D_FF % FF_CHUNK == 0
    t = batch * seq
    tiles_per_batch = seq // TOKEN_TILE
    x2 = x.reshape(t, d)

    inv_freq = (ROPE_THETA ** (-2.0 * jnp.arange(HEAD_DIM // 2, dtype=F32) / HEAD_DIM)).reshape(-1, 1)
    c_pad = jnp.pad(c, ((0, SUBLANES - batch % SUBLANES), (0, 0))) if batch % SUBLANES else c

    for l in range(depth):
        w_t = w_in[l].T
        cos_t, sin_t, mod, w_q, w_mid = _prep(positions.reshape(1, t), inv_freq, c_pad, w_ada[l],
                                              b_ada[l].reshape(1, -1), w_t)
        mod3 = mod[:batch].reshape(batch, 6, d)
        back_weights = (w_branch_attn[l], w_branch_mlstm[l], w_out[l], w_ffn_gate[l], w_ffn_up[l],
                        w_ffn_down[l])
        qa, kv, qm, kt, vm, om, gates, ga, gm, wa, wb, wo, wg, wu, wd = _inproj(
            x2, mod3, g_pre_mix[l].reshape(1, d), cos_t, sin_t, w_q, w_mid, w_t,
            conv_w[l], conv_b[l].reshape(1, -1), back_weights, tiles_per_batch)
        bif = jnp.pad(b_if[l], (0, LANES - 2 * MLSTM_HEADS)).reshape(1, LANES)
        x2 = _back(attn_sinks[l], qa, kv, qm, kt, vm, om, gates, bif, mlstm_norm_w[l].reshape(1, -1),
                   x2, ga, gm, mod3, g_post_mix[l].reshape(1, d), g_pre_ffn[l].reshape(1, d),
                   g_post_ffn[l].reshape(1, d), wa, wb, wo, wg, wu, wd, tiles_per_batch)
    return x2.reshape(batch, seq, d)
```

```python
import functools

import numpy as np
import jax
import jax.numpy as jnp
from jax import lax
from jax.experimental import pallas as pl
from jax.experimental.pallas import tpu as pltpu

F32 = jnp.float32
BF16 = jnp.bfloat16

D_MODEL = 1024
N_Q_HEADS = 8
N_KV_HEADS = 2
HEAD_DIM = 64
ROPE_THETA = 10000.0
MLSTM_HEADS = 4
MLSTM_HEAD_DIM = 128
CONV_WIDTH = 4
D_FF = 2816
NORM_EPS = 1e-6
ATTN_Q_WIDTH = N_Q_HEADS * HEAD_DIM
ATTN_KV_WIDTH = N_KV_HEADS * HEAD_DIM
MLSTM_WIDTH = MLSTM_HEADS * MLSTM_HEAD_DIM
ATTN_BLOCK = 128

LANES = 128
SUBLANES = 8
MXU_WIDTH = 256
V7X_SCOPED_VMEM_BYTES = 60000 * 1024

TOKEN_TILE = 512
MLSTM_CHUNK = 128
FF_CHUNK = LANES
ROPE_TILE = 2048

C_QA = 0
C_KV = C_QA + ATTN_Q_WIDTH
C_QK = C_KV + 2 * ATTN_KV_WIDTH
C_VM = C_QK + 2 * MLSTM_WIDTH
C_OM = C_VM + MLSTM_WIDTH
C_IF = C_OM + MLSTM_WIDTH
C_GA = C_IF + LANES
C_GM = C_GA + D_MODEL
IN_COLS = C_GM + D_MODEL


def _vmem_limit(estimate_bytes):
    return int(min(V7X_SCOPED_VMEM_BYTES, max(estimate_bytes, 16 * 1024 * 1024)))


def _rms(x, g):
    return (x * lax.rsqrt(jnp.mean(x * x, axis=-1, keepdims=True) + NORM_EPS)) * g


def _prep_kernel(pos_ref, freq_ref, c_ref, w_ref, b_ref, wq_f32, wmid_f32,
                 cos_ref, sin_ref, mod_ref, wq_ref, wmid_ref):
    wq_ref[...] = wq_f32[...].astype(BF16)
    wmid_ref[...] = wmid_f32[...].astype(BF16)
    mod_ref[...] = jnp.dot(c_ref[...], w_ref[...], preferred_element_type=F32,
                           precision=lax.Precision.HIGHEST) + b_ref[...]
    ang = freq_ref[...] * pos_ref[...].astype(F32)
    c = jnp.cos(ang)
    s = jnp.sin(ang)
    cos_ref[...] = jnp.concatenate([c, c, c, c], axis=0).T
    sin_ref[...] = jnp.concatenate([-s, s, -s, s], axis=0).T


def _q_source_block(out_block, blocks_per_head):
    per_kv = N_Q_HEADS // N_KV_HEADS
    head = out_block // blocks_per_head
    src_head = (head % N_KV_HEADS) * per_kv + head // N_KV_HEADS
    return src_head * blocks_per_head + out_block % blocks_per_head


def _prep(pos_row, inv_freq, c_pad, w_ada, b_ada, w_t):
    t = pos_row.shape[1]
    half = inv_freq.shape[0]
    rows = c_pad.shape[0]
    n = w_ada.shape[1]
    steps = t // ROPE_TILE
    cols = n // steps
    q_rows = ATTN_Q_WIDTH // steps
    mid_rows = (C_IF - C_KV) // steps
    assert n % steps == 0 and cols % LANES == 0
    assert ATTN_Q_WIDTH % steps == 0 and HEAD_DIM % q_rows == 0 and q_rows % (2 * SUBLANES) == 0
    assert (C_IF - C_KV) % steps == 0 and mid_rows % (2 * SUBLANES) == 0
    return pl.pallas_call(
        _prep_kernel,
        grid=(steps,),
        in_specs=[pl.BlockSpec((1, ROPE_TILE), lambda i: (0, i)),
                  pl.BlockSpec((half, 1), lambda i: (0, 0)),
                  pl.BlockSpec((rows, D_MODEL), lambda i: (0, 0)),
                  pl.BlockSpec((D_MODEL, cols), lambda i: (0, i)),
                  pl.BlockSpec((1, cols), lambda i: (0, i)),
                  pl.BlockSpec((q_rows, D_MODEL), lambda i: (_q_source_block(i, HEAD_DIM // q_rows), 0)),
                  pl.BlockSpec((pl.Element(mid_rows), pl.Element(D_MODEL)),
                               lambda i: (pl.multiple_of(C_KV + i * mid_rows, 2 * SUBLANES), 0))],
        out_specs=[pl.BlockSpec((ROPE_TILE, LANES), lambda i: (i, 0)),
                   pl.BlockSpec((ROPE_TILE, LANES), lambda i: (i, 0)),
                   pl.BlockSpec((rows, cols), lambda i: (0, i)),
                   pl.BlockSpec((q_rows, D_MODEL), lambda i: (i, 0)),
                   pl.BlockSpec((mid_rows, D_MODEL), lambda i: (i, 0))],
        out_shape=[jax.ShapeDtypeStruct((t, LANES), F32),
                   jax.ShapeDtypeStruct((t, LANES), F32),
                   jax.ShapeDtypeStruct((rows, n), F32),
                   jax.ShapeDtypeStruct((ATTN_Q_WIDTH, D_MODEL), BF16),
                   jax.ShapeDtypeStruct((C_IF - C_KV, D_MODEL), BF16)],
        name="prep",
    )(pos_row, inv_freq, c_pad, w_ada, b_ada, w_t, w_t)


def _modulated_norm(x, g, mod_ref, row):
    return (_rms(x, g) * (1.0 + mod_ref[0, row + 1:row + 2, :]) + mod_ref[0, row:row + 1, :]).astype(BF16)


def _inproj_kernel(n_tiles, tiles_per_batch, x0_ref, mod0_ref, xn_ref, modn_ref, g_ref, cos_ref, sin_ref,
                   wq_ref, wmid_ref, wtail_ref, cw_ref, cb_ref, *rest):
    n_cast = len(BACK_WEIGHT_BLOCKS)
    cast_in, rest = rest[:n_cast], rest[n_cast:]
    (qa_ref, kv_ref, qm_ref, kt_ref, vm_ref, om_ref, if_ref, ga_ref, gm_ref), rest = rest[:9], rest[9:]
    cast_out, (s_ref, raw_s, rot_s, wtail_s) = rest[:n_cast], rest[n_cast:]
    for src, dst in zip(cast_in, cast_out):
        dst[...] = src[...].astype(BF16)
    _inproj_body(n_tiles, tiles_per_batch, x0_ref, mod0_ref, xn_ref, modn_ref, g_ref, cos_ref, sin_ref,
                 wq_ref, wmid_ref, wtail_ref, cw_ref, cb_ref,
                 qa_ref, kv_ref, qm_ref, kt_ref, vm_ref, om_ref, if_ref, ga_ref, gm_ref,
                 s_ref, raw_s, rot_s, wtail_s)


BACK_WEIGHT_BLOCKS = (16, 16, 16, 16, 16, 64)


def _inproj_body(n_tiles, tiles_per_batch, x0_ref, mod0_ref, xn_ref, modn_ref, g_ref, cos_ref, sin_ref,
                 wq_ref, wmid_ref, wtail_ref, cw_ref, cb_ref,
                 qa_ref, kv_ref, qm_ref, kt_ref, vm_ref, om_ref, if_ref, ga_ref, gm_ref,
                 s_ref, raw_s, rot_s, wtail_s):
    tm = TOKEN_TILE
    i = pl.program_id(0)
    par = i % 2
    post_tile = jnp.maximum(i - 1, 0)
    h_cur = s_ref.at[par]
    h_next = s_ref.at[1 - par]
    out_a = s_ref.at[2 + par]
    out_b = s_ref.at[4 + par]
    kv_lo = ATTN_Q_WIDTH

    @pl.when(i == 0)
    def _():
        s_ref[0] = _modulated_norm(x0_ref[...], g_ref[...], mod0_ref, 0)
        raw_s[...] = jnp.zeros_like(raw_s)
        rot_s[...] = jnp.zeros_like(rot_s)
        n_if = 2 * MLSTM_HEADS
        wtail_s[0:LANES, :] = jnp.concatenate(
            [wtail_ref[0:n_if, :], jnp.zeros((LANES - n_if, D_MODEL), F32)], axis=0).astype(BF16)
        for r0 in range(0, 2 * D_MODEL, MXU_WIDTH):
            wtail_s[LANES + r0:LANES + r0 + MXU_WIDTH, :] = (
                wtail_ref[n_if + r0:n_if + r0 + MXU_WIDTH, :].astype(BF16))

    @pl.when(post_tile % tiles_per_batch == 0)
    def _():
        raw_s[0:SUBLANES, :] = jnp.zeros((SUBLANES, raw_s.shape[1]), F32)

    def proj(lo, width):
        if lo < C_KV:
            w = wq_ref[lo:lo + width, :]
        elif lo < C_IF:
            w = wmid_ref[lo - C_KV:lo - C_KV + width, :]
        else:
            w = wtail_s[lo - C_IF:lo - C_IF + width, :]
        return lax.dot_general(h_cur[...], w, (((1,), (1,)), ((), ())), preferred_element_type=F32)

    lane = lax.broadcasted_iota(jnp.int32, (1, LANES), 1)
    first_half = (lane % HEAD_DIM) < (HEAD_DIM // 2)

    def rope(t):
        swapped = jnp.where(first_half, pltpu.roll(t, LANES - HEAD_DIM // 2, axis=1),
                            pltpu.roll(t, HEAD_DIM // 2, axis=1))
        return t * cos_ref[...] + swapped * sin_ref[...]

    def post_q():
        for p in range(ATTN_Q_WIDTH // LANES):
            sl = slice(p * LANES, (p + 1) * LANES)
            out_a[:, sl] = (rope(rot_s[:, sl]) * (HEAD_DIM ** -0.5)).astype(BF16)

    def post_kv():
        out_a[:, kv_lo:kv_lo + LANES] = rope(rot_s[:, kv_lo:kv_lo + LANES]).astype(BF16)
        out_a[:, kv_lo + LANES:kv_lo + 2 * LANES] = rot_s[:, kv_lo + LANES:].astype(BF16)

    blk = MXU_WIDTH

    def post_conv(cbk):
        cs = slice(cbk * blk, (cbk + 1) * blk)
        acc = cb_ref[:, cs]
        for sh in range(CONV_WIDTH):
            acc = acc + (cw_ref[CONV_WIDTH - 1 - sh:CONV_WIDTH - sh, cs]
                         * raw_s[SUBLANES - sh:SUBLANES - sh + tm, cs])
        raw_s[0:SUBLANES, cs] = raw_s[tm:tm + SUBLANES, cs]
        a = acc * jax.nn.sigmoid(acc)
        if cbk * blk < MLSTM_WIDTH:
            out_b[:, cs] = a.astype(BF16)
        else:
            a = a * (MLSTM_HEAD_DIM ** -0.5)
            for d0 in range(0, blk, LANES):
                r0 = cbk * blk - MLSTM_WIDTH + d0
                for t0 in range(0, tm, LANES):
                    out_b[r0:r0 + LANES, MLSTM_WIDTH + t0:MLSTM_WIDTH + t0 + LANES] = (
                        a[t0:t0 + LANES, d0:d0 + LANES].T.astype(BF16))

    def ship_q():
        qa_ref[...] = out_a[:, :ATTN_Q_WIDTH]

    def ship_kv():
        kv_ref[...] = out_a[:, kv_lo:kv_lo + 2 * ATTN_KV_WIDTH]

    def ship_conv(cbk):
        cs = slice(cbk * blk, (cbk + 1) * blk)
        if cbk * blk < MLSTM_WIDTH:
            qm_ref[:, cs] = out_b[:, cs]
        else:
            rs = slice(cbk * blk - MLSTM_WIDTH, (cbk + 1) * blk - MLSTM_WIDTH)
            kt_ref[rs, :] = out_b[rs, MLSTM_WIDTH:MLSTM_WIDTH + tm]

    def norm_next(part):
        rows = slice(part * tm // INPROJ_NORM_PARTS, (part + 1) * tm // INPROJ_NORM_PARTS)
        h_next[rows, :] = _modulated_norm(xn_ref[rows, :], g_ref[...], modn_ref, 0)

    def main_rot(lo_w, lo_s, width):
        rot_s[:, lo_s:lo_s + width] = proj(lo_w, width)

    def main_conv(cbk):
        raw_s[SUBLANES:, cbk * blk:(cbk + 1) * blk] = proj(C_QK + cbk * blk, blk)

    def main_direct(out_ref, lo_w, lo, width):
        out_ref[:, lo:lo + width] = proj(lo_w + lo, width).astype(out_ref.dtype)

    half = D_MODEL // 2
    main_direct(vm_ref, C_VM, 0, MLSTM_WIDTH)
    post_q()
    main_rot(C_QA, 0, ATTN_Q_WIDTH)
    ship_q()
    post_kv()
    post_conv(0)
    main_rot(C_KV, kv_lo, 2 * ATTN_KV_WIDTH)
    ship_kv()
    main_conv(0)
    ship_conv(0)
    post_conv(1)
    main_direct(om_ref, C_OM, 0, MLSTM_WIDTH)
    main_conv(1)
    ship_conv(1)
    post_conv(2)
    main_direct(if_ref, C_IF, 0, LANES)
    main_conv(2)
    ship_conv(2)
    post_conv(3)
    main_direct(ga_ref, C_GA, 0, half)
    main_conv(3)
    ship_conv(3)
    norm_next(0)
    main_direct(ga_ref, C_GA, half, half)
    norm_next(1)
    main_direct(gm_ref, C_GM, 0, half)
    norm_next(2)
    norm_next(3)
    main_direct(gm_ref, C_GM, half, half)


INPROJ_NORM_PARTS = 4


def _inproj(x2, mod3, g_pre, cos_t, sin_t, w_q, w_mid, w_t, conv_w, conv_b, back_weights, tiles_per_batch):
    t = x2.shape[0]
    tm = TOKEN_TILE
    n = t // tm
    main = lambda i: (jnp.minimum(i, n - 1), 0)
    post = lambda i: (jnp.maximum(i - 1, 0), 0)
    nxt = lambda i: (jnp.minimum(i + 1, n - 1), 0)
    const = lambda i: (0, 0)
    once = pl.Buffered(1)
    outs = [(ATTN_Q_WIDTH, BF16, post), (2 * ATTN_KV_WIDTH, BF16, post), (MLSTM_WIDTH, BF16, post),
            (None, BF16, None), (MLSTM_WIDTH, BF16, main), (MLSTM_WIDTH, BF16, main),
            (LANES, F32, main), (D_MODEL, BF16, main), (D_MODEL, BF16, main)]
    out_specs = [pl.BlockSpec((MLSTM_WIDTH, tm), lambda i: (0, jnp.maximum(i - 1, 0))) if w is None
                 else pl.BlockSpec((tm, w), m) for w, _, m in outs]
    out_shape = [jax.ShapeDtypeStruct((MLSTM_WIDTH, t) if w is None else (t, w), d) for w, d, _ in outs]
    out_bytes = sum((w or MLSTM_WIDTH) * np.dtype(d).itemsize for w, d, _ in outs) * tm
    rot_cols = ATTN_Q_WIDTH + 2 * ATTN_KV_WIDTH
    assert tm == MLSTM_WIDTH
    scratch_bytes = (6 * tm * D_MODEL * 2 + (tm + SUBLANES) * 2 * MLSTM_WIDTH * 4 + tm * rot_cols * 4)
    tail_rows = w_t.shape[0] - C_IF
    tail_aligned = LANES + 2 * D_MODEL
    assert tail_rows == 2 * MLSTM_HEADS + 2 * D_MODEL
    weight_bytes = (w_q.size + w_mid.size + tail_aligned * D_MODEL) * 2 + tail_rows * D_MODEL * 4
    cast_in, cast_out, cast_bytes = [], [], 0
    for k, (w, rows) in enumerate(zip(back_weights, BACK_WEIGHT_BLOCKS)):
        blocks = w.shape[0] // rows
        assert w.shape[0] % rows == 0 and blocks <= n and rows % (2 * SUBLANES) == 0
        dst = lambda i, blocks=blocks: (jnp.minimum(i, blocks - 1), 0)
        src = dst if k else (lambda i, blocks=blocks, per_head=HEAD_DIM // rows:
                             (_q_source_block(jnp.minimum(i, blocks - 1), per_head), 0))
        cast_in.append(pl.BlockSpec((rows, w.shape[1]), src))
        cast_out.append(pl.BlockSpec((rows, w.shape[1]), dst))
        out_shape.append(jax.ShapeDtypeStruct(w.shape, BF16))
        cast_bytes += 2 * rows * w.shape[1] * 6
    est = (weight_bytes + 3 * tm * D_MODEL * 4 + 2 * out_bytes + 4 * tm * LANES * 4
           + scratch_bytes + 4 * tm * D_MODEL * 4 + cast_bytes)
    return pl.pallas_call(
        functools.partial(_inproj_kernel, n, tiles_per_batch),
        grid=(n + 1,),
        in_specs=[pl.BlockSpec((tm, D_MODEL), const, pipeline_mode=once),
                  pl.BlockSpec((1, 6, D_MODEL), lambda i: (0, 0, 0)),
                  pl.BlockSpec((tm, D_MODEL), nxt),
                  pl.BlockSpec((1, 6, D_MODEL),
                               lambda i: (jnp.minimum(i + 1, n - 1) // tiles_per_batch, 0, 0)),
                  pl.BlockSpec((1, D_MODEL), const),
                  pl.BlockSpec((tm, LANES), post),
                  pl.BlockSpec((tm, LANES), post),
                  pl.BlockSpec(w_q.shape, const, pipeline_mode=once),
                  pl.BlockSpec(w_mid.shape, const, pipeline_mode=once),
                  pl.BlockSpec((pl.Element(tail_rows), pl.Element(D_MODEL)), lambda i: (C_IF, 0),
                               pipeline_mode=once),
                  pl.BlockSpec((CONV_WIDTH, 2 * MLSTM_WIDTH), const),
                  pl.BlockSpec((1, 2 * MLSTM_WIDTH), const)] + cast_in,
        out_specs=out_specs + cast_out,
        out_shape=out_shape,
        scratch_shapes=[pltpu.VMEM((6, tm, D_MODEL), BF16),
                        pltpu.VMEM((tm + SUBLANES, 2 * MLSTM_WIDTH), F32),
                        pltpu.VMEM((tm, rot_cols), F32),
                        pltpu.VMEM((tail_aligned, D_MODEL), BF16)],
        compiler_params=pltpu.CompilerParams(dimension_semantics=("arbitrary",),
                                             vmem_limit_bytes=_vmem_limit(est)),
        name="inproj",
    )(x2, mod3, x2, mod3, g_pre, cos_t, sin_t, w_q, w_mid, w_t, conv_w, conv_b, *back_weights)


def _attn_thread(j, sink_ref, q_ref, kv_ref, kvp_ref, o_ref):
    blk = ATTN_BLOCK
    nblk = TOKEN_TILE // blk
    ngrp = ATTN_Q_WIDTH // LANES
    lane = lax.broadcasted_iota(jnp.int32, (1, LANES), 1)
    low = lane < HEAD_DIM
    qi = lax.broadcasted_iota(jnp.int32, (blk, 2 * blk), 0)
    kj = lax.broadcasted_iota(jnp.int32, (blk, 2 * blk), 1)
    in_cur = (kj >= blk) & (kj - blk <= qi)
    in_prev = (kj < blk) & (kj > qi)
    in_prev_first = (kj < blk) & (kj > qi + jnp.where(j > 0, 0, blk))
    ones = jnp.ones((2 * blk, LANES), F32)
    zeros = jnp.zeros((2 * blk, LANES), F32)
    one_lo = jnp.where(low, ones, zeros).astype(BF16)
    one_hi = jnp.where(low, zeros, ones).astype(BF16)
    band = {}
    live = {}

    def prep(i):
        cur = kv_ref[i * blk:(i + 1) * blk, :]
        prev = kvp_ref[...] if i == 0 else kv_ref[(i - 1) * blk:i * blk, :]
        kv = jnp.concatenate([prev, cur], axis=0).astype(F32)
        k = kv[:, :LANES]
        v = kv[:, LANES:]
        k_cat = jnp.concatenate([jnp.where(low, k, 0.0), jnp.where(low, 0.0, k)], axis=0).astype(BF16)
        v_cat = jnp.concatenate(
            [jnp.concatenate([jnp.where(low, v, 0.0).astype(BF16), one_lo], axis=1),
             jnp.concatenate([jnp.where(low, 0.0, v).astype(BF16), one_hi], axis=1)], axis=0)
        band[i] = (k_cat, v_cat, in_cur | (in_prev_first if i == 0 else in_prev))

    def scores(u):
        i, p = divmod(u, ngrp)
        q = q_ref[i * blk:(i + 1) * blk, p * LANES:(p + 1) * LANES]
        live[u] = lax.dot_general(q, band[i][0], (((1,), (1,)), ((), ())), preferred_element_type=F32)

    def softmax(u):
        i, p = divmod(u, ngrp)
        s = live[u]
        mask = band[i][2]
        s0 = jnp.where(mask, s[:, :2 * blk], -jnp.inf)
        s1 = jnp.where(mask, s[:, 2 * blk:], -jnp.inf)
        m0 = jnp.maximum(jnp.max(s0, axis=-1, keepdims=True), sink_ref[p])
        m1 = jnp.maximum(jnp.max(s1, axis=-1, keepdims=True), sink_ref[p + N_Q_HEADS // N_KV_HEADS])
        pr = jnp.concatenate([jnp.exp(s0 - m0), jnp.exp(s1 - m1)], axis=1).astype(BF16)
        live[u] = (pr, m0, m1)

    def values(u):
        i, p = divmod(u, ngrp)
        pr, m0, m1 = live.pop(u)
        r = jnp.dot(pr, band[i][1], preferred_element_type=F32)
        den = r[:, LANES:] + jnp.where(low, jnp.exp(sink_ref[p] - m0),
                                       jnp.exp(sink_ref[p + N_Q_HEADS // N_KV_HEADS] - m1))
        o_ref[i * blk:(i + 1) * blk, p * LANES:(p + 1) * LANES] = (r[:, :LANES] / den).astype(BF16)

    n = nblk * ngrp
    prep(0)
    for k in range(n + 2):
        if k < n:
            scores(k)
            if k + 1 < n and (k + 1) % ngrp == 0:
                prep((k + 1) // ngrp)
        if 0 <= k - 2 < n:
            values(k - 2)
        if 0 <= k - 1 < n:
            softmax(k - 1)
        yield


ATTN_STAGES = (TOKEN_TILE // ATTN_BLOCK) * (ATTN_Q_WIDTH // LANES) + 2


def _mlstm_thread(q_ref, kt_ref, v_ref, om_ref, if_ref, bif_ref, nw_ref, y_ref,
                  cn_ref, mrow_ref, mcol_ref):
    ts = TOKEN_TILE
    L = MLSTM_CHUNK
    D = MLSTM_HEAD_DIM
    nch = ts // L

    gates = if_ref[...] + bif_ref[...]
    logf_all = jax.nn.log_sigmoid(pltpu.roll(gates, LANES - MLSTM_HEADS, axis=1))

    ti = lax.broadcasted_iota(jnp.int32, (L, L), 0)
    si = lax.broadcasted_iota(jnp.int32, (L, L), 1)
    causal = si <= ti
    tri = causal.astype(BF16)
    ones_v = jnp.ones((L, LANES), BF16)
    chunk = {}
    head = {}

    def rows_of(c):
        return slice(c * L, (c + 1) * L)

    def p1(c):
        lf = logf_all[rows_of(c)]
        lf_hi = lf.astype(BF16)
        rem = lf - lf_hi.astype(F32)
        lf_mid = rem.astype(BF16)
        lf_lo = (rem - lf_mid.astype(F32)).astype(BF16)
        chunk[c] = (jnp.dot(tri, lf_hi, preferred_element_type=F32)
                    + jnp.dot(tri, lf_mid, preferred_element_type=F32)
                    + jnp.dot(tri, lf_lo, preferred_element_type=F32))

    def p2(c):
        b = chunk[c]
        r = b - gates[rows_of(c)]
        pm = -r
        sh = 1
        while sh < L:
            pm = jnp.maximum(pm, jnp.where(ti >= sh, pltpu.roll(pm, sh, axis=0), -jnp.inf))
            sh *= 2
        m_row = mrow_ref[0:1, :]
        inter = b + m_row
        mt = jnp.maximum(inter, b + pm)
        b_last = b[L - 1:L, :]
        m_new_row = jnp.maximum(b_last + m_row, jnp.max(b_last - r, axis=0, keepdims=True))
        mrow_ref[...] = jnp.broadcast_to(m_new_row, mrow_ref.shape)
        b_t = b.T[:SUBLANES]
        r_t = r.T[:SUBLANES]
        m_col = mcol_ref[:, 0:1]
        bl_col = b_t[:, L - 1:L]
        a_t = bl_col - r_t
        m_new_col = jnp.maximum(bl_col + m_col, jnp.max(a_t, axis=1, keepdims=True))
        mcol_ref[...] = jnp.broadcast_to(m_new_col, mcol_ref.shape)
        chunk[c] = dict(e_col=b - mt, r_t=r_t, w_inter=jnp.exp(inter - mt), e_mt=jnp.exp(-mt),
                        decay_row=jnp.exp(b_last + m_row - m_new_row),
                        wk_t=jnp.exp(a_t - m_new_col))

    def ha(c, h):
        hs = slice(h * D, (h + 1) * D)
        qh = q_ref[rows_of(c), hs]
        kt = kt_ref[hs, rows_of(c)]
        head[c, h] = dict(qh=qh, kt=kt, s=jnp.dot(qh, kt, preferred_element_type=F32))

    def hb(c, h):
        st, ch = head[c, h], chunk[c]
        w_intra = jnp.where(causal, jnp.exp(ch["e_col"][:, h:h + 1] - ch["r_t"][h:h + 1, :]), 0.0)
        pr = (st.pop("s") * w_intra).astype(BF16)
        qw = (st.pop("qh").astype(F32) * ch["w_inter"][:, h:h + 1]).astype(BF16)
        st["lhs"] = jnp.concatenate([pr, qw], axis=1)
        st["kw"] = (st.pop("kt").astype(F32) * ch["wk_t"][h:h + 1, :]).astype(BF16)

    def hc(c, h):
        st = head[c, h]
        hs = slice(h * D, (h + 1) * D)
        v_ext = jnp.concatenate([v_ref[rows_of(c), hs], ones_v], axis=1)
        rhs = jnp.concatenate([v_ext, cn_ref[h].astype(BF16)], axis=0)
        st["nd"] = jnp.dot(st.pop("lhs"), rhs, preferred_element_type=F32)
        st["kv"] = jnp.dot(st.pop("kw"), v_ext, preferred_element_type=F32)

    def hd(c, h):
        st, ch = head.pop((c, h)), chunk[c]
        hs = slice(h * D, (h + 1) * D)
        nd = st["nd"]
        hh = nd[:, :D] / jnp.maximum(jnp.abs(nd[:, D:]), ch["e_mt"][:, h:h + 1])
        cn_ref[h] = ch["decay_row"][:, h:h + 1] * cn_ref[h] + st["kv"]
        mu = jnp.mean(hh, axis=-1, keepdims=True)
        xc = hh - mu
        var = jnp.mean(xc * xc, axis=-1, keepdims=True)
        yn = (xc * lax.rsqrt(var + NORM_EPS)) * nw_ref[:, hs]
        y_ref[rows_of(c), hs] = (jax.nn.sigmoid(om_ref[rows_of(c), hs].astype(F32)) * yn).astype(BF16)

    half = MLSTM_HEADS // 2
    for g in range(MLSTM_STAGES):
        for c in range(nch):
            o = g - MLSTM_CHUNK_SKEW * c
            if o == 0:
                p1(c)
            elif o == 1:
                for h in range(MLSTM_HEADS):
                    ha(c, h)
                p2(c)
            elif o == 2:
                for h in range(half):
                    hb(c, h)
            elif o == 3:
                for h in range(half):
                    hc(c, h)
                for h in range(half, MLSTM_HEADS):
                    hb(c, h)
            elif o == 4:
                for h in range(half, MLSTM_HEADS):
                    hc(c, h)
                for h in range(half):
                    hd(c, h)
            elif o == 5:
                for h in range(half, MLSTM_HEADS):
                    hd(c, h)
        yield


MLSTM_CHUNK_SKEW = 4
MLSTM_STAGES = MLSTM_CHUNK_SKEW * (TOKEN_TILE // MLSTM_CHUNK - 1) + 6


def _merge_thread(x_ref, ya_ref, ym_ref, ga_ref, gm_ref, mod_ref, gpm_ref, gqf_ref,
                  wa_ref, wb_ref, wo_ref, x1_ref, h2_ref):
    tm = TOKEN_TILE
    parts = [slice(r * tm // MERGE_ROW_PARTS, (r + 1) * tm // MERGE_ROW_PARTS)
             for r in range(MERGE_ROW_PARTS)]
    gate_m = mod_ref[0, 2:3, :]
    ba = jnp.dot(ya_ref[...], wa_ref[...], preferred_element_type=F32)
    yield
    bb = jnp.dot(ym_ref[...], wb_ref[...], preferred_element_type=F32)
    yield
    merged = []
    for rows in parts:
        merged.append((jax.nn.sigmoid(ga_ref[rows, :].astype(F32)) * ba[rows]
                       + jax.nn.sigmoid(gm_ref[rows, :].astype(F32)) * bb[rows]).astype(BF16))
        yield
    mix = jnp.dot(jnp.concatenate(merged, axis=0), wo_ref[...], preferred_element_type=F32)
    yield
    for rows in parts:
        x1 = x_ref[rows, :] + gate_m * _rms(mix[rows], gpm_ref[...])
        x1_ref[rows, :] = x1
        yield
        h2_ref[rows, :] = _modulated_norm(x1, gqf_ref[...], mod_ref, 3)
        yield


MERGE_ROW_PARTS = 4
MERGE_STAGES = 3 + 3 * MERGE_ROW_PARTS


def _ffn_thread(x1_ref, h2_ref, mod_ref, gpf_ref, wg_ref, wu_ref, wd_ref, o_ref, act_s, after_up):
    gate_f = mod_ref[0, 5:6, :]
    for c in range(D_FF // FF_CHUNK):
        cs = slice(c * FF_CHUNK, (c + 1) * FF_CHUNK)
        gu = jnp.dot(h2_ref[...], jnp.concatenate([wg_ref[:, cs], wu_ref[:, cs]], axis=1),
                     preferred_element_type=F32)
        g = gu[:, :FF_CHUNK]
        act_s[:, c * FF_CHUNK:(c + 1) * FF_CHUNK] = ((g * jax.nn.sigmoid(g)) * gu[:, FF_CHUNK:]).astype(BF16)
        yield
    after_up()
    ff = []
    for c in range(FFN_DOWN_SPLIT):
        cs = slice(c * D_MODEL // FFN_DOWN_SPLIT, (c + 1) * D_MODEL // FFN_DOWN_SPLIT)
        ff.append(jnp.dot(act_s[...], wd_ref[:, cs], preferred_element_type=F32))
        yield
    o_ref[...] = x1_ref[...] + gate_f * _rms(jnp.concatenate(ff, axis=1), gpf_ref[...])
    yield


FFN_DOWN_SPLIT = 2
FFN_UP_STAGES = D_FF // FF_CHUNK
FFN_STAGES = FFN_UP_STAGES + FFN_DOWN_SPLIT + 1


def _interleave(main, n_main, fills):
    done = [0] * len(fills)
    for k in range(n_main):
        next(main)
        for f, (gen, count, first, last) in enumerate(fills):
            span = last - first + 1
            want = 0 if k < first else min(count, ((k - first + 1) * count + span - 1) // span)
            while done[f] < want:
                next(gen)
                done[f] += 1
    assert all(d == f[1] for d, f in zip(done, fills))
    assert next(main, None) is None and all(next(f[0], None) is None for f in fills)


def _back_kernel(n_tiles, tiles_per_batch, sink_ref,
                 qa_ref, kv_ref, kvp_ref, qm_ref, kt_ref, vm_ref, om_ref, if_ref, bif_ref, nw_ref,
                 x_ref, ga_ref, gm_ref, modm_ref, gpm_ref, gqf_ref,
                 modf_ref, gpf_ref,
                 wa_ref, wb_ref, wo_ref, wg_ref, wu_ref, wd_ref, o_ref,
                 ya_s, ym_s, x1_s, h2_cur, h2_next, act_s, cn_ref, mrow_ref, mcol_ref):
    i = pl.program_id(0)
    j = jnp.minimum(i, n_tiles - 1) % tiles_per_batch
    slot = i % 2

    @pl.when(i == 0)
    def _():
        ya_s[...] = jnp.zeros_like(ya_s)
        ym_s[...] = jnp.zeros_like(ym_s)
        x1_s[...] = jnp.zeros_like(x1_s)
        h2_cur[...] = jnp.zeros_like(h2_cur)

    @pl.when(j == 0)
    def _():
        cn_ref[...] = jnp.zeros_like(cn_ref)
        mrow_ref[...] = jnp.zeros_like(mrow_ref)
        mcol_ref[...] = jnp.zeros_like(mcol_ref)

    def hand_over():
        h2_cur[...] = h2_next[...]

    def ffn():
        return _ffn_thread(x1_s.at[slot], h2_cur, modf_ref, gpf_ref, wg_ref, wu_ref, wd_ref, o_ref,
                           act_s, hand_over)

    def merge():
        return _merge_thread(x_ref, ya_s, ym_s, ga_ref, gm_ref, modm_ref, gpm_ref, gqf_ref,
                             wa_ref, wb_ref, wo_ref, x1_s.at[1 - slot], h2_next)

    def attn():
        return _attn_thread(j, sink_ref, qa_ref, kv_ref, kvp_ref, ya_s)

    def mlstm():
        return _mlstm_thread(qm_ref, kt_ref, vm_ref, om_ref, if_ref, bif_ref, nw_ref, ym_s,
                             cn_ref, mrow_ref, mcol_ref)

    @pl.when(i < 2)
    def _():
        _interleave(merge(), MERGE_STAGES,
                    [(attn(), ATTN_STAGES, 2, MERGE_STAGES - 1),
                     (mlstm(), MLSTM_STAGES, 2, MERGE_STAGES - 1)])
        hand_over()

    @pl.when((i >= 2) & (i < n_tiles))
    def _():
        _interleave(ffn(), FFN_STAGES,
                    [(merge(), MERGE_STAGES, 0, FFN_UP_STAGES - 2),
                     (attn(), ATTN_STAGES, 2, FFN_STAGES - 2),
                     (mlstm(), MLSTM_STAGES, 2, FFN_STAGES - 2)])

    @pl.when(i >= n_tiles)
    def _():
        _interleave(ffn(), FFN_STAGES, [(merge(), MERGE_STAGES, 0, FFN_UP_STAGES - 2)])


def _back(sinks, qa, kv, qm, kt, vm, om, gates, bif, norm_w, x2, ga, gm, mod3,
          g_post_mix, g_pre_ffn, g_post_ffn, wa, wb, wo, wg, wu, wd, tiles_per_batch):
    t = x2.shape[0]
    tm = TOKEN_TILE
    n = t // tm
    per_tile = tm // ATTN_BLOCK
    clamp = lambda v: jnp.clip(v, 0, n - 1)
    cur = lambda i: (clamp(i), 0)
    cur_t = lambda i: (0, clamp(i))

    def prev_block(i):
        tile = clamp(i)
        first = (tile // tiles_per_batch) * tiles_per_batch * per_tile
        return (jnp.maximum(tile * per_tile - 1, first), 0)

    mid = lambda i: (clamp(i - 1), 0)
    last = lambda i: (clamp(i - 2), 0)
    const = lambda i: (0, 0)
    once = pl.Buffered(1)
    weight_bytes = 2 * (wa.size + wb.size + wo.size + wg.size + wu.size + wd.size)
    mixer_in = tm * (ATTN_Q_WIDTH + 2 * ATTN_KV_WIDTH + 4 * MLSTM_WIDTH) * 2 + tm * LANES * 4
    scratch_bytes = tm * (D_FF + 2 * MLSTM_WIDTH + 2 * D_MODEL) * 2 + 2 * tm * D_MODEL * 4
    est = (weight_bytes + 2 * 2 * tm * D_MODEL * 4 + 2 * tm * 2 * D_MODEL * 2 + 2 * mixer_in
           + scratch_bytes + 6 * tm * D_MODEL * 4)
    return pl.pallas_call(
        functools.partial(_back_kernel, n, tiles_per_batch),
        grid=(n + 2,),
        in_specs=[pl.BlockSpec(memory_space=pltpu.SMEM),
                  pl.BlockSpec((tm, ATTN_Q_WIDTH), cur),
                  pl.BlockSpec((tm, 2 * ATTN_KV_WIDTH), cur),
                  pl.BlockSpec((ATTN_BLOCK, 2 * ATTN_KV_WIDTH), prev_block),
                  pl.BlockSpec((tm, MLSTM_WIDTH), cur),
                  pl.BlockSpec((MLSTM_WIDTH, tm), cur_t),
                  pl.BlockSpec((tm, MLSTM_WIDTH), cur),
                  pl.BlockSpec((tm, MLSTM_WIDTH), cur),
                  pl.BlockSpec((tm, LANES), cur),
                  pl.BlockSpec((1, LANES), const),
                  pl.BlockSpec((1, MLSTM_WIDTH), const),
                  pl.BlockSpec((tm, D_MODEL), mid),
                  pl.BlockSpec((tm, D_MODEL), mid),
                  pl.BlockSpec((tm, D_MODEL), mid),
                  pl.BlockSpec((1, 6, D_MODEL), lambda i: (clamp(i - 1) // tiles_per_batch, 0, 0)),
                  pl.BlockSpec((1, D_MODEL), const),
                  pl.BlockSpec((1, D_MODEL), const),
                  pl.BlockSpec((1, 6, D_MODEL), lambda i: (clamp(i - 2) // tiles_per_batch, 0, 0)),
                  pl.BlockSpec((1, D_MODEL), const),
                  pl.BlockSpec(wa.shape, const, pipeline_mode=once),
                  pl.BlockSpec(wb.shape, const, pipeline_mode=once),
                  pl.BlockSpec(wo.shape, const, pipeline_mode=once),
                  pl.BlockSpec(wg.shape, const, pipeline_mode=once),
                  pl.BlockSpec(wu.shape, const, pipeline_mode=once),
                  pl.BlockSpec(wd.shape, const, pipeline_mode=once)],
        out_specs=pl.BlockSpec((tm, D_MODEL), last),
        out_shape=jax.ShapeDtypeStruct((t, D_MODEL), F32),
        scratch_shapes=[pltpu.VMEM((tm, ATTN_Q_WIDTH), BF16),
                        pltpu.VMEM((tm, MLSTM_WIDTH), BF16),
                        pltpu.VMEM((2, tm, D_MODEL), F32),
                        pltpu.VMEM((tm, D_MODEL), BF16),
                        pltpu.VMEM((tm, D_MODEL), BF16),
                        pltpu.VMEM((tm, D_FF), BF16),
                        pltpu.VMEM((MLSTM_HEADS, MLSTM_HEAD_DIM, 2 * MLSTM_HEAD_DIM), F32),
                        pltpu.VMEM((SUBLANES, LANES), F32),
                        pltpu.VMEM((SUBLANES, LANES), F32)],
        compiler_params=pltpu.CompilerParams(dimension_semantics=("arbitrary",),
                                             vmem_limit_bytes=_vmem_limit(est)),
        name="back",
    )(sinks, qa, kv, kv, qm, kt, vm, om, gates, bif, norm_w, x2, ga, gm, mod3,
      g_post_mix, g_pre_ffn, mod3, g_post_ffn, wa, wb, wo, wg, wu, wd)


def kernel(x, c, positions, w_ada, b_ada, g_pre_mix, g_post_mix, w_in, b_if, conv_w, conv_b,
           attn_sinks, mlstm_norm_w, w_branch_attn, w_branch_mlstm, w_out, g_pre_ffn, g_post_ffn,
           w_ffn_gate, w_ffn_up, w_ffn_down):
    batch, seq, d = x.shape
    depth = w_in.shape[0]
    assert d == D_MODEL and seq % TOKEN_TILE == 0 and (batch * seq) % ROPE_TILE == 0
    assert D_FF % FF_CHUNK == 0
    t = batch * seq
    tiles_per_batch = seq // TOKEN_TILE
    x2 = x.reshape(t, d)

    inv_freq = (ROPE_THETA ** (-2.0 * jnp.arange(HEAD_DIM // 2, dtype=F32) / HEAD_DIM)).reshape(-1, 1)
    c_pad = jnp.pad(c, ((0, SUBLANES - batch % SUBLANES), (0, 0))) if batch % SUBLANES else c

    for l in range(depth):
        w_t = w_in[l].T
        cos_t, sin_t, mod, w_q, w_mid = _prep(positions.reshape(1, t), inv_freq, c_pad, w_ada[l],
                                              b_ada[l].reshape(1, -1), w_t)
        mod3 = mod[:batch].reshape(batch, 6, d)
        back_weights = (w_branch_attn[l], w_branch_mlstm[l], w_out[l], w_ffn_gate[l], w_ffn_up[l],
                        w_ffn_down[l])
        qa, kv, qm, kt, vm, om, gates, ga, gm, wa, wb, wo, wg, wu, wd = _inproj(
            x2, mod3, g_pre_mix[l].reshape(1, d), cos_t, sin_t, w_q, w_mid, w_t,
            conv_w[l], conv_b[l].reshape(1, -1), back_weights, tiles_per_batch)
        bif = jnp.pad(b_if[l], (0, LANES - 2 * MLSTM_HEADS)).reshape(1, LANES)
        x2 = _back(attn_sinks[l], qa, kv, qm, kt, vm, om, gates, bif, mlstm_norm_w[l].reshape(1, -1),
                   x2, ga, gm, mod3, g_post_mix[l].reshape(1, d), g_pre_ffn[l].reshape(1, d),
                   g_post_ffn[l].reshape(1, d), wa, wb, wo, wg, wu, wd, tiles_per_batch)
    return x2.reshape(batch, seq, d)
```

```python
import functools

import numpy as np
import jax
import jax.numpy as jnp
from jax import lax
from jax.experimental import pallas as pl
from jax.experimental.pallas import tpu as pltpu

F32 = jnp.float32
BF16 = jnp.bfloat16

D_MODEL = 1024
N_Q_HEADS = 8
N_KV_HEADS = 2
HEAD_DIM = 64
ROPE_THETA = 10000.0
MLSTM_HEADS = 4
MLSTM_HEAD_DIM = 128
CONV_WIDTH = 4
D_FF = 2816
NORM_EPS = 1e-6
ATTN_Q_WIDTH = N_Q_HEADS * HEAD_DIM
ATTN_KV_WIDTH = N_KV_HEADS * HEAD_DIM
MLSTM_WIDTH = MLSTM_HEADS * MLSTM_HEAD_DIM
ATTN_BLOCK = 128

LANES = 128
SUBLANES = 8
MXU_WIDTH = 256
V7X_SCOPED_VMEM_BYTES = 60000 * 1024

TOKEN_TILE = 512
MLSTM_CHUNK = 128
FF_CHUNK = LANES
ROPE_TILE = 4096

C_QA = 0
C_KV = C_QA + ATTN_Q_WIDTH
C_QK = C_KV + 2 * ATTN_KV_WIDTH
C_VM = C_QK + 2 * MLSTM_WIDTH
C_OM = C_VM + MLSTM_WIDTH
C_IF = C_OM + MLSTM_WIDTH
C_GA = C_IF + LANES
C_GM = C_GA + D_MODEL
IN_COLS = C_GM + D_MODEL


def _vmem_limit(estimate_bytes):
    return int(min(V7X_SCOPED_VMEM_BYTES, max(estimate_bytes, 16 * 1024 * 1024)))


def _rms(x, g):
    return (x * lax.rsqrt(jnp.mean(x * x, axis=-1, keepdims=True) + NORM_EPS)) * g


def _prep_kernel(pos_ref, freq_ref, c_ref, w_ref, b_ref, wq_f32, wmid_f32,
                 cos_ref, sin_ref, mod_ref, wq_ref, wmid_ref):
    wq_ref[...] = wq_f32[...].astype(BF16)
    wmid_ref[...] = wmid_f32[...].astype(BF16)
    mod_ref[...] = jnp.dot(c_ref[...].astype(BF16), w_ref[...].astype(BF16),
                           preferred_element_type=F32) + b_ref[...]
    ang = freq_ref[...] * pos_ref[...].astype(F32)
    c = jnp.cos(ang)
    s = jnp.sin(ang)
    cos_ref[...] = jnp.concatenate([c, c, c, c], axis=0).T
    sin_ref[...] = jnp.concatenate([-s, s, -s, s], axis=0).T


def _q_source_block(out_block, blocks_per_head):
    per_kv = N_Q_HEADS // N_KV_HEADS
    head = out_block // blocks_per_head
    src_head = (head % N_KV_HEADS) * per_kv + head // N_KV_HEADS
    return src_head * blocks_per_head + out_block % blocks_per_head


def _prep(pos_row, inv_freq, c_pad, w_ada, b_ada, w_t):
    t = pos_row.shape[1]
    half = inv_freq.shape[0]
    rows = c_pad.shape[0]
    n = w_ada.shape[1]
    steps = t // ROPE_TILE
    cols = n // steps
    q_rows = ATTN_Q_WIDTH // steps
    mid_rows = (C_IF - C_KV) // steps
    assert n % steps == 0 and cols % LANES == 0
    assert ATTN_Q_WIDTH % steps == 0 and HEAD_DIM % q_rows == 0 and q_rows % (2 * SUBLANES) == 0
    assert (C_IF - C_KV) % steps == 0 and mid_rows % (2 * SUBLANES) == 0
    return pl.pallas_call(
        _prep_kernel,
        grid=(steps,),
        in_specs=[pl.BlockSpec((1, ROPE_TILE), lambda i: (0, i)),
                  pl.BlockSpec((half, 1), lambda i: (0, 0)),
                  pl.BlockSpec((rows, D_MODEL), lambda i: (0, 0)),
                  pl.BlockSpec((D_MODEL, cols), lambda i: (0, i)),
                  pl.BlockSpec((1, cols), lambda i: (0, i)),
                  pl.BlockSpec((q_rows, D_MODEL), lambda i: (_q_source_block(i, HEAD_DIM // q_rows), 0)),
                  pl.BlockSpec((pl.Element(mid_rows), pl.Element(D_MODEL)),
                               lambda i: (pl.multiple_of(C_KV + i * mid_rows, 2 * SUBLANES), 0))],
        out_specs=[pl.BlockSpec((ROPE_TILE, LANES), lambda i: (i, 0)),
                   pl.BlockSpec((ROPE_TILE, LANES), lambda i: (i, 0)),
                   pl.BlockSpec((rows, cols), lambda i: (0, i)),
                   pl.BlockSpec((q_rows, D_MODEL), lambda i: (i, 0)),
                   pl.BlockSpec((mid_rows, D_MODEL), lambda i: (i, 0))],
        out_shape=[jax.ShapeDtypeStruct((t, LANES), F32),
                   jax.ShapeDtypeStruct((t, LANES), F32),
                   jax.ShapeDtypeStruct((rows, n), F32),
                   jax.ShapeDtypeStruct((ATTN_Q_WIDTH, D_MODEL), BF16),
                   jax.ShapeDtypeStruct((C_IF - C_KV, D_MODEL), BF16)],
        name="prep",
    )(pos_row, inv_freq, c_pad, w_ada, b_ada, w_t, w_t)


def _modulated_norm(x, g, mod_ref, row):
    return (_rms(x, g) * (1.0 + mod_ref[0, row + 1:row + 2, :]) + mod_ref[0, row:row + 1, :]).astype(BF16)


def _inproj_kernel(n_tiles, tiles_per_batch, x0_ref, mod0_ref, xn_ref, modn_ref, g_ref, cos_ref, sin_ref,
                   wq_ref, wmid_ref, wtail_ref, cw_ref, cb_ref, *rest):
    n_cast = len(BACK_WEIGHT_BLOCKS)
    cast_in, rest = rest[:n_cast], rest[n_cast:]
    (qa_ref, kv_ref, qm_ref, kt_ref, vm_ref, om_ref, if_ref, ga_ref, gm_ref), rest = rest[:9], rest[9:]
    cast_out, (s_ref, raw_s, rot_s, wtail_s) = rest[:n_cast], rest[n_cast:]
    for src, dst in zip(cast_in, cast_out):
        dst[...] = src[...].astype(BF16)
    _inproj_body(n_tiles, tiles_per_batch, x0_ref, mod0_ref, xn_ref, modn_ref, g_ref, cos_ref, sin_ref,
                 wq_ref, wmid_ref, wtail_ref, cw_ref, cb_ref,
                 qa_ref, kv_ref, qm_ref, kt_ref, vm_ref, om_ref, if_ref, ga_ref, gm_ref,
                 s_ref, raw_s, rot_s, wtail_s)


BACK_WEIGHT_BLOCKS = (16, 16, 16, 16, 16, 64)


def _inproj_body(n_tiles, tiles_per_batch, x0_ref, mod0_ref, xn_ref, modn_ref, g_ref, cos_ref, sin_ref,
                 wq_ref, wmid_ref, wtail_ref, cw_ref, cb_ref,
                 qa_ref, kv_ref, qm_ref, kt_ref, vm_ref, om_ref, if_ref, ga_ref, gm_ref,
                 s_ref, raw_s, rot_s, wtail_s):
    tm = TOKEN_TILE
    i = pl.program_id(0)
    par = i % 2
    post_tile = jnp.maximum(i - 1, 0)
    h_cur = s_ref.at[par]
    h_next = s_ref.at[1 - par]
    out_a = s_ref.at[2 + par]
    out_b = s_ref.at[4 + par]
    kv_lo = ATTN_Q_WIDTH

    @pl.when(i == 0)
    def _():
        s_ref[0] = _modulated_norm(x0_ref[...], g_ref[...], mod0_ref, 0)
        raw_s[...] = jnp.zeros_like(raw_s)
        rot_s[...] = jnp.zeros_like(rot_s)
        n_if = 2 * MLSTM_HEADS
        wtail_s[0:LANES, :] = jnp.concatenate(
            [wtail_ref[0:n_if, :], jnp.zeros((LANES - n_if, D_MODEL), F32)], axis=0).astype(BF16)
        for r0 in range(0, 2 * D_MODEL, MXU_WIDTH):
            wtail_s[LANES + r0:LANES + r0 + MXU_WIDTH, :] = (
                wtail_ref[n_if + r0:n_if + r0 + MXU_WIDTH, :].astype(BF16))

    @pl.when(post_tile % tiles_per_batch == 0)
    def _():
        raw_s[0:SUBLANES, :] = jnp.zeros((SUBLANES, raw_s.shape[1]), F32)

    def proj(lo, width):
        if lo < C_KV:
            w = wq_ref[lo:lo + width, :]
        elif lo < C_IF:
            w = wmid_ref[lo - C_KV:lo - C_KV + width, :]
        else:
            w = wtail_s[lo - C_IF:lo - C_IF + width, :]
        return lax.dot_general(h_cur[...], w, (((1,), (1,)), ((), ())), preferred_element_type=F32)

    lane = lax.broadcasted_iota(jnp.int32, (1, LANES), 1)
    first_half = (lane % HEAD_DIM) < (HEAD_DIM // 2)

    def rope(t):
        swapped = jnp.where(first_half, pltpu.roll(t, LANES - HEAD_DIM // 2, axis=1),
                            pltpu.roll(t, HEAD_DIM // 2, axis=1))
        return t * cos_ref[...] + swapped * sin_ref[...]

    def post_q():
        for p in range(ATTN_Q_WIDTH // LANES):
            sl = slice(p * LANES, (p + 1) * LANES)
            out_a[:, sl] = (rope(rot_s[:, sl]) * (HEAD_DIM ** -0.5)).astype(BF16)

    def post_kv():
        out_a[:, kv_lo:kv_lo + LANES] = rope(rot_s[:, kv_lo:kv_lo + LANES]).astype(BF16)
        out_a[:, kv_lo + LANES:kv_lo + 2 * LANES] = rot_s[:, kv_lo + LANES:].astype(BF16)

    blk = MXU_WIDTH

    def post_conv(cbk):
        cs = slice(cbk * blk, (cbk + 1) * blk)
        acc = cb_ref[:, cs]
        for sh in range(CONV_WIDTH):
            acc = acc + (cw_ref[CONV_WIDTH - 1 - sh:CONV_WIDTH - sh, cs]
                         * raw_s[SUBLANES - sh:SUBLANES - sh + tm, cs])
        raw_s[0:SUBLANES, cs] = raw_s[tm:tm + SUBLANES, cs]
        a = acc * jax.nn.sigmoid(acc)
        if cbk * blk < MLSTM_WIDTH:
            out_b[:, cs] = a.astype(BF16)
        else:
            a = a * (MLSTM_HEAD_DIM ** -0.5)
            for d0 in range(0, blk, LANES):
                r0 = cbk * blk - MLSTM_WIDTH + d0
                for t0 in range(0, tm, LANES):
                    out_b[r0:r0 + LANES, MLSTM_WIDTH + t0:MLSTM_WIDTH + t0 + LANES] = (
                        a[t0:t0 + LANES, d0:d0 + LANES].T.astype(BF16))

    def ship_q():
        qa_ref[...] = out_a[:, :ATTN_Q_WIDTH]

    def ship_kv():
        kv_ref[...] = out_a[:, kv_lo:kv_lo + 2 * ATTN_KV_WIDTH]

    def ship_conv(cbk):
        cs = slice(cbk * blk, (cbk + 1) * blk)
        if cbk * blk < MLSTM_WIDTH:
            qm_ref[:, cs] = out_b[:, cs]
        else:
            rs = slice(cbk * blk - MLSTM_WIDTH, (cbk + 1) * blk - MLSTM_WIDTH)
            kt_ref[rs, :] = out_b[rs, MLSTM_WIDTH:MLSTM_WIDTH + tm]

    def norm_next(part):
        rows = slice(part * tm // INPROJ_NORM_PARTS, (part + 1) * tm // INPROJ_NORM_PARTS)
        h_next[rows, :] = _modulated_norm(xn_ref[rows, :], g_ref[...], modn_ref, 0)

    def main_rot(lo_w, lo_s, width):
        rot_s[:, lo_s:lo_s + width] = proj(lo_w, width)

    def main_conv(cbk):
        raw_s[SUBLANES:, cbk * blk:(cbk + 1) * blk] = proj(C_QK + cbk * blk, blk)

    def main_direct(out_ref, lo_w, lo, width):
        out_ref[:, lo:lo + width] = proj(lo_w + lo, width).astype(out_ref.dtype)

    half = D_MODEL // 2
    main_direct(vm_ref, C_VM, 0, MLSTM_WIDTH)
    post_q()
    main_rot(C_QA, 0, ATTN_Q_WIDTH)
    ship_q()
    post_kv()
    post_conv(0)
    main_rot(C_KV, kv_lo, 2 * ATTN_KV_WIDTH)
    ship_kv()
    main_conv(0)
    ship_conv(0)
    post_conv(1)
    main_direct(om_ref, C_OM, 0, MLSTM_WIDTH)
    main_conv(1)
    ship_conv(1)
    post_conv(2)
    main_direct(if_ref, C_IF, 0, LANES)
    main_conv(2)
    ship_conv(2)
    post_conv(3)
    main_direct(ga_ref, C_GA, 0, half)
    main_conv(3)
    ship_conv(3)
    norm_next(0)
    main_direct(ga_ref, C_GA, half, half)
    norm_next(1)
    main_direct(gm_ref, C_GM, 0, half)
    norm_next(2)
    norm_next(3)
    main_direct(gm_ref, C_GM, half, half)


INPROJ_NORM_PARTS = 4


def _inproj(x2, mod3, g_pre, cos_t, sin_t, w_q, w_mid, w_t, conv_w, conv_b, back_weights, tiles_per_batch):
    t = x2.shape[0]
    tm = TOKEN_TILE
    n = t // tm
    main = lambda i: (jnp.minimum(i, n - 1), 0)
    post = lambda i: (jnp.maximum(i - 1, 0), 0)
    nxt = lambda i: (jnp.minimum(i + 1, n - 1), 0)
    const = lambda i: (0, 0)
    once = pl.Buffered(1)
    outs = [(ATTN_Q_WIDTH, BF16, post), (2 * ATTN_KV_WIDTH, BF16, post), (MLSTM_WIDTH, BF16, post),
            (None, BF16, None), (MLSTM_WIDTH, BF16, main), (MLSTM_WIDTH, BF16, main),
            (LANES, F32, main), (D_MODEL, BF16, main), (D_MODEL, BF16, main)]
    out_specs = [pl.BlockSpec((MLSTM_WIDTH, tm), lambda i: (0, jnp.maximum(i - 1, 0))) if w is None
                 else pl.BlockSpec((tm, w), m) for w, _, m in outs]
    out_shape = [jax.ShapeDtypeStruct((MLSTM_WIDTH, t) if w is None else (t, w), d) for w, d, _ in outs]
    out_bytes = sum((w or MLSTM_WIDTH) * np.dtype(d).itemsize for w, d, _ in outs) * tm
    rot_cols = ATTN_Q_WIDTH + 2 * ATTN_KV_WIDTH
    assert tm == MLSTM_WIDTH
    scratch_bytes = (6 * tm * D_MODEL * 2 + (tm + SUBLANES) * 2 * MLSTM_WIDTH * 4 + tm * rot_cols * 4)
    tail_rows = w_t.shape[0] - C_IF
    tail_aligned = LANES + 2 * D_MODEL
    assert tail_rows == 2 * MLSTM_HEADS + 2 * D_MODEL
    weight_bytes = (w_q.size + w_mid.size + tail_aligned * D_MODEL) * 2 + tail_rows * D_MODEL * 4
    cast_in, cast_out, cast_bytes = [], [], 0
    for k, (w, rows) in enumerate(zip(back_weights, BACK_WEIGHT_BLOCKS)):
        blocks = w.shape[0] // rows
        assert w.shape[0] % rows == 0 and blocks <= n and rows % (2 * SUBLANES) == 0
        dst = lambda i, blocks=blocks: (jnp.minimum(i, blocks - 1), 0)
        src = dst if k else (lambda i, blocks=blocks, per_head=HEAD_DIM // rows:
                             (_q_source_block(jnp.minimum(i, blocks - 1), per_head), 0))
        cast_in.append(pl.BlockSpec((rows, w.shape[1]), src))
        cast_out.append(pl.BlockSpec((rows, w.shape[1]), dst))
        out_shape.append(jax.ShapeDtypeStruct(w.shape, BF16))
        cast_bytes += 2 * rows * w.shape[1] * 6
    est = (weight_bytes + 3 * tm * D_MODEL * 4 + 2 * out_bytes + 4 * tm * LANES * 4
           + scratch_bytes + 4 * tm * D_MODEL * 4 + cast_bytes)
    return pl.pallas_call(
        functools.partial(_inproj_kernel, n, tiles_per_batch),
        grid=(n + 1,),
        in_specs=[pl.BlockSpec((tm, D_MODEL), const, pipeline_mode=once),
                  pl.BlockSpec((1, 6, D_MODEL), lambda i: (0, 0, 0)),
                  pl.BlockSpec((tm, D_MODEL), nxt),
                  pl.BlockSpec((1, 6, D_MODEL),
                               lambda i: (jnp.minimum(i + 1, n - 1) // tiles_per_batch, 0, 0)),
                  pl.BlockSpec((1, D_MODEL), const),
                  pl.BlockSpec((tm, LANES), post),
                  pl.BlockSpec((tm, LANES), post),
                  pl.BlockSpec(w_q.shape, const, pipeline_mode=once),
                  pl.BlockSpec(w_mid.shape, const, pipeline_mode=once),
                  pl.BlockSpec((pl.Element(tail_rows), pl.Element(D_MODEL)), lambda i: (C_IF, 0),
                               pipeline_mode=once),
                  pl.BlockSpec((CONV_WIDTH, 2 * MLSTM_WIDTH), const),
                  pl.BlockSpec((1, 2 * MLSTM_WIDTH), const)] + cast_in,
        out_specs=out_specs + cast_out,
        out_shape=out_shape,
        scratch_shapes=[pltpu.VMEM((6, tm, D_MODEL), BF16),
                        pltpu.VMEM((tm + SUBLANES, 2 * MLSTM_WIDTH), F32),
                        pltpu.VMEM((tm, rot_cols), F32),
                        pltpu.VMEM((tail_aligned, D_MODEL), BF16)],
        compiler_params=pltpu.CompilerParams(dimension_semantics=("arbitrary",),
                                             vmem_limit_bytes=_vmem_limit(est)),
        name="inproj",
    )(x2, mod3, x2, mod3, g_pre, cos_t, sin_t, w_q, w_mid, w_t, conv_w, conv_b, *back_weights)


def _attn_thread(j, sink_ref, q_ref, kv_ref, kvp_ref, o_ref):
    blk = ATTN_BLOCK
    nblk = TOKEN_TILE // blk
    ngrp = ATTN_Q_WIDTH // LANES
    lane = lax.broadcasted_iota(jnp.int32, (1, LANES), 1)
    low = lane < HEAD_DIM
    qi = lax.broadcasted_iota(jnp.int32, (blk, 2 * blk), 0)
    kj = lax.broadcasted_iota(jnp.int32, (blk, 2 * blk), 1)
    in_cur = (kj >= blk) & (kj - blk <= qi)
    in_prev = (kj < blk) & (kj > qi)
    in_prev_first = (kj < blk) & (kj > qi + jnp.where(j > 0, 0, blk))
    ones = jnp.ones((2 * blk, LANES), F32)
    zeros = jnp.zeros((2 * blk, LANES), F32)
    one_lo = jnp.where(low, ones, zeros).astype(BF16)
    one_hi = jnp.where(low, zeros, ones).astype(BF16)
    band = {}
    live = {}

    def prep(i):
        cur = kv_ref[i * blk:(i + 1) * blk, :]
        prev = kvp_ref[...] if i == 0 else kv_ref[(i - 1) * blk:i * blk, :]
        kv = jnp.concatenate([prev, cur], axis=0).astype(F32)
        k = kv[:, :LANES]
        v = kv[:, LANES:]
        k_cat = jnp.concatenate([jnp.where(low, k, 0.0), jnp.where(low, 0.0, k)], axis=0).astype(BF16)
        v_cat = jnp.concatenate(
            [jnp.concatenate([jnp.where(low, v, 0.0).astype(BF16), one_lo], axis=1),
             jnp.concatenate([jnp.where(low, 0.0, v).astype(BF16), one_hi], axis=1)], axis=0)
        band[i] = (k_cat, v_cat, in_cur | (in_prev_first if i == 0 else in_prev))

    def scores(u):
        i, p = divmod(u, ngrp)
        q = q_ref[i * blk:(i + 1) * blk, p * LANES:(p + 1) * LANES]
        live[u] = lax.dot_general(q, band[i][0], (((1,), (1,)), ((), ())), preferred_element_type=F32)

    def softmax(u):
        i, p = divmod(u, ngrp)
        s = live[u]
        mask = band[i][2]
        s0 = jnp.where(mask, s[:, :2 * blk], -jnp.inf)
        s1 = jnp.where(mask, s[:, 2 * blk:], -jnp.inf)
        m0 = jnp.maximum(jnp.max(s0, axis=-1, keepdims=True), sink_ref[p])
        m1 = jnp.maximum(jnp.max(s1, axis=-1, keepdims=True), sink_ref[p + N_Q_HEADS // N_KV_HEADS])
        pr = jnp.concatenate([jnp.exp(s0 - m0), jnp.exp(s1 - m1)], axis=1).astype(BF16)
        live[u] = (pr, m0, m1)

    def values(u):
        i, p = divmod(u, ngrp)
        pr, m0, m1 = live.pop(u)
        r = jnp.dot(pr, band[i][1], preferred_element_type=F32)
        den = r[:, LANES:] + jnp.where(low, jnp.exp(sink_ref[p] - m0),
                                       jnp.exp(sink_ref[p + N_Q_HEADS // N_KV_HEADS] - m1))
        o_ref[i * blk:(i + 1) * blk, p * LANES:(p + 1) * LANES] = (r[:, :LANES] / den).astype(BF16)

    n = nblk * ngrp
    prep(0)
    for k in range(n + 2):
        if k < n:
            scores(k)
            if k + 1 < n and (k + 1) % ngrp == 0:
                prep((k + 1) // ngrp)
        if 0 <= k - 2 < n:
            values(k - 2)
        if 0 <= k - 1 < n:
            softmax(k - 1)
        yield


ATTN_STAGES = (TOKEN_TILE // ATTN_BLOCK) * (ATTN_Q_WIDTH // LANES) + 2


def _mlstm_thread(q_ref, kt_ref, v_ref, om_ref, if_ref, bif_ref, nw_ref, y_ref,
                  cn_ref, mrow_ref, mcol_ref):
    ts = TOKEN_TILE
    L = MLSTM_CHUNK
    D = MLSTM_HEAD_DIM
    nch = ts // L

    gates = if_ref[...] + bif_ref[...]
    logf_all = jax.nn.log_sigmoid(pltpu.roll(gates, LANES - MLSTM_HEADS, axis=1))

    ti = lax.broadcasted_iota(jnp.int32, (L, L), 0)
    si = lax.broadcasted_iota(jnp.int32, (L, L), 1)
    causal = si <= ti
    tri = causal.astype(BF16)
    ones_v = jnp.ones((L, LANES), BF16)
    chunk = {}
    head = {}

    def rows_of(c):
        return slice(c * L, (c + 1) * L)

    def p1(c):
        lf = logf_all[rows_of(c)]
        lf_hi = lf.astype(BF16)
        rem = lf - lf_hi.astype(F32)
        lf_mid = rem.astype(BF16)
        lf_lo = (rem - lf_mid.astype(F32)).astype(BF16)
        chunk[c] = (jnp.dot(tri, lf_hi, preferred_element_type=F32)
                    + jnp.dot(tri, lf_mid, preferred_element_type=F32)
                    + jnp.dot(tri, lf_lo, preferred_element_type=F32))

    def p2(c):
        b = chunk[c]
        r = b - gates[rows_of(c)]
        pm = -r
        sh = 1
        while sh < L:
            pm = jnp.maximum(pm, jnp.where(ti >= sh, pltpu.roll(pm, sh, axis=0), -jnp.inf))
            sh *= 2
        m_row = mrow_ref[0:1, :]
        inter = b + m_row
        mt = jnp.maximum(inter, b + pm)
        b_last = b[L - 1:L, :]
        m_new_row = jnp.maximum(b_last + m_row, jnp.max(b_last - r, axis=0, keepdims=True))
        mrow_ref[...] = jnp.broadcast_to(m_new_row, mrow_ref.shape)
        b_t = b.T[:SUBLANES]
        r_t = r.T[:SUBLANES]
        m_col = mcol_ref[:, 0:1]
        bl_col = b_t[:, L - 1:L]
        a_t = bl_col - r_t
        m_new_col = jnp.maximum(bl_col + m_col, jnp.max(a_t, axis=1, keepdims=True))
        mcol_ref[...] = jnp.broadcast_to(m_new_col, mcol_ref.shape)
        chunk[c] = dict(e_col=b - mt, r_t=r_t, w_inter=jnp.exp(inter - mt), e_mt=jnp.exp(-mt),
                        decay_row=jnp.exp(b_last + m_row - m_new_row),
                        wk_t=jnp.exp(a_t - m_new_col))

    def ha(c, h):
        hs = slice(h * D, (h + 1) * D)
        qh = q_ref[rows_of(c), hs]
        kt = kt_ref[hs, rows_of(c)]
        head[c, h] = dict(qh=qh, kt=kt, s=jnp.dot(qh, kt, preferred_element_type=F32))

    def hb(c, h):
        st, ch = head[c, h], chunk[c]
        w_intra = jnp.where(causal, jnp.exp(ch["e_col"][:, h:h + 1] - ch["r_t"][h:h + 1, :]), 0.0)
        pr = (st.pop("s") * w_intra).astype(BF16)
        qw = (st.pop("qh").astype(F32) * ch["w_inter"][:, h:h + 1]).astype(BF16)
        st["lhs"] = jnp.concatenate([pr, qw], axis=1)
        st["kw"] = (st.pop("kt").astype(F32) * ch["wk_t"][h:h + 1, :]).astype(BF16)

    def hc(c, h):
        st = head[c, h]
        hs = slice(h * D, (h + 1) * D)
        v_ext = jnp.concatenate([v_ref[rows_of(c), hs], ones_v], axis=1)
        rhs = jnp.concatenate([v_ext, cn_ref[h].astype(BF16)], axis=0)
        st["nd"] = jnp.dot(st.pop("lhs"), rhs, preferred_element_type=F32)
        st["kv"] = jnp.dot(st.pop("kw"), v_ext, preferred_element_type=F32)

    def hd(c, h):
        st, ch = head.pop((c, h)), chunk[c]
        hs = slice(h * D, (h + 1) * D)
        nd = st["nd"]
        hh = nd[:, :D] / jnp.maximum(jnp.abs(nd[:, D:]), ch["e_mt"][:, h:h + 1])
        cn_ref[h] = ch["decay_row"][:, h:h + 1] * cn_ref[h] + st["kv"]
        mu = jnp.mean(hh, axis=-1, keepdims=True)
        xc = hh - mu
        var = jnp.mean(xc * xc, axis=-1, keepdims=True)
        yn = (xc * lax.rsqrt(var + NORM_EPS)) * nw_ref[:, hs]
        y_ref[rows_of(c), hs] = (jax.nn.sigmoid(om_ref[rows_of(c), hs].astype(F32)) * yn).astype(BF16)

    half = MLSTM_HEADS // 2
    for g in range(MLSTM_STAGES):
        for c in range(nch):
            o = g - MLSTM_CHUNK_SKEW * c
            if o == 0:
                p1(c)
            elif o == 1:
                for h in range(MLSTM_HEADS):
                    ha(c, h)
                p2(c)
            elif o == 2:
                for h in range(half):
                    hb(c, h)
            elif o == 3:
                for h in range(half):
                    hc(c, h)
                for h in range(half, MLSTM_HEADS):
                    hb(c, h)
            elif o == 4:
                for h in range(half, MLSTM_HEADS):
                    hc(c, h)
                for h in range(half):
                    hd(c, h)
            elif o == 5:
                for h in range(half, MLSTM_HEADS):
                    hd(c, h)
        yield


MLSTM_CHUNK_SKEW = 4
MLSTM_STAGES = MLSTM_CHUNK_SKEW * (TOKEN_TILE // MLSTM_CHUNK - 1) + 6


def _merge_thread(x_ref, ya_ref, ym_ref, ga_ref, gm_ref, mod_ref, gpm_ref, gqf_ref,
                  wa_ref, wb_ref, wo_ref, x1_ref, h2_ref):
    tm = TOKEN_TILE
    parts = [slice(r * tm // MERGE_ROW_PARTS, (r + 1) * tm // MERGE_ROW_PARTS)
             for r in range(MERGE_ROW_PARTS)]
    gate_m = mod_ref[0, 2:3, :]
    ba = jnp.dot(ya_ref[...], wa_ref[...], preferred_element_type=F32)
    yield
    bb = jnp.dot(ym_ref[...], wb_ref[...], preferred_element_type=F32)
    yield
    merged = []
    for rows in parts:
        merged.append((jax.nn.sigmoid(ga_ref[rows, :].astype(F32)) * ba[rows]
                       + jax.nn.sigmoid(gm_ref[rows, :].astype(F32)) * bb[rows]).astype(BF16))
        yield
    mix = jnp.dot(jnp.concatenate(merged, axis=0), wo_ref[...], preferred_element_type=F32)
    yield
    for rows in parts:
        x1 = x_ref[rows, :] + gate_m * _rms(mix[rows], gpm_ref[...])
        x1_ref[rows, :] = x1
        yield
        h2_ref[rows, :] = _modulated_norm(x1, gqf_ref[...], mod_ref, 3)
        yield


MERGE_ROW_PARTS = 4
MERGE_STAGES = 3 + 3 * MERGE_ROW_PARTS


def _ffn_thread(x1_ref, h2_ref, mod_ref, gpf_ref, wg_ref, wu_ref, wd_ref, o_ref, act_s, after_up):
    gate_f = mod_ref[0, 5:6, :]
    for c in range(D_FF // FF_CHUNK):
        cs = slice(c * FF_CHUNK, (c + 1) * FF_CHUNK)
        gu = jnp.dot(h2_ref[...], jnp.concatenate([wg_ref[:, cs], wu_ref[:, cs]], axis=1),
                     preferred_element_type=F32)
        g = gu[:, :FF_CHUNK]
        act_s[:, c * FF_CHUNK:(c + 1) * FF_CHUNK] = ((g * jax.nn.sigmoid(g)) * gu[:, FF_CHUNK:]).astype(BF16)
        yield
    after_up()
    ff = []
    for c in range(FFN_DOWN_SPLIT):
        cs = slice(c * D_MODEL // FFN_DOWN_SPLIT, (c + 1) * D_MODEL // FFN_DOWN_SPLIT)
        ff.append(jnp.dot(act_s[...], wd_ref[:, cs], preferred_element_type=F32))
        yield
    o_ref[...] = x1_ref[...] + gate_f * _rms(jnp.concatenate(ff, axis=1), gpf_ref[...])
    yield


FFN_DOWN_SPLIT = 2
FFN_UP_STAGES = D_FF // FF_CHUNK
FFN_STAGES = FFN_UP_STAGES + FFN_DOWN_SPLIT + 1


def _interleave(main, n_main, fills):
    done = [0] * len(fills)
    for k in range(n_main):
        next(main)
        for f, (gen, count, first, last) in enumerate(fills):
            span = last - first + 1
            want = 0 if k < first else min(count, ((k - first + 1) * count + span - 1) // span)
            while done[f] < want:
                next(gen)
                done[f] += 1
    assert all(d == f[1] for d, f in zip(done, fills))
    assert next(main, None) is None and all(next(f[0], None) is None for f in fills)


def _back_kernel(n_tiles, tiles_per_batch, sink_ref,
                 qa_ref, kv_ref, kvp_ref, qm_ref, kt_ref, vm_ref, om_ref, if_ref, bif_ref, nw_ref,
                 x_ref, ga_ref, gm_ref, modm_ref, gpm_ref, gqf_ref,
                 modf_ref, gpf_ref,
                 wa_ref, wb_ref, wo_ref, wg_ref, wu_ref, wd_ref, o_ref,
                 ya_s, ym_s, x1_s, h2_cur, h2_next, act_s, cn_ref, mrow_ref, mcol_ref):
    i = pl.program_id(0)
    j = jnp.minimum(i, n_tiles - 1) % tiles_per_batch
    slot = i % 2

    @pl.when(i == 0)
    def _():
        ya_s[...] = jnp.zeros_like(ya_s)
        ym_s[...] = jnp.zeros_like(ym_s)
        x1_s[...] = jnp.zeros_like(x1_s)
        h2_cur[...] = jnp.zeros_like(h2_cur)

    @pl.when(j == 0)
    def _():
        cn_ref[...] = jnp.zeros_like(cn_ref)
        mrow_ref[...] = jnp.zeros_like(mrow_ref)
        mcol_ref[...] = jnp.zeros_like(mcol_ref)

    def hand_over():
        h2_cur[...] = h2_next[...]

    ffn = _ffn_thread(x1_s.at[slot], h2_cur, modf_ref, gpf_ref, wg_ref, wu_ref, wd_ref, o_ref, act_s,
                      hand_over)
    merge = _merge_thread(x_ref, ya_s, ym_s, ga_ref, gm_ref, modm_ref, gpm_ref, gqf_ref,
                          wa_ref, wb_ref, wo_ref, x1_s.at[1 - slot], h2_next)
    attn = _attn_thread(j, sink_ref, qa_ref, kv_ref, kvp_ref, ya_s)
    mlstm = _mlstm_thread(qm_ref, kt_ref, vm_ref, om_ref, if_ref, bif_ref, nw_ref, ym_s,
                          cn_ref, mrow_ref, mcol_ref)
    _interleave(ffn, FFN_STAGES,
                [(merge, MERGE_STAGES, 0, FFN_UP_STAGES - 2),
                 (attn, ATTN_STAGES, 2, FFN_STAGES - 2),
                 (mlstm, MLSTM_STAGES, 2, FFN_STAGES - 2)])


def _back(sinks, qa, kv, qm, kt, vm, om, gates, bif, norm_w, x2, ga, gm, mod3,
          g_post_mix, g_pre_ffn, g_post_ffn, wa, wb, wo, wg, wu, wd, tiles_per_batch):
    t = x2.shape[0]
    tm = TOKEN_TILE
    n = t // tm
    per_tile = tm // ATTN_BLOCK
    clamp = lambda v: jnp.clip(v, 0, n - 1)
    cur = lambda i: (clamp(i), 0)
    cur_t = lambda i: (0, clamp(i))

    def prev_block(i):
        tile = clamp(i)
        first = (tile // tiles_per_batch) * tiles_per_batch * per_tile
        return (jnp.maximum(tile * per_tile - 1, first), 0)

    mid = lambda i: (clamp(i - 1), 0)
    last = lambda i: (clamp(i - 2), 0)
    const = lambda i: (0, 0)
    once = pl.Buffered(1)
    weight_bytes = 2 * (wa.size + wb.size + wo.size + wg.size + wu.size + wd.size)
    mixer_in = tm * (ATTN_Q_WIDTH + 2 * ATTN_KV_WIDTH + 4 * MLSTM_WIDTH) * 2 + tm * LANES * 4
    scratch_bytes = tm * (D_FF + 2 * MLSTM_WIDTH + 2 * D_MODEL) * 2 + 2 * tm * D_MODEL * 4
    est = (weight_bytes + 2 * 2 * tm * D_MODEL * 4 + 2 * tm * 2 * D_MODEL * 2 + 2 * mixer_in
           + scratch_bytes + 6 * tm * D_MODEL * 4)
    return pl.pallas_call(
        functools.partial(_back_kernel, n, tiles_per_batch),
        grid=(n + 2,),
        in_specs=[pl.BlockSpec(memory_space=pltpu.SMEM),
                  pl.BlockSpec((tm, ATTN_Q_WIDTH), cur),
                  pl.BlockSpec((tm, 2 * ATTN_KV_WIDTH), cur),
                  pl.BlockSpec((ATTN_BLOCK, 2 * ATTN_KV_WIDTH), prev_block),
                  pl.BlockSpec((tm, MLSTM_WIDTH), cur),
                  pl.BlockSpec((MLSTM_WIDTH, tm), cur_t),
                  pl.BlockSpec((tm, MLSTM_WIDTH), cur),
                  pl.BlockSpec((tm, MLSTM_WIDTH), cur),
                  pl.BlockSpec((tm, LANES), cur),
                  pl.BlockSpec((1, LANES), const),
                  pl.BlockSpec((1, MLSTM_WIDTH), const),
                  pl.BlockSpec((tm, D_MODEL), mid),
                  pl.BlockSpec((tm, D_MODEL), mid),
                  pl.BlockSpec((tm, D_MODEL), mid),
                  pl.BlockSpec((1, 6, D_MODEL), lambda i: (clamp(i - 1) // tiles_per_batch, 0, 0)),
                  pl.BlockSpec((1, D_MODEL), const),
                  pl.BlockSpec((1, D_MODEL), const),
                  pl.BlockSpec((1, 6, D_MODEL), lambda i: (clamp(i - 2) // tiles_per_batch, 0, 0)),
                  pl.BlockSpec((1, D_MODEL), const),
                  pl.BlockSpec(wa.shape, const, pipeline_mode=once),
                  pl.BlockSpec(wb.shape, const, pipeline_mode=once),
                  pl.BlockSpec(wo.shape, const, pipeline_mode=once),
                  pl.BlockSpec(wg.shape, const, pipeline_mode=once),
                  pl.BlockSpec(wu.shape, const, pipeline_mode=once),
                  pl.BlockSpec(wd.shape, const, pipeline_mode=once)],
        out_specs=pl.BlockSpec((tm, D_MODEL), last),
        out_shape=jax.ShapeDtypeStruct((t, D_MODEL), F32),
        scratch_shapes=[pltpu.VMEM((tm, ATTN_Q_WIDTH), BF16),
                        pltpu.VMEM((tm, MLSTM_WIDTH), BF16),
                        pltpu.VMEM((2, tm, D_MODEL), F32),
                        pltpu.VMEM((tm, D_MODEL), BF16),
                        pltpu.VMEM((tm, D_MODEL), BF16),
                        pltpu.VMEM((tm, D_FF), BF16),
                        pltpu.VMEM((MLSTM_HEADS, MLSTM_HEAD_DIM, 2 * MLSTM_HEAD_DIM), F32),
                        pltpu.VMEM((SUBLANES, LANES), F32),
                        pltpu.VMEM((SUBLANES, LANES), F32)],
        compiler_params=pltpu.CompilerParams(dimension_semantics=("arbitrary",),
                                             vmem_limit_bytes=_vmem_limit(est)),
        name="back",
    )(sinks, qa, kv, kv, qm, kt, vm, om, gates, bif, norm_w, x2, ga, gm, mod3,
      g_post_mix, g_pre_ffn, mod3, g_post_ffn, wa, wb, wo, wg, wu, wd)


def kernel(x, c, positions, w_ada, b_ada, g_pre_mix, g_post_mix, w_in, b_if, conv_w, conv_b,
           attn_sinks, mlstm_norm_w, w_branch_attn, w_branch_mlstm, w_out, g_pre_ffn, g_post_ffn,
           w_ffn_gate, w_ffn_up, w_ffn_down):
    batch, seq, d = x.shape
    depth = w_in.shape[0]
    assert d == D_MODEL and seq % TOKEN_TILE == 0 and (batch * seq) % ROPE_TILE == 0
    assert D_FF % FF_CHUNK == 0
    t = batch * seq
    tiles_per_batch = seq // TOKEN_TILE
    x2 = x.reshape(t, d)

    inv_freq = (ROPE_THETA ** (-2.0 * jnp.arange(HEAD_DIM // 2, dtype=F32) / HEAD_DIM)).reshape(-1, 1)
    c_pad = jnp.pad(c, ((0, SUBLANES - batch % SUBLANES), (0, 0))) if batch % SUBLANES else c

    for l in range(depth):
        w_t = w_in[l].T
        cos_t, sin_t, mod, w_q, w_mid = _prep(positions.reshape(1, t), inv_freq, c_pad, w_ada[l],
                                              b_ada[l].reshape(1, -1), w_t)
        mod3 = mod[:batch].reshape(batch, 6, d)
        back_weights = (w_branch_attn[l], w_branch_mlstm[l], w_out[l], w_ffn_gate[l], w_ffn_up[l],
                        w_ffn_down[l])
        qa, kv, qm, kt, vm, om, gates, ga, gm, wa, wb, wo, wg, wu, wd = _inproj(
            x2, mod3, g_pre_mix[l].reshape(1, d), cos_t, sin_t, w_q, w_mid, w_t,
            conv_w[l], conv_b[l].reshape(1, -1), back_weights, tiles_per_batch)
        bif = jnp.pad(b_if[l], (0, LANES - 2 * MLSTM_HEADS)).reshape(1, LANES)
        x2 = _back(attn_sinks[l], qa, kv, qm, kt, vm, om, gates, bif, mlstm_norm_w[l].reshape(1, -1),
                   x2, ga, gm, mod3, g_post_mix[l].reshape(1, d), g_pre_ffn[l].reshape(1, d),
                   g_post_ffn[l].reshape(1, d), wa, wb, wo, wg, wu, wd, tiles_per_batch)
    return x2.reshape(batch, seq, d)
```

```python
import functools

import numpy as np
import jax
import jax.numpy as jnp
from jax import lax
from jax.experimental import pallas as pl
from jax.experimental.pallas import tpu as pltpu

F32 = jnp.float32
BF16 = jnp.bfloat16

D_MODEL = 1024
N_Q_HEADS = 8
N_KV_HEADS = 2
HEAD_DIM = 64
ROPE_THETA = 10000.0
MLSTM_HEADS = 4
MLSTM_HEAD_DIM = 128
CONV_WIDTH = 4
D_FF = 2816
NORM_EPS = 1e-6
ATTN_Q_WIDTH = N_Q_HEADS * HEAD_DIM
ATTN_KV_WIDTH = N_KV_HEADS * HEAD_DIM
MLSTM_WIDTH = MLSTM_HEADS * MLSTM_HEAD_DIM
ATTN_BLOCK = 128

LANES = 128
SUBLANES = 8
MXU_WIDTH = 256
V7X_SCOPED_VMEM_BYTES = 60000 * 1024

TOKEN_TILE = 512
MLSTM_CHUNK = 128
FF_CHUNK = LANES
ROPE_TILE = 4096

C_QA = 0
C_KV = C_QA + ATTN_Q_WIDTH
C_QK = C_KV + 2 * ATTN_KV_WIDTH
C_VM = C_QK + 2 * MLSTM_WIDTH
C_OM = C_VM + MLSTM_WIDTH
C_IF = C_OM + MLSTM_WIDTH
C_GA = C_IF + LANES
C_GM = C_GA + D_MODEL


def _vmem_limit(estimate_bytes):
    return int(min(V7X_SCOPED_VMEM_BYTES, max(estimate_bytes, 16 * 1024 * 1024)))


def _rms(x, g):
    return (x * lax.rsqrt(jnp.mean(x * x, axis=-1, keepdims=True) + NORM_EPS)) * g


def _prep_kernel(pos_ref, freq_ref, c_ref, w_ref, b_ref, wq_f32, wmid_f32,
                 cos_ref, sin_ref, mod_ref, wq_ref, wmid_ref):
    wq_ref[...] = wq_f32[...].astype(BF16)
    wmid_ref[...] = wmid_f32[...].astype(BF16)
    mod_ref[...] = jnp.dot(c_ref[...].astype(BF16), w_ref[...].astype(BF16),
                           preferred_element_type=F32) + b_ref[...]
    ang = freq_ref[...] * pos_ref[...].astype(F32)
    c = jnp.cos(ang)
    s = jnp.sin(ang)
    cos_ref[...] = jnp.concatenate([c, c, c, c], axis=0).T
    sin_ref[...] = jnp.concatenate([-s, s, -s, s], axis=0).T


def _q_source_block(out_block, blocks_per_head):
    per_kv = N_Q_HEADS // N_KV_HEADS
    head = out_block // blocks_per_head
    src_head = (head % N_KV_HEADS) * per_kv + head // N_KV_HEADS
    return src_head * blocks_per_head + out_block % blocks_per_head


def _prep(pos_row, inv_freq, c_pad, w_ada, b_ada, w_t):
    t = pos_row.shape[1]
    half = inv_freq.shape[0]
    rows = c_pad.shape[0]
    n = w_ada.shape[1]
    steps = t // ROPE_TILE
    cols = n // steps
    q_rows = ATTN_Q_WIDTH // steps
    mid_rows = (C_IF - C_KV) // steps
    assert n % steps == 0 and cols % LANES == 0
    assert ATTN_Q_WIDTH % steps == 0 and HEAD_DIM % q_rows == 0 and q_rows % (2 * SUBLANES) == 0
    assert (C_IF - C_KV) % steps == 0 and mid_rows % (2 * SUBLANES) == 0
    return pl.pallas_call(
        _prep_kernel,
        grid=(steps,),
        in_specs=[pl.BlockSpec((1, ROPE_TILE), lambda i: (0, i)),
                  pl.BlockSpec((half, 1), lambda i: (0, 0)),
                  pl.BlockSpec((rows, D_MODEL), lambda i: (0, 0)),
                  pl.BlockSpec((D_MODEL, cols), lambda i: (0, i)),
                  pl.BlockSpec((1, cols), lambda i: (0, i)),
                  pl.BlockSpec((q_rows, D_MODEL), lambda i: (_q_source_block(i, HEAD_DIM // q_rows), 0)),
                  pl.BlockSpec((pl.Element(mid_rows), pl.Element(D_MODEL)),
                               lambda i: (pl.multiple_of(C_KV + i * mid_rows, 2 * SUBLANES), 0))],
        out_specs=[pl.BlockSpec((ROPE_TILE, LANES), lambda i: (i, 0)),
                   pl.BlockSpec((ROPE_TILE, LANES), lambda i: (i, 0)),
                   pl.BlockSpec((rows, cols), lambda i: (0, i)),
                   pl.BlockSpec((q_rows, D_MODEL), lambda i: (i, 0)),
                   pl.BlockSpec((mid_rows, D_MODEL), lambda i: (i, 0))],
        out_shape=[jax.ShapeDtypeStruct((t, LANES), F32),
                   jax.ShapeDtypeStruct((t, LANES), F32),
                   jax.ShapeDtypeStruct((rows, n), F32),
                   jax.ShapeDtypeStruct((ATTN_Q_WIDTH, D_MODEL), BF16),
                   jax.ShapeDtypeStruct((C_IF - C_KV, D_MODEL), BF16)],
        name="prep",
    )(pos_row, inv_freq, c_pad, w_ada, b_ada, w_t, w_t)


def _modulated_norm(x, g, mod_ref, row):
    return (_rms(x, g) * (1.0 + mod_ref[0, row + 1:row + 2, :]) + mod_ref[0, row:row + 1, :]).astype(BF16)


def _inproj_kernel(n_tiles, tiles_per_batch, x0_ref, mod0_ref, xn_ref, modn_ref, g_ref, cos_ref, sin_ref,
                   wq_ref, wmid_ref, wtail_ref, cw_ref, cb_ref, *rest):
    n_cast = len(BACK_WEIGHT_BLOCKS)
    cast_in, rest = rest[:n_cast], rest[n_cast:]
    (qa_ref, kv_ref, qm_ref, kt_ref, vm_ref, om_ref, if_ref, ga_ref, gm_ref), rest = rest[:9], rest[9:]
    cast_out, (s_ref, raw_s, rot_s, wtail_s) = rest[:n_cast], rest[n_cast:]
    for src, dst in zip(cast_in, cast_out):
        dst[...] = src[...].astype(BF16)
    _inproj_body(n_tiles, tiles_per_batch, x0_ref, mod0_ref, xn_ref, modn_ref, g_ref, cos_ref, sin_ref,
                 wq_ref, wmid_ref, wtail_ref, cw_ref, cb_ref,
                 qa_ref, kv_ref, qm_ref, kt_ref, vm_ref, om_ref, if_ref, ga_ref, gm_ref,
                 s_ref, raw_s, rot_s, wtail_s)


BACK_WEIGHT_BLOCKS = (16, 16, 16, 16, 16, 64)


def _inproj_body(n_tiles, tiles_per_batch, x0_ref, mod0_ref, xn_ref, modn_ref, g_ref, cos_ref, sin_ref,
                 wq_ref, wmid_ref, wtail_ref, cw_ref, cb_ref,
                 qa_ref, kv_ref, qm_ref, kt_ref, vm_ref, om_ref, if_ref, ga_ref, gm_ref,
                 s_ref, raw_s, rot_s, wtail_s):
    tm = TOKEN_TILE
    i = pl.program_id(0)
    par = i % 2
    post_tile = jnp.maximum(i - 1, 0)
    h_cur = s_ref.at[par]
    h_next = s_ref.at[1 - par]
    out_a = s_ref.at[2 + par]
    out_b = s_ref.at[4 + par]
    kv_lo = ATTN_Q_WIDTH

    @pl.when(i == 0)
    def _():
        s_ref[0] = _modulated_norm(x0_ref[...], g_ref[...], mod0_ref, 0)
        raw_s[...] = jnp.zeros_like(raw_s)
        rot_s[...] = jnp.zeros_like(rot_s)
        n_if = 2 * MLSTM_HEADS
        wtail_s[0:LANES, :] = jnp.concatenate(
            [wtail_ref[0:n_if, :], jnp.zeros((LANES - n_if, D_MODEL), F32)], axis=0).astype(BF16)
        for r0 in range(0, 2 * D_MODEL, MXU_WIDTH):
            wtail_s[LANES + r0:LANES + r0 + MXU_WIDTH, :] = (
                wtail_ref[n_if + r0:n_if + r0 + MXU_WIDTH, :].astype(BF16))

    @pl.when(post_tile % tiles_per_batch == 0)
    def _():
        raw_s[0:SUBLANES, :] = jnp.zeros((SUBLANES, raw_s.shape[1]), F32)

    def proj(lo, width):
        if lo < C_KV:
            w = wq_ref[lo:lo + width, :]
        elif lo < C_IF:
            w = wmid_ref[lo - C_KV:lo - C_KV + width, :]
        else:
            w = wtail_s[lo - C_IF:lo - C_IF + width, :]
        return lax.dot_general(h_cur[...], w, (((1,), (1,)), ((), ())), preferred_element_type=F32)

    lane = lax.broadcasted_iota(jnp.int32, (1, LANES), 1)
    first_half = (lane % HEAD_DIM) < (HEAD_DIM // 2)

    def rope(t):
        swapped = jnp.where(first_half, pltpu.roll(t, LANES - HEAD_DIM // 2, axis=1),
                            pltpu.roll(t, HEAD_DIM // 2, axis=1))
        return t * cos_ref[...] + swapped * sin_ref[...]

    def post_q():
        for p in range(ATTN_Q_WIDTH // LANES):
            sl = slice(p * LANES, (p + 1) * LANES)
            out_a[:, sl] = (rope(rot_s[:, sl]) * (HEAD_DIM ** -0.5)).astype(BF16)

    def post_kv():
        out_a[:, kv_lo:kv_lo + LANES] = rope(rot_s[:, kv_lo:kv_lo + LANES]).astype(BF16)
        out_a[:, kv_lo + LANES:kv_lo + 2 * LANES] = rot_s[:, kv_lo + LANES:].astype(BF16)

    blk = MXU_WIDTH

    def post_conv(cbk):
        cs = slice(cbk * blk, (cbk + 1) * blk)
        acc = cb_ref[:, cs]
        for sh in range(CONV_WIDTH):
            acc = acc + (cw_ref[CONV_WIDTH - 1 - sh:CONV_WIDTH - sh, cs]
                         * raw_s[SUBLANES - sh:SUBLANES - sh + tm, cs])
        raw_s[0:SUBLANES, cs] = raw_s[tm:tm + SUBLANES, cs]
        a = acc * jax.nn.sigmoid(acc)
        if cbk * blk < MLSTM_WIDTH:
            out_b[:, cs] = a.astype(BF16)
        else:
            a = a * (MLSTM_HEAD_DIM ** -0.5)
            for d0 in range(0, blk, LANES):
                r0 = cbk * blk - MLSTM_WIDTH + d0
                for t0 in range(0, tm, LANES):
                    out_b[r0:r0 + LANES, MLSTM_WIDTH + t0:MLSTM_WIDTH + t0 + LANES] = (
                        a[t0:t0 + LANES, d0:d0 + LANES].T.astype(BF16))

    def ship_q():
        qa_ref[...] = out_a[:, :ATTN_Q_WIDTH]

    def ship_kv():
        kv_ref[...] = out_a[:, kv_lo:kv_lo + 2 * ATTN_KV_WIDTH]

    def ship_conv(cbk):
        cs = slice(cbk * blk, (cbk + 1) * blk)
        if cbk * blk < MLSTM_WIDTH:
            qm_ref[:, cs] = out_b[:, cs]
        else:
            rs = slice(cbk * blk - MLSTM_WIDTH, (cbk + 1) * blk - MLSTM_WIDTH)
            kt_ref[rs, :] = out_b[rs, MLSTM_WIDTH:MLSTM_WIDTH + tm]

    def norm_next(part):
        rows = slice(part * tm // INPROJ_NORM_PARTS, (part + 1) * tm // INPROJ_NORM_PARTS)
        h_next[rows, :] = _modulated_norm(xn_ref[rows, :], g_ref[...], modn_ref, 0)

    def main_rot(lo_w, lo_s, width):
        rot_s[:, lo_s:lo_s + width] = proj(lo_w, width)

    def main_conv(cbk):
        raw_s[SUBLANES:, cbk * blk:(cbk + 1) * blk] = proj(C_QK + cbk * blk, blk)

    def main_direct(out_ref, lo_w, lo, width):
        out_ref[:, lo:lo + width] = proj(lo_w + lo, width).astype(out_ref.dtype)

    half = D_MODEL // 2
    main_direct(vm_ref, C_VM, 0, MLSTM_WIDTH)
    post_q()
    main_rot(C_QA, 0, ATTN_Q_WIDTH)
    ship_q()
    post_kv()
    post_conv(0)
    main_rot(C_KV, kv_lo, 2 * ATTN_KV_WIDTH)
    ship_kv()
    main_conv(0)
    ship_conv(0)
    post_conv(1)
    main_direct(om_ref, C_OM, 0, MLSTM_WIDTH)
    main_conv(1)
    ship_conv(1)
    post_conv(2)
    main_direct(if_ref, C_IF, 0, LANES)
    main_conv(2)
    ship_conv(2)
    post_conv(3)
    main_direct(ga_ref, C_GA, 0, half)
    main_conv(3)
    ship_conv(3)
    norm_next(0)
    main_direct(ga_ref, C_GA, half, half)
    norm_next(1)
    main_direct(gm_ref, C_GM, 0, half)
    norm_next(2)
    norm_next(3)
    main_direct(gm_ref, C_GM, half, half)


INPROJ_NORM_PARTS = 4


def _inproj(x2, mod3, g_pre, cos_t, sin_t, w_q, w_mid, w_t, conv_w, conv_b, back_weights, tiles_per_batch):
    t = x2.shape[0]
    tm = TOKEN_TILE
    n = t // tm
    main = lambda i: (jnp.minimum(i, n - 1), 0)
    post = lambda i: (jnp.maximum(i - 1, 0), 0)
    nxt = lambda i: (jnp.minimum(i + 1, n - 1), 0)
    const = lambda i: (0, 0)
    once = pl.Buffered(1)
    outs = [(ATTN_Q_WIDTH, BF16, post), (2 * ATTN_KV_WIDTH, BF16, post), (MLSTM_WIDTH, BF16, post),
            (None, BF16, None), (MLSTM_WIDTH, BF16, main), (MLSTM_WIDTH, BF16, main),
            (LANES, F32, main), (D_MODEL, BF16, main), (D_MODEL, BF16, main)]
    out_specs = [pl.BlockSpec((MLSTM_WIDTH, tm), lambda i: (0, jnp.maximum(i - 1, 0))) if w is None
                 else pl.BlockSpec((tm, w), m) for w, _, m in outs]
    out_shape = [jax.ShapeDtypeStruct((MLSTM_WIDTH, t) if w is None else (t, w), d) for w, d, _ in outs]
    out_bytes = sum((w or MLSTM_WIDTH) * np.dtype(d).itemsize for w, d, _ in outs) * tm
    rot_cols = ATTN_Q_WIDTH + 2 * ATTN_KV_WIDTH
    assert tm == MLSTM_WIDTH
    scratch_bytes = (6 * tm * D_MODEL * 2 + (tm + SUBLANES) * 2 * MLSTM_WIDTH * 4 + tm * rot_cols * 4)
    tail_rows = w_t.shape[0] - C_IF
    tail_aligned = LANES + 2 * D_MODEL
    assert tail_rows == 2 * MLSTM_HEADS + 2 * D_MODEL
    weight_bytes = (w_q.size + w_mid.size + tail_aligned * D_MODEL) * 2 + tail_rows * D_MODEL * 4
    cast_in, cast_out, cast_bytes = [], [], 0
    for k, (w, rows) in enumerate(zip(back_weights, BACK_WEIGHT_BLOCKS)):
        blocks = w.shape[0] // rows
        assert w.shape[0] % rows == 0 and blocks <= n and rows % (2 * SUBLANES) == 0
        dst = lambda i, blocks=blocks: (jnp.minimum(i, blocks - 1), 0)
        src = dst if k else (lambda i, blocks=blocks, per_head=HEAD_DIM // rows:
                             (_q_source_block(jnp.minimum(i, blocks - 1), per_head), 0))
        cast_in.append(pl.BlockSpec((rows, w.shape[1]), src))
        cast_out.append(pl.BlockSpec((rows, w.shape[1]), dst))
        out_shape.append(jax.ShapeDtypeStruct(w.shape, BF16))
        cast_bytes += 2 * rows * w.shape[1] * 6
    est = (weight_bytes + 3 * tm * D_MODEL * 4 + 2 * out_bytes + 4 * tm * LANES * 4
           + scratch_bytes + 4 * tm * D_MODEL * 4 + cast_bytes)
    return pl.pallas_call(
        functools.partial(_inproj_kernel, n, tiles_per_batch),
        grid=(n + 1,),
        in_specs=[pl.BlockSpec((tm, D_MODEL), const, pipeline_mode=once),
                  pl.BlockSpec((1, 6, D_MODEL), lambda i: (0, 0, 0)),
                  pl.BlockSpec((tm, D_MODEL), nxt),
                  pl.BlockSpec((1, 6, D_MODEL),
                               lambda i: (jnp.minimum(i + 1, n - 1) // tiles_per_batch, 0, 0)),
                  pl.BlockSpec((1, D_MODEL), const),
                  pl.BlockSpec((tm, LANES), post),
                  pl.BlockSpec((tm, LANES), post),
                  pl.BlockSpec(w_q.shape, const, pipeline_mode=once),
                  pl.BlockSpec(w_mid.shape, const, pipeline_mode=once),
                  pl.BlockSpec((pl.Element(tail_rows), pl.Element(D_MODEL)), lambda i: (C_IF, 0),
                               pipeline_mode=once),
                  pl.BlockSpec((CONV_WIDTH, 2 * MLSTM_WIDTH), const),
                  pl.BlockSpec((1, 2 * MLSTM_WIDTH), const)] + cast_in,
        out_specs=out_specs + cast_out,
        out_shape=out_shape,
        scratch_shapes=[pltpu.VMEM((6, tm, D_MODEL), BF16),
                        pltpu.VMEM((tm + SUBLANES, 2 * MLSTM_WIDTH), F32),
                        pltpu.VMEM((tm, rot_cols), F32),
                        pltpu.VMEM((tail_aligned, D_MODEL), BF16)],
        compiler_params=pltpu.CompilerParams(dimension_semantics=("arbitrary",),
                                             vmem_limit_bytes=_vmem_limit(est)),
        name="inproj",
    )(x2, mod3, x2, mod3, g_pre, cos_t, sin_t, w_q, w_mid, w_t, conv_w, conv_b, *back_weights)


def _attn_thread(j, sink_ref, q_ref, kv_ref, kvp_ref, o_ref):
    blk = ATTN_BLOCK
    nblk = TOKEN_TILE // blk
    ngrp = ATTN_Q_WIDTH // LANES
    lane = lax.broadcasted_iota(jnp.int32, (1, LANES), 1)
    low = lane < HEAD_DIM
    qi = lax.broadcasted_iota(jnp.int32, (blk, 2 * blk), 0)
    kj = lax.broadcasted_iota(jnp.int32, (blk, 2 * blk), 1)
    in_cur = (kj >= blk) & (kj - blk <= qi)
    in_prev = (kj < blk) & (kj > qi)
    in_prev_first = (kj < blk) & (kj > qi + jnp.where(j > 0, 0, blk))
    ones = jnp.ones((2 * blk, LANES), F32)
    zeros = jnp.zeros((2 * blk, LANES), F32)
    one_lo = jnp.where(low, ones, zeros).astype(BF16)
    one_hi = jnp.where(low, zeros, ones).astype(BF16)
    band = {}
    live = {}

    def prep(i):
        cur = kv_ref[i * blk:(i + 1) * blk, :]
        prev = kvp_ref[...] if i == 0 else kv_ref[(i - 1) * blk:i * blk, :]
        kv = jnp.concatenate([prev, cur], axis=0).astype(F32)
        k = kv[:, :LANES]
        v = kv[:, LANES:]
        k_cat = jnp.concatenate([jnp.where(low, k, 0.0), jnp.where(low, 0.0, k)], axis=0).astype(BF16)
        v_cat = jnp.concatenate(
            [jnp.concatenate([jnp.where(low, v, 0.0).astype(BF16), one_lo], axis=1),
             jnp.concatenate([jnp.where(low, 0.0, v).astype(BF16), one_hi], axis=1)], axis=0)
        band[i] = (k_cat, v_cat, in_cur | (in_prev_first if i == 0 else in_prev))

    def scores(u):
        i, p = divmod(u, ngrp)
        q = q_ref[i * blk:(i + 1) * blk, p * LANES:(p + 1) * LANES]
        live[u] = lax.dot_general(q, band[i][0], (((1,), (1,)), ((), ())), preferred_element_type=F32)

    def softmax(u):
        i, p = divmod(u, ngrp)
        s = live[u]
        mask = band[i][2]
        s0 = jnp.where(mask, s[:, :2 * blk], -jnp.inf)
        s1 = jnp.where(mask, s[:, 2 * blk:], -jnp.inf)
        m0 = jnp.maximum(jnp.max(s0, axis=-1, keepdims=True), sink_ref[p])
        m1 = jnp.maximum(jnp.max(s1, axis=-1, keepdims=True), sink_ref[p + N_Q_HEADS // N_KV_HEADS])
        pr = jnp.concatenate([jnp.exp(s0 - m0), jnp.exp(s1 - m1)], axis=1).astype(BF16)
        live[u] = (pr, m0, m1)

    def values(u):
        i, p = divmod(u, ngrp)
        pr, m0, m1 = live.pop(u)
        r = jnp.dot(pr, band[i][1], preferred_element_type=F32)
        den = r[:, LANES:] + jnp.where(low, jnp.exp(sink_ref[p] - m0),
                                       jnp.exp(sink_ref[p + N_Q_HEADS // N_KV_HEADS] - m1))
        o_ref[i * blk:(i + 1) * blk, p * LANES:(p + 1) * LANES] = (r[:, :LANES] / den).astype(BF16)

    n = nblk * ngrp
    prep(0)
    for k in range(n + 2):
        if k < n:
            scores(k)
            if k + 1 < n and (k + 1) % ngrp == 0:
                prep((k + 1) // ngrp)
        if 0 <= k - 2 < n:
            values(k - 2)
        if 0 <= k - 1 < n:
            softmax(k - 1)
        yield


ATTN_STAGES = (TOKEN_TILE // ATTN_BLOCK) * (ATTN_Q_WIDTH // LANES) + 2


def _mlstm_thread(q_ref, kt_ref, v_ref, om_ref, if_ref, bif_ref, nw_ref, y_ref,
                  cn_ref, mrow_ref, mcol_ref):
    ts = TOKEN_TILE
    L = MLSTM_CHUNK
    D = MLSTM_HEAD_DIM
    nch = ts // L

    gates = if_ref[...] + bif_ref[...]
    logf_all = jax.nn.log_sigmoid(pltpu.roll(gates, LANES - MLSTM_HEADS, axis=1))

    ti = lax.broadcasted_iota(jnp.int32, (L, L), 0)
    si = lax.broadcasted_iota(jnp.int32, (L, L), 1)
    causal = si <= ti
    tri = causal.astype(BF16)
    ones_v = jnp.ones((L, LANES), BF16)
    chunk = {}
    head = {}

    def rows_of(c):
        return slice(c * L, (c + 1) * L)

    def p1(c):
        lf = logf_all[rows_of(c)]
        lf_hi = lf.astype(BF16)
        rem = lf - lf_hi.astype(F32)
        lf_mid = rem.astype(BF16)
        lf_lo = (rem - lf_mid.astype(F32)).astype(BF16)
        chunk[c] = (jnp.dot(tri, lf_hi, preferred_element_type=F32)
                    + jnp.dot(tri, lf_mid, preferred_element_type=F32)
                    + jnp.dot(tri, lf_lo, preferred_element_type=F32))

    def p2(c):
        b = chunk[c]
        r = b - gates[rows_of(c)]
        pm = -r
        sh = 1
        while sh < L:
            pm = jnp.maximum(pm, jnp.where(ti >= sh, pltpu.roll(pm, sh, axis=0), -jnp.inf))
            sh *= 2
        m_row = mrow_ref[0:1, :]
        inter = b + m_row
        mt = jnp.maximum(inter, b + pm)
        b_last = b[L - 1:L, :]
        m_new_row = jnp.maximum(b_last + m_row, jnp.max(b_last - r, axis=0, keepdims=True))
        mrow_ref[...] = jnp.broadcast_to(m_new_row, mrow_ref.shape)
        b_t = b.T[:SUBLANES]
        r_t = r.T[:SUBLANES]
        m_col = mcol_ref[:, 0:1]
        bl_col = b_t[:, L - 1:L]
        a_t = bl_col - r_t
        m_new_col = jnp.maximum(bl_col + m_col, jnp.max(a_t, axis=1, keepdims=True))
        mcol_ref[...] = jnp.broadcast_to(m_new_col, mcol_ref.shape)
        chunk[c] = dict(e_col=b - mt, r_t=r_t, w_inter=jnp.exp(inter - mt), e_mt=jnp.exp(-mt),
                        decay_row=jnp.exp(b_last + m_row - m_new_row),
                        wk_t=jnp.exp(a_t - m_new_col))

    def ha(c, h):
        hs = slice(h * D, (h + 1) * D)
        qh = q_ref[rows_of(c), hs]
        kt = kt_ref[hs, rows_of(c)]
        head[c, h] = dict(qh=qh, kt=kt, s=jnp.dot(qh, kt, preferred_element_type=F32))

    def hb(c, h):
        st, ch = head[c, h], chunk[c]
        w_intra = jnp.where(causal, jnp.exp(ch["e_col"][:, h:h + 1] - ch["r_t"][h:h + 1, :]), 0.0)
        pr = (st.pop("s") * w_intra).astype(BF16)
        qw = (st.pop("qh").astype(F32) * ch["w_inter"][:, h:h + 1]).astype(BF16)
        st["lhs"] = jnp.concatenate([pr, qw], axis=1)
        st["kw"] = (st.pop("kt").astype(F32) * ch["wk_t"][h:h + 1, :]).astype(BF16)

    def hc(c, h):
        st = head[c, h]
        hs = slice(h * D, (h + 1) * D)
        v_ext = jnp.concatenate([v_ref[rows_of(c), hs], ones_v], axis=1)
        rhs = jnp.concatenate([v_ext, cn_ref[h].astype(BF16)], axis=0)
        st["nd"] = jnp.dot(st.pop("lhs"), rhs, preferred_element_type=F32)
        st["kv"] = jnp.dot(st.pop("kw"), v_ext, preferred_element_type=F32)

    def hd(c, h):
        st, ch = head.pop((c, h)), chunk[c]
        hs = slice(h * D, (h + 1) * D)
        nd = st["nd"]
        hh = nd[:, :D] / jnp.maximum(jnp.abs(nd[:, D:]), ch["e_mt"][:, h:h + 1])
        cn_ref[h] = ch["decay_row"][:, h:h + 1] * cn_ref[h] + st["kv"]
        mu = jnp.mean(hh, axis=-1, keepdims=True)
        xc = hh - mu
        var = jnp.mean(xc * xc, axis=-1, keepdims=True)
        yn = (xc * lax.rsqrt(var + NORM_EPS)) * nw_ref[:, hs]
        y_ref[rows_of(c), hs] = (jax.nn.sigmoid(om_ref[rows_of(c), hs].astype(F32)) * yn).astype(BF16)

    half = MLSTM_HEADS // 2
    for g in range(MLSTM_STAGES):
        for c in range(nch):
            o = g - MLSTM_CHUNK_SKEW * c
            if o == 0:
                p1(c)
            elif o == 1:
                for h in range(MLSTM_HEADS):
                    ha(c, h)
                p2(c)
            elif o == 2:
                for h in range(half):
                    hb(c, h)
            elif o == 3:
                for h in range(half):
                    hc(c, h)
                for h in range(half, MLSTM_HEADS):
                    hb(c, h)
            elif o == 4:
                for h in range(half, MLSTM_HEADS):
                    hc(c, h)
                for h in range(half):
                    hd(c, h)
            elif o == 5:
                for h in range(half, MLSTM_HEADS):
                    hd(c, h)
        yield


MLSTM_CHUNK_SKEW = 4
MLSTM_STAGES = MLSTM_CHUNK_SKEW * (TOKEN_TILE // MLSTM_CHUNK - 1) + 6


def _merge_thread(x_ref, ya_ref, ym_ref, ga_ref, gm_ref, mod_ref, gpm_ref, gqf_ref,
                  wa_ref, wb_ref, wo_ref, x1_ref, h2_ref):
    tm = TOKEN_TILE
    parts = [slice(r * tm // MERGE_ROW_PARTS, (r + 1) * tm // MERGE_ROW_PARTS)
             for r in range(MERGE_ROW_PARTS)]
    gate_m = mod_ref[0, 2:3, :]
    ba = jnp.dot(ya_ref[...], wa_ref[...], preferred_element_type=F32)
    yield
    bb = jnp.dot(ym_ref[...], wb_ref[...], preferred_element_type=F32)
    yield
    merged = []
    for rows in parts:
        merged.append((jax.nn.sigmoid(ga_ref[rows, :].astype(F32)) * ba[rows]
                       + jax.nn.sigmoid(gm_ref[rows, :].astype(F32)) * bb[rows]).astype(BF16))
        yield
    mix = jnp.dot(jnp.concatenate(merged, axis=0), wo_ref[...], preferred_element_type=F32)
    yield
    for rows in parts:
        x1 = x_ref[rows, :] + gate_m * _rms(mix[rows], gpm_ref[...])
        x1_ref[rows, :] = x1
        yield
        h2_ref[rows, :] = _modulated_norm(x1, gqf_ref[...], mod_ref, 3)
        yield


MERGE_ROW_PARTS = 4
MERGE_STAGES = 3 + 3 * MERGE_ROW_PARTS


def _ffn_thread(x1_ref, h2_ref, mod_ref, gpf_ref, wg_ref, wu_ref, wd_ref, o_ref, act_s, after_up):
    gate_f = mod_ref[0, 5:6, :]
    for c in range(D_FF // FF_CHUNK):
        cs = slice(c * FF_CHUNK, (c + 1) * FF_CHUNK)
        gu = jnp.dot(h2_ref[...], jnp.concatenate([wg_ref[:, cs], wu_ref[:, cs]], axis=1),
                     preferred_element_type=F32)
        g = gu[:, :FF_CHUNK]
        act_s[:, c * FF_CHUNK:(c + 1) * FF_CHUNK] = ((g * jax.nn.sigmoid(g)) * gu[:, FF_CHUNK:]).astype(BF16)
        yield
    after_up()
    ff = []
    for c in range(FFN_DOWN_SPLIT):
        cs = slice(c * D_MODEL // FFN_DOWN_SPLIT, (c + 1) * D_MODEL // FFN_DOWN_SPLIT)
        ff.append(jnp.dot(act_s[...], wd_ref[:, cs], preferred_element_type=F32))
        yield
    o_ref[...] = x1_ref[...] + gate_f * _rms(jnp.concatenate(ff, axis=1), gpf_ref[...])
    yield


FFN_DOWN_SPLIT = 2
FFN_UP_STAGES = D_FF // FF_CHUNK
FFN_STAGES = FFN_UP_STAGES + FFN_DOWN_SPLIT + 1


def _interleave(main, n_main, fills):
    done = [0] * len(fills)
    for k in range(n_main):
        next(main)
        for f, (gen, count, first, last) in enumerate(fills):
            span = last - first + 1
            want = 0 if k < first else min(count, ((k - first + 1) * count + span - 1) // span)
            while done[f] < want:
                next(gen)
                done[f] += 1
    assert all(d == f[1] for d, f in zip(done, fills))
    assert next(main, None) is None and all(next(f[0], None) is None for f in fills)


def _back_kernel(n_tiles, tiles_per_batch, sink_ref,
                 qa_ref, kv_ref, kvp_ref, qm_ref, kt_ref, vm_ref, om_ref, if_ref, bif_ref, nw_ref,
                 x_ref, ga_ref, gm_ref, modm_ref, gpm_ref, gqf_ref,
                 modf_ref, gpf_ref,
                 wa_ref, wb_ref, wo_ref, wg_ref, wu_ref, wd_ref, o_ref,
                 ya_s, ym_s, x1_s, h2_cur, h2_next, act_s, cn_ref, mrow_ref, mcol_ref):
    i = pl.program_id(0)
    j = jnp.minimum(i, n_tiles - 1) % tiles_per_batch
    slot = i % 2

    @pl.when(i == 0)
    def _():
        ya_s[...] = jnp.zeros_like(ya_s)
        ym_s[...] = jnp.zeros_like(ym_s)
        x1_s[...] = jnp.zeros_like(x1_s)
        h2_cur[...] = jnp.zeros_like(h2_cur)

    @pl.when(j == 0)
    def _():
        cn_ref[...] = jnp.zeros_like(cn_ref)
        mrow_ref[...] = jnp.zeros_like(mrow_ref)
        mcol_ref[...] = jnp.zeros_like(mcol_ref)

    def hand_over():
        h2_cur[...] = h2_next[...]

    ffn = _ffn_thread(x1_s.at[slot], h2_cur, modf_ref, gpf_ref, wg_ref, wu_ref, wd_ref, o_ref, act_s,
                      hand_over)
    merge = _merge_thread(x_ref, ya_s, ym_s, ga_ref, gm_ref, modm_ref, gpm_ref, gqf_ref,
                          wa_ref, wb_ref, wo_ref, x1_s.at[1 - slot], h2_next)
    attn = _attn_thread(j, sink_ref, qa_ref, kv_ref, kvp_ref, ya_s)
    mlstm = _mlstm_thread(qm_ref, kt_ref, vm_ref, om_ref, if_ref, bif_ref, nw_ref, ym_s,
                          cn_ref, mrow_ref, mcol_ref)
    _interleave(ffn, FFN_STAGES,
                [(merge, MERGE_STAGES, 0, FFN_UP_STAGES - 2),
                 (attn, ATTN_STAGES, 2, FFN_STAGES - 2),
                 (mlstm, MLSTM_STAGES, 2, FFN_STAGES - 2)])


def _back(sinks, qa, kv, qm, kt, vm, om, gates, bif, norm_w, x2, ga, gm, mod3,
          g_post_mix, g_pre_ffn, g_post_ffn, wa, wb, wo, wg, wu, wd, tiles_per_batch):
    t = x2.shape[0]
    tm = TOKEN_TILE
    n = t // tm
    per_tile = tm // ATTN_BLOCK
    clamp = lambda v: jnp.clip(v, 0, n - 1)
    cur = lambda i: (clamp(i), 0)
    cur_t = lambda i: (0, clamp(i))

    def prev_block(i):
        tile = clamp(i)
        first = (tile // tiles_per_batch) * tiles_per_batch * per_tile
        return (jnp.maximum(tile * per_tile - 1, first), 0)

    mid = lambda i: (clamp(i - 1), 0)
    last = lambda i: (clamp(i - 2), 0)
    const = lambda i: (0, 0)
    once = pl.Buffered(1)
    weight_bytes = 2 * (wa.size + wb.size + wo.size + wg.size + wu.size + wd.size)
    mixer_in = tm * (ATTN_Q_WIDTH + 2 * ATTN_KV_WIDTH + 4 * MLSTM_WIDTH) * 2 + tm * LANES * 4
    scratch_bytes = tm * (D_FF + 2 * MLSTM_WIDTH + 2 * D_MODEL) * 2 + 2 * tm * D_MODEL * 4
    est = (weight_bytes + 2 * 2 * tm * D_MODEL * 4 + 2 * tm * 2 * D_MODEL * 2 + 2 * mixer_in
           + scratch_bytes + 6 * tm * D_MODEL * 4)
    return pl.pallas_call(
        functools.partial(_back_kernel, n, tiles_per_batch),
        grid=(n + 2,),
        in_specs=[pl.BlockSpec(memory_space=pltpu.SMEM),
                  pl.BlockSpec((tm, ATTN_Q_WIDTH), cur),
                  pl.BlockSpec((tm, 2 * ATTN_KV_WIDTH), cur),
                  pl.BlockSpec((ATTN_BLOCK, 2 * ATTN_KV_WIDTH), prev_block),
                  pl.BlockSpec((tm, MLSTM_WIDTH), cur),
                  pl.BlockSpec((MLSTM_WIDTH, tm), cur_t),
                  pl.BlockSpec((tm, MLSTM_WIDTH), cur),
                  pl.BlockSpec((tm, MLSTM_WIDTH), cur),
                  pl.BlockSpec((tm, LANES), cur),
                  pl.BlockSpec((1, LANES), const),
                  pl.BlockSpec((1, MLSTM_WIDTH), const),
                  pl.BlockSpec((tm, D_MODEL), mid),
                  pl.BlockSpec((tm, D_MODEL), mid),
                  pl.BlockSpec((tm, D_MODEL), mid),
                  pl.BlockSpec((1, 6, D_MODEL), lambda i: (clamp(i - 1) // tiles_per_batch, 0, 0)),
                  pl.BlockSpec((1, D_MODEL), const),
                  pl.BlockSpec((1, D_MODEL), const),
                  pl.BlockSpec((1, 6, D_MODEL), lambda i: (clamp(i - 2) // tiles_per_batch, 0, 0)),
                  pl.BlockSpec((1, D_MODEL), const),
                  pl.BlockSpec(wa.shape, const, pipeline_mode=once),
                  pl.BlockSpec(wb.shape, const, pipeline_mode=once),
                  pl.BlockSpec(wo.shape, const, pipeline_mode=once),
                  pl.BlockSpec(wg.shape, const, pipeline_mode=once),
                  pl.BlockSpec(wu.shape, const, pipeline_mode=once),
                  pl.BlockSpec(wd.shape, const, pipeline_mode=once)],
        out_specs=pl.BlockSpec((tm, D_MODEL), last),
        out_shape=jax.ShapeDtypeStruct((t, D_MODEL), F32),
        scratch_shapes=[pltpu.VMEM((tm, ATTN_Q_WIDTH), BF16),
                        pltpu.VMEM((tm, MLSTM_WIDTH), BF16),
                        pltpu.VMEM((2, tm, D_MODEL), F32),
                        pltpu.VMEM((tm, D_MODEL), BF16),
                        pltpu.VMEM((tm, D_MODEL), BF16),
                        pltpu.VMEM((tm, D_FF), BF16),
                        pltpu.VMEM((MLSTM_HEADS, MLSTM_HEAD_DIM, 2 * MLSTM_HEAD_DIM), F32),
                        pltpu.VMEM((SUBLANES, LANES), F32),
                        pltpu.VMEM((SUBLANES, LANES), F32)],
        compiler_params=pltpu.CompilerParams(dimension_semantics=("arbitrary",),
                                             vmem_limit_bytes=_vmem_limit(est)),
        name="back",
    )(sinks, qa, kv, kv, qm, kt, vm, om, gates, bif, norm_w, x2, ga, gm, mod3,
      g_post_mix, g_pre_ffn, mod3, g_post_ffn, wa, wb, wo, wg, wu, wd)


def kernel(x, c, positions, w_ada, b_ada, g_pre_mix, g_post_mix, w_in, b_if, conv_w, conv_b,
           attn_sinks, mlstm_norm_w, w_branch_attn, w_branch_mlstm, w_out, g_pre_ffn, g_post_ffn,
           w_ffn_gate, w_ffn_up, w_ffn_down):
    batch, seq, d = x.shape
    depth = w_in.shape[0]
    assert d == D_MODEL and seq % TOKEN_TILE == 0 and (batch * seq) % ROPE_TILE == 0
    assert D_FF % FF_CHUNK == 0
    t = batch * seq
    tiles_per_batch = seq // TOKEN_TILE
    x2 = x.reshape(t, d)

    inv_freq = (ROPE_THETA ** (-2.0 * jnp.arange(HEAD_DIM // 2, dtype=F32) / HEAD_DIM)).reshape(-1, 1)
    c_pad = jnp.pad(c, ((0, SUBLANES - batch % SUBLANES), (0, 0))) if batch % SUBLANES else c

    for l in range(depth):
        w_t = w_in[l].T
        cos_t, sin_t, mod, w_q, w_mid = _prep(positions.reshape(1, t), inv_freq, c_pad, w_ada[l],
                                              b_ada[l].reshape(1, -1), w_t)
        mod3 = mod[:batch].reshape(batch, 6, d)
        back_weights = (w_branch_attn[l], w_branch_mlstm[l], w_out[l], w_ffn_gate[l], w_ffn_up[l],
                        w_ffn_down[l])
        qa, kv, qm, kt, vm, om, gates, ga, gm, wa, wb, wo, wg, wu, wd = _inproj(
            x2, mod3, g_pre_mix[l].reshape(1, d), cos_t, sin_t, w_q, w_mid, w_t,
            conv_w[l], conv_b[l].reshape(1, -1), back_weights, tiles_per_batch)
        bif = jnp.pad(b_if[l], (0, LANES - 2 * MLSTM_HEADS)).reshape(1, LANES)
        x2 = _back(attn_sinks[l], qa, kv, qm, kt, vm, om, gates, bif, mlstm_norm_w[l].reshape(1, -1),
                   x2, ga, gm, mod3, g_post_mix[l].reshape(1, d), g_pre_ffn[l].reshape(1, d),
                   g_post_ffn[l].reshape(1, d), wa, wb, wo, wg, wu, wd, tiles_per_batch)
    return x2.reshape(batch, seq, d)
```

```python
import functools

import numpy as np
import jax
import jax.numpy as jnp
from jax import lax
from jax.experimental import pallas as pl
from jax.experimental.pallas import tpu as pltpu

F32 = jnp.float32
BF16 = jnp.bfloat16

D_MODEL = 1024
N_Q_HEADS = 8
N_KV_HEADS = 2
HEAD_DIM = 64
ROPE_THETA = 10000.0
MLSTM_HEADS = 4
MLSTM_HEAD_DIM = 128
CONV_WIDTH = 4
D_FF = 2816
NORM_EPS = 1e-6
ATTN_Q_WIDTH = N_Q_HEADS * HEAD_DIM
ATTN_KV_WIDTH = N_KV_HEADS * HEAD_DIM
MLSTM_WIDTH = MLSTM_HEADS * MLSTM_HEAD_DIM
ATTN_BLOCK = 128

LANES = 128
SUBLANES = 8
MXU_WIDTH = 256
V7X_SCOPED_VMEM_BYTES = 60000 * 1024

TOKEN_TILE = 512
MLSTM_CHUNK = 128
FF_CHUNK = LANES
ROPE_TILE = 4096

C_QA = 0
C_KV = C_QA + ATTN_Q_WIDTH
C_QK = C_KV + 2 * ATTN_KV_WIDTH
C_VM = C_QK + 2 * MLSTM_WIDTH
C_OM = C_VM + MLSTM_WIDTH
C_IF = C_OM + MLSTM_WIDTH
C_GA = C_IF + LANES
C_GM = C_GA + D_MODEL


def _vmem_limit(estimate_bytes):
    return int(min(V7X_SCOPED_VMEM_BYTES, max(estimate_bytes, 16 * 1024 * 1024)))


def _rms(x, g):
    return (x * lax.rsqrt(jnp.mean(x * x, axis=-1, keepdims=True) + NORM_EPS)) * g


def _prep_kernel(pos_ref, freq_ref, c_ref, w_ref, b_ref, wq_f32, wmid_f32,
                 cos_ref, sin_ref, mod_ref, wq_ref, wmid_ref):
    wq_ref[...] = wq_f32[...].astype(BF16)
    wmid_ref[...] = wmid_f32[...].astype(BF16)
    mod_ref[...] = jnp.dot(c_ref[...].astype(BF16), w_ref[...].astype(BF16),
                           preferred_element_type=F32) + b_ref[...]
    ang = freq_ref[...] * pos_ref[...].astype(F32)
    c = jnp.cos(ang)
    s = jnp.sin(ang)
    cos_ref[...] = jnp.concatenate([c, c, c, c], axis=0).T
    sin_ref[...] = jnp.concatenate([-s, s, -s, s], axis=0).T


def _q_source_block(out_block, blocks_per_head):
    per_kv = N_Q_HEADS // N_KV_HEADS
    head = out_block // blocks_per_head
    src_head = (head % N_KV_HEADS) * per_kv + head // N_KV_HEADS
    return src_head * blocks_per_head + out_block % blocks_per_head


def _prep(pos_row, inv_freq, c_pad, w_ada, b_ada, w_t):
    t = pos_row.shape[1]
    half = inv_freq.shape[0]
    rows = c_pad.shape[0]
    n = w_ada.shape[1]
    steps = t // ROPE_TILE
    cols = n // steps
    q_rows = ATTN_Q_WIDTH // steps
    mid_rows = (C_IF - C_KV) // steps
    assert n % steps == 0 and cols % LANES == 0
    assert ATTN_Q_WIDTH % steps == 0 and HEAD_DIM % q_rows == 0 and q_rows % (2 * SUBLANES) == 0
    assert (C_IF - C_KV) % steps == 0 and mid_rows % (2 * SUBLANES) == 0
    return pl.pallas_call(
        _prep_kernel,
        grid=(steps,),
        in_specs=[pl.BlockSpec((1, ROPE_TILE), lambda i: (0, i)),
                  pl.BlockSpec((half, 1), lambda i: (0, 0)),
                  pl.BlockSpec((rows, D_MODEL), lambda i: (0, 0)),
                  pl.BlockSpec((D_MODEL, cols), lambda i: (0, i)),
                  pl.BlockSpec((1, cols), lambda i: (0, i)),
                  pl.BlockSpec((q_rows, D_MODEL), lambda i: (_q_source_block(i, HEAD_DIM // q_rows), 0)),
                  pl.BlockSpec((pl.Element(mid_rows), pl.Element(D_MODEL)),
                               lambda i: (pl.multiple_of(C_KV + i * mid_rows, 2 * SUBLANES), 0))],
        out_specs=[pl.BlockSpec((ROPE_TILE, LANES), lambda i: (i, 0)),
                   pl.BlockSpec((ROPE_TILE, LANES), lambda i: (i, 0)),
                   pl.BlockSpec((rows, cols), lambda i: (0, i)),
                   pl.BlockSpec((q_rows, D_MODEL), lambda i: (i, 0)),
                   pl.BlockSpec((mid_rows, D_MODEL), lambda i: (i, 0))],
        out_shape=[jax.ShapeDtypeStruct((t, LANES), F32),
                   jax.ShapeDtypeStruct((t, LANES), F32),
                   jax.ShapeDtypeStruct((rows, n), F32),
                   jax.ShapeDtypeStruct((ATTN_Q_WIDTH, D_MODEL), BF16),
                   jax.ShapeDtypeStruct((C_IF - C_KV, D_MODEL), BF16)],
        name="prep",
    )(pos_row, inv_freq, c_pad, w_ada, b_ada, w_t, w_t)


def _modulated_norm(x, g, mod_ref, row):
    return (_rms(x, g) * (1.0 + mod_ref[0, row + 1:row + 2, :]) + mod_ref[0, row:row + 1, :]).astype(BF16)


def _inproj_kernel(n_tiles, tiles_per_batch, x0_ref, mod0_ref, xn_ref, modn_ref, g_ref, cos_ref, sin_ref,
                   wq_ref, wmid_ref, wtail_ref, cw_ref, cb_ref, *rest):
    n_cast = len(BACK_WEIGHT_BLOCKS)
    cast_in, rest = rest[:n_cast], rest[n_cast:]
    (qa_ref, kv_ref, qm_ref, kt_ref, vm_ref, om_ref, if_ref, ga_ref, gm_ref), rest = rest[:9], rest[9:]
    cast_out, (s_ref, raw_s, rot_s, wtail_s) = rest[:n_cast], rest[n_cast:]
    for src, dst in zip(cast_in, cast_out):
        dst[...] = src[...].astype(BF16)
    _inproj_body(n_tiles, tiles_per_batch, x0_ref, mod0_ref, xn_ref, modn_ref, g_ref, cos_ref, sin_ref,
                 wq_ref, wmid_ref, wtail_ref, cw_ref, cb_ref,
                 qa_ref, kv_ref, qm_ref, kt_ref, vm_ref, om_ref, if_ref, ga_ref, gm_ref,
                 s_ref, raw_s, rot_s, wtail_s)


BACK_WEIGHT_BLOCKS = (16, 16, 16, 16, 16, 64)


def _inproj_body(n_tiles, tiles_per_batch, x0_ref, mod0_ref, xn_ref, modn_ref, g_ref, cos_ref, sin_ref,
                 wq_ref, wmid_ref, wtail_ref, cw_ref, cb_ref,
                 qa_ref, kv_ref, qm_ref, kt_ref, vm_ref, om_ref, if_ref, ga_ref, gm_ref,
                 s_ref, raw_s, rot_s, wtail_s):
    tm = TOKEN_TILE
    i = pl.program_id(0)
    par = i % 2
    post_tile = jnp.maximum(i - 1, 0)
    h_cur = s_ref.at[par]
    h_next = s_ref.at[1 - par]
    out_a = s_ref.at[2 + par]
    out_b = s_ref.at[4 + par]
    kv_lo = ATTN_Q_WIDTH

    @pl.when(i == 0)
    def _():
        s_ref[0] = _modulated_norm(x0_ref[...], g_ref[...], mod0_ref, 0)
        raw_s[...] = jnp.zeros_like(raw_s)
        rot_s[...] = jnp.zeros_like(rot_s)
        n_if = 2 * MLSTM_HEADS
        wtail_s[0:LANES, :] = jnp.concatenate(
            [wtail_ref[0:n_if, :], jnp.zeros((LANES - n_if, D_MODEL), F32)], axis=0).astype(BF16)
        for r0 in range(0, 2 * D_MODEL, MXU_WIDTH):
            wtail_s[LANES + r0:LANES + r0 + MXU_WIDTH, :] = (
                wtail_ref[n_if + r0:n_if + r0 + MXU_WIDTH, :].astype(BF16))

    @pl.when(post_tile % tiles_per_batch == 0)
    def _():
        raw_s[0:SUBLANES, :] = jnp.zeros((SUBLANES, raw_s.shape[1]), F32)

    def proj(lo, width):
        if lo < C_KV:
            w = wq_ref[lo:lo + width, :]
        elif lo < C_IF:
            w = wmid_ref[lo - C_KV:lo - C_KV + width, :]
        else:
            w = wtail_s[lo - C_IF:lo - C_IF + width, :]
        return lax.dot_general(h_cur[...], w, (((1,), (1,)), ((), ())), preferred_element_type=F32)

    lane = lax.broadcasted_iota(jnp.int32, (1, LANES), 1)
    first_half = (lane % HEAD_DIM) < (HEAD_DIM // 2)

    def rope(t):
        swapped = jnp.where(first_half, pltpu.roll(t, LANES - HEAD_DIM // 2, axis=1),
                            pltpu.roll(t, HEAD_DIM // 2, axis=1))
        return t * cos_ref[...] + swapped * sin_ref[...]

    def post_q():
        for p in range(ATTN_Q_WIDTH // LANES):
            sl = slice(p * LANES, (p + 1) * LANES)
            out_a[:, sl] = (rope(rot_s[:, sl]) * (HEAD_DIM ** -0.5)).astype(BF16)

    def post_kv():
        out_a[:, kv_lo:kv_lo + LANES] = rope(rot_s[:, kv_lo:kv_lo + LANES]).astype(BF16)
        out_a[:, kv_lo + LANES:kv_lo + 2 * LANES] = rot_s[:, kv_lo + LANES:].astype(BF16)

    blk = MXU_WIDTH

    def post_conv(cbk):
        cs = slice(cbk * blk, (cbk + 1) * blk)
        acc = cb_ref[:, cs]
        for sh in range(CONV_WIDTH):
            acc = acc + (cw_ref[CONV_WIDTH - 1 - sh:CONV_WIDTH - sh, cs]
                         * raw_s[SUBLANES - sh:SUBLANES - sh + tm, cs])
        raw_s[0:SUBLANES, cs] = raw_s[tm:tm + SUBLANES, cs]
        a = acc * jax.nn.sigmoid(acc)
        if cbk * blk < MLSTM_WIDTH:
            out_b[:, cs] = a.astype(BF16)
        else:
            a = a * (MLSTM_HEAD_DIM ** -0.5)
            for d0 in range(0, blk, LANES):
                r0 = cbk * blk - MLSTM_WIDTH + d0
                for t0 in range(0, tm, LANES):
                    out_b[r0:r0 + LANES, MLSTM_WIDTH + t0:MLSTM_WIDTH + t0 + LANES] = (
                        a[t0:t0 + LANES, d0:d0 + LANES].T.astype(BF16))

    def ship_q():
        qa_ref[...] = out_a[:, :ATTN_Q_WIDTH]

    def ship_kv():
        kv_ref[...] = out_a[:, kv_lo:kv_lo + 2 * ATTN_KV_WIDTH]

    def ship_conv(cbk):
        cs = slice(cbk * blk, (cbk + 1) * blk)
        if cbk * blk < MLSTM_WIDTH:
            qm_ref[:, cs] = out_b[:, cs]
        else:
            rs = slice(cbk * blk - MLSTM_WIDTH, (cbk + 1) * blk - MLSTM_WIDTH)
            kt_ref[rs, :] = out_b[rs, MLSTM_WIDTH:MLSTM_WIDTH + tm]

    def norm_next(part):
        rows = slice(part * tm // INPROJ_NORM_PARTS, (part + 1) * tm // INPROJ_NORM_PARTS)
        h_next[rows, :] = _modulated_norm(xn_ref[rows, :], g_ref[...], modn_ref, 0)

    def main_rot(lo_w, lo_s, width):
        rot_s[:, lo_s:lo_s + width] = proj(lo_w, width)

    def main_conv(cbk):
        raw_s[SUBLANES:, cbk * blk:(cbk + 1) * blk] = proj(C_QK + cbk * blk, blk)

    def main_direct(out_ref, lo_w, lo, width):
        out_ref[:, lo:lo + width] = proj(lo_w + lo, width).astype(out_ref.dtype)

    half = D_MODEL // 2
    main_direct(vm_ref, C_VM, 0, MLSTM_WIDTH)
    post_q()
    main_rot(C_QA, 0, ATTN_Q_WIDTH)
    ship_q()
    post_kv()
    post_conv(0)
    main_rot(C_KV, kv_lo, 2 * ATTN_KV_WIDTH)
    ship_kv()
    main_conv(0)
    ship_conv(0)
    post_conv(1)
    main_direct(om_ref, C_OM, 0, MLSTM_WIDTH)
    main_conv(1)
    ship_conv(1)
    post_conv(2)
    main_direct(if_ref, C_IF, 0, LANES)
    main_conv(2)
    ship_conv(2)
    post_conv(3)
    main_direct(ga_ref, C_GA, 0, half)
    main_conv(3)
    ship_conv(3)
    norm_next(0)
    main_direct(ga_ref, C_GA, half, half)
    norm_next(1)
    main_direct(gm_ref, C_GM, 0, half)
    norm_next(2)
    norm_next(3)
    main_direct(gm_ref, C_GM, half, half)


INPROJ_NORM_PARTS = 4


def _inproj(x2, mod3, g_pre, cos_t, sin_t, w_q, w_mid, w_t, conv_w, conv_b, back_weights, tiles_per_batch):
    t = x2.shape[0]
    tm = TOKEN_TILE
    n = t // tm
    main = lambda i: (jnp.minimum(i, n - 1), 0)
    post = lambda i: (jnp.maximum(i - 1, 0), 0)
    nxt = lambda i: (jnp.minimum(i + 1, n - 1), 0)
    const = lambda i: (0, 0)
    once = pl.Buffered(1)
    outs = [(ATTN_Q_WIDTH, BF16, post), (2 * ATTN_KV_WIDTH, BF16, post), (MLSTM_WIDTH, BF16, post),
            (None, BF16, None), (MLSTM_WIDTH, BF16, main), (MLSTM_WIDTH, BF16, main),
            (LANES, F32, main), (D_MODEL, BF16, main), (D_MODEL, BF16, main)]
    out_specs = [pl.BlockSpec((MLSTM_WIDTH, tm), lambda i: (0, jnp.maximum(i - 1, 0))) if w is None
                 else pl.BlockSpec((tm, w), m) for w, _, m in outs]
    out_shape = [jax.ShapeDtypeStruct((MLSTM_WIDTH, t) if w is None else (t, w), d) for w, d, _ in outs]
    out_bytes = sum((w or MLSTM_WIDTH) * np.dtype(d).itemsize for w, d, _ in outs) * tm
    rot_cols = ATTN_Q_WIDTH + 2 * ATTN_KV_WIDTH
    assert tm == MLSTM_WIDTH
    scratch_bytes = (6 * tm * D_MODEL * 2 + (tm + SUBLANES) * 2 * MLSTM_WIDTH * 4 + tm * rot_cols * 4)
    tail_rows = w_t.shape[0] - C_IF
    tail_aligned = LANES + 2 * D_MODEL
    assert tail_rows == 2 * MLSTM_HEADS + 2 * D_MODEL
    weight_bytes = (w_q.size + w_mid.size + tail_aligned * D_MODEL) * 2 + tail_rows * D_MODEL * 4
    cast_in, cast_out, cast_bytes = [], [], 0
    for k, (w, rows) in enumerate(zip(back_weights, BACK_WEIGHT_BLOCKS)):
        blocks = w.shape[0] // rows
        assert w.shape[0] % rows == 0 and blocks <= n and rows % (2 * SUBLANES) == 0
        dst = lambda i, blocks=blocks: (jnp.minimum(i, blocks - 1), 0)
        src = dst if k else (lambda i, blocks=blocks, per_head=HEAD_DIM // rows:
                             (_q_source_block(jnp.minimum(i, blocks - 1), per_head), 0))
        cast_in.append(pl.BlockSpec((rows, w.shape[1]), src))
        cast_out.append(pl.BlockSpec((rows, w.shape[1]), dst))
        out_shape.append(jax.ShapeDtypeStruct(w.shape, BF16))
        cast_bytes += 2 * rows * w.shape[1] * 6
    est = (weight_bytes + 3 * tm * D_MODEL * 4 + 2 * out_bytes + 4 * tm * LANES * 4
           + scratch_bytes + 4 * tm * D_MODEL * 4 + cast_bytes)
    return pl.pallas_call(
        functools.partial(_inproj_kernel, n, tiles_per_batch),
        grid=(n + 1,),
        in_specs=[pl.BlockSpec((tm, D_MODEL), const, pipeline_mode=once),
                  pl.BlockSpec((1, 6, D_MODEL), lambda i: (0, 0, 0)),
                  pl.BlockSpec((tm, D_MODEL), nxt),
                  pl.BlockSpec((1, 6, D_MODEL),
                               lambda i: (jnp.minimum(i + 1, n - 1) // tiles_per_batch, 0, 0)),
                  pl.BlockSpec((1, D_MODEL), const),
                  pl.BlockSpec((tm, LANES), post),
                  pl.BlockSpec((tm, LANES), post),
                  pl.BlockSpec(w_q.shape, const, pipeline_mode=once),
                  pl.BlockSpec(w_mid.shape, const, pipeline_mode=once),
                  pl.BlockSpec((pl.Element(tail_rows), pl.Element(D_MODEL)), lambda i: (C_IF, 0),
                               pipeline_mode=once),
                  pl.BlockSpec((CONV_WIDTH, 2 * MLSTM_WIDTH), const),
                  pl.BlockSpec((1, 2 * MLSTM_WIDTH), const)] + cast_in,
        out_specs=out_specs + cast_out,
        out_shape=out_shape,
        scratch_shapes=[pltpu.VMEM((6, tm, D_MODEL), BF16),
                        pltpu.VMEM((tm + SUBLANES, 2 * MLSTM_WIDTH), F32),
                        pltpu.VMEM((tm, rot_cols), F32),
                        pltpu.VMEM((tail_aligned, D_MODEL), BF16)],
        compiler_params=pltpu.CompilerParams(dimension_semantics=("arbitrary",),
                                             vmem_limit_bytes=_vmem_limit(est)),
        name="inproj",
    )(x2, mod3, x2, mod3, g_pre, cos_t, sin_t, w_q, w_mid, w_t, conv_w, conv_b, *back_weights)


def _attn_thread(j, sink_ref, q_ref, kv_ref, kvp_ref, o_ref):
    blk = ATTN_BLOCK
    nblk = TOKEN_TILE // blk
    ngrp = ATTN_Q_WIDTH // LANES
    lane = lax.broadcasted_iota(jnp.int32, (1, LANES), 1)
    low = lane < HEAD_DIM
    qi = lax.broadcasted_iota(jnp.int32, (blk, 2 * blk), 0)
    kj = lax.broadcasted_iota(jnp.int32, (blk, 2 * blk), 1)
    in_cur = (kj >= blk) & (kj - blk <= qi)
    in_prev = (kj < blk) & (kj > qi)
    in_prev_first = (kj < blk) & (kj > qi + jnp.where(j > 0, 0, blk))
    ones = jnp.ones((2 * blk, LANES), F32)
    zeros = jnp.zeros((2 * blk, LANES), F32)
    one_lo = jnp.where(low, ones, zeros).astype(BF16)
    one_hi = jnp.where(low, zeros, ones).astype(BF16)
    band = {}
    live = {}

    def prep(i):
        cur = kv_ref[i * blk:(i + 1) * blk, :]
        prev = kvp_ref[...] if i == 0 else kv_ref[(i - 1) * blk:i * blk, :]
        kv = jnp.concatenate([prev, cur], axis=0).astype(F32)
        k = kv[:, :LANES]
        v = kv[:, LANES:]
        k_cat = jnp.concatenate([jnp.where(low, k, 0.0), jnp.where(low, 0.0, k)], axis=0).astype(BF16)
        v_cat = jnp.concatenate(
            [jnp.concatenate([jnp.where(low, v, 0.0).astype(BF16), one_lo], axis=1),
             jnp.concatenate([jnp.where(low, 0.0, v).astype(BF16), one_hi], axis=1)], axis=0)
        band[i] = (k_cat, v_cat, in_cur | (in_prev_first if i == 0 else in_prev))

    def scores(u):
        i, p = divmod(u, ngrp)
        q = q_ref[i * blk:(i + 1) * blk, p * LANES:(p + 1) * LANES]
        live[u] = lax.dot_general(q, band[i][0], (((1,), (1,)), ((), ())), preferred_element_type=F32)

    def softmax(u):
        i, p = divmod(u, ngrp)
        s = live[u]
        mask = band[i][2]
        s0 = jnp.where(mask, s[:, :2 * blk], -jnp.inf)
        s1 = jnp.where(mask, s[:, 2 * blk:], -jnp.inf)
        m0 = jnp.maximum(jnp.max(s0, axis=-1, keepdims=True), sink_ref[p])
        m1 = jnp.maximum(jnp.max(s1, axis=-1, keepdims=True), sink_ref[p + N_Q_HEADS // N_KV_HEADS])
        pr = jnp.concatenate([jnp.exp(s0 - m0), jnp.exp(s1 - m1)], axis=1).astype(BF16)
        live[u] = (pr, m0, m1)

    def values(u):
        i, p = divmod(u, ngrp)
        pr, m0, m1 = live.pop(u)
        r = jnp.dot(pr, band[i][1], preferred_element_type=F32)
        den = r[:, LANES:] + jnp.where(low, jnp.exp(sink_ref[p] - m0),
                                       jnp.exp(sink_ref[p + N_Q_HEADS // N_KV_HEADS] - m1))
        o_ref[i * blk:(i + 1) * blk, p * LANES:(p + 1) * LANES] = (r[:, :LANES] / den).astype(BF16)

    n = nblk * ngrp
    prep(0)
    for k in range(n + 2):
        if k < n:
            scores(k)
            if k + 1 < n and (k + 1) % ngrp == 0:
                prep((k + 1) // ngrp)
        if 0 <= k - 2 < n:
            values(k - 2)
        if 0 <= k - 1 < n:
            softmax(k - 1)
        yield


ATTN_STAGES = (TOKEN_TILE // ATTN_BLOCK) * (ATTN_Q_WIDTH // LANES) + 2


def _mlstm_thread(q_ref, kt_ref, v_ref, om_ref, if_ref, bif_ref, nw_ref, y_ref,
                  cn_ref, mrow_ref, mcol_ref):
    ts = TOKEN_TILE
    L = MLSTM_CHUNK
    D = MLSTM_HEAD_DIM
    nch = ts // L

    gates = if_ref[...] + bif_ref[...]
    logf_all = jax.nn.log_sigmoid(pltpu.roll(gates, LANES - MLSTM_HEADS, axis=1))

    ti = lax.broadcasted_iota(jnp.int32, (L, L), 0)
    si = lax.broadcasted_iota(jnp.int32, (L, L), 1)
    causal = si <= ti
    tri = causal.astype(BF16)
    ones_v = jnp.ones((L, LANES), BF16)
    chunk = {}
    head = {}

    def rows_of(c):
        return slice(c * L, (c + 1) * L)

    def p1(c):
        lf = logf_all[rows_of(c)]
        lf_hi = lf.astype(BF16)
        rem = lf - lf_hi.astype(F32)
        lf_mid = rem.astype(BF16)
        lf_lo = (rem - lf_mid.astype(F32)).astype(BF16)
        chunk[c] = (jnp.dot(tri, lf_hi, preferred_element_type=F32)
                    + jnp.dot(tri, lf_mid, preferred_element_type=F32)
                    + jnp.dot(tri, lf_lo, preferred_element_type=F32))

    def p2(c):
        b = chunk[c]
        r = b - gates[rows_of(c)]
        pm = -r
        sh = 1
        while sh < L:
            pm = jnp.maximum(pm, jnp.where(ti >= sh, pltpu.roll(pm, sh, axis=0), -jnp.inf))
            sh *= 2
        m_row = mrow_ref[0:1, :]
        inter = b + m_row
        mt = jnp.maximum(inter, b + pm)
        b_last = b[L - 1:L, :]
        m_new_row = jnp.maximum(b_last + m_row, jnp.max(b_last - r, axis=0, keepdims=True))
        mrow_ref[...] = jnp.broadcast_to(m_new_row, mrow_ref.shape)
        b_t = b.T[:SUBLANES]
        r_t = r.T[:SUBLANES]
        m_col = mcol_ref[:, 0:1]
        bl_col = b_t[:, L - 1:L]
        a_t = bl_col - r_t
        m_new_col = jnp.maximum(bl_col + m_col, jnp.max(a_t, axis=1, keepdims=True))
        mcol_ref[...] = jnp.broadcast_to(m_new_col, mcol_ref.shape)
        chunk[c] = dict(e_col=b - mt, r_t=r_t, w_inter=jnp.exp(inter - mt), e_mt=jnp.exp(-mt),
                        decay_row=jnp.exp(b_last + m_row - m_new_row),
                        wk_t=jnp.exp(a_t - m_new_col))

    def ha(c, h):
        hs = slice(h * D, (h + 1) * D)
        qh = q_ref[rows_of(c), hs]
        kt = kt_ref[hs, rows_of(c)]
        head[c, h] = dict(qh=qh, kt=kt, s=jnp.dot(qh, kt, preferred_element_type=F32))

    def hb(c, h):
        st, ch = head[c, h], chunk[c]
        w_intra = jnp.where(causal, jnp.exp(ch["e_col"][:, h:h + 1] - ch["r_t"][h:h + 1, :]), 0.0)
        pr = (st.pop("s") * w_intra).astype(BF16)
        qw = (st.pop("qh").astype(F32) * ch["w_inter"][:, h:h + 1]).astype(BF16)
        st["lhs"] = jnp.concatenate([pr, qw], axis=1)
        st["kw"] = (st.pop("kt").astype(F32) * ch["wk_t"][h:h + 1, :]).astype(BF16)

    def hc(c, h):
        st = head[c, h]
        hs = slice(h * D, (h + 1) * D)
        v_ext = jnp.concatenate([v_ref[rows_of(c), hs], ones_v], axis=1)
        rhs = jnp.concatenate([v_ext, cn_ref[h].astype(BF16)], axis=0)
        st["nd"] = jnp.dot(st.pop("lhs"), rhs, preferred_element_type=F32)
        st["kv"] = jnp.dot(st.pop("kw"), v_ext, preferred_element_type=F32)

    def hd(c, h):
        st, ch = head.pop((c, h)), chunk[c]
        hs = slice(h * D, (h + 1) * D)
        nd = st["nd"]
        hh = nd[:, :D] / jnp.maximum(jnp.abs(nd[:, D:]), ch["e_mt"][:, h:h + 1])
        cn_ref[h] = ch["decay_row"][:, h:h + 1] * cn_ref[h] + st["kv"]
        mu = jnp.mean(hh, axis=-1, keepdims=True)
        xc = hh - mu
        var = jnp.mean(xc * xc, axis=-1, keepdims=True)
        yn = (xc * lax.rsqrt(var + NORM_EPS)) * nw_ref[:, hs]
        y_ref[rows_of(c), hs] = (jax.nn.sigmoid(om_ref[rows_of(c), hs].astype(F32)) * yn).astype(BF16)

    half = MLSTM_HEADS // 2
    for g in range(MLSTM_STAGES):
        for c in range(nch):
            o = g - MLSTM_CHUNK_SKEW * c
            if o == 0:
                p1(c)
            elif o == 1:
                for h in range(MLSTM_HEADS):
                    ha(c, h)
                p2(c)
            elif o == 2:
                for h in range(half):
                    hb(c, h)
            elif o == 3:
                for h in range(half):
                    hc(c, h)
                for h in range(half, MLSTM_HEADS):
                    hb(c, h)
            elif o == 4:
                for h in range(half, MLSTM_HEADS):
                    hc(c, h)
                for h in range(half):
                    hd(c, h)
            elif o == 5:
                for h in range(half, MLSTM_HEADS):
                    hd(c, h)
        yield


MLSTM_CHUNK_SKEW = 4
MLSTM_STAGES = MLSTM_CHUNK_SKEW * (TOKEN_TILE // MLSTM_CHUNK - 1) + 6


def _merge_thread(x_ref, ya_ref, ym_ref, ga_ref, gm_ref, mod_ref, gpm_ref, gqf_ref,
                  wa_ref, wb_ref, wo_ref, x1_ref, h2_ref):
    tm = TOKEN_TILE
    parts = [slice(r * tm // MERGE_ROW_PARTS, (r + 1) * tm // MERGE_ROW_PARTS)
             for r in range(MERGE_ROW_PARTS)]
    gate_m = mod_ref[0, 2:3, :]
    ba = jnp.dot(ya_ref[...], wa_ref[...], preferred_element_type=F32)
    yield
    bb = jnp.dot(ym_ref[...], wb_ref[...], preferred_element_type=F32)
    yield
    merged = []
    for rows in parts:
        merged.append((jax.nn.sigmoid(ga_ref[rows, :].astype(F32)) * ba[rows]
                       + jax.nn.sigmoid(gm_ref[rows, :].astype(F32)) * bb[rows]).astype(BF16))
        yield
    mix = jnp.dot(jnp.concatenate(merged, axis=0), wo_ref[...], preferred_element_type=F32)
    yield
    for rows in parts:
        x1 = x_ref[rows, :] + gate_m * _rms(mix[rows], gpm_ref[...])
        x1_ref[rows, :] = x1
        yield
        h2_ref[rows, :] = _modulated_norm(x1, gqf_ref[...], mod_ref, 3)
        yield


MERGE_ROW_PARTS = 4
MERGE_STAGES = 3 + 3 * MERGE_ROW_PARTS


def _ffn_thread(x1_ref, h2_ref, mod_ref, gpf_ref, wg_ref, wu_ref, wd_ref, o_ref, act_s, after_up):
    gate_f = mod_ref[0, 5:6, :]
    for c in range(D_FF // FF_CHUNK):
        cs = slice(c * FF_CHUNK, (c + 1) * FF_CHUNK)
        gu = jnp.dot(h2_ref[...], jnp.concatenate([wg_ref[:, cs], wu_ref[:, cs]], axis=1),
                     preferred_element_type=F32)
        g = gu[:, :FF_CHUNK]
        act_s[:, c * FF_CHUNK:(c + 1) * FF_CHUNK] = ((g * jax.nn.sigmoid(g)) * gu[:, FF_CHUNK:]).astype(BF16)
        yield
    after_up()
    ff = []
    for c in range(FFN_DOWN_SPLIT):
        cs = slice(c * D_MODEL // FFN_DOWN_SPLIT, (c + 1) * D_MODEL // FFN_DOWN_SPLIT)
        ff.append(jnp.dot(act_s[...], wd_ref[:, cs], preferred_element_type=F32))
        yield
    o_ref[...] = x1_ref[...] + gate_f * _rms(jnp.concatenate(ff, axis=1), gpf_ref[...])
    yield


FFN_DOWN_SPLIT = 2
FFN_UP_STAGES = D_FF // FF_CHUNK
FFN_STAGES = FFN_UP_STAGES + FFN_DOWN_SPLIT + 1


def _interleave(main, n_main, fills):
    done = [0] * len(fills)
    for k in range(n_main):
        next(main)
        for f, (gen, count, first, last) in enumerate(fills):
            span = last - first + 1
            want = 0 if k < first else min(count, ((k - first + 1) * count + span - 1) // span)
            while done[f] < want:
                next(gen)
                done[f] += 1
    assert all(d == f[1] for d, f in zip(done, fills))
    assert next(main, None) is None and all(next(f[0], None) is None for f in fills)


def _back_kernel(n_tiles, tiles_per_batch, sink_ref,
                 qa_ref, kv_ref, kvp_ref, qm_ref, kt_ref, vm_ref, om_ref, if_ref, bif_ref, nw_ref,
                 x_ref, ga_ref, gm_ref, modm_ref, gpm_ref, gqf_ref,
                 modf_ref, gpf_ref,
                 wa_ref, wb_ref, wo_ref, wg_ref, wu_ref, wd_ref, o_ref,
                 ya_s, ym_s, x1_s, h2_cur, h2_next, act_s, cn_ref, mrow_ref, mcol_ref):
    i = pl.program_id(0)
    j = jnp.minimum(i, n_tiles - 1) % tiles_per_batch
    slot = i % 2

    @pl.when(i == 0)
    def _():
        ya_s[...] = jnp.zeros_like(ya_s)
        ym_s[...] = jnp.zeros_like(ym_s)
        x1_s[...] = jnp.zeros_like(x1_s)
        h2_cur[...] = jnp.zeros_like(h2_cur)

    @pl.when(j == 0)
    def _():
        cn_ref[...] = jnp.zeros_like(cn_ref)
        mrow_ref[...] = jnp.zeros_like(mrow_ref)
        mcol_ref[...] = jnp.zeros_like(mcol_ref)

    del slot, h2_next
    ffn = _ffn_thread(x1_s.at[0], h2_cur, modf_ref, gpf_ref, wg_ref, wu_ref, wd_ref, o_ref, act_s,
                      lambda: None)
    merge = _merge_thread(x_ref, ya_s, ym_s, ga_ref, gm_ref, modm_ref, gpm_ref, gqf_ref,
                          wa_ref, wb_ref, wo_ref, x1_s.at[0], h2_cur)

    def chain():
        yield from merge
        yield from ffn

    attn = _attn_thread(j, sink_ref, qa_ref, kv_ref, kvp_ref, ya_s)
    mlstm = _mlstm_thread(qm_ref, kt_ref, vm_ref, om_ref, if_ref, bif_ref, nw_ref, ym_s,
                          cn_ref, mrow_ref, mcol_ref)
    n_main = MERGE_STAGES + FFN_STAGES
    _interleave(chain(), n_main,
                [(attn, ATTN_STAGES, 2, n_main - 2),
                 (mlstm, MLSTM_STAGES, 2, n_main - 2)])


def _back(sinks, qa, kv, qm, kt, vm, om, gates, bif, norm_w, x2, ga, gm, mod3,
          g_post_mix, g_pre_ffn, g_post_ffn, wa, wb, wo, wg, wu, wd, tiles_per_batch):
    t = x2.shape[0]
    tm = TOKEN_TILE
    n = t // tm
    per_tile = tm // ATTN_BLOCK
    clamp = lambda v: jnp.clip(v, 0, n - 1)
    cur = lambda i: (clamp(i), 0)
    cur_t = lambda i: (0, clamp(i))

    def prev_block(i):
        tile = clamp(i)
        first = (tile // tiles_per_batch) * tiles_per_batch * per_tile
        return (jnp.maximum(tile * per_tile - 1, first), 0)

    mid = lambda i: (clamp(i - 1), 0)
    last = lambda i: (clamp(i - 1), 0)
    const = lambda i: (0, 0)
    once = pl.Buffered(1)
    weight_bytes = 2 * (wa.size + wb.size + wo.size + wg.size + wu.size + wd.size)
    mixer_in = tm * (ATTN_Q_WIDTH + 2 * ATTN_KV_WIDTH + 4 * MLSTM_WIDTH) * 2 + tm * LANES * 4
    scratch_bytes = tm * (D_FF + 2 * MLSTM_WIDTH + 2 * D_MODEL) * 2 + 2 * tm * D_MODEL * 4
    est = (weight_bytes + 2 * 2 * tm * D_MODEL * 4 + 2 * tm * 2 * D_MODEL * 2 + 2 * mixer_in
           + scratch_bytes + 6 * tm * D_MODEL * 4)
    return pl.pallas_call(
        functools.partial(_back_kernel, n, tiles_per_batch),
        grid=(n + 1,),
        in_specs=[pl.BlockSpec(memory_space=pltpu.SMEM),
                  pl.BlockSpec((tm, ATTN_Q_WIDTH), cur),
                  pl.BlockSpec((tm, 2 * ATTN_KV_WIDTH), cur),
                  pl.BlockSpec((ATTN_BLOCK, 2 * ATTN_KV_WIDTH), prev_block),
                  pl.BlockSpec((tm, MLSTM_WIDTH), cur),
                  pl.BlockSpec((MLSTM_WIDTH, tm), cur_t),
                  pl.BlockSpec((tm, MLSTM_WIDTH), cur),
                  pl.BlockSpec((tm, MLSTM_WIDTH), cur),
                  pl.BlockSpec((tm, LANES), cur),
                  pl.BlockSpec((1, LANES), const),
                  pl.BlockSpec((1, MLSTM_WIDTH), const),
                  pl.BlockSpec((tm, D_MODEL), mid),
                  pl.BlockSpec((tm, D_MODEL), mid),
                  pl.BlockSpec((tm, D_MODEL), mid),
                  pl.BlockSpec((1, 6, D_MODEL), lambda i: (clamp(i - 1) // tiles_per_batch, 0, 0)),
                  pl.BlockSpec((1, D_MODEL), const),
                  pl.BlockSpec((1, D_MODEL), const),
                  pl.BlockSpec((1, 6, D_MODEL), lambda i: (clamp(i - 1) // tiles_per_batch, 0, 0)),
                  pl.BlockSpec((1, D_MODEL), const),
                  pl.BlockSpec(wa.shape, const, pipeline_mode=once),
                  pl.BlockSpec(wb.shape, const, pipeline_mode=once),
                  pl.BlockSpec(wo.shape, const, pipeline_mode=once),
                  pl.BlockSpec(wg.shape, const, pipeline_mode=once),
                  pl.BlockSpec(wu.shape, const, pipeline_mode=once),
                  pl.BlockSpec(wd.shape, const, pipeline_mode=once)],
        out_specs=pl.BlockSpec((tm, D_MODEL), last),
        out_shape=jax.ShapeDtypeStruct((t, D_MODEL), F32),
        scratch_shapes=[pltpu.VMEM((tm, ATTN_Q_WIDTH), BF16),
                        pltpu.VMEM((tm, MLSTM_WIDTH), BF16),
                        pltpu.VMEM((2, tm, D_MODEL), F32),
                        pltpu.VMEM((tm, D_MODEL), BF16),
                        pltpu.VMEM((tm, D_MODEL), BF16),
                        pltpu.VMEM((tm, D_FF), BF16),
                        pltpu.VMEM((MLSTM_HEADS, MLSTM_HEAD_DIM, 2 * MLSTM_HEAD_DIM), F32),
                        pltpu.VMEM((SUBLANES, LANES), F32),
                        pltpu.VMEM((SUBLANES, LANES), F32)],
        compiler_params=pltpu.CompilerParams(dimension_semantics=("arbitrary",),
                                             vmem_limit_bytes=_vmem_limit(est)),
        name="back",
    )(sinks, qa, kv, kv, qm, kt, vm, om, gates, bif, norm_w, x2, ga, gm, mod3,
      g_post_mix, g_pre_ffn, mod3, g_post_ffn, wa, wb, wo, wg, wu, wd)


def kernel(x, c, positions, w_ada, b_ada, g_pre_mix, g_post_mix, w_in, b_if, conv_w, conv_b,
           attn_sinks, mlstm_norm_w, w_branch_attn, w_branch_mlstm, w_out, g_pre_ffn, g_post_ffn,
           w_ffn_gate, w_ffn_up, w_ffn_down):
    batch, seq, d = x.shape
    depth = w_in.shape[0]
    assert d == D_MODEL and seq % TOKEN_TILE == 0 and (batch * seq) % ROPE_TILE == 0
    assert D_FF % FF_CHUNK == 0
    t = batch * seq
    tiles_per_batch = seq // TOKEN_TILE
    x2 = x.reshape(t, d)

    inv_freq = (ROPE_THETA ** (-2.0 * jnp.arange(HEAD_DIM // 2, dtype=F32) / HEAD_DIM)).reshape(-1, 1)
    c_pad = jnp.pad(c, ((0, SUBLANES - batch % SUBLANES), (0, 0))) if batch % SUBLANES else c

    for l in range(depth):
        w_t = w_in[l].T
        cos_t, sin_t, mod, w_q, w_mid = _prep(positions.reshape(1, t), inv_freq, c_pad, w_ada[l],
                                              b_ada[l].reshape(1, -1), w_t)
        mod3 = mod[:batch].reshape(batch, 6, d)
        back_weights = (w_branch_attn[l], w_branch_mlstm[l], w_out[l], w_ffn_gate[l], w_ffn_up[l],
                        w_ffn_down[l])
        qa, kv, qm, kt, vm, om, gates, ga, gm, wa, wb, wo, wg, wu, wd = _inproj(
            x2, mod3, g_pre_mix[l].reshape(1, d), cos_t, sin_t, w_q, w_mid, w_t,
            conv_w[l], conv_b[l].reshape(1, -1), back_weights, tiles_per_batch)
        bif = jnp.pad(b_if[l], (0, LANES - 2 * MLSTM_HEADS)).reshape(1, LANES)
        x2 = _back(attn_sinks[l], qa, kv, qm, kt, vm, om, gates, bif, mlstm_norm_w[l].reshape(1, -1),
                   x2, ga, gm, mod3, g_post_mix[l].reshape(1, d), g_pre_ffn[l].reshape(1, d),
                   g_post_ffn[l].reshape(1, d), wa, wb, wo, wg, wu, wd, tiles_per_batch)
    return x2.reshape(batch, seq, d)
```

```python
import functools

import numpy as np
import jax
import jax.numpy as jnp
from jax import lax
from jax.experimental import pallas as pl
from jax.experimental.pallas import tpu as pltpu

F32 = jnp.float32
BF16 = jnp.bfloat16

D_MODEL = 1024
N_Q_HEADS = 8
N_KV_HEADS = 2
HEAD_DIM = 64
ROPE_THETA = 10000.0
MLSTM_HEADS = 4
MLSTM_HEAD_DIM = 128
CONV_WIDTH = 4
D_FF = 2816
NORM_EPS = 1e-6
ATTN_Q_WIDTH = N_Q_HEADS * HEAD_DIM
ATTN_KV_WIDTH = N_KV_HEADS * HEAD_DIM
MLSTM_WIDTH = MLSTM_HEADS * MLSTM_HEAD_DIM
ATTN_BLOCK = 128

LANES = 128
SUBLANES = 8
MXU_WIDTH = 256
V7X_SCOPED_VMEM_BYTES = 60000 * 1024

TOKEN_TILE = 512
MLSTM_CHUNK = 128
FF_CHUNK = LANES
ROPE_TILE = 4096

C_QA = 0
C_KV = C_QA + ATTN_Q_WIDTH
C_QK = C_KV + 2 * ATTN_KV_WIDTH
C_VM = C_QK + 2 * MLSTM_WIDTH
C_OM = C_VM + MLSTM_WIDTH
C_IF = C_OM + MLSTM_WIDTH
C_GA = C_IF + LANES
C_GM = C_GA + D_MODEL


def _vmem_limit(estimate_bytes):
    return int(min(V7X_SCOPED_VMEM_BYTES, max(estimate_bytes, 16 * 1024 * 1024)))


def _rms(x, g):
    return (x * lax.rsqrt(jnp.mean(x * x, axis=-1, keepdims=True) + NORM_EPS)) * g


def _prep_kernel(pos_ref, freq_ref, c_ref, w_ref, b_ref, wq_f32, wmid_f32,
                 cos_ref, sin_ref, mod_ref, wq_ref, wmid_ref):
    wq_ref[...] = wq_f32[...].astype(BF16)
    wmid_ref[...] = wmid_f32[...].astype(BF16)
    mod_ref[...] = jnp.dot(c_ref[...].astype(BF16), w_ref[...].astype(BF16),
                           preferred_element_type=F32) + b_ref[...]
    ang = freq_ref[...] * pos_ref[...].astype(F32)
    c = jnp.cos(ang)
    s = jnp.sin(ang)
    cos_ref[...] = jnp.concatenate([c, c, c, c], axis=0).T
    sin_ref[...] = jnp.concatenate([-s, s, -s, s], axis=0).T


def _q_source_block(out_block, blocks_per_head):
    per_kv = N_Q_HEADS // N_KV_HEADS
    head = out_block // blocks_per_head
    src_head = (head % N_KV_HEADS) * per_kv + head // N_KV_HEADS
    return src_head * blocks_per_head + out_block % blocks_per_head


def _prep(pos_row, inv_freq, c_pad, w_ada, b_ada, w_t):
    t = pos_row.shape[1]
    half = inv_freq.shape[0]
    rows = c_pad.shape[0]
    n = w_ada.shape[1]
    steps = t // ROPE_TILE
    cols = n // steps
    q_rows = ATTN_Q_WIDTH // steps
    mid_rows = (C_IF - C_KV) // steps
    assert n % steps == 0 and cols % LANES == 0
    assert ATTN_Q_WIDTH % steps == 0 and HEAD_DIM % q_rows == 0 and q_rows % (2 * SUBLANES) == 0
    assert (C_IF - C_KV) % steps == 0 and mid_rows % (2 * SUBLANES) == 0
    return pl.pallas_call(
        _prep_kernel,
        grid=(steps,),
        in_specs=[pl.BlockSpec((1, ROPE_TILE), lambda i: (0, i)),
                  pl.BlockSpec((half, 1), lambda i: (0, 0)),
                  pl.BlockSpec((rows, D_MODEL), lambda i: (0, 0)),
                  pl.BlockSpec((D_MODEL, cols), lambda i: (0, i)),
                  pl.BlockSpec((1, cols), lambda i: (0, i)),
                  pl.BlockSpec((q_rows, D_MODEL), lambda i: (_q_source_block(i, HEAD_DIM // q_rows), 0)),
                  pl.BlockSpec((pl.Element(mid_rows), pl.Element(D_MODEL)),
                               lambda i: (pl.multiple_of(C_KV + i * mid_rows, 2 * SUBLANES), 0))],
        out_specs=[pl.BlockSpec((ROPE_TILE, LANES), lambda i: (i, 0)),
                   pl.BlockSpec((ROPE_TILE, LANES), lambda i: (i, 0)),
                   pl.BlockSpec((rows, cols), lambda i: (0, i)),
                   pl.BlockSpec((q_rows, D_MODEL), lambda i: (i, 0)),
                   pl.BlockSpec((mid_rows, D_MODEL), lambda i: (i, 0))],
        out_shape=[jax.ShapeDtypeStruct((t, LANES), F32),
                   jax.ShapeDtypeStruct((t, LANES), F32),
                   jax.ShapeDtypeStruct((rows, n), F32),
                   jax.ShapeDtypeStruct((ATTN_Q_WIDTH, D_MODEL), BF16),
                   jax.ShapeDtypeStruct((C_IF - C_KV, D_MODEL), BF16)],
        name="prep",
    )(pos_row, inv_freq, c_pad, w_ada, b_ada, w_t, w_t)


def _modulated_norm(x, g, mod_ref, row):
    return (_rms(x, g) * (1.0 + mod_ref[0, row + 1:row + 2, :]) + mod_ref[0, row:row + 1, :]).astype(BF16)


def _inproj_kernel(n_tiles, tiles_per_batch, x0_ref, mod0_ref, xn_ref, modn_ref, g_ref, cos_ref, sin_ref,
                   wq_ref, wmid_ref, wtail_ref, cw_ref, cb_ref, *rest):
    n_cast = len(BACK_WEIGHT_BLOCKS)
    cast_in, rest = rest[:n_cast], rest[n_cast:]
    (qa_ref, kv_ref, qm_ref, kt_ref, vm_ref, om_ref, if_ref, ga_ref, gm_ref), rest = rest[:9], rest[9:]
    cast_out, (s_ref, raw_s, rot_s, wtail_s) = rest[:n_cast], rest[n_cast:]
    for src, dst in zip(cast_in, cast_out):
        dst[...] = src[...].astype(BF16)
    _inproj_body(n_tiles, tiles_per_batch, x0_ref, mod0_ref, xn_ref, modn_ref, g_ref, cos_ref, sin_ref,
                 wq_ref, wmid_ref, wtail_ref, cw_ref, cb_ref,
                 qa_ref, kv_ref, qm_ref, kt_ref, vm_ref, om_ref, if_ref, ga_ref, gm_ref,
                 s_ref, raw_s, rot_s, wtail_s)


BACK_WEIGHT_BLOCKS = (16, 16, 16, 16, 16, 64)


def _inproj_body(n_tiles, tiles_per_batch, x0_ref, mod0_ref, xn_ref, modn_ref, g_ref, cos_ref, sin_ref,
                 wq_ref, wmid_ref, wtail_ref, cw_ref, cb_ref,
                 qa_ref, kv_ref, qm_ref, kt_ref, vm_ref, om_ref, if_ref, ga_ref, gm_ref,
                 s_ref, raw_s, rot_s, wtail_s):
    tm = TOKEN_TILE
    i = pl.program_id(0)
    par = i % 2
    post_tile = jnp.maximum(i - 1, 0)
    h_cur = s_ref.at[par]
    h_next = s_ref.at[1 - par]
    out_a = s_ref.at[2 + par]
    out_b = s_ref.at[4 + par]
    kv_lo = ATTN_Q_WIDTH

    @pl.when(i == 0)
    def _():
        s_ref[0] = _modulated_norm(x0_ref[...], g_ref[...], mod0_ref, 0)
        raw_s[...] = jnp.zeros_like(raw_s)
        rot_s[...] = jnp.zeros_like(rot_s)
        n_if = 2 * MLSTM_HEADS
        wtail_s[0:LANES, :] = jnp.concatenate(
            [wtail_ref[0:n_if, :], jnp.zeros((LANES - n_if, D_MODEL), F32)], axis=0).astype(BF16)
        for r0 in range(0, 2 * D_MODEL, MXU_WIDTH):
            wtail_s[LANES + r0:LANES + r0 + MXU_WIDTH, :] = (
                wtail_ref[n_if + r0:n_if + r0 + MXU_WIDTH, :].astype(BF16))

    @pl.when(post_tile % tiles_per_batch == 0)
    def _():
        raw_s[0:SUBLANES, :] = jnp.zeros((SUBLANES, raw_s.shape[1]), F32)

    def proj(lo, width):
        if lo < C_KV:
            w = wq_ref[lo:lo + width, :]
        elif lo < C_IF:
            w = wmid_ref[lo - C_KV:lo - C_KV + width, :]
        else:
            w = wtail_s[lo - C_IF:lo - C_IF + width, :]
        return lax.dot_general(h_cur[...], w, (((1,), (1,)), ((), ())), preferred_element_type=F32)

    lane = lax.broadcasted_iota(jnp.int32, (1, LANES), 1)
    first_half = (lane % HEAD_DIM) < (HEAD_DIM // 2)

    def rope(t):
        swapped = jnp.where(first_half, pltpu.roll(t, LANES - HEAD_DIM // 2, axis=1),
                            pltpu.roll(t, HEAD_DIM // 2, axis=1))
        return t * cos_ref[...] + swapped * sin_ref[...]

    def post_q():
        for p in range(ATTN_Q_WIDTH // LANES):
            sl = slice(p * LANES, (p + 1) * LANES)
            out_a[:, sl] = (rope(rot_s[:, sl]) * (HEAD_DIM ** -0.5)).astype(BF16)

    def post_kv():
        out_a[:, kv_lo:kv_lo + LANES] = rope(rot_s[:, kv_lo:kv_lo + LANES]).astype(BF16)
        out_a[:, kv_lo + LANES:kv_lo + 2 * LANES] = rot_s[:, kv_lo + LANES:].astype(BF16)

    blk = MXU_WIDTH

    def post_conv(cbk):
        cs = slice(cbk * blk, (cbk + 1) * blk)
        acc = cb_ref[:, cs]
        for sh in range(CONV_WIDTH):
            acc = acc + (cw_ref[CONV_WIDTH - 1 - sh:CONV_WIDTH - sh, cs]
                         * raw_s[SUBLANES - sh:SUBLANES - sh + tm, cs])
        raw_s[0:SUBLANES, cs] = raw_s[tm:tm + SUBLANES, cs]
        a = acc * jax.nn.sigmoid(acc)
        if cbk * blk < MLSTM_WIDTH:
            out_b[:, cs] = a.astype(BF16)
        else:
            a = a * (MLSTM_HEAD_DIM ** -0.5)
            for d0 in range(0, blk, LANES):
                r0 = cbk * blk - MLSTM_WIDTH + d0
                for t0 in range(0, tm, LANES):
                    out_b[r0:r0 + LANES, MLSTM_WIDTH + t0:MLSTM_WIDTH + t0 + LANES] = (
                        a[t0:t0 + LANES, d0:d0 + LANES].T.astype(BF16))

    def ship_q():
        qa_ref[...] = out_a[:, :ATTN_Q_WIDTH]

    def ship_kv():
        kv_ref[...] = out_a[:, kv_lo:kv_lo + 2 * ATTN_KV_WIDTH]

    def ship_conv(cbk):
        cs = slice(cbk * blk, (cbk + 1) * blk)
        if cbk * blk < MLSTM_WIDTH:
            qm_ref[:, cs] = out_b[:, cs]
        else:
            rs = slice(cbk * blk - MLSTM_WIDTH, (cbk + 1) * blk - MLSTM_WIDTH)
            kt_ref[rs, :] = out_b[rs, MLSTM_WIDTH:MLSTM_WIDTH + tm]

    def norm_next(part):
        rows = slice(part * tm // INPROJ_NORM_PARTS, (part + 1) * tm // INPROJ_NORM_PARTS)
        h_next[rows, :] = _modulated_norm(xn_ref[rows, :], g_ref[...], modn_ref, 0)

    def main_rot(lo_w, lo_s, width):
        rot_s[:, lo_s:lo_s + width] = proj(lo_w, width)

    def main_conv(cbk):
        raw_s[SUBLANES:, cbk * blk:(cbk + 1) * blk] = proj(C_QK + cbk * blk, blk)

    def main_direct(out_ref, lo_w, lo, width):
        out_ref[:, lo:lo + width] = proj(lo_w + lo, width).astype(out_ref.dtype)

    half = D_MODEL // 2
    main_direct(vm_ref, C_VM, 0, MLSTM_WIDTH)
    post_q()
    main_rot(C_QA, 0, ATTN_Q_WIDTH)
    ship_q()
    post_kv()
    post_conv(0)
    main_rot(C_KV, kv_lo, 2 * ATTN_KV_WIDTH)
    ship_kv()
    main_conv(0)
    ship_conv(0)
    post_conv(1)
    main_direct(om_ref, C_OM, 0, MLSTM_WIDTH)
    main_conv(1)
    ship_conv(1)
    post_conv(2)
    main_direct(if_ref, C_IF, 0, LANES)
    main_conv(2)
    ship_conv(2)
    post_conv(3)
    main_direct(ga_ref, C_GA, 0, half)
    main_conv(3)
    ship_conv(3)
    norm_next(0)
    main_direct(ga_ref, C_GA, half, half)
    norm_next(1)
    main_direct(gm_ref, C_GM, 0, half)
    norm_next(2)
    norm_next(3)
    main_direct(gm_ref, C_GM, half, half)


INPROJ_NORM_PARTS = 4


def _inproj(x2, mod3, g_pre, cos_t, sin_t, w_q, w_mid, w_t, conv_w, conv_b, back_weights, tiles_per_batch):
    t = x2.shape[0]
    tm = TOKEN_TILE
    n = t // tm
    main = lambda i: (jnp.minimum(i, n - 1), 0)
    post = lambda i: (jnp.maximum(i - 1, 0), 0)
    nxt = lambda i: (jnp.minimum(i + 1, n - 1), 0)
    const = lambda i: (0, 0)
    once = pl.Buffered(1)
    outs = [(ATTN_Q_WIDTH, BF16, post), (2 * ATTN_KV_WIDTH, BF16, post), (MLSTM_WIDTH, BF16, post),
            (None, BF16, None), (MLSTM_WIDTH, BF16, main), (MLSTM_WIDTH, BF16, main),
            (LANES, F32, main), (D_MODEL, BF16, main), (D_MODEL, BF16, main)]
    out_specs = [pl.BlockSpec((MLSTM_WIDTH, tm), lambda i: (0, jnp.maximum(i - 1, 0))) if w is None
                 else pl.BlockSpec((tm, w), m) for w, _, m in outs]
    out_shape = [jax.ShapeDtypeStruct((MLSTM_WIDTH, t) if w is None else (t, w), d) for w, d, _ in outs]
    out_bytes = sum((w or MLSTM_WIDTH) * np.dtype(d).itemsize for w, d, _ in outs) * tm
    rot_cols = ATTN_Q_WIDTH + 2 * ATTN_KV_WIDTH
    assert tm == MLSTM_WIDTH
    scratch_bytes = (6 * tm * D_MODEL * 2 + (tm + SUBLANES) * 2 * MLSTM_WIDTH * 4 + tm * rot_cols * 4)
    tail_rows = w_t.shape[0] - C_IF
    tail_aligned = LANES + 2 * D_MODEL
    assert tail_rows == 2 * MLSTM_HEADS + 2 * D_MODEL
    weight_bytes = (w_q.size + w_mid.size + tail_aligned * D_MODEL) * 2 + tail_rows * D_MODEL * 4
    cast_in, cast_out, cast_bytes = [], [], 0
    for k, (w, rows) in enumerate(zip(back_weights, BACK_WEIGHT_BLOCKS)):
        blocks = w.shape[0] // rows
        assert w.shape[0] % rows == 0 and blocks <= n and rows % (2 * SUBLANES) == 0
        dst = lambda i, blocks=blocks: (jnp.minimum(i, blocks - 1), 0)
        src = dst if k else (lambda i, blocks=blocks, per_head=HEAD_DIM // rows:
                             (_q_source_block(jnp.minimum(i, blocks - 1), per_head), 0))
        cast_in.append(pl.BlockSpec((rows, w.shape[1]), src))
        cast_out.append(pl.BlockSpec((rows, w.shape[1]), dst))
        out_shape.append(jax.ShapeDtypeStruct(w.shape, BF16))
        cast_bytes += 2 * rows * w.shape[1] * 6
    est = (weight_bytes + 3 * tm * D_MODEL * 4 + 2 * out_bytes + 4 * tm * LANES * 4
           + scratch_bytes + 4 * tm * D_MODEL * 4 + cast_bytes)
    return pl.pallas_call(
        functools.partial(_inproj_kernel, n, tiles_per_batch),
        grid=(n + 1,),
        in_specs=[pl.BlockSpec((tm, D_MODEL), const, pipeline_mode=once),
                  pl.BlockSpec((1, 6, D_MODEL), lambda i: (0, 0, 0)),
                  pl.BlockSpec((tm, D_MODEL), nxt),
                  pl.BlockSpec((1, 6, D_MODEL),
                               lambda i: (jnp.minimum(i + 1, n - 1) // tiles_per_batch, 0, 0)),
                  pl.BlockSpec((1, D_MODEL), const),
                  pl.BlockSpec((tm, LANES), post),
                  pl.BlockSpec((tm, LANES), post),
                  pl.BlockSpec(w_q.shape, const, pipeline_mode=once),
                  pl.BlockSpec(w_mid.shape, const, pipeline_mode=once),
                  pl.BlockSpec((pl.Element(tail_rows), pl.Element(D_MODEL)), lambda i: (C_IF, 0),
                               pipeline_mode=once),
                  pl.BlockSpec((CONV_WIDTH, 2 * MLSTM_WIDTH), const),
                  pl.BlockSpec((1, 2 * MLSTM_WIDTH), const)] + cast_in,
        out_specs=out_specs + cast_out,
        out_shape=out_shape,
        scratch_shapes=[pltpu.VMEM((6, tm, D_MODEL), BF16),
                        pltpu.VMEM((tm + SUBLANES, 2 * MLSTM_WIDTH), F32),
                        pltpu.VMEM((tm, rot_cols), F32),
                        pltpu.VMEM((tail_aligned, D_MODEL), BF16)],
        compiler_params=pltpu.CompilerParams(dimension_semantics=("arbitrary",),
                                             vmem_limit_bytes=_vmem_limit(est)),
        name="inproj",
    )(x2, mod3, x2, mod3, g_pre, cos_t, sin_t, w_q, w_mid, w_t, conv_w, conv_b, *back_weights)


def _attn_thread(j, sink_ref, q_ref, kv_ref, kvp_ref, o_ref):
    blk = ATTN_BLOCK
    nblk = TOKEN_TILE // blk
    ngrp = ATTN_Q_WIDTH // LANES
    lane = lax.broadcasted_iota(jnp.int32, (1, LANES), 1)
    low = lane < HEAD_DIM
    qi = lax.broadcasted_iota(jnp.int32, (blk, 2 * blk), 0)
    kj = lax.broadcasted_iota(jnp.int32, (blk, 2 * blk), 1)
    in_cur = (kj >= blk) & (kj - blk <= qi)
    in_prev = (kj < blk) & (kj > qi)
    in_prev_first = (kj < blk) & (kj > qi + jnp.where(j > 0, 0, blk))
    ones = jnp.ones((2 * blk, LANES), F32)
    zeros = jnp.zeros((2 * blk, LANES), F32)
    one_lo = jnp.where(low, ones, zeros).astype(BF16)
    one_hi = jnp.where(low, zeros, ones).astype(BF16)
    band = {}
    live = {}

    def prep(i):
        cur = kv_ref[i * blk:(i + 1) * blk, :]
        prev = kvp_ref[...] if i == 0 else kv_ref[(i - 1) * blk:i * blk, :]
        kv = jnp.concatenate([prev, cur], axis=0).astype(F32)
        k = kv[:, :LANES]
        v = kv[:, LANES:]
        k_cat = jnp.concatenate([jnp.where(low, k, 0.0), jnp.where(low, 0.0, k)], axis=0).astype(BF16)
        v_cat = jnp.concatenate(
            [jnp.concatenate([jnp.where(low, v, 0.0).astype(BF16), one_lo], axis=1),
             jnp.concatenate([jnp.where(low, 0.0, v).astype(BF16), one_hi], axis=1)], axis=0)
        band[i] = (k_cat, v_cat, in_cur | (in_prev_first if i == 0 else in_prev))

    def scores(u):
        i, p = divmod(u, ngrp)
        q = q_ref[i * blk:(i + 1) * blk, p * LANES:(p + 1) * LANES]
        live[u] = lax.dot_general(q, band[i][0], (((1,), (1,)), ((), ())), preferred_element_type=F32)

    def softmax(u):
        i, p = divmod(u, ngrp)
        s = live[u]
        mask = band[i][2]
        s0 = jnp.where(mask, s[:, :2 * blk], -jnp.inf)
        s1 = jnp.where(mask, s[:, 2 * blk:], -jnp.inf)
        m0 = jnp.maximum(jnp.max(s0, axis=-1, keepdims=True), sink_ref[p])
        m1 = jnp.maximum(jnp.max(s1, axis=-1, keepdims=True), sink_ref[p + N_Q_HEADS // N_KV_HEADS])
        pr = jnp.concatenate([jnp.exp(s0 - m0), jnp.exp(s1 - m1)], axis=1).astype(BF16)
        live[u] = (pr, m0, m1)

    def values(u):
        i, p = divmod(u, ngrp)
        pr, m0, m1 = live.pop(u)
        r = jnp.dot(pr, band[i][1], preferred_element_type=F32)
        den = r[:, LANES:] + jnp.where(low, jnp.exp(sink_ref[p] - m0),
                                       jnp.exp(sink_ref[p + N_Q_HEADS // N_KV_HEADS] - m1))
        o_ref[i * blk:(i + 1) * blk, p * LANES:(p + 1) * LANES] = (r[:, :LANES] / den).astype(BF16)

    n = nblk * ngrp
    prep(0)
    for k in range(n + 2):
        if k < n:
            scores(k)
            if k + 1 < n and (k + 1) % ngrp == 0:
                prep((k + 1) // ngrp)
        if 0 <= k - 2 < n:
            values(k - 2)
        if 0 <= k - 1 < n:
            softmax(k - 1)
        yield


ATTN_STAGES = (TOKEN_TILE // ATTN_BLOCK) * (ATTN_Q_WIDTH // LANES) + 2


def _mlstm_thread(q_ref, kt_ref, v_ref, om_ref, if_ref, bif_ref, nw_ref, y_ref,
                  cn_ref, mrow_ref, mcol_ref):
    ts = TOKEN_TILE
    L = MLSTM_CHUNK
    D = MLSTM_HEAD_DIM
    nch = ts // L

    gates = if_ref[...] + bif_ref[...]
    logf_all = jax.nn.log_sigmoid(pltpu.roll(gates, LANES - MLSTM_HEADS, axis=1))

    ti = lax.broadcasted_iota(jnp.int32, (L, L), 0)
    si = lax.broadcasted_iota(jnp.int32, (L, L), 1)
    causal = si <= ti
    tri = causal.astype(BF16)
    ones_v = jnp.ones((L, LANES), BF16)
    chunk = {}
    head = {}

    def rows_of(c):
        return slice(c * L, (c + 1) * L)

    def p1(c):
        lf = logf_all[rows_of(c)]
        lf_hi = lf.astype(BF16)
        rem = lf - lf_hi.astype(F32)
        lf_mid = rem.astype(BF16)
        lf_lo = (rem - lf_mid.astype(F32)).astype(BF16)
        chunk[c] = (jnp.dot(tri, lf_hi, preferred_element_type=F32)
                    + jnp.dot(tri, lf_mid, preferred_element_type=F32)
                    + jnp.dot(tri, lf_lo, preferred_element_type=F32))

    def p2(c):
        b = chunk[c]
        r = b - gates[rows_of(c)]
        pm = -r
        sh = 1
        while sh < L:
            pm = jnp.maximum(pm, jnp.where(ti >= sh, pltpu.roll(pm, sh, axis=0), -jnp.inf))
            sh *= 2
        m_row = mrow_ref[0:1, :]
        inter = b + m_row
        mt = jnp.maximum(inter, b + pm)
        b_last = b[L - 1:L, :]
        m_new_row = jnp.maximum(b_last + m_row, jnp.max(b_last - r, axis=0, keepdims=True))
        mrow_ref[...] = jnp.broadcast_to(m_new_row, mrow_ref.shape)
        b_t = b.T[:SUBLANES]
        r_t = r.T[:SUBLANES]
        m_col = mcol_ref[:, 0:1]
        bl_col = b_t[:, L - 1:L]
        a_t = bl_col - r_t
        m_new_col = jnp.maximum(bl_col + m_col, jnp.max(a_t, axis=1, keepdims=True))
        mcol_ref[...] = jnp.broadcast_to(m_new_col, mcol_ref.shape)
        chunk[c] = dict(e_col=b - mt, r_t=r_t, w_inter=jnp.exp(inter - mt), e_mt=jnp.exp(-mt),
                        decay_row=jnp.exp(b_last + m_row - m_new_row),
                        wk_t=jnp.exp(a_t - m_new_col))

    def ha(c, h):
        hs = slice(h * D, (h + 1) * D)
        qh = q_ref[rows_of(c), hs]
        kt = kt_ref[hs, rows_of(c)]
        head[c, h] = dict(qh=qh, kt=kt, s=jnp.dot(qh, kt, preferred_element_type=F32))

    def hb(c, h):
        st, ch = head[c, h], chunk[c]
        w_intra = jnp.where(causal, jnp.exp(ch["e_col"][:, h:h + 1] - ch["r_t"][h:h + 1, :]), 0.0)
        pr = (st.pop("s") * w_intra).astype(BF16)
        qw = (st.pop("qh").astype(F32) * ch["w_inter"][:, h:h + 1]).astype(BF16)
        st["lhs"] = jnp.concatenate([pr, qw], axis=1)
        st["kw"] = (st.pop("kt").astype(F32) * ch["wk_t"][h:h + 1, :]).astype(BF16)

    def hc(c, h):
        st = head[c, h]
        hs = slice(h * D, (h + 1) * D)
        v_ext = jnp.concatenate([v_ref[rows_of(c), hs], ones_v], axis=1)
        rhs = jnp.concatenate([v_ext, cn_ref[h].astype(BF16)], axis=0)
        st["nd"] = jnp.dot(st.pop("lhs"), rhs, preferred_element_type=F32)
        st["kv"] = jnp.dot(st.pop("kw"), v_ext, preferred_element_type=F32)

    def hd(c, h):
        st, ch = head.pop((c, h)), chunk[c]
        hs = slice(h * D, (h + 1) * D)
        nd = st["nd"]
        hh = nd[:, :D] / jnp.maximum(jnp.abs(nd[:, D:]), ch["e_mt"][:, h:h + 1])
        cn_ref[h] = ch["decay_row"][:, h:h + 1] * cn_ref[h] + st["kv"]
        mu = jnp.mean(hh, axis=-1, keepdims=True)
        xc = hh - mu
        var = jnp.mean(xc * xc, axis=-1, keepdims=True)
        yn = (xc * lax.rsqrt(var + NORM_EPS)) * nw_ref[:, hs]
        y_ref[rows_of(c), hs] = (jax.nn.sigmoid(om_ref[rows_of(c), hs].astype(F32)) * yn).astype(BF16)

    half = MLSTM_HEADS // 2
    for g in range(MLSTM_STAGES):
        for c in range(nch):
            o = g - MLSTM_CHUNK_SKEW * c
            if o == 0:
                p1(c)
            elif o == 1:
                for h in range(MLSTM_HEADS):
                    ha(c, h)
                p2(c)
            elif o == 2:
                for h in range(half):
                    hb(c, h)
            elif o == 3:
                for h in range(half):
                    hc(c, h)
                for h in range(half, MLSTM_HEADS):
                    hb(c, h)
            elif o == 4:
                for h in range(half, MLSTM_HEADS):
                    hc(c, h)
                for h in range(half):
                    hd(c, h)
            elif o == 5:
                for h in range(half, MLSTM_HEADS):
                    hd(c, h)
        yield


MLSTM_CHUNK_SKEW = 4
MLSTM_STAGES = MLSTM_CHUNK_SKEW * (TOKEN_TILE // MLSTM_CHUNK - 1) + 6


def _merge_thread(x_ref, ya_ref, ym_ref, ga_ref, gm_ref, mod_ref, gpm_ref, gqf_ref,
                  wa_ref, wb_ref, wo_ref, x1_ref, h2_ref):
    tm = TOKEN_TILE
    parts = [slice(r * tm // MERGE_ROW_PARTS, (r + 1) * tm // MERGE_ROW_PARTS)
             for r in range(MERGE_ROW_PARTS)]
    gate_m = mod_ref[0, 2:3, :]
    ba = jnp.dot(ya_ref[...], wa_ref[...], preferred_element_type=F32)
    yield
    bb = jnp.dot(ym_ref[...], wb_ref[...], preferred_element_type=F32)
    yield
    merged = []
    for rows in parts:
        merged.append((jax.nn.sigmoid(ga_ref[rows, :].astype(F32)) * ba[rows]
                       + jax.nn.sigmoid(gm_ref[rows, :].astype(F32)) * bb[rows]).astype(BF16))
        yield
    mix = jnp.dot(jnp.concatenate(merged, axis=0), wo_ref[...], preferred_element_type=F32)
    yield
    for rows in parts:
        x1 = x_ref[rows, :] + _rms(mix[rows], gate_m * gpm_ref[...])
        x1_ref[rows, :] = x1
        yield
        h2_ref[rows, :] = _modulated_norm(x1, gqf_ref[...], mod_ref, 3)
        yield


MERGE_ROW_PARTS = 4
MERGE_STAGES = 3 + 3 * MERGE_ROW_PARTS


def _ffn_thread(x1_ref, h2_ref, mod_ref, gpf_ref, wg_ref, wu_ref, wd_ref, o_ref, act_s, after_up):
    gate_f = mod_ref[0, 5:6, :]
    for c in range(D_FF // FF_CHUNK):
        cs = slice(c * FF_CHUNK, (c + 1) * FF_CHUNK)
        gu = jnp.dot(h2_ref[...], jnp.concatenate([wg_ref[:, cs], wu_ref[:, cs]], axis=1),
                     preferred_element_type=F32)
        g = gu[:, :FF_CHUNK]
        act_s[:, c * FF_CHUNK:(c + 1) * FF_CHUNK] = ((g * jax.nn.sigmoid(g)) * gu[:, FF_CHUNK:]).astype(BF16)
        yield
    after_up()
    ff = []
    for c in range(FFN_DOWN_SPLIT):
        cs = slice(c * D_MODEL // FFN_DOWN_SPLIT, (c + 1) * D_MODEL // FFN_DOWN_SPLIT)
        ff.append(jnp.dot(act_s[...], wd_ref[:, cs], preferred_element_type=F32))
        yield
    o_ref[...] = x1_ref[...] + _rms(jnp.concatenate(ff, axis=1), gate_f * gpf_ref[...])
    yield


FFN_DOWN_SPLIT = 2
FFN_UP_STAGES = D_FF // FF_CHUNK
FFN_STAGES = FFN_UP_STAGES + FFN_DOWN_SPLIT + 1


def _interleave(main, n_main, fills):
    done = [0] * len(fills)
    for k in range(n_main):
        next(main)
        for f, (gen, count, first, last) in enumerate(fills):
            span = last - first + 1
            want = 0 if k < first else min(count, ((k - first + 1) * count + span - 1) // span)
            while done[f] < want:
                next(gen)
                done[f] += 1
    assert all(d == f[1] for d, f in zip(done, fills))
    assert next(main, None) is None and all(next(f[0], None) is None for f in fills)


def _back_kernel(n_tiles, tiles_per_batch, sink_ref,
                 qa_ref, kv_ref, kvp_ref, qm_ref, kt_ref, vm_ref, om_ref, if_ref, bif_ref, nw_ref,
                 x_ref, ga_ref, gm_ref, modm_ref, gpm_ref, gqf_ref,
                 modf_ref, gpf_ref,
                 wa_ref, wb_ref, wo_ref, wg_ref, wu_ref, wd_ref, o_ref,
                 ya_s, ym_s, x1_s, h2_cur, h2_next, act_s, cn_ref, mrow_ref, mcol_ref):
    i = pl.program_id(0)
    j = jnp.minimum(i, n_tiles - 1) % tiles_per_batch
    slot = i % 2

    @pl.when(i == 0)
    def _():
        ya_s[...] = jnp.zeros_like(ya_s)
        ym_s[...] = jnp.zeros_like(ym_s)
        x1_s[...] = jnp.zeros_like(x1_s)
        h2_cur[...] = jnp.zeros_like(h2_cur)

    @pl.when(j == 0)
    def _():
        cn_ref[...] = jnp.zeros_like(cn_ref)
        mrow_ref[...] = jnp.zeros_like(mrow_ref)
        mcol_ref[...] = jnp.zeros_like(mcol_ref)

    def hand_over():
        h2_cur[...] = h2_next[...]

    ffn = _ffn_thread(x1_s.at[slot], h2_cur, modf_ref, gpf_ref, wg_ref, wu_ref, wd_ref, o_ref, act_s,
                      hand_over)
    merge = _merge_thread(x_ref, ya_s, ym_s, ga_ref, gm_ref, modm_ref, gpm_ref, gqf_ref,
                          wa_ref, wb_ref, wo_ref, x1_s.at[1 - slot], h2_next)
    attn = _attn_thread(j, sink_ref, qa_ref, kv_ref, kvp_ref, ya_s)
    mlstm = _mlstm_thread(qm_ref, kt_ref, vm_ref, om_ref, if_ref, bif_ref, nw_ref, ym_s,
                          cn_ref, mrow_ref, mcol_ref)
    _interleave(ffn, FFN_STAGES,
                [(merge, MERGE_STAGES, 0, FFN_UP_STAGES - 2),
                 (attn, ATTN_STAGES, 2, FFN_STAGES - 2),
                 (mlstm, MLSTM_STAGES, 2, FFN_STAGES - 2)])


def _back(sinks, qa, kv, qm, kt, vm, om, gates, bif, norm_w, x2, ga, gm, mod3,
          g_post_mix, g_pre_ffn, g_post_ffn, wa, wb, wo, wg, wu, wd, tiles_per_batch):
    t = x2.shape[0]
    tm = TOKEN_TILE
    n = t // tm
    per_tile = tm // ATTN_BLOCK
    clamp = lambda v: jnp.clip(v, 0, n - 1)
    cur = lambda i: (clamp(i), 0)
    cur_t = lambda i: (0, clamp(i))

    def prev_block(i):
        tile = clamp(i)
        first = (tile // tiles_per_batch) * tiles_per_batch * per_tile
        return (jnp.maximum(tile * per_tile - 1, first), 0)

    mid = lambda i: (clamp(i - 1), 0)
    last = lambda i: (clamp(i - 2), 0)
    const = lambda i: (0, 0)
    once = pl.Buffered(1)
    weight_bytes = 2 * (wa.size + wb.size + wo.size + wg.size + wu.size + wd.size)
    mixer_in = tm * (ATTN_Q_WIDTH + 2 * ATTN_KV_WIDTH + 4 * MLSTM_WIDTH) * 2 + tm * LANES * 4
    scratch_bytes = tm * (D_FF + 2 * MLSTM_WIDTH + 2 * D_MODEL) * 2 + 2 * tm * D_MODEL * 4
    est = (weight_bytes + 2 * 2 * tm * D_MODEL * 4 + 2 * tm * 2 * D_MODEL * 2 + 2 * mixer_in
           + scratch_bytes + 6 * tm * D_MODEL * 4)
    return pl.pallas_call(
        functools.partial(_back_kernel, n, tiles_per_batch),
        grid=(n + 2,),
        in_specs=[pl.BlockSpec(memory_space=pltpu.SMEM),
                  pl.BlockSpec((tm, ATTN_Q_WIDTH), cur),
                  pl.BlockSpec((tm, 2 * ATTN_KV_WIDTH), cur),
                  pl.BlockSpec((ATTN_BLOCK, 2 * ATTN_KV_WIDTH), prev_block),
                  pl.BlockSpec((tm, MLSTM_WIDTH), cur),
                  pl.BlockSpec((MLSTM_WIDTH, tm), cur_t),
                  pl.BlockSpec((tm, MLSTM_WIDTH), cur),
                  pl.BlockSpec((tm, MLSTM_WIDTH), cur),
                  pl.BlockSpec((tm, LANES), cur),
                  pl.BlockSpec((1, LANES), const),
                  pl.BlockSpec((1, MLSTM_WIDTH), const),
                  pl.BlockSpec((tm, D_MODEL), mid),
                  pl.BlockSpec((tm, D_MODEL), mid),
                  pl.BlockSpec((tm, D_MODEL), mid),
                  pl.BlockSpec((1, 6, D_MODEL), lambda i: (clamp(i - 1) // tiles_per_batch, 0, 0)),
                  pl.BlockSpec((1, D_MODEL), const),
                  pl.BlockSpec((1, D_MODEL), const),
                  pl.BlockSpec((1, 6, D_MODEL), lambda i: (clamp(i - 2) // tiles_per_batch, 0, 0)),
                  pl.BlockSpec((1, D_MODEL), const),
                  pl.BlockSpec(wa.shape, const, pipeline_mode=once),
                  pl.BlockSpec(wb.shape, const, pipeline_mode=once),
                  pl.BlockSpec(wo.shape, const, pipeline_mode=once),
                  pl.BlockSpec(wg.shape, const, pipeline_mode=once),
                  pl.BlockSpec(wu.shape, const, pipeline_mode=once),
                  pl.BlockSpec(wd.shape, const, pipeline_mode=once)],
        out_specs=pl.BlockSpec((tm, D_MODEL), last),
        out_shape=jax.ShapeDtypeStruct((t, D_MODEL), F32),
        scratch_shapes=[pltpu.VMEM((tm, ATTN_Q_WIDTH), BF16),
                        pltpu.VMEM((tm, MLSTM_WIDTH), BF16),
                        pltpu.VMEM((2, tm, D_MODEL), F32),
                        pltpu.VMEM((tm, D_MODEL), BF16),
                        pltpu.VMEM((tm, D_MODEL), BF16),
                        pltpu.VMEM((tm, D_FF), BF16),
                        pltpu.VMEM((MLSTM_HEADS, MLSTM_HEAD_DIM, 2 * MLSTM_HEAD_DIM), F32),
                        pltpu.VMEM((SUBLANES, LANES), F32),
                        pltpu.VMEM((SUBLANES, LANES), F32)],
        compiler_params=pltpu.CompilerParams(dimension_semantics=("arbitrary",),
                                             vmem_limit_bytes=_vmem_limit(est)),
        name="back",
    )(sinks, qa, kv, kv, qm, kt, vm, om, gates, bif, norm_w, x2, ga, gm, mod3,
      g_post_mix, g_pre_ffn, mod3, g_post_ffn, wa, wb, wo, wg, wu, wd)


def kernel(x, c, positions, w_ada, b_ada, g_pre_mix, g_post_mix, w_in, b_if, conv_w, conv_b,
           attn_sinks, mlstm_norm_w, w_branch_attn, w_branch_mlstm, w_out, g_pre_ffn, g_post_ffn,
           w_ffn_gate, w_ffn_up, w_ffn_down):
    batch, seq, d = x.shape
    depth = w_in.shape[0]
    assert d == D_MODEL and seq % TOKEN_TILE == 0 and (batch * seq) % ROPE_TILE == 0
    assert D_FF % FF_CHUNK == 0
    t = batch * seq
    tiles_per_batch = seq // TOKEN_TILE
    x2 = x.reshape(t, d)

    inv_freq = (ROPE_THETA ** (-2.0 * jnp.arange(HEAD_DIM // 2, dtype=F32) / HEAD_DIM)).reshape(-1, 1)
    c_pad = jnp.pad(c, ((0, SUBLANES - batch % SUBLANES), (0, 0))) if batch % SUBLANES else c

    for l in range(depth):
        w_t = w_in[l].T
        cos_t, sin_t, mod, w_q, w_mid = _prep(positions.reshape(1, t), inv_freq, c_pad, w_ada[l],
                                              b_ada[l].reshape(1, -1), w_t)
        mod3 = mod[:batch].reshape(batch, 6, d)
        back_weights = (w_branch_attn[l], w_branch_mlstm[l], w_out[l], w_ffn_gate[l], w_ffn_up[l],
                        w_ffn_down[l])
        qa, kv, qm, kt, vm, om, gates, ga, gm, wa, wb, wo, wg, wu, wd = _inproj(
            x2, mod3, g_pre_mix[l].reshape(1, d), cos_t, sin_t, w_q, w_mid, w_t,
            conv_w[l], conv_b[l].reshape(1, -1), back_weights, tiles_per_batch)
        bif = jnp.pad(b_if[l], (0, LANES - 2 * MLSTM_HEADS)).reshape(1, LANES)
        x2 = _back(attn_sinks[l], qa, kv, qm, kt, vm, om, gates, bif, mlstm_norm_w[l].reshape(1, -1),
                   x2, ga, gm, mod3, g_post_mix[l].reshape(1, d), g_pre_ffn[l].reshape(1, d),
                   g_post_ffn[l].reshape(1, d), wa, wb, wo, wg, wu, wd, tiles_per_batch)
    return x2.reshape(batch, seq, d)
```

```python
import functools

import numpy as np
import jax
import jax.numpy as jnp
from jax import lax
from jax.experimental import pallas as pl
from jax.experimental.pallas import tpu as pltpu

F32 = jnp.float32
BF16 = jnp.bfloat16

D_MODEL = 1024
N_Q_HEADS = 8
N_KV_HEADS = 2
HEAD_DIM = 64
ROPE_THETA = 10000.0
MLSTM_HEADS = 4
MLSTM_HEAD_DIM = 128
CONV_WIDTH = 4
D_FF = 2816
NORM_EPS = 1e-6
ATTN_Q_WIDTH = N_Q_HEADS * HEAD_DIM
ATTN_KV_WIDTH = N_KV_HEADS * HEAD_DIM
MLSTM_WIDTH = MLSTM_HEADS * MLSTM_HEAD_DIM
ATTN_BLOCK = 128

LANES = 128
SUBLANES = 8
MXU_WIDTH = 256
V7X_SCOPED_VMEM_BYTES = 60000 * 1024

TOKEN_TILE = 512
MLSTM_CHUNK = 128
FF_CHUNK = LANES
ROPE_TILE = 4096

C_QA = 0
C_KV = C_QA + ATTN_Q_WIDTH
C_QK = C_KV + 2 * ATTN_KV_WIDTH
C_VM = C_QK + 2 * MLSTM_WIDTH
C_OM = C_VM + MLSTM_WIDTH
C_IF = C_OM + MLSTM_WIDTH
C_GA = C_IF + LANES
C_GM = C_GA + D_MODEL


def _vmem_limit(estimate_bytes):
    return int(min(V7X_SCOPED_VMEM_BYTES, max(estimate_bytes, 16 * 1024 * 1024)))


def _rms(x, g):
    return (x * lax.rsqrt(jnp.mean(x * x, axis=-1, keepdims=True) + NORM_EPS)) * g


def _prep_kernel(pos_ref, freq_ref, c_ref, w_ref, b_ref, wq_f32, wmid_f32,
                 cos_ref, sin_ref, mod_ref, wq_ref, wmid_ref):
    wq_ref[...] = wq_f32[...].astype(BF16)
    wmid_ref[...] = wmid_f32[...].astype(BF16)
    mod_ref[...] = jnp.dot(c_ref[...].astype(BF16), w_ref[...].astype(BF16),
                           preferred_element_type=F32) + b_ref[...]
    ang = freq_ref[...] * pos_ref[...].astype(F32)
    c = jnp.cos(ang)
    s = jnp.sin(ang)
    cos_ref[...] = jnp.concatenate([c, c, c, c], axis=0).T
    sin_ref[...] = jnp.concatenate([-s, s, -s, s], axis=0).T


def _q_source_block(out_block, blocks_per_head):
    per_kv = N_Q_HEADS // N_KV_HEADS
    head = out_block // blocks_per_head
    src_head = (head % N_KV_HEADS) * per_kv + head // N_KV_HEADS
    return src_head * blocks_per_head + out_block % blocks_per_head


def _prep(pos_row, inv_freq, c_pad, w_ada, b_ada, w_t):
    t = pos_row.shape[1]
    half = inv_freq.shape[0]
    rows = c_pad.shape[0]
    n = w_ada.shape[1]
    steps = t // ROPE_TILE
    cols = n // steps
    q_rows = ATTN_Q_WIDTH // steps
    mid_rows = (C_IF - C_KV) // steps
    assert n % steps == 0 and cols % LANES == 0
    assert ATTN_Q_WIDTH % steps == 0 and HEAD_DIM % q_rows == 0 and q_rows % (2 * SUBLANES) == 0
    assert (C_IF - C_KV) % steps == 0 and mid_rows % (2 * SUBLANES) == 0
    return pl.pallas_call(
        _prep_kernel,
        grid=(steps,),
        in_specs=[pl.BlockSpec((1, ROPE_TILE), lambda i: (0, i)),
                  pl.BlockSpec((half, 1), lambda i: (0, 0)),
                  pl.BlockSpec((rows, D_MODEL), lambda i: (0, 0)),
                  pl.BlockSpec((D_MODEL, cols), lambda i: (0, i)),
                  pl.BlockSpec((1, cols), lambda i: (0, i)),
                  pl.BlockSpec((q_rows, D_MODEL), lambda i: (_q_source_block(i, HEAD_DIM // q_rows), 0)),
                  pl.BlockSpec((pl.Element(mid_rows), pl.Element(D_MODEL)),
                               lambda i: (pl.multiple_of(C_KV + i * mid_rows, 2 * SUBLANES), 0))],
        out_specs=[pl.BlockSpec((ROPE_TILE, LANES), lambda i: (i, 0)),
                   pl.BlockSpec((ROPE_TILE, LANES), lambda i: (i, 0)),
                   pl.BlockSpec((rows, cols), lambda i: (0, i)),
                   pl.BlockSpec((q_rows, D_MODEL), lambda i: (i, 0)),
                   pl.BlockSpec((mid_rows, D_MODEL), lambda i: (i, 0))],
        out_shape=[jax.ShapeDtypeStruct((t, LANES), F32),
                   jax.ShapeDtypeStruct((t, LANES), F32),
                   jax.ShapeDtypeStruct((rows, n), F32),
                   jax.ShapeDtypeStruct((ATTN_Q_WIDTH, D_MODEL), BF16),
                   jax.ShapeDtypeStruct((C_IF - C_KV, D_MODEL), BF16)],
        name="prep",
    )(pos_row, inv_freq, c_pad, w_ada, b_ada, w_t, w_t)


def _modulated_norm(x, g, mod_ref, row):
    return (_rms(x, g * (1.0 + mod_ref[0, row + 1:row + 2, :])) + mod_ref[0, row:row + 1, :]).astype(BF16)


def _inproj_kernel(n_tiles, tiles_per_batch, x0_ref, mod0_ref, xn_ref, modn_ref, g_ref, cos_ref, sin_ref,
                   wq_ref, wmid_ref, wtail_ref, cw_ref, cb_ref, *rest):
    n_cast = len(BACK_WEIGHT_BLOCKS)
    cast_in, rest = rest[:n_cast], rest[n_cast:]
    (qa_ref, kv_ref, qm_ref, kt_ref, vm_ref, om_ref, if_ref, ga_ref, gm_ref), rest = rest[:9], rest[9:]
    cast_out, (s_ref, raw_s, rot_s, wtail_s) = rest[:n_cast], rest[n_cast:]
    for src, dst in zip(cast_in, cast_out):
        dst[...] = src[...].astype(BF16)
    _inproj_body(n_tiles, tiles_per_batch, x0_ref, mod0_ref, xn_ref, modn_ref, g_ref, cos_ref, sin_ref,
                 wq_ref, wmid_ref, wtail_ref, cw_ref, cb_ref,
                 qa_ref, kv_ref, qm_ref, kt_ref, vm_ref, om_ref, if_ref, ga_ref, gm_ref,
                 s_ref, raw_s, rot_s, wtail_s)


BACK_WEIGHT_BLOCKS = (16, 16, 16, 16, 16, 64)


def _inproj_body(n_tiles, tiles_per_batch, x0_ref, mod0_ref, xn_ref, modn_ref, g_ref, cos_ref, sin_ref,
                 wq_ref, wmid_ref, wtail_ref, cw_ref, cb_ref,
                 qa_ref, kv_ref, qm_ref, kt_ref, vm_ref, om_ref, if_ref, ga_ref, gm_ref,
                 s_ref, raw_s, rot_s, wtail_s):
    tm = TOKEN_TILE
    i = pl.program_id(0)
    par = i % 2
    post_tile = jnp.maximum(i - 1, 0)
    h_cur = s_ref.at[par]
    h_next = s_ref.at[1 - par]
    out_a = s_ref.at[2 + par]
    out_b = s_ref.at[4 + par]
    kv_lo = ATTN_Q_WIDTH

    @pl.when(i == 0)
    def _():
        s_ref[0] = _modulated_norm(x0_ref[...], g_ref[...], mod0_ref, 0)
        raw_s[...] = jnp.zeros_like(raw_s)
        rot_s[...] = jnp.zeros_like(rot_s)
        n_if = 2 * MLSTM_HEADS
        wtail_s[0:LANES, :] = jnp.concatenate(
            [wtail_ref[0:n_if, :], jnp.zeros((LANES - n_if, D_MODEL), F32)], axis=0).astype(BF16)
        for r0 in range(0, 2 * D_MODEL, MXU_WIDTH):
            wtail_s[LANES + r0:LANES + r0 + MXU_WIDTH, :] = (
                wtail_ref[n_if + r0:n_if + r0 + MXU_WIDTH, :].astype(BF16))

    @pl.when(post_tile % tiles_per_batch == 0)
    def _():
        raw_s[0:SUBLANES, :] = jnp.zeros((SUBLANES, raw_s.shape[1]), F32)

    def proj(lo, width):
        if lo < C_KV:
            w = wq_ref[lo:lo + width, :]
        elif lo < C_IF:
            w = wmid_ref[lo - C_KV:lo - C_KV + width, :]
        else:
            w = wtail_s[lo - C_IF:lo - C_IF + width, :]
        return lax.dot_general(h_cur[...], w, (((1,), (1,)), ((), ())), preferred_element_type=F32)

    lane = lax.broadcasted_iota(jnp.int32, (1, LANES), 1)
    first_half = (lane % HEAD_DIM) < (HEAD_DIM // 2)

    def rope(t):
        swapped = jnp.where(first_half, pltpu.roll(t, LANES - HEAD_DIM // 2, axis=1),
                            pltpu.roll(t, HEAD_DIM // 2, axis=1))
        return t * cos_ref[...] + swapped * sin_ref[...]

    def post_q():
        for p in range(ATTN_Q_WIDTH // LANES):
            sl = slice(p * LANES, (p + 1) * LANES)
            out_a[:, sl] = (rope(rot_s[:, sl]) * (HEAD_DIM ** -0.5)).astype(BF16)

    def post_kv():
        out_a[:, kv_lo:kv_lo + LANES] = rope(rot_s[:, kv_lo:kv_lo + LANES]).astype(BF16)
        out_a[:, kv_lo + LANES:kv_lo + 2 * LANES] = rot_s[:, kv_lo + LANES:].astype(BF16)

    blk = MXU_WIDTH

    def post_conv(cbk):
        cs = slice(cbk * blk, (cbk + 1) * blk)
        acc = cb_ref[:, cs]
        for sh in range(CONV_WIDTH):
            acc = acc + (cw_ref[CONV_WIDTH - 1 - sh:CONV_WIDTH - sh, cs]
                         * raw_s[SUBLANES - sh:SUBLANES - sh + tm, cs])
        raw_s[0:SUBLANES, cs] = raw_s[tm:tm + SUBLANES, cs]
        a = acc * jax.nn.sigmoid(acc)
        if cbk * blk < MLSTM_WIDTH:
            out_b[:, cs] = a.astype(BF16)
        else:
            a = a * (MLSTM_HEAD_DIM ** -0.5)
            for d0 in range(0, blk, LANES):
                r0 = cbk * blk - MLSTM_WIDTH + d0
                for t0 in range(0, tm, LANES):
                    out_b[r0:r0 + LANES, MLSTM_WIDTH + t0:MLSTM_WIDTH + t0 + LANES] = (
                        a[t0:t0 + LANES, d0:d0 + LANES].T.astype(BF16))

    def ship_q():
        qa_ref[...] = out_a[:, :ATTN_Q_WIDTH]

    def ship_kv():
        kv_ref[...] = out_a[:, kv_lo:kv_lo + 2 * ATTN_KV_WIDTH]

    def ship_conv(cbk):
        cs = slice(cbk * blk, (cbk + 1) * blk)
        if cbk * blk < MLSTM_WIDTH:
            qm_ref[:, cs] = out_b[:, cs]
        else:
            rs = slice(cbk * blk - MLSTM_WIDTH, (cbk + 1) * blk - MLSTM_WIDTH)
            kt_ref[rs, :] = out_b[rs, MLSTM_WIDTH:MLSTM_WIDTH + tm]

    def norm_next(part):
        rows = slice(part * tm // INPROJ_NORM_PARTS, (part + 1) * tm // INPROJ_NORM_PARTS)
        h_next[rows, :] = _modulated_norm(xn_ref[rows, :], g_ref[...], modn_ref, 0)

    def main_rot(lo_w, lo_s, width):
        rot_s[:, lo_s:lo_s + width] = proj(lo_w, width)

    def main_conv(cbk):
        raw_s[SUBLANES:, cbk * blk:(cbk + 1) * blk] = proj(C_QK + cbk * blk, blk)

    def main_direct(out_ref, lo_w, lo, width):
        out_ref[:, lo:lo + width] = proj(lo_w + lo, width).astype(out_ref.dtype)

    half = D_MODEL // 2
    main_direct(vm_ref, C_VM, 0, MLSTM_WIDTH)
    post_q()
    main_rot(C_QA, 0, ATTN_Q_WIDTH)
    ship_q()
    post_kv()
    post_conv(0)
    main_rot(C_KV, kv_lo, 2 * ATTN_KV_WIDTH)
    ship_kv()
    main_conv(0)
    ship_conv(0)
    post_conv(1)
    main_direct(om_ref, C_OM, 0, MLSTM_WIDTH)
    main_conv(1)
    ship_conv(1)
    post_conv(2)
    main_direct(if_ref, C_IF, 0, LANES)
    main_conv(2)
    ship_conv(2)
    post_conv(3)
    main_direct(ga_ref, C_GA, 0, half)
    main_conv(3)
    ship_conv(3)
    norm_next(0)
    main_direct(ga_ref, C_GA, half, half)
    norm_next(1)
    main_direct(gm_ref, C_GM, 0, half)
    norm_next(2)
    norm_next(3)
    main_direct(gm_ref, C_GM, half, half)


INPROJ_NORM_PARTS = 4


def _inproj(x2, mod3, g_pre, cos_t, sin_t, w_q, w_mid, w_t, conv_w, conv_b, back_weights, tiles_per_batch):
    t = x2.shape[0]
    tm = TOKEN_TILE
    n = t // tm
    main = lambda i: (jnp.minimum(i, n - 1), 0)
    post = lambda i: (jnp.maximum(i - 1, 0), 0)
    nxt = lambda i: (jnp.minimum(i + 1, n - 1), 0)
    const = lambda i: (0, 0)
    once = pl.Buffered(1)
    outs = [(ATTN_Q_WIDTH, BF16, post), (2 * ATTN_KV_WIDTH, BF16, post), (MLSTM_WIDTH, BF16, post),
            (None, BF16, None), (MLSTM_WIDTH, BF16, main), (MLSTM_WIDTH, BF16, main),
            (LANES, F32, main), (D_MODEL, BF16, main), (D_MODEL, BF16, main)]
    out_specs = [pl.BlockSpec((MLSTM_WIDTH, tm), lambda i: (0, jnp.maximum(i - 1, 0))) if w is None
                 else pl.BlockSpec((tm, w), m) for w, _, m in outs]
    out_shape = [jax.ShapeDtypeStruct((MLSTM_WIDTH, t) if w is None else (t, w), d) for w, d, _ in outs]
    out_bytes = sum((w or MLSTM_WIDTH) * np.dtype(d).itemsize for w, d, _ in outs) * tm
    rot_cols = ATTN_Q_WIDTH + 2 * ATTN_KV_WIDTH
    assert tm == MLSTM_WIDTH
    scratch_bytes = (6 * tm * D_MODEL * 2 + (tm + SUBLANES) * 2 * MLSTM_WIDTH * 4 + tm * rot_cols * 4)
    tail_rows = w_t.shape[0] - C_IF
    tail_aligned = LANES + 2 * D_MODEL
    assert tail_rows == 2 * MLSTM_HEADS + 2 * D_MODEL
    weight_bytes = (w_q.size + w_mid.size + tail_aligned * D_MODEL) * 2 + tail_rows * D_MODEL * 4
    cast_in, cast_out, cast_bytes = [], [], 0
    for k, (w, rows) in enumerate(zip(back_weights, BACK_WEIGHT_BLOCKS)):
        blocks = w.shape[0] // rows
        assert w.shape[0] % rows == 0 and blocks <= n and rows % (2 * SUBLANES) == 0
        dst = lambda i, blocks=blocks: (jnp.minimum(i, blocks - 1), 0)
        src = dst if k else (lambda i, blocks=blocks, per_head=HEAD_DIM // rows:
                             (_q_source_block(jnp.minimum(i, blocks - 1), per_head), 0))
        cast_in.append(pl.BlockSpec((rows, w.shape[1]), src))
        cast_out.append(pl.BlockSpec((rows, w.shape[1]), dst))
        out_shape.append(jax.ShapeDtypeStruct(w.shape, BF16))
        cast_bytes += 2 * rows * w.shape[1] * 6
    est = (weight_bytes + 3 * tm * D_MODEL * 4 + 2 * out_bytes + 4 * tm * LANES * 4
           + scratch_bytes + 4 * tm * D_MODEL * 4 + cast_bytes)
    return pl.pallas_call(
        functools.partial(_inproj_kernel, n, tiles_per_batch),
        grid=(n + 1,),
        in_specs=[pl.BlockSpec((tm, D_MODEL), const, pipeline_mode=once),
                  pl.BlockSpec((1, 6, D_MODEL), lambda i: (0, 0, 0)),
                  pl.BlockSpec((tm, D_MODEL), nxt),
                  pl.BlockSpec((1, 6, D_MODEL),
                               lambda i: (jnp.minimum(i + 1, n - 1) // tiles_per_batch, 0, 0)),
                  pl.BlockSpec((1, D_MODEL), const),
                  pl.BlockSpec((tm, LANES), post),
                  pl.BlockSpec((tm, LANES), post),
                  pl.BlockSpec(w_q.shape, const, pipeline_mode=once),
                  pl.BlockSpec(w_mid.shape, const, pipeline_mode=once),
                  pl.BlockSpec((pl.Element(tail_rows), pl.Element(D_MODEL)), lambda i: (C_IF, 0),
                               pipeline_mode=once),
                  pl.BlockSpec((CONV_WIDTH, 2 * MLSTM_WIDTH), const),
                  pl.BlockSpec((1, 2 * MLSTM_WIDTH), const)] + cast_in,
        out_specs=out_specs + cast_out,
        out_shape=out_shape,
        scratch_shapes=[pltpu.VMEM((6, tm, D_MODEL), BF16),
                        pltpu.VMEM((tm + SUBLANES, 2 * MLSTM_WIDTH), F32),
                        pltpu.VMEM((tm, rot_cols), F32),
                        pltpu.VMEM((tail_aligned, D_MODEL), BF16)],
        compiler_params=pltpu.CompilerParams(dimension_semantics=("arbitrary",),
                                             vmem_limit_bytes=_vmem_limit(est)),
        name="inproj",
    )(x2, mod3, x2, mod3, g_pre, cos_t, sin_t, w_q, w_mid, w_t, conv_w, conv_b, *back_weights)


def _attn_thread(j, sink_ref, q_ref, kv_ref, kvp_ref, o_ref):
    blk = ATTN_BLOCK
    nblk = TOKEN_TILE // blk
    ngrp = ATTN_Q_WIDTH // LANES
    lane = lax.broadcasted_iota(jnp.int32, (1, LANES), 1)
    low = lane < HEAD_DIM
    qi = lax.broadcasted_iota(jnp.int32, (blk, 2 * blk), 0)
    kj = lax.broadcasted_iota(jnp.int32, (blk, 2 * blk), 1)
    in_cur = (kj >= blk) & (kj - blk <= qi)
    in_prev = (kj < blk) & (kj > qi)
    in_prev_first = (kj < blk) & (kj > qi + jnp.where(j > 0, 0, blk))
    ones = jnp.ones((2 * blk, LANES), F32)
    zeros = jnp.zeros((2 * blk, LANES), F32)
    one_lo = jnp.where(low, ones, zeros).astype(BF16)
    one_hi = jnp.where(low, zeros, ones).astype(BF16)
    band = {}
    live = {}

    def prep(i):
        cur = kv_ref[i * blk:(i + 1) * blk, :]
        prev = kvp_ref[...] if i == 0 else kv_ref[(i - 1) * blk:i * blk, :]
        kv = jnp.concatenate([prev, cur], axis=0).astype(F32)
        k = kv[:, :LANES]
        v = kv[:, LANES:]
        k_cat = jnp.concatenate([jnp.where(low, k, 0.0), jnp.where(low, 0.0, k)], axis=0).astype(BF16)
        v_cat = jnp.concatenate(
            [jnp.concatenate([jnp.where(low, v, 0.0).astype(BF16), one_lo], axis=1),
             jnp.concatenate([jnp.where(low, 0.0, v).astype(BF16), one_hi], axis=1)], axis=0)
        band[i] = (k_cat, v_cat, in_cur | (in_prev_first if i == 0 else in_prev))

    def scores(u):
        i, p = divmod(u, ngrp)
        q = q_ref[i * blk:(i + 1) * blk, p * LANES:(p + 1) * LANES]
        live[u] = lax.dot_general(q, band[i][0], (((1,), (1,)), ((), ())), preferred_element_type=F32)

    def softmax(u):
        i, p = divmod(u, ngrp)
        s = live[u]
        mask = band[i][2]
        s0 = jnp.where(mask, s[:, :2 * blk], -jnp.inf)
        s1 = jnp.where(mask, s[:, 2 * blk:], -jnp.inf)
        m0 = jnp.maximum(jnp.max(s0, axis=-1, keepdims=True), sink_ref[p])
        m1 = jnp.maximum(jnp.max(s1, axis=-1, keepdims=True), sink_ref[p + N_Q_HEADS // N_KV_HEADS])
        pr = jnp.concatenate([jnp.exp(s0 - m0), jnp.exp(s1 - m1)], axis=1).astype(BF16)
        live[u] = (pr, m0, m1)

    def values(u):
        i, p = divmod(u, ngrp)
        pr, m0, m1 = live.pop(u)
        r = jnp.dot(pr, band[i][1], preferred_element_type=F32)
        den = r[:, LANES:] + jnp.where(low, jnp.exp(sink_ref[p] - m0),
                                       jnp.exp(sink_ref[p + N_Q_HEADS // N_KV_HEADS] - m1))
        o_ref[i * blk:(i + 1) * blk, p * LANES:(p + 1) * LANES] = (r[:, :LANES] / den).astype(BF16)

    n = nblk * ngrp
    prep(0)
    for k in range(n + 2):
        if k < n:
            scores(k)
            if k + 1 < n and (k + 1) % ngrp == 0:
                prep((k + 1) // ngrp)
        if 0 <= k - 2 < n:
            values(k - 2)
        if 0 <= k - 1 < n:
            softmax(k - 1)
        yield


ATTN_STAGES = (TOKEN_TILE // ATTN_BLOCK) * (ATTN_Q_WIDTH // LANES) + 2


def _mlstm_thread(q_ref, kt_ref, v_ref, om_ref, if_ref, bif_ref, nw_ref, y_ref,
                  cn_ref, mrow_ref, mcol_ref):
    ts = TOKEN_TILE
    L = MLSTM_CHUNK
    D = MLSTM_HEAD_DIM
    nch = ts // L

    gates = if_ref[...] + bif_ref[...]
    logf_all = jax.nn.log_sigmoid(pltpu.roll(gates, LANES - MLSTM_HEADS, axis=1))

    ti = lax.broadcasted_iota(jnp.int32, (L, L), 0)
    si = lax.broadcasted_iota(jnp.int32, (L, L), 1)
    causal = si <= ti
    tri = causal.astype(BF16)
    ones_v = jnp.ones((L, LANES), BF16)
    chunk = {}
    head = {}

    def rows_of(c):
        return slice(c * L, (c + 1) * L)

    def p1(c):
        lf = logf_all[rows_of(c)]
        lf_hi = lf.astype(BF16)
        rem = lf - lf_hi.astype(F32)
        lf_mid = rem.astype(BF16)
        lf_lo = (rem - lf_mid.astype(F32)).astype(BF16)
        chunk[c] = (jnp.dot(tri, lf_hi, preferred_element_type=F32)
                    + jnp.dot(tri, lf_mid, preferred_element_type=F32)
                    + jnp.dot(tri, lf_lo, preferred_element_type=F32))

    def p2(c):
        b = chunk[c]
        r = b - gates[rows_of(c)]
        pm = -r
        sh = 1
        while sh < L:
            pm = jnp.maximum(pm, jnp.where(ti >= sh, pltpu.roll(pm, sh, axis=0), -jnp.inf))
            sh *= 2
        m_row = mrow_ref[0:1, :]
        inter = b + m_row
        mt = jnp.maximum(inter, b + pm)
        b_last = b[L - 1:L, :]
        m_new_row = jnp.maximum(b_last + m_row, jnp.max(b_last - r, axis=0, keepdims=True))
        mrow_ref[...] = jnp.broadcast_to(m_new_row, mrow_ref.shape)
        b_t = b.T[:SUBLANES]
        r_t = r.T[:SUBLANES]
        m_col = mcol_ref[:, 0:1]
        bl_col = b_t[:, L - 1:L]
        a_t = bl_col - r_t
        m_new_col = jnp.maximum(bl_col + m_col, jnp.max(a_t, axis=1, keepdims=True))
        mcol_ref[...] = jnp.broadcast_to(m_new_col, mcol_ref.shape)
        chunk[c] = dict(e_col=b - mt, r_t=r_t, w_inter=jnp.exp(inter - mt), e_mt=jnp.exp(-mt),
                        decay_row=jnp.exp(b_last + m_row - m_new_row),
                        wk_t=jnp.exp(a_t - m_new_col))

    def ha(c, h):
        hs = slice(h * D, (h + 1) * D)
        qh = q_ref[rows_of(c), hs]
        kt = kt_ref[hs, rows_of(c)]
        head[c, h] = dict(qh=qh, kt=kt, s=jnp.dot(qh, kt, preferred_element_type=F32))

    def hb(c, h):
        st, ch = head[c, h], chunk[c]
        w_intra = jnp.where(causal, jnp.exp(ch["e_col"][:, h:h + 1] - ch["r_t"][h:h + 1, :]), 0.0)
        pr = (st.pop("s") * w_intra).astype(BF16)
        qw = (st.pop("qh").astype(F32) * ch["w_inter"][:, h:h + 1]).astype(BF16)
        st["lhs"] = jnp.concatenate([pr, qw], axis=1)
        st["kw"] = (st.pop("kt").astype(F32) * ch["wk_t"][h:h + 1, :]).astype(BF16)

    def hc(c, h):
        st = head[c, h]
        hs = slice(h * D, (h + 1) * D)
        v_ext = jnp.concatenate([v_ref[rows_of(c), hs], ones_v], axis=1)
        rhs = jnp.concatenate([v_ext, cn_ref[h].astype(BF16)], axis=0)
        st["nd"] = jnp.dot(st.pop("lhs"), rhs, preferred_element_type=F32)
        st["kv"] = jnp.dot(st.pop("kw"), v_ext, preferred_element_type=F32)

    def hd(c, h):
        st, ch = head.pop((c, h)), chunk[c]
        hs = slice(h * D, (h + 1) * D)
        nd = st["nd"]
        hh = nd[:, :D] / jnp.maximum(jnp.abs(nd[:, D:]), ch["e_mt"][:, h:h + 1])
        cn_ref[h] = ch["decay_row"][:, h:h + 1] * cn_ref[h] + st["kv"]
        mu = jnp.mean(hh, axis=-1, keepdims=True)
        xc = hh - mu
        var = jnp.mean(xc * xc, axis=-1, keepdims=True)
        yn = (xc * lax.rsqrt(var + NORM_EPS)) * nw_ref[:, hs]
        y_ref[rows_of(c), hs] = (jax.nn.sigmoid(om_ref[rows_of(c), hs].astype(F32)) * yn).astype(BF16)

    half = MLSTM_HEADS // 2
    for g in range(MLSTM_STAGES):
        for c in range(nch):
            o = g - MLSTM_CHUNK_SKEW * c
            if o == 0:
                p1(c)
            elif o == 1:
                for h in range(MLSTM_HEADS):
                    ha(c, h)
                p2(c)
            elif o == 2:
                for h in range(half):
                    hb(c, h)
            elif o == 3:
                for h in range(half):
                    hc(c, h)
                for h in range(half, MLSTM_HEADS):
                    hb(c, h)
            elif o == 4:
                for h in range(half, MLSTM_HEADS):
                    hc(c, h)
                for h in range(half):
                    hd(c, h)
            elif o == 5:
                for h in range(half, MLSTM_HEADS):
                    hd(c, h)
        yield


MLSTM_CHUNK_SKEW = 4
MLSTM_STAGES = MLSTM_CHUNK_SKEW * (TOKEN_TILE // MLSTM_CHUNK - 1) + 6


def _merge_thread(x_ref, ya_ref, ym_ref, ga_ref, gm_ref, mod_ref, gpm_ref, gqf_ref,
                  wa_ref, wb_ref, wo_ref, x1_ref, h2_ref):
    tm = TOKEN_TILE
    parts = [slice(r * tm // MERGE_ROW_PARTS, (r + 1) * tm // MERGE_ROW_PARTS)
             for r in range(MERGE_ROW_PARTS)]
    gate_m = mod_ref[0, 2:3, :]
    ba = jnp.dot(ya_ref[...], wa_ref[...], preferred_element_type=F32)
    yield
    bb = jnp.dot(ym_ref[...], wb_ref[...], preferred_element_type=F32)
    yield
    merged = []
    for rows in parts:
        merged.append((jax.nn.sigmoid(ga_ref[rows, :].astype(F32)) * ba[rows]
                       + jax.nn.sigmoid(gm_ref[rows, :].astype(F32)) * bb[rows]).astype(BF16))
        yield
    mix = jnp.dot(jnp.concatenate(merged, axis=0), wo_ref[...], preferred_element_type=F32)
    yield
    for rows in parts:
        x1 = x_ref[rows, :] + _rms(mix[rows], gate_m * gpm_ref[...])
        x1_ref[rows, :] = x1
        yield
        h2_ref[rows, :] = _modulated_norm(x1, gqf_ref[...], mod_ref, 3)
        yield


MERGE_ROW_PARTS = 4
MERGE_STAGES = 3 + 3 * MERGE_ROW_PARTS


def _ffn_thread(x1_ref, h2_ref, mod_ref, gpf_ref, wg_ref, wu_ref, wd_ref, o_ref, act_s, after_up):
    gate_f = mod_ref[0, 5:6, :]
    for c in range(D_FF // FF_CHUNK):
        cs = slice(c * FF_CHUNK, (c + 1) * FF_CHUNK)
        gu = jnp.dot(h2_ref[...], jnp.concatenate([wg_ref[:, cs], wu_ref[:, cs]], axis=1),
                     preferred_element_type=F32)
        g = gu[:, :FF_CHUNK]
        act_s[:, c * FF_CHUNK:(c + 1) * FF_CHUNK] = ((g * jax.nn.sigmoid(g)) * gu[:, FF_CHUNK:]).astype(BF16)
        yield
    after_up()
    ff = []
    for c in range(FFN_DOWN_SPLIT):
        cs = slice(c * D_MODEL // FFN_DOWN_SPLIT, (c + 1) * D_MODEL // FFN_DOWN_SPLIT)
        ff.append(jnp.dot(act_s[...], wd_ref[:, cs], preferred_element_type=F32))
        yield
    o_ref[...] = x1_ref[...] + _rms(jnp.concatenate(ff, axis=1), gate_f * gpf_ref[...])
    yield


FFN_DOWN_SPLIT = 2
FFN_UP_STAGES = D_FF // FF_CHUNK
FFN_STAGES = FFN_UP_STAGES + FFN_DOWN_SPLIT + 1


def _interleave(main, n_main, fills):
    done = [0] * len(fills)
    for k in range(n_main):
        next(main)
        for f, (gen, count, first, last) in enumerate(fills):
            span = last - first + 1
            want = 0 if k < first else min(count, ((k - first + 1) * count + span - 1) // span)
            while done[f] < want:
                next(gen)
                done[f] += 1
    assert all(d == f[1] for d, f in zip(done, fills))
    assert next(main, None) is None and all(next(f[0], None) is None for f in fills)


def _back_kernel(n_tiles, tiles_per_batch, sink_ref,
                 qa_ref, kv_ref, kvp_ref, qm_ref, kt_ref, vm_ref, om_ref, if_ref, bif_ref, nw_ref,
                 x_ref, ga_ref, gm_ref, modm_ref, gpm_ref, gqf_ref,
                 modf_ref, gpf_ref,
                 wa_ref, wb_ref, wo_ref, wg_ref, wu_ref, wd_ref, o_ref,
                 ya_s, ym_s, x1_s, h2_cur, h2_next, act_s, cn_ref, mrow_ref, mcol_ref):
    i = pl.program_id(0)
    j = jnp.minimum(i, n_tiles - 1) % tiles_per_batch
    slot = i % 2

    @pl.when(i == 0)
    def _():
        ya_s[...] = jnp.zeros_like(ya_s)
        ym_s[...] = jnp.zeros_like(ym_s)
        x1_s[...] = jnp.zeros_like(x1_s)
        h2_cur[...] = jnp.zeros_like(h2_cur)

    @pl.when(j == 0)
    def _():
        cn_ref[...] = jnp.zeros_like(cn_ref)
        mrow_ref[...] = jnp.zeros_like(mrow_ref)
        mcol_ref[...] = jnp.zeros_like(mcol_ref)

    def hand_over():
        h2_cur[...] = h2_next[...]

    ffn = _ffn_thread(x1_s.at[slot], h2_cur, modf_ref, gpf_ref, wg_ref, wu_ref, wd_ref, o_ref, act_s,
                      hand_over)
    merge = _merge_thread(x_ref, ya_s, ym_s, ga_ref, gm_ref, modm_ref, gpm_ref, gqf_ref,
                          wa_ref, wb_ref, wo_ref, x1_s.at[1 - slot], h2_next)
    attn = _attn_thread(j, sink_ref, qa_ref, kv_ref, kvp_ref, ya_s)
    mlstm = _mlstm_thread(qm_ref, kt_ref, vm_ref, om_ref, if_ref, bif_ref, nw_ref, ym_s,
                          cn_ref, mrow_ref, mcol_ref)
    _interleave(ffn, FFN_STAGES,
                [(merge, MERGE_STAGES, 0, FFN_UP_STAGES - 2),
                 (attn, ATTN_STAGES, 2, FFN_STAGES - 2),
                 (mlstm, MLSTM_STAGES, 2, FFN_STAGES - 2)])


def _back(sinks, qa, kv, qm, kt, vm, om, gates, bif, norm_w, x2, ga, gm, mod3,
          g_post_mix, g_pre_ffn, g_post_ffn, wa, wb, wo, wg, wu, wd, tiles_per_batch):
    t = x2.shape[0]
    tm = TOKEN_TILE
    n = t // tm
    per_tile = tm // ATTN_BLOCK
    clamp = lambda v: jnp.clip(v, 0, n - 1)
    cur = lambda i: (clamp(i), 0)
    cur_t = lambda i: (0, clamp(i))

    def prev_block(i):
        tile = clamp(i)
        first = (tile // tiles_per_batch) * tiles_per_batch * per_tile
        return (jnp.maximum(tile * per_tile - 1, first), 0)

    mid = lambda i: (clamp(i - 1), 0)
    last = lambda i: (clamp(i - 2), 0)
    const = lambda i: (0, 0)
    once = pl.Buffered(1)
    weight_bytes = 2 * (wa.size + wb.size + wo.size + wg.size + wu.size + wd.size)
    mixer_in = tm * (ATTN_Q_WIDTH + 2 * ATTN_KV_WIDTH + 4 * MLSTM_WIDTH) * 2 + tm * LANES * 4
    scratch_bytes = tm * (D_FF + 2 * MLSTM_WIDTH + 2 * D_MODEL) * 2 + 2 * tm * D_MODEL * 4
    est = (weight_bytes + 2 * 2 * tm * D_MODEL * 4 + 2 * tm * 2 * D_MODEL * 2 + 2 * mixer_in
           + scratch_bytes + 6 * tm * D_MODEL * 4)
    return pl.pallas_call(
        functools.partial(_back_kernel, n, tiles_per_batch),
        grid=(n + 2,),
        in_specs=[pl.BlockSpec(memory_space=pltpu.SMEM),
                  pl.BlockSpec((tm, ATTN_Q_WIDTH), cur),
                  pl.BlockSpec((tm, 2 * ATTN_KV_WIDTH), cur),
                  pl.BlockSpec((ATTN_BLOCK, 2 * ATTN_KV_WIDTH), prev_block),
                  pl.BlockSpec((tm, MLSTM_WIDTH), cur),
                  pl.BlockSpec((MLSTM_WIDTH, tm), cur_t),
                  pl.BlockSpec((tm, MLSTM_WIDTH), cur),
                  pl.BlockSpec((tm, MLSTM_WIDTH), cur),
                  pl.BlockSpec((tm, LANES), cur),
                  pl.BlockSpec((1, LANES), const),
                  pl.BlockSpec((1, MLSTM_WIDTH), const),
                  pl.BlockSpec((tm, D_MODEL), mid),
                  pl.BlockSpec((tm, D_MODEL), mid),
                  pl.BlockSpec((tm, D_MODEL), mid),
                  pl.BlockSpec((1, 6, D_MODEL), lambda i: (clamp(i - 1) // tiles_per_batch, 0, 0)),
                  pl.BlockSpec((1, D_MODEL), const),
                  pl.BlockSpec((1, D_MODEL), const),
                  pl.BlockSpec((1, 6, D_MODEL), lambda i: (clamp(i - 2) // tiles_per_batch, 0, 0)),
                  pl.BlockSpec((1, D_MODEL), const),
                  pl.BlockSpec(wa.shape, const, pipeline_mode=once),
                  pl.BlockSpec(wb.shape, const, pipeline_mode=once),
                  pl.BlockSpec(wo.shape, const, pipeline_mode=once),
                  pl.BlockSpec(wg.shape, const, pipeline_mode=once),
                  pl.BlockSpec(wu.shape, const, pipeline_mode=once),
                  pl.BlockSpec(wd.shape, const, pipeline_mode=once)],
        out_specs=pl.BlockSpec((tm, D_MODEL), last),
        out_shape=jax.ShapeDtypeStruct((t, D_MODEL), F32),
        scratch_shapes=[pltpu.VMEM((tm, ATTN_Q_WIDTH), BF16),
                        pltpu.VMEM((tm, MLSTM_WIDTH), BF16),
                        pltpu.VMEM((2, tm, D_MODEL), F32),
                        pltpu.VMEM((tm, D_MODEL), BF16),
                        pltpu.VMEM((tm, D_MODEL), BF16),
                        pltpu.VMEM((tm, D_FF), BF16),
                        pltpu.VMEM((MLSTM_HEADS, MLSTM_HEAD_DIM, 2 * MLSTM_HEAD_DIM), F32),
                        pltpu.VMEM((SUBLANES, LANES), F32),
                        pltpu.VMEM((SUBLANES, LANES), F32)],
        compiler_params=pltpu.CompilerParams(dimension_semantics=("arbitrary",),
                                             vmem_limit_bytes=_vmem_limit(est)),
        name="back",
    )(sinks, qa, kv, kv, qm, kt, vm, om, gates, bif, norm_w, x2, ga, gm, mod3,
      g_post_mix, g_pre_ffn, mod3, g_post_ffn, wa, wb, wo, wg, wu, wd)


def kernel(x, c, positions, w_ada, b_ada, g_pre_mix, g_post_mix, w_in, b_if, conv_w, conv_b,
           attn_sinks, mlstm_norm_w, w_branch_attn, w_branch_mlstm, w_out, g_pre_ffn, g_post_ffn,
           w_ffn_gate, w_ffn_up, w_ffn_down):
    batch, seq, d = x.shape
    depth = w_in.shape[0]
    assert d == D_MODEL and seq % TOKEN_TILE == 0 and (batch * seq) % ROPE_TILE == 0
    assert D_FF % FF_CHUNK == 0
    t = batch * seq
    tiles_per_batch = seq // TOKEN_TILE
    x2 = x.reshape(t, d)

    inv_freq = (ROPE_THETA ** (-2.0 * jnp.arange(HEAD_DIM // 2, dtype=F32) / HEAD_DIM)).reshape(-1, 1)
    c_pad = jnp.pad(c, ((0, SUBLANES - batch % SUBLANES), (0, 0))) if batch % SUBLANES else c

    for l in range(depth):
        w_t = w_in[l].T
        cos_t, sin_t, mod, w_q, w_mid = _prep(positions.reshape(1, t), inv_freq, c_pad, w_ada[l],
                                              b_ada[l].reshape(1, -1), w_t)
        mod3 = mod[:batch].reshape(batch, 6, d)
        back_weights = (w_branch_attn[l], w_branch_mlstm[l], w_out[l], w_ffn_gate[l], w_ffn_up[l],
                        w_ffn_down[l])
        qa, kv, qm, kt, vm, om, gates, ga, gm, wa, wb, wo, wg, wu, wd = _inproj(
            x2, mod3, g_pre_mix[l].reshape(1, d), cos_t, sin_t, w_q, w_mid, w_t,
            conv_w[l], conv_b[l].reshape(1, -1), back_weights, tiles_per_batch)
        bif = jnp.pad(b_if[l], (0, LANES - 2 * MLSTM_HEADS)).reshape(1, LANES)
        x2 = _back(attn_sinks[l], qa, kv, qm, kt, vm, om, gates, bif, mlstm_norm_w[l].reshape(1, -1),
                   x2, ga, gm, mod3, g_post_mix[l].reshape(1, d), g_pre_ffn[l].reshape(1, d),
                   g_post_ffn[l].reshape(1, d), wa, wb, wo, wg, wu, wd, tiles_per_batch)
    return x2.reshape(batch, seq, d)
```
